```python
import math
import jax, jax.numpy as jnp
from jax import lax
import numpy as np

D_MODEL = 1024
BATCH = 32
SEQ = 2048
DEPTH = 1

CHUNK = 64
D_SSM = D_MODEL // 2
SSM_GROUP = 16
N_SSM_GROUPS = D_SSM // SSM_GROUP
SSM_STATE = 64
HEAD_DIM = 64
D_ATT = D_MODEL // 2
N_HEADS = D_ATT // HEAD_DIM
Q_BLOCK = 128
D_IN = D_SSM + 3 * D_ATT
N_EXPERT_GROUPS = 4
EXPERTS_PER_GROUP = 8
N_EXPERTS = N_EXPERT_GROUPS * EXPERTS_PER_GROUP
TOP_K = 2
D_EXPERT = D_MODEL // 2
MOE_BLOCK = 128
EPS = 1e-6

kernel_name = "hybrid_s5_stickbreaking_hmoe_block"


def rms_norm(x, g):
    xf = x.astype(jnp.float32)
    y = xf * lax.rsqrt(jnp.mean(xf * xf, axis=-1, keepdims=True) + EPS)
    return (y * g.astype(jnp.float32)).astype(x.dtype)


def s5_mixer(u, lam_re, lam_im, log_dt, b_re, b_im, c_re, c_im, d_skip, w_glu, b_glu):
    bsz, t_len, _ = u.shape
    f32 = jnp.float32
    uf = u.astype(f32).reshape(bsz, t_len, N_SSM_GROUPS, SSM_GROUP)
    dt = jnp.exp(log_dt.astype(f32))[:, None]
    lr, li = lam_re.astype(f32), lam_im.astype(f32)
    mag = jnp.exp(lr * dt)
    abar_r, abar_i = mag * jnp.cos(li * dt), mag * jnp.sin(li * dt)
    den = lr * lr + li * li
    nr, ni = abar_r - 1.0, abar_i
    coef_r = (nr * lr + ni * li) / den
    coef_i = (ni * lr - nr * li) / den
    br, bi = b_re.astype(f32), b_im.astype(f32)
    bbar_r = coef_r[..., None] * br - coef_i[..., None] * bi
    bbar_i = coef_r[..., None] * bi + coef_i[..., None] * br
    bu_r = jnp.einsum('btgh,gph->tbgp', uf, bbar_r)
    bu_i = jnp.einsum('btgh,gph->tbgp', uf, bbar_i)
    a_r = jnp.broadcast_to(abar_r, (t_len, N_SSM_GROUPS, SSM_STATE))
    a_i = jnp.broadcast_to(abar_i, (t_len, N_SSM_GROUPS, SSM_STATE))

    def combine(e1, e2):
        a1r, a1i, b1r, b1i = e1
        a2r, a2i, b2r, b2i = e2
        ar = a1r * a2r - a1i * a2i
        ai = a1r * a2i + a1i * a2r
        a2rb, a2ib = a2r[:, None], a2i[:, None]
        outr = a2rb * b1r - a2ib * b1i + b2r
        outi = a2rb * b1i + a2ib * b1r + b2i
        return (ar, ai, outr, outi)

    _, _, h_r, h_i = lax.associative_scan(combine, (a_r, a_i, bu_r, bu_i), axis=0)
    y = (jnp.einsum('ghp,tbgp->btgh', c_re.astype(f32), h_r)
         - jnp.einsum('ghp,tbgp->btgh', c_im.astype(f32), h_i)
         + d_skip.astype(f32) * uf)
    y = jax.nn.gelu(y.reshape(bsz, t_len, D_SSM))
    y = y * jax.nn.sigmoid(y @ w_glu.astype(f32) + b_glu.astype(f32))
    return y.astype(u.dtype)


def stick_breaking_attention(q, k, v):
    t_len = q.shape[1]
    scale = 1.0 / math.sqrt(HEAD_DIM)
    outs = []
    for i in range(t_len // Q_BLOCK):
        start, end = i * Q_BLOCK, (i + 1) * Q_BLOCK
        qb, kb, vb = q[:, start:end], k[:, :end], v[:, :end]
        z = jnp.einsum('bqhd,bkhd->bhqk', qb, kb).astype(jnp.float32) * scale
        qpos = jnp.arange(start, end)
        kpos = jnp.arange(end)
        causal = kpos[None, :] < qpos[:, None]
        log_1m = jnp.where(causal, jax.nn.log_sigmoid(-z), 0.0)
        after = lax.cumsum(log_1m, axis=3, reverse=True) - log_1m
        att = jnp.where(causal, jnp.exp(jax.nn.log_sigmoid(z) + after), 0.0)
        outs.append(jnp.einsum('bhqk,bkhd->bqhd', att.astype(v.dtype), vb))
    return jnp.concatenate(outs, axis=1)


def hierarchical_moe(h, w_rg, b_rg, w_re, b_re, w_gate, w_up, w_down):
    bsz, t_len, d = h.shape
    n_tok = bsz * t_len
    hf = h.reshape(n_tok, d)
    g_logits = (hf @ w_rg).astype(jnp.float32) + b_rg.astype(jnp.float32)
    g_prob = jax.nn.softmax(g_logits, axis=-1)
    p_grp, grp = lax.top_k(g_prob, 1)
    e_logits = (jnp.einsum('nd,dge->nge', hf, w_re).astype(jnp.float32)
                + b_re.astype(jnp.float32))
    sel = e_logits[jnp.arange(n_tok), grp[:, 0]]
    e_prob = jax.nn.softmax(sel, axis=-1)
    top_p, top_i = lax.top_k(e_prob, TOP_K)
    top_p = top_p / jnp.sum(top_p, axis=-1, keepdims=True)
    gates = p_grp * top_p
    expert_id = grp * EXPERTS_PER_GROUP + top_i

    nk = n_tok * TOP_K
    flat_e = expert_id.reshape(nk).astype(jnp.int32)
    flat_tok = jnp.repeat(jnp.arange(n_tok, dtype=jnp.int32), TOP_K)
    flat_g = gates.reshape(nk)
    order = jnp.argsort(flat_e)
    se, stok, sg = flat_e[order], flat_tok[order], flat_g[order]
    counts = jax.ops.segment_sum(jnp.ones_like(flat_e), flat_e, num_segments=N_EXPERTS)
    starts = jnp.cumsum(counts) - counts
    pcounts = ((counts + MOE_BLOCK - 1) // MOE_BLOCK) * MOE_BLOCK
    pends = jnp.cumsum(pcounts)
    pstarts = pends - pcounts
    dest = pstarts[se] + (jnp.arange(nk, dtype=jnp.int32) - starts[se])
    n_rows = nk + N_EXPERTS * MOE_BLOCK
    n_blk = n_rows // MOE_BLOCK
    buf = jnp.zeros((n_rows, d), h.dtype).at[dest].set(hf[stok])
    blk_e = jnp.clip(jnp.searchsorted(pends, jnp.arange(n_blk, dtype=jnp.int32) * MOE_BLOCK,
                                      side='right'), 0, N_EXPERTS - 1)

    def expert_block(args):
        xb, e = args
        return (jax.nn.silu(xb @ w_gate[e]) * (xb @ w_up[e])) @ w_down[e]

    out = lax.map(expert_block, (buf.reshape(n_blk, MOE_BLOCK, d), blk_e)).reshape(n_rows, d)
    y = jnp.zeros((n_tok, d), h.dtype).at[stok].add(out[dest] * sg[:, None].astype(h.dtype))
    return y.reshape(bsz, t_len, d)


def setup_inputs(seed: int = 0) -> dict:
    key = jax.random.key(seed)
    ks = jax.random.split(key, 32)
    f32 = jnp.float32
    L, G, P, GS = DEPTH, N_SSM_GROUPS, SSM_STATE, SSM_GROUP
    nrm = lambda k, shape, s: jax.random.normal(k, shape, f32) * s
    lam_im0 = jnp.pi * jnp.arange(P, dtype=f32)
    return {
        "x": jax.random.normal(ks[0], (BATCH, SEQ, D_MODEL), f32),
        "g_mix": 1.0 + nrm(ks[1], (L, D_MODEL), 0.02),
        "w_in": nrm(ks[2], (L, D_MODEL, D_IN), D_MODEL ** -0.5),
        "ssm_lambda_re": -0.5 + nrm(ks[3], (L, G, P), 0.01),
        "ssm_lambda_im": lam_im0 + nrm(ks[4], (L, G, P), 0.01),
        "ssm_log_dt": jax.random.uniform(ks[5], (L, G), f32, math.log(1e-3), math.log(1e-1)),
        "ssm_b_re": nrm(ks[6], (L, G, P, GS), (2.0 * GS) ** -0.5),
        "ssm_b_im": nrm(ks[7], (L, G, P, GS), (2.0 * GS) ** -0.5),
        "ssm_c_re": nrm(ks[8], (L, G, GS, P), (2.0 * P) ** -0.5),
        "ssm_c_im": nrm(ks[9], (L, G, GS, P), (2.0 * P) ** -0.5),
        "ssm_d": nrm(ks[10], (L, G, GS), 1.0),
        "ssm_w_glu": nrm(ks[11], (L, D_SSM, D_SSM), D_SSM ** -0.5),
        "ssm_b_glu": nrm(ks[12], (L, D_SSM), 0.01),
        "g_q": 1.0 + nrm(ks[13], (L, HEAD_DIM), 0.02),
        "g_k": 1.0 + nrm(ks[14], (L, HEAD_DIM), 0.02),
        "g_ssm_out": 1.0 + nrm(ks[15], (L, D_SSM), 0.02),
        "g_attn_out": 1.0 + nrm(ks[16], (L, D_ATT), 0.02),
        "w_out": nrm(ks[17], (L, D_SSM + D_ATT, D_MODEL), (D_SSM + D_ATT) ** -0.5),
        "g_ffn": 1.0 + nrm(ks[18], (L, D_MODEL), 0.02),
        "w_router_group": nrm(ks[19], (L, D_MODEL, N_EXPERT_GROUPS), D_MODEL ** -0.5),
        "b_router_group": nrm(ks[20], (L, N_EXPERT_GROUPS), 0.01),
        "w_router_expert": nrm(ks[21], (L, D_MODEL, N_EXPERT_GROUPS, EXPERTS_PER_GROUP), D_MODEL ** -0.5),
        "b_router_expert": nrm(ks[22], (L, N_EXPERT_GROUPS, EXPERTS_PER_GROUP), 0.01),
        "w_gate": nrm(ks[23], (L, N_EXPERTS, D_MODEL, D_EXPERT), D_MODEL ** -0.5),
        "w_up": nrm(ks[24], (L, N_EXPERTS, D_MODEL, D_EXPERT), D_MODEL ** -0.5),
        "w_down": nrm(ks[25], (L, N_EXPERTS, D_EXPERT, D_MODEL), D_EXPERT ** -0.5),
    }


def reference(x, g_mix, w_in, ssm_lambda_re, ssm_lambda_im, ssm_log_dt, ssm_b_re, ssm_b_im,
              ssm_c_re, ssm_c_im, ssm_d, ssm_w_glu, ssm_b_glu, g_q, g_k, g_ssm_out, g_attn_out,
              w_out, g_ffn, w_router_group, b_router_group, w_router_expert, b_router_expert,
              w_gate, w_up, w_down):
    bsz, t_len, _ = x.shape
    for l in range(DEPTH):
        h = rms_norm(x, g_mix[l])
        proj = h @ w_in[l]
        u = proj[..., :D_SSM]
        q = proj[..., D_SSM:D_SSM + D_ATT].reshape(bsz, t_len, N_HEADS, HEAD_DIM)
        k = proj[..., D_SSM + D_ATT:D_SSM + 2 * D_ATT].reshape(bsz, t_len, N_HEADS, HEAD_DIM)
        v = proj[..., D_SSM + 2 * D_ATT:].reshape(bsz, t_len, N_HEADS, HEAD_DIM)
        y_ssm = s5_mixer(u, ssm_lambda_re[l], ssm_lambda_im[l], ssm_log_dt[l], ssm_b_re[l],
                         ssm_b_im[l], ssm_c_re[l], ssm_c_im[l], ssm_d[l], ssm_w_glu[l], ssm_b_glu[l])
        q = rms_norm(q, g_q[l])
        k = rms_norm(k, g_k[l])
        y_att = stick_breaking_attention(q, k, v).reshape(bsz, t_len, D_ATT)
        mixed = jnp.concatenate([rms_norm(y_ssm, g_ssm_out[l]), rms_norm(y_att, g_attn_out[l])], axis=-1)
        x = x + mixed @ w_out[l]
        h2 = rms_norm(x, g_ffn[l])
        x = x + hierarchical_moe(h2, w_router_group[l], b_router_group[l], w_router_expert[l],
                                 b_router_expert[l], w_gate[l], w_up[l], w_down[l])
    return x
```

```python
import functools
import math

import jax
import jax.numpy as jnp
from jax import lax
from jax.experimental import pallas as pl
from jax.experimental.pallas import tpu as pltpu

F32 = jnp.float32
BF16 = jnp.bfloat16
EPS = 1e-6

LANES = 128
VMEM_LIMIT_BYTES = 56 * 1024 * 1024

SSM_GROUP = 16
SSM_STATE = 64
SSM_CHUNK = 16
HEAD_DIM = 64
N_EXPERT_GROUPS = 4
EXPERTS_PER_GROUP = 8
N_EXPERTS = N_EXPERT_GROUPS * EXPERTS_PER_GROUP
ROUTER_LANE0 = N_EXPERT_GROUPS
MOE_ROWS = 256
ATT_SKIP = 110.0

IN_TM = 512
ATT_TQ = 256
ATT_TK = 256
POST_TM = 512
DISPATCH_TM = 512
COMBINE_TM = 256


def _cparams(*sem):
    return pltpu.CompilerParams(dimension_semantics=sem, vmem_limit_bytes=VMEM_LIMIT_BYTES)


def _lane_iota(shape):
    return lax.broadcasted_iota(jnp.int32, shape, len(shape) - 1)


def _head_rms(t, gain):
    outs = []
    for c in range(t.shape[1] // LANES):
        blk = t[:, c * LANES:(c + 1) * LANES]
        sq = blk * blk
        lo = _lane_iota(blk.shape) < HEAD_DIM
        s_lo = jnp.sum(jnp.where(lo, sq, 0.0), axis=-1, keepdims=True)
        s_hi = jnp.sum(jnp.where(lo, 0.0, sq), axis=-1, keepdims=True)
        inv = jnp.where(lo, lax.rsqrt(s_lo * (1.0 / HEAD_DIM) + EPS),
                        lax.rsqrt(s_hi * (1.0 / HEAD_DIM) + EPS))
        outs.append(blk * inv * gain[:, c * LANES:(c + 1) * LANES])
    return jnp.concatenate(outs, axis=-1)


def _in_proj_kernel(x_ref, g_ref, w_ref, gq_ref, gk_ref, u_ref, q_ref, k_ref, v_ref, *, d_ssm, d_att, scale):
    x = x_ref[...]
    inv = lax.rsqrt(jnp.mean(x * x, axis=-1, keepdims=True) + EPS)
    h = (x * inv * g_ref[...]).astype(BF16)
    proj = jnp.dot(h, w_ref[...], preferred_element_type=F32)
    u_ref[...] = proj[:, :d_ssm].astype(BF16)
    q = _head_rms(proj[:, d_ssm:d_ssm + d_att], gq_ref[...])
    k = _head_rms(proj[:, d_ssm + d_att:d_ssm + 2 * d_att], gk_ref[...])
    q_ref[...] = (q * scale).astype(BF16)
    k_ref[...] = k.astype(BF16)
    v_ref[...] = proj[:, d_ssm + 2 * d_att:].astype(BF16)


def _in_proj(x2, g_mix, w_in_bf, gq_t, gk_t, d_ssm, d_att):
    n, d = x2.shape
    d_in = w_in_bf.shape[1]
    tm = IN_TM
    row = lambda w: pl.BlockSpec((tm, w), lambda i: (i, 0))
    full = lambda a: pl.BlockSpec(a.shape, lambda i: (0,) * a.ndim)
    out_sd = jax.ShapeDtypeStruct((n, d_att), BF16)
    return pl.pallas_call(
        functools.partial(_in_proj_kernel, d_ssm=d_ssm, d_att=d_att, scale=1.0 / math.sqrt(HEAD_DIM)),
        grid=(n // tm,),
        in_specs=[row(d), full(g_mix), full(w_in_bf), full(gq_t), full(gk_t)],
        out_specs=[row(d_ssm), row(d_att), row(d_att), row(d_att)],
        out_shape=[jax.ShapeDtypeStruct((n, d_ssm), BF16), out_sd, out_sd, out_sd],
        compiler_params=_cparams("parallel"),
        name="in_proj",
    )(x2, g_mix, w_in_bf, gq_t, gk_t)


def _s5_tables(lam_re, lam_im, log_dt, b_re, b_im, c_re, c_im, d_skip):
    hp = lax.Precision.HIGHEST
    L = SSM_CHUNK
    g_n, p_n = lam_re.shape
    dt = jnp.exp(log_dt)[:, None]
    lr, li = lam_re, lam_im
    ls = jnp.arange(L + 1, dtype=F32)[:, None, None]
    mag = jnp.exp(lr * dt * ls)
    pr, pi = mag * jnp.cos(li * dt * ls), mag * jnp.sin(li * dt * ls)
    abar_r, abar_i = pr[1], pi[1]
    den = lr * lr + li * li
    nr, ni = abar_r - 1.0, abar_i
    coef_r = (nr * lr + ni * li) / den
    coef_i = (ni * lr - nr * li) / den
    bbr = coef_r[..., None] * b_re - coef_i[..., None] * b_im
    bbi = coef_r[..., None] * b_im + coef_i[..., None] * b_re
    wr = pr[..., None] * bbr - pi[..., None] * bbi
    wi = pr[..., None] * bbi + pi[..., None] * bbr
    kl = (jnp.einsum('gop,lgpi->lgoi', c_re, wr[:L], precision=hp)
          - jnp.einsum('gop,lgpi->lgoi', c_im, wi[:L], precision=hp))
    kl = kl.at[0].add(jax.vmap(jnp.diag)(d_skip))
    s_idx = jnp.arange(L)[:, None]
    t_idx = jnp.arange(L)[None, :]
    lag = jnp.clip(t_idx - s_idx, 0, L - 1)
    m = jnp.where((t_idx >= s_idx)[:, :, None, None, None], kl[lag], 0.0)
    m = m.transpose(2, 0, 4, 1, 3).reshape(g_n, L * SSM_GROUP, L * SSM_GROUP)
    bst_r = wr[:L][::-1].transpose(1, 0, 3, 2).reshape(g_n, L * SSM_GROUP, p_n)
    bst_i = wi[:L][::-1].transpose(1, 0, 3, 2).reshape(g_n, L * SSM_GROUP, p_n)
    p1r, p1i = pr[1:], pi[1:]
    cst_r = (c_re[None] * p1r[:, :, None, :] - c_im[None] * p1i[:, :, None, :])
    cst_i = -(c_re[None] * p1i[:, :, None, :] + c_im[None] * p1r[:, :, None, :])
    cst_r = cst_r.transpose(1, 3, 0, 2).reshape(g_n, p_n, L * SSM_GROUP)
    cst_i = cst_i.transpose(1, 3, 0, 2).reshape(g_n, p_n, L * SSM_GROUP)
    a_l = jnp.stack([pr[L], pi[L]], axis=1)
    return (m.astype(BF16), bst_r.astype(BF16), bst_i.astype(BF16),
            cst_r.astype(BF16), cst_i.astype(BF16), a_l)


def _s5_kernel(x_ref, m_ref, br_ref, bi_ref, cr_ref, ci_ref, a_ref, y_ref, sr_ref, si_ref, pr_ref, pi_ref,
               *, n_chunks, bsz):
    x = x_ref[0]
    sr_ref[...] = jnp.dot(x, br_ref[0], preferred_element_type=F32)
    si_ref[...] = jnp.dot(x, bi_ref[0], preferred_element_type=F32)
    ar = a_ref[0, 0:1, :]
    ai = a_ref[0, 1:2, :]

    def step(c, carry):
        hr, hi = carry
        rows = pl.ds(pl.multiple_of(c * bsz, bsz), bsz)
        pr_ref[rows, :] = hr
        pi_ref[rows, :] = hi
        return (ar * hr - ai * hi + sr_ref[rows, :], ar * hi + ai * hr + si_ref[rows, :])

    zero = jnp.zeros((bsz, SSM_STATE), F32)
    lax.fori_loop(0, n_chunks, step, (zero, zero))
    y = jnp.dot(x, m_ref[0], preferred_element_type=F32)
    y += jnp.dot(pr_ref[...].astype(BF16), cr_ref[0], preferred_element_type=F32)
    y += jnp.dot(pi_ref[...].astype(BF16), ci_ref[0], preferred_element_type=F32)
    y_ref[0] = y


def _s5(xg, tables, bsz):
    g_n, rows, w = xg.shape
    m, bst_r, bst_i, cst_r, cst_i, a_l = tables
    per_g = lambda a: pl.BlockSpec((1,) + a.shape[1:], lambda g: (g,) + (0,) * (a.ndim - 1))
    state = pltpu.VMEM((rows, SSM_STATE), F32)
    return pl.pallas_call(
        functools.partial(_s5_kernel, n_chunks=rows // bsz, bsz=bsz),
        grid=(g_n,),
        in_specs=[per_g(xg), per_g(m), per_g(bst_r), per_g(bst_i), per_g(cst_r), per_g(cst_i), per_g(a_l)],
        out_specs=per_g(xg),
        out_shape=jax.ShapeDtypeStruct((g_n, rows, w), F32),
        scratch_shapes=[state, state, state, state],
        compiler_params=_cparams("parallel"),
        name="s5",
    )(xg, m, bst_r, bst_i, cst_r, cst_i, a_l)


def _softplus(z):
    return jnp.maximum(z, 0.0) + jnp.log(1.0 + jnp.exp(-jnp.abs(z)))


def _att_tile(qm, k, v, tri, r_in, causal):
    z = lax.dot_general(qm, k, (((1,), (1,)), ((), ())), preferred_element_type=F32)
    sp = _softplus(z)
    if causal:
        rows = lax.broadcasted_iota(jnp.int32, z.shape, 0)
        cols = lax.broadcasted_iota(jnp.int32, z.shape, 1)
        keep = cols < rows
        sp_m = jnp.where(keep, sp, 0.0)
    else:
        sp_m = sp
    hi = sp_m.astype(BF16)
    lo = (sp_m - hi.astype(F32)).astype(BF16)
    newer = jnp.dot(hi, tri, preferred_element_type=F32) + jnp.dot(lo, tri, preferred_element_type=F32)
    att = jnp.exp(z - sp - newer - r_in)
    if causal:
        att = jnp.where(keep, att, 0.0)
    pv = jnp.dot(att.astype(BF16), v, preferred_element_type=F32)
    return pv, newer[:, 0:1] + sp_m[:, 0:1]


def _attn_kernel(q_ref, k_ref, v_ref, o_ref, *, t_len, tq, tk):
    assert tq == tk
    nq = t_len // tq
    r_i = lax.broadcasted_iota(jnp.int32, (tk, tk), 0)
    c_i = lax.broadcasted_iota(jnp.int32, (tk, tk), 1)
    tri = jnp.where(r_i > c_i, 1.0, 0.0).astype(BF16)
    head0 = _lane_iota((tq, LANES)) < HEAD_DIM

    def q_tile(i, _):
        q0 = pl.multiple_of(i * tq, tq)
        q = q_ref[pl.ds(q0, tq), :]
        k_d = k_ref[pl.ds(q0, tk), :]
        v_d = v_ref[pl.ds(q0, tk), :]
        outs = []
        for first in (True, False):
            qm = jnp.where(head0 if first else jnp.logical_not(head0), q, jnp.zeros_like(q))
            acc, r = _att_tile(qm, k_d, v_d, tri, jnp.zeros((tq, 1), F32), causal=True)

            def cond(c):
                j, _, r_c = c
                return jnp.logical_and(j >= 0, jnp.min(r_c) < ATT_SKIP)

            def body(c):
                j, acc_c, r_c = c
                k0 = pl.multiple_of(j * tk, tk)
                pv, dr = _att_tile(qm, k_ref[pl.ds(k0, tk), :], v_ref[pl.ds(k0, tk), :], tri, r_c, causal=False)
                return j - 1, acc_c + pv, r_c + dr

            _, acc, _ = lax.while_loop(cond, body, (i - 1, acc, r))
            outs.append(acc)
        o_ref[pl.ds(q0, tq), :] = jnp.where(head0, outs[0], outs[1])
        return 0

    lax.fori_loop(0, nq, q_tile, 0)


def _attention(q, k, v, bsz, t_len):
    n, d_att = q.shape
    spec = pl.BlockSpec((t_len, LANES), lambda b, p: (b, p))
    return pl.pallas_call(
        functools.partial(_attn_kernel, t_len=t_len, tq=ATT_TQ, tk=ATT_TK),
        grid=(bsz, d_att // LANES),
        in_specs=[spec, spec, spec],
        out_specs=spec,
        out_shape=jax.ShapeDtypeStruct((n, d_att), F32),
        compiler_params=_cparams("parallel", "parallel"),
        name="attn",
    )(q, k, v)


def _rms(t, gain):
    return t * lax.rsqrt(jnp.mean(t * t, axis=-1, keepdims=True) + EPS) * gain


def _gelu_tanh(y):
    return 0.5 * y * (1.0 + jnp.tanh(math.sqrt(2.0 / math.pi) * (y + 0.044715 * (y * y * y))))


def _pack_bf16_pairs(a, b):
    ua = pltpu.bitcast(a.astype(BF16).astype(F32), jnp.uint32)
    ub = pltpu.bitcast(b.astype(BF16).astype(F32), jnp.uint32)
    return ua | (ub >> 16)


def _unpack_bf16_pairs(w):
    a = pltpu.bitcast(w & jnp.uint32(0xFFFF0000), F32).astype(BF16)
    b = pltpu.bitcast(w << 16, F32).astype(BF16)
    return a, b


def _post_kernel(x_ref, ys_ref, ya_ref, wglu_ref, bglu_ref, gs_ref, ga_ref, wo_ref, gf_ref,
                 wrh_ref, wrl_ref, br_ref, x1_ref, h2_ref, route_ref, cnt_ref, run_ref, *, d_ssm):
    i = pl.program_id(0)

    @pl.when(i == 0)
    def _():
        run_ref[...] = jnp.zeros_like(run_ref)

    y = _gelu_tanh(ys_ref[...])
    gate = jnp.dot(y.astype(BF16), wglu_ref[...], preferred_element_type=F32) + bglu_ref[...]
    y = y * jax.nn.sigmoid(gate)
    m_s = _rms(y, gs_ref[...]).astype(BF16)
    m_a = _rms(ya_ref[...], ga_ref[...]).astype(BF16)
    mix = (jnp.dot(m_s, wo_ref[:d_ssm, :], preferred_element_type=F32)
           + jnp.dot(m_a, wo_ref[d_ssm:, :], preferred_element_type=F32))
    x1 = x_ref[...] + mix
    x1_ref[...] = x1
    h2 = _rms(x1, gf_ref[...])
    half = h2.shape[1] // 2
    h2_ref[...] = _pack_bf16_pairs(h2[:, :half], h2[:, half:])

    h_hi = h2.astype(BF16)
    h_lo = (h2 - h_hi.astype(F32)).astype(BF16)
    lg = (jnp.dot(h_hi, wrh_ref[...], preferred_element_type=F32)
          + jnp.dot(h_lo, wrh_ref[...], preferred_element_type=F32)
          + jnp.dot(h_hi, wrl_ref[...], preferred_element_type=F32)) + br_ref[...]
    tm = lg.shape[0]
    lane = _lane_iota(lg.shape).astype(F32)
    neg = -jnp.inf
    first = lambda hit: jnp.min(jnp.where(hit, lane, float(LANES)), axis=-1, keepdims=True)
    glog = jnp.where(lane < N_EXPERT_GROUPS, lg, neg)
    gmax = jnp.max(glog, axis=-1, keepdims=True)
    p_grp = 1.0 / jnp.sum(jnp.exp(glog - gmax), axis=-1, keepdims=True)
    grp = first(glog == gmax)
    e0 = ROUTER_LANE0 + grp * EXPERTS_PER_GROUP
    elog = jnp.where(jnp.logical_and(lane >= e0, lane < e0 + EXPERTS_PER_GROUP), lg, neg)
    m1 = jnp.max(elog, axis=-1, keepdims=True)
    i1 = first(elog == m1)
    elog2 = jnp.where(lane == i1, neg, elog)
    m2 = jnp.max(elog2, axis=-1, keepdims=True)
    i2 = first(elog2 == m2)
    e21 = jnp.exp(m2 - m1)
    g1 = p_grp * (1.0 / (1.0 + e21))
    g2 = p_grp * (e21 / (1.0 + e21))

    sel1 = lane == i1
    sel2 = lane == i2
    onehot = jnp.where(jnp.logical_or(sel1, sel2), 1.0, 0.0)
    r_i = lax.broadcasted_iota(jnp.int32, (tm, tm), 0)
    c_i = lax.broadcasted_iota(jnp.int32, (tm, tm), 1)
    lower = jnp.where(c_i < r_i, 1.0, 0.0).astype(BF16)
    before = jnp.dot(lower, onehot.astype(BF16), preferred_element_type=F32) + run_ref[0:1, :]
    rank1 = jnp.sum(jnp.where(sel1, before, 0.0), axis=-1, keepdims=True)
    rank2 = jnp.sum(jnp.where(sel2, before, 0.0), axis=-1, keepdims=True)
    run_ref[0:1, :] = run_ref[0:1, :] + jnp.sum(onehot, axis=0, keepdims=True)
    cnt_ref[...] = run_ref[...]

    fields = (i1 - ROUTER_LANE0, i2 - ROUTER_LANE0, g1, g2, rank1, rank2)
    route = jnp.zeros(lg.shape, F32)
    for pos, val in enumerate(fields):
        route = jnp.where(lane == pos, val, route)
    route_ref[...] = route


def _post(x2, y_ssm, y_att, w_glu_bf, b_glu, g_ssm, g_att, w_out_bf, g_ffn, wr_hi, wr_lo, b_r):
    n, d = x2.shape
    d_ssm = y_ssm.shape[1]
    tm = POST_TM
    row = lambda w: pl.BlockSpec((tm, w), lambda i: (i, 0))
    full = lambda a: pl.BlockSpec(a.shape, lambda i: (0,) * a.ndim)
    cnt_spec = pl.BlockSpec((8, LANES), lambda i: (0, 0))
    return pl.pallas_call(
        functools.partial(_post_kernel, d_ssm=d_ssm),
        grid=(n // tm,),
        in_specs=[row(d), row(d_ssm), row(y_att.shape[1]), full(w_glu_bf), full(b_glu), full(g_ssm),
                  full(g_att), full(w_out_bf), full(g_ffn), full(wr_hi), full(wr_lo), full(b_r)],
        out_specs=[row(d), row(d // 2), row(LANES), cnt_spec],
        out_shape=[jax.ShapeDtypeStruct((n, d), F32), jax.ShapeDtypeStruct((n, d // 2), jnp.uint32),
                   jax.ShapeDtypeStruct((n, LANES), F32), jax.ShapeDtypeStruct((8, LANES), F32)],
        scratch_shapes=[pltpu.VMEM((8, LANES), F32)],
        compiler_params=_cparams("arbitrary"),
        name="post",
    )(x2, y_ssm, y_att, w_glu_bf, b_glu, g_ssm, g_att, w_out_bf, g_ffn, wr_hi, wr_lo, b_r)


def _row_copy(src_ref, src_row, dst_ref, dst_row, sem):
    return pltpu.make_async_copy(src_ref.at[pl.ds(src_row, 1), :], dst_ref.at[pl.ds(dst_row, 1), :], sem)


def _dispatch_kernel(dest_ref, h_ref, buf_in_ref, buf_ref, sem, *, tm):
    del buf_in_ref

    def issue(r, _):
        _row_copy(h_ref, r, buf_ref, dest_ref[0, 0, 2 * r], sem).start()
        _row_copy(h_ref, r, buf_ref, dest_ref[0, 0, 2 * r + 1], sem).start()
        return 0

    lax.fori_loop(0, tm, issue, 0)
    for _ in range(2):
        pltpu.make_async_copy(h_ref, buf_ref.at[pl.ds(0, tm), :], sem).wait()


def _dispatch(dest, h2p, n_rows):
    n, w = h2p.shape
    tm = DISPATCH_TM
    dest3 = dest.reshape(n // tm, 1, 2 * tm)
    buf0 = jnp.zeros((n_rows, w), h2p.dtype)
    return pl.pallas_call(
        functools.partial(_dispatch_kernel, tm=tm),
        grid=(n // tm,),
        in_specs=[pl.BlockSpec((1, 1, 2 * tm), lambda i: (i, 0, 0), memory_space=pltpu.SMEM),
                  pl.BlockSpec((tm, w), lambda i: (i, 0)),
                  pl.BlockSpec(memory_space=pl.ANY)],
        out_specs=pl.BlockSpec(memory_space=pl.ANY),
        out_shape=jax.ShapeDtypeStruct((n_rows, w), h2p.dtype),
        scratch_shapes=[pltpu.SemaphoreType.DMA(())],
        input_output_aliases={2: 0},
        compiler_params=_cparams("arbitrary"),
        name="dispatch",
    )(dest3, h2p, buf0)


def _experts_kernel(blk_e_ref, used_ref, x_ref, wg_ref, wu_ref, wd_ref, o_ref):
    del blk_e_ref
    i = pl.program_id(0)

    @pl.when(i < used_ref[0])
    def _():
        xa, xb = _unpack_bf16_pairs(x_ref[...])
        half = xa.shape[1]
        gate = (jnp.dot(xa, wg_ref[0, :half, :], preferred_element_type=F32)
                + jnp.dot(xb, wg_ref[0, half:, :], preferred_element_type=F32))
        up = (jnp.dot(xa, wu_ref[0, :half, :], preferred_element_type=F32)
              + jnp.dot(xb, wu_ref[0, half:, :], preferred_element_type=F32))
        hid = (jax.nn.silu(gate) * up).astype(BF16)
        o_ref[...] = jnp.dot(hid, wd_ref[0], preferred_element_type=F32)

    @pl.when(i >= used_ref[0])
    def _():
        o_ref[...] = jnp.zeros_like(o_ref)


def _experts(blk_e, n_used, buf, wg_bf, wu_bf, wd_bf):
    n_rows, w = buf.shape
    d = wd_bf.shape[2]
    wspec = lambda a: pl.BlockSpec((1,) + a.shape[1:], lambda i, be, nu: (be[i], 0, 0))
    grid_spec = pltpu.PrefetchScalarGridSpec(
        num_scalar_prefetch=2,
        grid=(n_rows // MOE_ROWS,),
        in_specs=[pl.BlockSpec((MOE_ROWS, w), lambda i, be, nu: (i, 0)), wspec(wg_bf), wspec(wu_bf), wspec(wd_bf)],
        out_specs=pl.BlockSpec((MOE_ROWS, d), lambda i, be, nu: (i, 0)),
    )
    return pl.pallas_call(
        _experts_kernel,
        grid_spec=grid_spec,
        out_shape=jax.ShapeDtypeStruct((n_rows, d), F32),
        compiler_params=_cparams("arbitrary"),
        name="experts",
    )(blk_e, n_used, buf, wg_bf, wu_bf, wd_bf)


def _combine_kernel(dest_ref, x1_ref, route_ref, eo_ref, o_ref, rows_ref, sem, *, tm):
    def issue(r, _):
        _row_copy(eo_ref, dest_ref[0, 0, 2 * r], rows_ref.at[0], r, sem).start()
        _row_copy(eo_ref, dest_ref[0, 0, 2 * r + 1], rows_ref.at[1], r, sem).start()
        return 0

    lax.fori_loop(0, tm, issue, 0)
    for s in range(2):
        pltpu.make_async_copy(eo_ref.at[pl.ds(0, tm), :], rows_ref.at[s], sem).wait()
    route = route_ref[...]
    o_ref[...] = x1_ref[...] + (route[:, 2:3] * rows_ref[0] + route[:, 3:4] * rows_ref[1])


def _combine(dest, x1, route, eo):
    n, d = x1.shape
    tm = COMBINE_TM
    dest3 = dest.reshape(n // tm, 1, 2 * tm)
    return pl.pallas_call(
        functools.partial(_combine_kernel, tm=tm),
        grid=(n // tm,),
        in_specs=[pl.BlockSpec((1, 1, 2 * tm), lambda i: (i, 0, 0), memory_space=pltpu.SMEM),
                  pl.BlockSpec((tm, d), lambda i: (i, 0)),
                  pl.BlockSpec((tm, LANES), lambda i: (i, 0)),
                  pl.BlockSpec(memory_space=pl.ANY)],
        out_specs=pl.BlockSpec((tm, d), lambda i: (i, 0)),
        out_shape=jax.ShapeDtypeStruct((n, d), F32),
        scratch_shapes=[pltpu.VMEM((2, tm, d), F32), pltpu.SemaphoreType.DMA(())],
        compiler_params=_cparams("arbitrary"),
        name="combine",
    )(dest3, x1, route, eo)


def _layer(x, g_mix, w_in, lam_re, lam_im, log_dt, b_re, b_im, c_re, c_im, d_skip, w_glu, b_glu, g_q, g_k,
           g_ssm_out, g_attn_out, w_out, g_ffn, w_rg, b_rg, w_re, b_re_router, w_gate, w_up, w_down):
    bsz, t_len, d = x.shape
    n = bsz * t_len
    d_ssm = w_glu.shape[0]
    d_att = g_attn_out.shape[0]
    n_heads = d_att // HEAD_DIM
    n_groups = d_ssm // SSM_GROUP
    n_chunks = t_len // SSM_CHUNK
    x2 = x.reshape(n, d)

    u, q, k, v = _in_proj(x2, g_mix[None], w_in.astype(BF16), jnp.tile(g_q, n_heads)[None],
                          jnp.tile(g_k, n_heads)[None], d_ssm, d_att)

    xg = (u.reshape(bsz, n_chunks, SSM_CHUNK, n_groups, SSM_GROUP).transpose(3, 1, 0, 2, 4)
          .reshape(n_groups, n_chunks * bsz, SSM_CHUNK * SSM_GROUP))
    yg = _s5(xg, _s5_tables(lam_re, lam_im, log_dt, b_re, b_im, c_re, c_im, d_skip), bsz)
    y_ssm = (yg.reshape(n_groups, n_chunks, bsz, SSM_CHUNK, SSM_GROUP).transpose(2, 1, 3, 0, 4)
             .reshape(n, d_ssm))

    y_att = _attention(q, k, v, bsz, t_len)

    w_r = jnp.concatenate([w_rg, w_re.reshape(d, N_EXPERTS)], axis=1)
    w_r = jnp.pad(w_r, ((0, 0), (0, LANES - w_r.shape[1])))
    wr_hi = w_r.astype(BF16)
    wr_lo = (w_r - wr_hi.astype(F32)).astype(BF16)
    b_r = jnp.pad(jnp.concatenate([b_rg, b_re_router.reshape(N_EXPERTS)]), (0, LANES - ROUTER_LANE0 - N_EXPERTS))[None]
    x1, h2p, route, cnt = _post(x2, y_ssm, y_att, w_glu.astype(BF16), b_glu[None], g_ssm_out[None],
                                g_attn_out[None], w_out.astype(BF16), g_ffn[None], wr_hi, wr_lo, b_r)

    counts = cnt[0, ROUTER_LANE0:ROUTER_LANE0 + N_EXPERTS].astype(jnp.int32)
    pcounts = ((counts + MOE_ROWS - 1) // MOE_ROWS) * MOE_ROWS
    pends = jnp.cumsum(pcounts)
    pstarts = pends - pcounts
    expert = route[:, 0:2].astype(jnp.int32)
    dest = pstarts[expert] + route[:, 4:6].astype(jnp.int32)
    n_rows = n * 2 + N_EXPERTS * MOE_ROWS
    n_blk = n_rows // MOE_ROWS
    blk_e = jnp.clip(jnp.searchsorted(pends, jnp.arange(n_blk, dtype=jnp.int32) * MOE_ROWS, side='right'),
                     0, N_EXPERTS - 1).astype(jnp.int32)
    n_used = (pends[-1:] // MOE_ROWS).astype(jnp.int32)

    buf = _dispatch(dest, h2p, n_rows)
    eo = _experts(blk_e, n_used, buf, w_gate.astype(BF16), w_up.astype(BF16), w_down.astype(BF16))
    out = _combine(dest, x1, route, eo)
    return out.reshape(bsz, t_len, d)


def kernel(x, g_mix, w_in, ssm_lambda_re, ssm_lambda_im, ssm_log_dt, ssm_b_re, ssm_b_im, ssm_c_re, ssm_c_im, ssm_d, ssm_w_glu, ssm_b_glu, g_q, g_k, g_ssm_out, g_attn_out, w_out, g_ffn, w_router_group, b_router_group, w_router_expert, b_router_expert, w_gate, w_up, w_down):
    for l in range(g_mix.shape[0]):
        x = _layer(x, g_mix[l], w_in[l], ssm_lambda_re[l], ssm_lambda_im[l], ssm_log_dt[l], ssm_b_re[l],
                   ssm_b_im[l], ssm_c_re[l], ssm_c_im[l], ssm_d[l], ssm_w_glu[l], ssm_b_glu[l], g_q[l], g_k[l],
                   g_ssm_out[l], g_attn_out[l], w_out[l], g_ffn[l], w_router_group[l], b_router_group[l],
                   w_router_expert[l], b_router_expert[l], w_gate[l], w_up[l], w_down[l])
    return x
```

```python
import functools
import math

import jax
import jax.numpy as jnp
from jax import lax
from jax.experimental import pallas as pl
from jax.experimental.pallas import tpu as pltpu

F32 = jnp.float32
BF16 = jnp.bfloat16
EPS = 1e-6

LANES = 128
VMEM_LIMIT_BYTES = 56 * 1024 * 1024

SSM_GROUP = 16
SSM_STATE = 64
SSM_CHUNK = 16
HEAD_DIM = 64
N_EXPERT_GROUPS = 4
EXPERTS_PER_GROUP = 8
N_EXPERTS = N_EXPERT_GROUPS * EXPERTS_PER_GROUP
ROUTER_LANE0 = N_EXPERT_GROUPS
MOE_ROWS = 256
ATT_SKIP = 110.0

S5_GPB = LANES // SSM_GROUP
S5_CHUNKS_PER_STEP = 8

IN_TM = 512
ATT_TILE = 256
POST_TM = 512
DISPATCH_TM = 512
COMBINE_TM = 256


def _cparams(*sem):
    return pltpu.CompilerParams(dimension_semantics=sem, vmem_limit_bytes=VMEM_LIMIT_BYTES)


def _lane_iota(shape):
    return lax.broadcasted_iota(jnp.int32, shape, len(shape) - 1)


def _head_rms(t, gain):
    outs = []
    for c in range(t.shape[1] // LANES):
        blk = t[:, c * LANES:(c + 1) * LANES]
        sq = blk * blk
        lo = _lane_iota(blk.shape) < HEAD_DIM
        s_lo = jnp.sum(jnp.where(lo, sq, 0.0), axis=-1, keepdims=True)
        s_hi = jnp.sum(jnp.where(lo, 0.0, sq), axis=-1, keepdims=True)
        inv = jnp.where(lo, lax.rsqrt(s_lo * (1.0 / HEAD_DIM) + EPS),
                        lax.rsqrt(s_hi * (1.0 / HEAD_DIM) + EPS))
        outs.append(blk * inv * gain[:, c * LANES:(c + 1) * LANES])
    return jnp.concatenate(outs, axis=-1)


def _in_proj_kernel(x_ref, g_ref, w_ref, gq_ref, gk_ref, u_ref, q_ref, k_ref, v_ref, *, d_ssm, d_att, scale):
    x = x_ref[...]
    inv = lax.rsqrt(jnp.mean(x * x, axis=-1, keepdims=True) + EPS)
    h = (x * inv * g_ref[...]).astype(BF16)
    proj = jnp.dot(h, w_ref[...], preferred_element_type=F32)
    u_ref[...] = proj[:, :d_ssm].astype(BF16)
    q = _head_rms(proj[:, d_ssm:d_ssm + d_att], gq_ref[...])
    k = _head_rms(proj[:, d_ssm + d_att:d_ssm + 2 * d_att], gk_ref[...])
    q_ref[...] = (q * scale).astype(BF16)
    k_ref[...] = k.astype(BF16)
    v_ref[...] = proj[:, d_ssm + 2 * d_att:].astype(BF16)


def _in_proj(x2, g_mix, w_in_bf, gq_t, gk_t, d_ssm, d_att, bsz, t_len):
    n, d = x2.shape
    tm = IN_TM
    nt = t_len // tm
    row = lambda w: pl.BlockSpec((tm, w), lambda b, t: (b * nt + t, 0))
    full = lambda a: pl.BlockSpec(a.shape, lambda b, t: (0,) * a.ndim)
    out_sd = jax.ShapeDtypeStruct((n, d_att), BF16)
    return pl.pallas_call(
        functools.partial(_in_proj_kernel, d_ssm=d_ssm, d_att=d_att, scale=1.0 / math.sqrt(HEAD_DIM)),
        grid=(bsz, nt),
        in_specs=[row(d), full(g_mix), full(w_in_bf), full(gq_t), full(gk_t)],
        out_specs=[pl.BlockSpec((tm, d_ssm), lambda b, t: (t, b)), row(d_att), row(d_att), row(d_att)],
        out_shape=[jax.ShapeDtypeStruct((t_len, bsz * d_ssm), BF16), out_sd, out_sd, out_sd],
        compiler_params=_cparams("parallel", "parallel"),
        name="in_proj",
    )(x2, g_mix, w_in_bf, gq_t, gk_t)


def _s5_tables(lam_re, lam_im, log_dt, b_re, b_im, c_re, c_im, d_skip):
    hp = lax.Precision.HIGHEST
    L = SSM_CHUNK
    g_n, p_n = lam_re.shape
    dt = jnp.exp(log_dt)[:, None]
    lr, li = lam_re, lam_im
    ls = jnp.arange(L + 1, dtype=F32)[:, None, None]
    mag = jnp.exp(lr * dt * ls)
    pr, pi = mag * jnp.cos(li * dt * ls), mag * jnp.sin(li * dt * ls)
    abar_r, abar_i = pr[1], pi[1]
    den = lr * lr + li * li
    nr, ni = abar_r - 1.0, abar_i
    coef_r = (nr * lr + ni * li) / den
    coef_i = (ni * lr - nr * li) / den
    bbr = coef_r[..., None] * b_re - coef_i[..., None] * b_im
    bbi = coef_r[..., None] * b_im + coef_i[..., None] * b_re
    wr = pr[..., None] * bbr - pi[..., None] * bbi
    wi = pr[..., None] * bbi + pi[..., None] * bbr
    kl = (jnp.einsum('gop,lgpi->lgoi', c_re, wr[:L], precision=hp)
          - jnp.einsum('gop,lgpi->lgoi', c_im, wi[:L], precision=hp))
    kl = kl.at[0].add(jax.vmap(jnp.diag)(d_skip))
    n_lb = g_n // S5_GPB
    eye = jnp.eye(S5_GPB, dtype=F32)
    blk = lambda a: a.reshape(a.shape[0], n_lb, S5_GPB, *a.shape[2:])
    kbd = jnp.einsum('xy,dlxoi->ldxiyo', eye, blk(kl)).reshape(n_lb, L, LANES, LANES)
    top = kbd.transpose(0, 2, 1, 3).reshape(n_lb, LANES, L * LANES)
    bot = jnp.concatenate([jnp.zeros((n_lb, LANES, LANES), F32), top[:, :, :-LANES]], axis=2)
    w0 = jnp.concatenate([top, bot], axis=1)
    bst_r = jnp.einsum('xy,slxpi->lsxiyp', eye, blk(wr[:L][::-1])).reshape(n_lb, L * LANES, S5_GPB * p_n)
    bst_i = jnp.einsum('xy,slxpi->lsxiyp', eye, blk(wi[:L][::-1])).reshape(n_lb, L * LANES, S5_GPB * p_n)
    p1r, p1i = pr[1:], pi[1:]
    cst_r = (c_re[None] * p1r[:, :, None, :] - c_im[None] * p1i[:, :, None, :])
    cst_i = -(c_re[None] * p1i[:, :, None, :] + c_im[None] * p1r[:, :, None, :])
    cst_r = jnp.einsum('xy,tlxop->lxptyo', eye, blk(cst_r)).reshape(n_lb, S5_GPB * p_n, L * LANES)
    cst_i = jnp.einsum('xy,tlxop->lxptyo', eye, blk(cst_i)).reshape(n_lb, S5_GPB * p_n, L * LANES)
    a_l = jnp.stack([pr[L].reshape(n_lb, S5_GPB * p_n), pi[L].reshape(n_lb, S5_GPB * p_n)], axis=1)
    return (w0.astype(BF16), bst_r.astype(BF16), bst_i.astype(BF16),
            cst_r.astype(BF16), cst_i.astype(BF16), a_l)


def _s5_kernel(u_ref, w0_ref, br_ref, bi_ref, cr_ref, ci_ref, a_ref, y_ref, hr_ref, hi_ref, acc_ref):
    n_chunks, L, bsz, _ = u_ref.shape
    rows = n_chunks * bsz

    @pl.when(pl.program_id(1) == 0)
    def _():
        hr_ref[...] = jnp.zeros_like(hr_ref)
        hi_ref[...] = jnp.zeros_like(hi_ref)

    us = [u_ref[:, s].reshape(rows, LANES) for s in range(L)]
    lhs = jnp.concatenate(us, axis=1)
    sin_r = jnp.dot(lhs, br_ref[0], preferred_element_type=F32)
    sin_i = jnp.dot(lhs, bi_ref[0], preferred_element_type=F32)
    ar = a_ref[0, 0:1, :]
    ai = a_ref[0, 1:2, :]
    hr, hi = hr_ref[...], hi_ref[...]
    prev_r, prev_i = [], []
    for c in range(n_chunks):
        prev_r.append(hr)
        prev_i.append(hi)
        sl = slice(c * bsz, (c + 1) * bsz)
        hr, hi = ar * hr - ai * hi + sin_r[sl], ar * hi + ai * hr + sin_i[sl]
    hr_ref[...] = hr
    hi_ref[...] = hi
    pr = jnp.concatenate(prev_r, axis=0).astype(BF16)
    pi = jnp.concatenate(prev_i, axis=0).astype(BF16)
    acc_ref[...] = (jnp.dot(pr, cr_ref[0], preferred_element_type=F32)
                    + jnp.dot(pi, ci_ref[0], preferred_element_type=F32))
    for p in range(L // 2):
        off = 2 * p * LANES
        pair = jnp.concatenate([us[2 * p], us[2 * p + 1]], axis=1)
        acc_ref[:, off:] += jnp.dot(pair, w0_ref[0, :, :L * LANES - off], preferred_element_type=F32)
    for t in range(L):
        y_ref[:, t] = acc_ref[:, t * LANES:(t + 1) * LANES].reshape(n_chunks, bsz, LANES)


def _s5(u4, tables):
    n_chunks, L, bsz, d_ssm = u4.shape
    w0, bst_r, bst_i, cst_r, cst_i, a_l = tables
    cb = S5_CHUNKS_PER_STEP
    data = pl.BlockSpec((cb, L, bsz, LANES), lambda lb, c: (c, 0, 0, lb))
    per_lb = lambda a: pl.BlockSpec((1,) + a.shape[1:], lambda lb, c: (lb,) + (0,) * (a.ndim - 1))
    state = pltpu.VMEM((bsz, a_l.shape[2]), F32)
    return pl.pallas_call(
        _s5_kernel,
        grid=(d_ssm // LANES, n_chunks // cb),
        in_specs=[data, per_lb(w0), per_lb(bst_r), per_lb(bst_i), per_lb(cst_r), per_lb(cst_i), per_lb(a_l)],
        out_specs=data,
        out_shape=jax.ShapeDtypeStruct(u4.shape, F32),
        scratch_shapes=[state, state, pltpu.VMEM((cb * bsz, L * LANES), F32)],
        compiler_params=_cparams("arbitrary", "arbitrary"),
        name="s5",
    )(u4, w0, bst_r, bst_i, cst_r, cst_i, a_l)


def _softplus(z):
    return jnp.maximum(z, 0.0) + jnp.log(1.0 + jnp.exp(-jnp.abs(z)))


def _att_tile(qm, k, v, tri, r_in, causal):
    z = lax.dot_general(qm, k, (((1,), (1,)), ((), ())), preferred_element_type=F32)
    sp = _softplus(z)
    if causal:
        rows = lax.broadcasted_iota(jnp.int32, z.shape, 0)
        cols = lax.broadcasted_iota(jnp.int32, z.shape, 1)
        keep = cols < rows
        sp_m = jnp.where(keep, sp, 0.0)
    else:
        sp_m = sp
    hi = sp_m.astype(BF16)
    lo = (sp_m - hi.astype(F32)).astype(BF16)
    newer = jnp.dot(hi, tri, preferred_element_type=F32) + jnp.dot(lo, tri, preferred_element_type=F32)
    att = jnp.exp(z - sp - newer - r_in)
    if causal:
        att = jnp.where(keep, att, 0.0)
    pv = jnp.dot(att.astype(BF16), v, preferred_element_type=F32)
    return pv, newer[:, 0:1] + sp_m[:, 0:1]


def _attn_kernel(q_ref, k_ref, v_ref, o_ref, *, t_len, tile):
    nq = t_len // tile
    r_i = lax.broadcasted_iota(jnp.int32, (tile, tile), 0)
    c_i = lax.broadcasted_iota(jnp.int32, (tile, tile), 1)
    tri = jnp.where(r_i > c_i, 1.0, 0.0).astype(BF16)
    head0 = _lane_iota((tile, LANES)) < HEAD_DIM
    zero_r = jnp.zeros((tile, 1), F32)

    def q_tile(q0, j_older):
        q = q_ref[pl.ds(q0, tile), :]
        zq = jnp.zeros_like(q)
        qms = (jnp.where(head0, q, zq), jnp.where(head0, zq, q))
        k_d, v_d = k_ref[pl.ds(q0, tile), :], v_ref[pl.ds(q0, tile), :]
        if j_older is not None:
            k_p, v_p = k_ref[pl.ds(q0 - tile, tile), :], v_ref[pl.ds(q0 - tile, tile), :]
        accs, rs = [], []
        for qm in qms:
            acc, r = _att_tile(qm, k_d, v_d, tri, zero_r, causal=True)
            if j_older is not None:
                pv, dr = _att_tile(qm, k_p, v_p, tri, r, causal=False)
                acc, r = acc + pv, r + dr
            accs.append(acc)
            rs.append(r)
        if j_older is not None:
            def cond(c):
                j, _, _, r0, r1 = c
                return jnp.logical_and(j >= 0, jnp.min(jnp.minimum(r0, r1)) < ATT_SKIP)

            def body(c):
                j, a0, a1, r0, r1 = c
                k0 = pl.multiple_of(j * tile, tile)
                k_j, v_j = k_ref[pl.ds(k0, tile), :], v_ref[pl.ds(k0, tile), :]
                pv0, d0 = _att_tile(qms[0], k_j, v_j, tri, r0, causal=False)
                pv1, d1 = _att_tile(qms[1], k_j, v_j, tri, r1, causal=False)
                return j - 1, a0 + pv0, a1 + pv1, r0 + d0, r1 + d1

            _, a0, a1, _, _ = lax.while_loop(cond, body, (j_older, accs[0], accs[1], rs[0], rs[1]))
            accs = [a0, a1]
        o_ref[pl.ds(q0, tile), :] = jnp.where(head0, accs[0], accs[1])

    q_tile(0, None)

    def later(i, _):
        q_tile(pl.multiple_of(i * tile, tile), i - 2)
        return 0

    lax.fori_loop(1, nq, later, 0)


def _attention(q, k, v, bsz, t_len):
    n, d_att = q.shape
    spec = pl.BlockSpec((t_len, LANES), lambda b, p: (b, p))
    return pl.pallas_call(
        functools.partial(_attn_kernel, t_len=t_len, tile=ATT_TILE),
        grid=(bsz, d_att // LANES),
        in_specs=[spec, spec, spec],
        out_specs=spec,
        out_shape=jax.ShapeDtypeStruct((n, d_att), F32),
        compiler_params=_cparams("parallel", "parallel"),
        name="attn",
    )(q, k, v)


def _rms(t, gain):
    return t * lax.rsqrt(jnp.mean(t * t, axis=-1, keepdims=True) + EPS) * gain


def _gelu_tanh(y):
    return 0.5 * y * (1.0 + jnp.tanh(math.sqrt(2.0 / math.pi) * (y + 0.044715 * (y * y * y))))


def _pack_bf16_pairs(a, b):
    ua = pltpu.bitcast(a.astype(BF16).astype(F32), jnp.uint32)
    ub = pltpu.bitcast(b.astype(BF16).astype(F32), jnp.uint32)
    return ua | (ub >> 16)


def _unpack_bf16_pairs(w):
    a = pltpu.bitcast(w & jnp.uint32(0xFFFF0000), F32).astype(BF16)
    b = pltpu.bitcast(w << 16, F32).astype(BF16)
    return a, b


def _post_kernel(x_ref, ys_ref, ya_ref, wglu_ref, bglu_ref, gs_ref, ga_ref, wo_ref, gf_ref,
                 wrh_ref, wrl_ref, br_ref, x1_ref, h2_ref, route_ref, cnt_ref, run_ref, *, d_ssm):
    i = pl.program_id(0)

    @pl.when(i == 0)
    def _():
        run_ref[...] = jnp.zeros_like(run_ref)

    y = _gelu_tanh(ys_ref[...])
    gate = jnp.dot(y.astype(BF16), wglu_ref[...], preferred_element_type=F32) + bglu_ref[...]
    y = y * jax.nn.sigmoid(gate)
    m_s = _rms(y, gs_ref[...]).astype(BF16)
    m_a = _rms(ya_ref[...], ga_ref[...]).astype(BF16)
    mix = (jnp.dot(m_s, wo_ref[:d_ssm, :], preferred_element_type=F32)
           + jnp.dot(m_a, wo_ref[d_ssm:, :], preferred_element_type=F32))
    x1 = x_ref[...] + mix
    x1_ref[...] = x1
    h2 = _rms(x1, gf_ref[...])
    half = h2.shape[1] // 2
    h2_ref[...] = _pack_bf16_pairs(h2[:, :half], h2[:, half:])

    h_hi = h2.astype(BF16)
    h_lo = (h2 - h_hi.astype(F32)).astype(BF16)
    lg = (jnp.dot(h_hi, wrh_ref[...], preferred_element_type=F32)
          + jnp.dot(h_lo, wrh_ref[...], preferred_element_type=F32)
          + jnp.dot(h_hi, wrl_ref[...], preferred_element_type=F32)) + br_ref[...]
    tm = lg.shape[0]
    lane = _lane_iota(lg.shape).astype(F32)
    neg = -jnp.inf
    first = lambda hit: jnp.min(jnp.where(hit, lane, float(LANES)), axis=-1, keepdims=True)
    glog = jnp.where(lane < N_EXPERT_GROUPS, lg, neg)
    gmax = jnp.max(glog, axis=-1, keepdims=True)
    p_grp = 1.0 / jnp.sum(jnp.exp(glog - gmax), axis=-1, keepdims=True)
    grp = first(glog == gmax)
    e0 = ROUTER_LANE0 + grp * EXPERTS_PER_GROUP
    elog = jnp.where(jnp.logical_and(lane >= e0, lane < e0 + EXPERTS_PER_GROUP), lg, neg)
    m1 = jnp.max(elog, axis=-1, keepdims=True)
    i1 = first(elog == m1)
    elog2 = jnp.where(lane == i1, neg, elog)
    m2 = jnp.max(elog2, axis=-1, keepdims=True)
    i2 = first(elog2 == m2)
    e21 = jnp.exp(m2 - m1)
    g1 = p_grp * (1.0 / (1.0 + e21))
    g2 = p_grp * (e21 / (1.0 + e21))

    sel1 = lane == i1
    sel2 = lane == i2
    onehot = jnp.where(jnp.logical_or(sel1, sel2), 1.0, 0.0)
    r_i = lax.broadcasted_iota(jnp.int32, (tm, tm), 0)
    c_i = lax.broadcasted_iota(jnp.int32, (tm, tm), 1)
    lower = jnp.where(c_i < r_i, 1.0, 0.0).astype(BF16)
    before = jnp.dot(lower, onehot.astype(BF16), preferred_element_type=F32) + run_ref[0:1, :]
    rank1 = jnp.sum(jnp.where(sel1, before, 0.0), axis=-1, keepdims=True)
    rank2 = jnp.sum(jnp.where(sel2, before, 0.0), axis=-1, keepdims=True)
    run_ref[0:1, :] = run_ref[0:1, :] + jnp.sum(onehot, axis=0, keepdims=True)
    cnt_ref[...] = run_ref[...]

    fields = (i1 - ROUTER_LANE0, i2 - ROUTER_LANE0, g1, g2, rank1, rank2)
    route = jnp.zeros(lg.shape, F32)
    for pos, val in enumerate(fields):
        route = jnp.where(lane == pos, val, route)
    route_ref[...] = route


def _post(x2, y_ssm_t, y_att, w_glu_bf, b_glu, g_ssm, g_att, w_out_bf, g_ffn, wr_hi, wr_lo, b_r):
    n, d = x2.shape
    d_ssm = w_glu_bf.shape[0]
    tm = POST_TM
    nt = y_ssm_t.shape[0] // tm
    row = lambda w: pl.BlockSpec((tm, w), lambda i: (i, 0))
    ssm_spec = pl.BlockSpec((tm, d_ssm), lambda i: (i % nt, i // nt))
    full = lambda a: pl.BlockSpec(a.shape, lambda i: (0,) * a.ndim)
    cnt_spec = pl.BlockSpec((8, LANES), lambda i: (0, 0))
    return pl.pallas_call(
        functools.partial(_post_kernel, d_ssm=d_ssm),
        grid=(n // tm,),
        in_specs=[row(d), ssm_spec, row(y_att.shape[1]), full(w_glu_bf), full(b_glu), full(g_ssm),
                  full(g_att), full(w_out_bf), full(g_ffn), full(wr_hi), full(wr_lo), full(b_r)],
        out_specs=[row(d), row(d // 2), row(LANES), cnt_spec],
        out_shape=[jax.ShapeDtypeStruct((n, d), F32), jax.ShapeDtypeStruct((n, d // 2), jnp.uint32),
                   jax.ShapeDtypeStruct((n, LANES), F32), jax.ShapeDtypeStruct((8, LANES), F32)],
        scratch_shapes=[pltpu.VMEM((8, LANES), F32)],
        compiler_params=_cparams("arbitrary"),
        name="post",
    )(x2, y_ssm_t, y_att, w_glu_bf, b_glu, g_ssm, g_att, w_out_bf, g_ffn, wr_hi, wr_lo, b_r)


def _row_copy(src_ref, src_row, dst_ref, dst_row, sem):
    return pltpu.make_async_copy(src_ref.at[pl.ds(src_row, 1), :], dst_ref.at[pl.ds(dst_row, 1), :], sem)


def _dispatch_kernel(dest_ref, h_ref, buf_in_ref, buf_ref, sem, *, tm):
    del buf_in_ref

    def issue(r, _):
        _row_copy(h_ref, r, buf_ref, dest_ref[0, 0, 2 * r], sem).start()
        _row_copy(h_ref, r, buf_ref, dest_ref[0, 0, 2 * r + 1], sem).start()
        return 0

    lax.fori_loop(0, tm, issue, 0)
    for _ in range(2):
        pltpu.make_async_copy(h_ref, buf_ref.at[pl.ds(0, tm), :], sem).wait()


def _dispatch(dest, h2p, n_rows):
    n, w = h2p.shape
    tm = DISPATCH_TM
    dest3 = dest.reshape(n // tm, 1, 2 * tm)
    buf0 = jnp.zeros((n_rows, w), h2p.dtype)
    return pl.pallas_call(
        functools.partial(_dispatch_kernel, tm=tm),
        grid=(n // tm,),
        in_specs=[pl.BlockSpec((1, 1, 2 * tm), lambda i: (i, 0, 0), memory_space=pltpu.SMEM),
                  pl.BlockSpec((tm, w), lambda i: (i, 0)),
                  pl.BlockSpec(memory_space=pl.ANY)],
        out_specs=pl.BlockSpec(memory_space=pl.ANY),
        out_shape=jax.ShapeDtypeStruct((n_rows, w), h2p.dtype),
        scratch_shapes=[pltpu.SemaphoreType.DMA(())],
        input_output_aliases={2: 0},
        compiler_params=_cparams("arbitrary"),
        name="dispatch",
    )(dest3, h2p, buf0)


def _experts_kernel(blk_e_ref, used_ref, x_ref, wg_ref, wu_ref, wd_ref, o_ref, wg_bf, wu_bf, wd_bf):
    i = pl.program_id(0)

    @pl.when(jnp.logical_or(i == 0, blk_e_ref[i] != blk_e_ref[jnp.maximum(i - 1, 0)]))
    def _():
        wg_bf[...] = wg_ref[0].astype(BF16)
        wu_bf[...] = wu_ref[0].astype(BF16)
        wd_bf[...] = wd_ref[0].astype(BF16)

    @pl.when(i < used_ref[0])
    def _():
        xa, xb = _unpack_bf16_pairs(x_ref[...])
        half = xa.shape[1]
        gate = (jnp.dot(xa, wg_bf[:half, :], preferred_element_type=F32)
                + jnp.dot(xb, wg_bf[half:, :], preferred_element_type=F32))
        up = (jnp.dot(xa, wu_bf[:half, :], preferred_element_type=F32)
              + jnp.dot(xb, wu_bf[half:, :], preferred_element_type=F32))
        hid = (jax.nn.silu(gate) * up).astype(BF16)
        o_ref[...] = jnp.dot(hid, wd_bf[...], preferred_element_type=F32)

    @pl.when(i >= used_ref[0])
    def _():
        o_ref[...] = jnp.zeros_like(o_ref)


def _experts(blk_e, n_used, buf, w_gate, w_up, w_down):
    n_rows, w = buf.shape
    d = w_down.shape[2]
    wspec = lambda a: pl.BlockSpec((1,) + a.shape[1:], lambda i, be, nu: (be[i], 0, 0))
    grid_spec = pltpu.PrefetchScalarGridSpec(
        num_scalar_prefetch=2,
        grid=(n_rows // MOE_ROWS,),
        in_specs=[pl.BlockSpec((MOE_ROWS, w), lambda i, be, nu: (i, 0)), wspec(w_gate), wspec(w_up), wspec(w_down)],
        out_specs=pl.BlockSpec((MOE_ROWS, d), lambda i, be, nu: (i, 0)),
        scratch_shapes=[pltpu.VMEM(w_gate.shape[1:], BF16), pltpu.VMEM(w_up.shape[1:], BF16),
                        pltpu.VMEM(w_down.shape[1:], BF16)],
    )
    return pl.pallas_call(
        _experts_kernel,
        grid_spec=grid_spec,
        out_shape=jax.ShapeDtypeStruct((n_rows, d), F32),
        compiler_params=_cparams("arbitrary"),
        name="experts",
    )(blk_e, n_used, buf, w_gate, w_up, w_down)


def _combine_kernel(dest_ref, x1_ref, route_ref, eo_ref, o_ref, rows_ref, sem, *, tm):
    def issue(r, _):
        _row_copy(eo_ref, dest_ref[0, 0, 2 * r], rows_ref.at[0], r, sem).start()
        _row_copy(eo_ref, dest_ref[0, 0, 2 * r + 1], rows_ref.at[1], r, sem).start()
        return 0

    lax.fori_loop(0, tm, issue, 0)
    for s in range(2):
        pltpu.make_async_copy(eo_ref.at[pl.ds(0, tm), :], rows_ref.at[s], sem).wait()
    route = route_ref[...]
    o_ref[...] = x1_ref[...] + (route[:, 2:3] * rows_ref[0] + route[:, 3:4] * rows_ref[1])


def _combine(dest, x1, route, eo):
    n, d = x1.shape
    tm = COMBINE_TM
    dest3 = dest.reshape(n // tm, 1, 2 * tm)
    return pl.pallas_call(
        functools.partial(_combine_kernel, tm=tm),
        grid=(n // tm,),
        in_specs=[pl.BlockSpec((1, 1, 2 * tm), lambda i: (i, 0, 0), memory_space=pltpu.SMEM),
                  pl.BlockSpec((tm, d), lambda i: (i, 0)),
                  pl.BlockSpec((tm, LANES), lambda i: (i, 0)),
                  pl.BlockSpec(memory_space=pl.ANY)],
        out_specs=pl.BlockSpec((tm, d), lambda i: (i, 0)),
        out_shape=jax.ShapeDtypeStruct((n, d), F32),
        scratch_shapes=[pltpu.VMEM((2, tm, d), F32), pltpu.SemaphoreType.DMA(())],
        compiler_params=_cparams("arbitrary"),
        name="combine",
    )(dest3, x1, route, eo)


def _layer(x, g_mix, w_in, lam_re, lam_im, log_dt, b_re, b_im, c_re, c_im, d_skip, w_glu, b_glu, g_q, g_k,
           g_ssm_out, g_attn_out, w_out, g_ffn, w_rg, b_rg, w_re, b_re_router, w_gate, w_up, w_down):
    bsz, t_len, d = x.shape
    n = bsz * t_len
    d_ssm = w_glu.shape[0]
    d_att = g_attn_out.shape[0]
    n_heads = d_att // HEAD_DIM
    n_chunks = t_len // SSM_CHUNK
    x2 = x.reshape(n, d)

    u_t, q, k, v = _in_proj(x2, g_mix[None], w_in.astype(BF16), jnp.tile(g_q, n_heads)[None],
                            jnp.tile(g_k, n_heads)[None], d_ssm, d_att, bsz, t_len)
    tables = _s5_tables(lam_re, lam_im, log_dt, b_re, b_im, c_re, c_im, d_skip)
    y_ssm_t = _s5(u_t.reshape(n_chunks, SSM_CHUNK, bsz, d_ssm), tables).reshape(t_len, bsz * d_ssm)

    y_att = _attention(q, k, v, bsz, t_len)

    w_r = jnp.concatenate([w_rg, w_re.reshape(d, N_EXPERTS)], axis=1)
    w_r = jnp.pad(w_r, ((0, 0), (0, LANES - w_r.shape[1])))
    wr_hi = w_r.astype(BF16)
    wr_lo = (w_r - wr_hi.astype(F32)).astype(BF16)
    b_r = jnp.pad(jnp.concatenate([b_rg, b_re_router.reshape(N_EXPERTS)]), (0, LANES - ROUTER_LANE0 - N_EXPERTS))[None]
    x1, h2p, route, cnt = _post(x2, y_ssm_t, y_att, w_glu.astype(BF16), b_glu[None], g_ssm_out[None],
                                g_attn_out[None], w_out.astype(BF16), g_ffn[None], wr_hi, wr_lo, b_r)

    counts = cnt[0, ROUTER_LANE0:ROUTER_LANE0 + N_EXPERTS].astype(jnp.int32)
    pcounts = ((counts + MOE_ROWS - 1) // MOE_ROWS) * MOE_ROWS
    pends = jnp.cumsum(pcounts)
    pstarts = pends - pcounts
    expert = route[:, 0:2].astype(jnp.int32)
    e_ids = jnp.arange(N_EXPERTS, dtype=jnp.int32)
    start = jnp.sum(jnp.where(expert[..., None] == e_ids, pstarts, 0), axis=-1)
    dest = start + route[:, 4:6].astype(jnp.int32)
    n_rows = n * 2 + N_EXPERTS * MOE_ROWS
    n_blk = n_rows // MOE_ROWS
    blk_row0 = jnp.arange(n_blk, dtype=jnp.int32)[:, None] * MOE_ROWS
    blk_e = jnp.minimum(jnp.sum((pends[None, :] <= blk_row0).astype(jnp.int32), axis=1), N_EXPERTS - 1)
    n_used = (pends[-1:] // MOE_ROWS).astype(jnp.int32)

    buf = _dispatch(dest, h2p, n_rows)
    eo = _experts(blk_e, n_used, buf, w_gate, w_up, w_down)
    out = _combine(dest, x1, route, eo)
    return out.reshape(bsz, t_len, d)


def kernel(x, g_mix, w_in, ssm_lambda_re, ssm_lambda_im, ssm_log_dt, ssm_b_re, ssm_b_im, ssm_c_re, ssm_c_im, ssm_d, ssm_w_glu, ssm_b_glu, g_q, g_k, g_ssm_out, g_attn_out, w_out, g_ffn, w_router_group, b_router_group, w_router_expert, b_router_expert, w_gate, w_up, w_down):
    for l in range(g_mix.shape[0]):
        x = _layer(x, g_mix[l], w_in[l], ssm_lambda_re[l], ssm_lambda_im[l], ssm_log_dt[l], ssm_b_re[l],
                   ssm_b_im[l], ssm_c_re[l], ssm_c_im[l], ssm_d[l], ssm_w_glu[l], ssm_b_glu[l], g_q[l], g_k[l],
                   g_ssm_out[l], g_attn_out[l], w_out[l], g_ffn[l], w_router_group[l], b_router_group[l],
                   w_router_expert[l], b_router_expert[l], w_gate[l], w_up[l], w_down[l])
    return x
```

```python
import functools
import math

import jax
import jax.numpy as jnp
from jax import lax
from jax.experimental import pallas as pl
from jax.experimental.pallas import tpu as pltpu

F32 = jnp.float32
BF16 = jnp.bfloat16
EPS = 1e-6

LANES = 128
VMEM_LIMIT_BYTES = 56 * 1024 * 1024

SSM_GROUP = 16
SSM_STATE = 64
SSM_CHUNK = 16
HEAD_DIM = 64
N_EXPERT_GROUPS = 4
EXPERTS_PER_GROUP = 8
N_EXPERTS = N_EXPERT_GROUPS * EXPERTS_PER_GROUP
ROUTER_LANE0 = N_EXPERT_GROUPS
MOE_ROWS = 512
ATT_SKIP = 110.0

S5_GPB = LANES // SSM_GROUP
S5_CHUNKS_PER_STEP = 8

IN_TM = 512
ATT_TILE = 256
POST_TM = 512
DISPATCH_TM = 512
COMBINE_TM = 256
ROW_DMA_UNROLL = 8


def _cparams(*sem):
    return pltpu.CompilerParams(dimension_semantics=sem, vmem_limit_bytes=VMEM_LIMIT_BYTES)


def _lane_iota(shape):
    return lax.broadcasted_iota(jnp.int32, shape, len(shape) - 1)


def _head_rms(t, gain):
    outs = []
    for c in range(t.shape[1] // LANES):
        blk = t[:, c * LANES:(c + 1) * LANES]
        sq = blk * blk
        lo = _lane_iota(blk.shape) < HEAD_DIM
        s_lo = jnp.sum(jnp.where(lo, sq, 0.0), axis=-1, keepdims=True)
        s_hi = jnp.sum(jnp.where(lo, 0.0, sq), axis=-1, keepdims=True)
        inv = jnp.where(lo, lax.rsqrt(s_lo * (1.0 / HEAD_DIM) + EPS),
                        lax.rsqrt(s_hi * (1.0 / HEAD_DIM) + EPS))
        outs.append(blk * inv * gain[:, c * LANES:(c + 1) * LANES])
    return jnp.concatenate(outs, axis=-1)


def _in_proj_kernel(x_ref, g_ref, w_ref, gq_ref, gk_ref, u_ref, q_ref, k_ref, v_ref, *, d_ssm, d_att, scale):
    x = x_ref[...]
    inv = lax.rsqrt(jnp.mean(x * x, axis=-1, keepdims=True) + EPS)
    h = (x * inv * g_ref[...]).astype(BF16)
    proj = jnp.dot(h, w_ref[...], preferred_element_type=F32)
    u_ref[...] = proj[:, :d_ssm].astype(BF16)
    q = _head_rms(proj[:, d_ssm:d_ssm + d_att], gq_ref[...])
    k = _head_rms(proj[:, d_ssm + d_att:d_ssm + 2 * d_att], gk_ref[...])
    q_ref[...] = (q * scale).astype(BF16)
    k_ref[...] = k.astype(BF16)
    v_ref[...] = proj[:, d_ssm + 2 * d_att:].astype(BF16)


def _in_proj(x2, g_mix, w_in_bf, gq_t, gk_t, d_ssm, d_att, bsz, t_len):
    n, d = x2.shape
    tm = IN_TM
    nt = t_len // tm
    row = lambda w: pl.BlockSpec((tm, w), lambda b, t: (b * nt + t, 0))
    full = lambda a: pl.BlockSpec(a.shape, lambda b, t: (0,) * a.ndim)
    out_sd = jax.ShapeDtypeStruct((n, d_att), BF16)
    return pl.pallas_call(
        functools.partial(_in_proj_kernel, d_ssm=d_ssm, d_att=d_att, scale=1.0 / math.sqrt(HEAD_DIM)),
        grid=(bsz, nt),
        in_specs=[row(d), full(g_mix), full(w_in_bf), full(gq_t), full(gk_t)],
        out_specs=[pl.BlockSpec((tm, d_ssm), lambda b, t: (t, b)), row(d_att), row(d_att), row(d_att)],
        out_shape=[jax.ShapeDtypeStruct((t_len, bsz * d_ssm), BF16), out_sd, out_sd, out_sd],
        compiler_params=_cparams("parallel", "parallel"),
        name="in_proj",
    )(x2, g_mix, w_in_bf, gq_t, gk_t)


def _s5_tables(lam_re, lam_im, log_dt, b_re, b_im, c_re, c_im, d_skip):
    hp = lax.Precision.HIGHEST
    L = SSM_CHUNK
    g_n, p_n = lam_re.shape
    dt = jnp.exp(log_dt)[:, None]
    lr, li = lam_re, lam_im
    ls = jnp.arange(L + 1, dtype=F32)[:, None, None]
    mag = jnp.exp(lr * dt * ls)
    pr, pi = mag * jnp.cos(li * dt * ls), mag * jnp.sin(li * dt * ls)
    abar_r, abar_i = pr[1], pi[1]
    den = lr * lr + li * li
    nr, ni = abar_r - 1.0, abar_i
    coef_r = (nr * lr + ni * li) / den
    coef_i = (ni * lr - nr * li) / den
    bbr = coef_r[..., None] * b_re - coef_i[..., None] * b_im
    bbi = coef_r[..., None] * b_im + coef_i[..., None] * b_re
    wr = pr[..., None] * bbr - pi[..., None] * bbi
    wi = pr[..., None] * bbi + pi[..., None] * bbr
    kl = (jnp.einsum('gop,lgpi->lgoi', c_re, wr[:L], precision=hp)
          - jnp.einsum('gop,lgpi->lgoi', c_im, wi[:L], precision=hp))
    kl = kl.at[0].add(jax.vmap(jnp.diag)(d_skip))
    n_lb = g_n // S5_GPB
    eye = jnp.eye(S5_GPB, dtype=F32)
    blk = lambda a: a.reshape(a.shape[0], n_lb, S5_GPB, *a.shape[2:])
    kbd = jnp.einsum('xy,dlxoi->ldxiyo', eye, blk(kl)).reshape(n_lb, L, LANES, LANES)
    top = kbd.transpose(0, 2, 1, 3).reshape(n_lb, LANES, L * LANES)
    bot = jnp.concatenate([jnp.zeros((n_lb, LANES, LANES), F32), top[:, :, :-LANES]], axis=2)
    w0 = jnp.concatenate([top, bot], axis=1)
    bst_r = jnp.einsum('xy,slxpi->lsxiyp', eye, blk(wr[:L][::-1])).reshape(n_lb, L * LANES, S5_GPB * p_n)
    bst_i = jnp.einsum('xy,slxpi->lsxiyp', eye, blk(wi[:L][::-1])).reshape(n_lb, L * LANES, S5_GPB * p_n)
    p1r, p1i = pr[1:], pi[1:]
    cst_r = (c_re[None] * p1r[:, :, None, :] - c_im[None] * p1i[:, :, None, :])
    cst_i = -(c_re[None] * p1i[:, :, None, :] + c_im[None] * p1r[:, :, None, :])
    cst_r = jnp.einsum('xy,tlxop->lxptyo', eye, blk(cst_r)).reshape(n_lb, S5_GPB * p_n, L * LANES)
    cst_i = jnp.einsum('xy,tlxop->lxptyo', eye, blk(cst_i)).reshape(n_lb, S5_GPB * p_n, L * LANES)
    a_l = jnp.stack([pr[L].reshape(n_lb, S5_GPB * p_n), pi[L].reshape(n_lb, S5_GPB * p_n)], axis=1)
    return (w0.astype(BF16), bst_r.astype(BF16), bst_i.astype(BF16),
            cst_r.astype(BF16), cst_i.astype(BF16), a_l)


def _s5_kernel(u_ref, w0_ref, br_ref, bi_ref, cr_ref, ci_ref, a_ref, y_ref, hr_ref, hi_ref, acc_ref):
    n_chunks, L, bsz, _ = u_ref.shape
    rows = n_chunks * bsz

    @pl.when(pl.program_id(1) == 0)
    def _():
        hr_ref[...] = jnp.zeros_like(hr_ref)
        hi_ref[...] = jnp.zeros_like(hi_ref)

    us = [u_ref[:, s].reshape(rows, LANES) for s in range(L)]
    lhs = jnp.concatenate(us, axis=1)
    sin_r = jnp.dot(lhs, br_ref[0], preferred_element_type=F32)
    sin_i = jnp.dot(lhs, bi_ref[0], preferred_element_type=F32)
    ar = a_ref[0, 0:1, :]
    ai = a_ref[0, 1:2, :]
    hr, hi = hr_ref[...], hi_ref[...]
    prev_r, prev_i = [], []
    for c in range(n_chunks):
        prev_r.append(hr)
        prev_i.append(hi)
        sl = slice(c * bsz, (c + 1) * bsz)
        hr, hi = ar * hr - ai * hi + sin_r[sl], ar * hi + ai * hr + sin_i[sl]
    hr_ref[...] = hr
    hi_ref[...] = hi
    pr = jnp.concatenate(prev_r, axis=0).astype(BF16)
    pi = jnp.concatenate(prev_i, axis=0).astype(BF16)
    acc_ref[...] = (jnp.dot(pr, cr_ref[0], preferred_element_type=F32)
                    + jnp.dot(pi, ci_ref[0], preferred_element_type=F32))
    for p in range(L // 2):
        off = 2 * p * LANES
        pair = jnp.concatenate([us[2 * p], us[2 * p + 1]], axis=1)
        acc_ref[:, off:] += jnp.dot(pair, w0_ref[0, :, :L * LANES - off], preferred_element_type=F32)
    for t in range(L):
        y_ref[:, t] = acc_ref[:, t * LANES:(t + 1) * LANES].reshape(n_chunks, bsz, LANES)


def _s5(u4, tables):
    n_chunks, L, bsz, d_ssm = u4.shape
    w0, bst_r, bst_i, cst_r, cst_i, a_l = tables
    cb = S5_CHUNKS_PER_STEP
    data = pl.BlockSpec((cb, L, bsz, LANES), lambda lb, c: (c, 0, 0, lb))
    per_lb = lambda a: pl.BlockSpec((1,) + a.shape[1:], lambda lb, c: (lb,) + (0,) * (a.ndim - 1))
    state = pltpu.VMEM((bsz, a_l.shape[2]), F32)
    return pl.pallas_call(
        _s5_kernel,
        grid=(d_ssm // LANES, n_chunks // cb),
        in_specs=[data, per_lb(w0), per_lb(bst_r), per_lb(bst_i), per_lb(cst_r), per_lb(cst_i), per_lb(a_l)],
        out_specs=data,
        out_shape=jax.ShapeDtypeStruct(u4.shape, F32),
        scratch_shapes=[state, state, pltpu.VMEM((cb * bsz, L * LANES), F32)],
        compiler_params=_cparams("arbitrary", "arbitrary"),
        name="s5",
    )(u4, w0, bst_r, bst_i, cst_r, cst_i, a_l)


def _softplus(z):
    return jnp.maximum(z, 0.0) + jnp.log(1.0 + jnp.exp(-jnp.abs(z)))


def _att_tile(qm, k, v, tri, r_in, causal):
    z = lax.dot_general(qm, k, (((1,), (1,)), ((), ())), preferred_element_type=F32)
    sp = _softplus(z)
    if causal:
        rows = lax.broadcasted_iota(jnp.int32, z.shape, 0)
        cols = lax.broadcasted_iota(jnp.int32, z.shape, 1)
        keep = cols < rows
        sp_m = jnp.where(keep, sp, 0.0)
    else:
        sp_m = sp
    newer = jnp.dot(sp_m.astype(BF16), tri, preferred_element_type=F32)
    att = jnp.exp(z - sp - newer - r_in)
    if causal:
        att = jnp.where(keep, att, 0.0)
    pv = jnp.dot(att.astype(BF16), v, preferred_element_type=F32)
    return pv, newer[:, 0:1] + sp_m[:, 0:1]


def _attn_kernel(q_ref, k_ref, v_ref, o_ref, *, t_len, tile):
    nq = t_len // tile
    r_i = lax.broadcasted_iota(jnp.int32, (tile, tile), 0)
    c_i = lax.broadcasted_iota(jnp.int32, (tile, tile), 1)
    tri = jnp.where(r_i > c_i, 1.0, 0.0).astype(BF16)
    head0 = _lane_iota((tile, LANES)) < HEAD_DIM
    zero_r = jnp.zeros((tile, 1), F32)

    def q_tile(q0, j_older):
        q = q_ref[pl.ds(q0, tile), :]
        zq = jnp.zeros_like(q)
        qms = (jnp.where(head0, q, zq), jnp.where(head0, zq, q))
        k_d, v_d = k_ref[pl.ds(q0, tile), :], v_ref[pl.ds(q0, tile), :]
        if j_older is not None:
            k_p, v_p = k_ref[pl.ds(q0 - tile, tile), :], v_ref[pl.ds(q0 - tile, tile), :]
        accs, rs = [], []
        for qm in qms:
            acc, r = _att_tile(qm, k_d, v_d, tri, zero_r, causal=True)
            if j_older is not None:
                pv, dr = _att_tile(qm, k_p, v_p, tri, r, causal=False)
                acc, r = acc + pv, r + dr
            accs.append(acc)
            rs.append(r)
        if j_older is not None:
            def cond(c):
                j, _, _, r0, r1 = c
                return jnp.logical_and(j >= 0, jnp.min(jnp.minimum(r0, r1)) < ATT_SKIP)

            def body(c):
                j, a0, a1, r0, r1 = c
                k0 = pl.multiple_of(j * tile, tile)
                k_j, v_j = k_ref[pl.ds(k0, tile), :], v_ref[pl.ds(k0, tile), :]
                pv0, d0 = _att_tile(qms[0], k_j, v_j, tri, r0, causal=False)
                pv1, d1 = _att_tile(qms[1], k_j, v_j, tri, r1, causal=False)
                return j - 1, a0 + pv0, a1 + pv1, r0 + d0, r1 + d1

            _, a0, a1, _, _ = lax.while_loop(cond, body, (j_older, accs[0], accs[1], rs[0], rs[1]))
            accs = [a0, a1]
        o_ref[pl.ds(q0, tile), :] = jnp.where(head0, accs[0], accs[1])

    q_tile(0, None)

    def later(i, _):
        q_tile(pl.multiple_of(i * tile, tile), i - 2)
        return 0

    lax.fori_loop(1, nq, later, 0)


def _attention(q, k, v, bsz, t_len):
    n, d_att = q.shape
    spec = pl.BlockSpec((t_len, LANES), lambda b, p: (b, p))
    return pl.pallas_call(
        functools.partial(_attn_kernel, t_len=t_len, tile=ATT_TILE),
        grid=(bsz, d_att // LANES),
        in_specs=[spec, spec, spec],
        out_specs=spec,
        out_shape=jax.ShapeDtypeStruct((n, d_att), F32),
        compiler_params=_cparams("parallel", "parallel"),
        name="attn",
    )(q, k, v)


def _rms(t, gain):
    return t * lax.rsqrt(jnp.mean(t * t, axis=-1, keepdims=True) + EPS) * gain


def _gelu_tanh(y):
    return 0.5 * y * (1.0 + jnp.tanh(math.sqrt(2.0 / math.pi) * (y + 0.044715 * (y * y * y))))


def _pack_bf16_pairs(a, b):
    ua = pltpu.bitcast(a.astype(BF16).astype(F32), jnp.uint32)
    ub = pltpu.bitcast(b.astype(BF16).astype(F32), jnp.uint32)
    return ua | (ub >> 16)


def _unpack_bf16_pairs(w):
    a = pltpu.bitcast(w & jnp.uint32(0xFFFF0000), F32).astype(BF16)
    b = pltpu.bitcast(w << 16, F32).astype(BF16)
    return a, b


def _post_kernel(x_ref, ys_ref, ya_ref, wglu_ref, bglu_ref, gs_ref, ga_ref, wo_ref, gf_ref,
                 wr_ref, br_ref, x1_ref, h2_ref, route_ref, cnt_ref, run_ref, *, d_ssm):
    i = pl.program_id(0)

    @pl.when(i == 0)
    def _():
        run_ref[...] = jnp.zeros_like(run_ref)

    y = _gelu_tanh(ys_ref[...])
    gate = jnp.dot(y.astype(BF16), wglu_ref[...], preferred_element_type=F32) + bglu_ref[...]
    y = y * jax.nn.sigmoid(gate)
    m_s = _rms(y, gs_ref[...]).astype(BF16)
    m_a = _rms(ya_ref[...], ga_ref[...]).astype(BF16)
    mix = (jnp.dot(m_s, wo_ref[:d_ssm, :], preferred_element_type=F32)
           + jnp.dot(m_a, wo_ref[d_ssm:, :], preferred_element_type=F32))
    x1 = x_ref[...] + mix
    x1_ref[...] = x1
    h2 = _rms(x1, gf_ref[...])
    half = h2.shape[1] // 2
    h2_ref[...] = _pack_bf16_pairs(h2[:, :half], h2[:, half:])

    lg = jnp.dot(h2.astype(BF16), wr_ref[...], preferred_element_type=F32) + br_ref[...]
    tm = lg.shape[0]
    lane = _lane_iota(lg.shape).astype(F32)
    neg = -jnp.inf
    first = lambda hit: jnp.min(jnp.where(hit, lane, float(LANES)), axis=-1, keepdims=True)
    glog = jnp.where(lane < N_EXPERT_GROUPS, lg, neg)
    gmax = jnp.max(glog, axis=-1, keepdims=True)
    p_grp = 1.0 / jnp.sum(jnp.exp(glog - gmax), axis=-1, keepdims=True)
    grp = first(glog == gmax)
    e0 = ROUTER_LANE0 + grp * EXPERTS_PER_GROUP
    elog = jnp.where(jnp.logical_and(lane >= e0, lane < e0 + EXPERTS_PER_GROUP), lg, neg)
    m1 = jnp.max(elog, axis=-1, keepdims=True)
    i1 = first(elog == m1)
    elog2 = jnp.where(lane == i1, neg, elog)
    m2 = jnp.max(elog2, axis=-1, keepdims=True)
    i2 = first(elog2 == m2)
    e21 = jnp.exp(m2 - m1)
    g1 = p_grp * (1.0 / (1.0 + e21))
    g2 = p_grp * (e21 / (1.0 + e21))

    sel1 = lane == i1
    sel2 = lane == i2
    onehot = jnp.where(jnp.logical_or(sel1, sel2), 1.0, 0.0)
    r_i = lax.broadcasted_iota(jnp.int32, (tm, tm), 0)
    c_i = lax.broadcasted_iota(jnp.int32, (tm, tm), 1)
    lower = jnp.where(c_i < r_i, 1.0, 0.0).astype(BF16)
    before = jnp.dot(lower, onehot.astype(BF16), preferred_element_type=F32) + run_ref[0:1, :]
    rank1 = jnp.sum(jnp.where(sel1, before, 0.0), axis=-1, keepdims=True)
    rank2 = jnp.sum(jnp.where(sel2, before, 0.0), axis=-1, keepdims=True)
    run_ref[0:1, :] = run_ref[0:1, :] + jnp.sum(onehot, axis=0, keepdims=True)
    cnt_ref[...] = run_ref[...]

    fields = (i1 - ROUTER_LANE0, i2 - ROUTER_LANE0, g1, g2, rank1, rank2)
    route = jnp.zeros(lg.shape, F32)
    for pos, val in enumerate(fields):
        route = jnp.where(lane == pos, val, route)
    route_ref[...] = route


def _post(x2, y_ssm_t, y_att, w_glu_bf, b_glu, g_ssm, g_att, w_out_bf, g_ffn, w_r_bf, b_r):
    n, d = x2.shape
    d_ssm = w_glu_bf.shape[0]
    tm = POST_TM
    nt = y_ssm_t.shape[0] // tm
    row = lambda w: pl.BlockSpec((tm, w), lambda i: (i, 0))
    ssm_spec = pl.BlockSpec((tm, d_ssm), lambda i: (i % nt, i // nt))
    full = lambda a: pl.BlockSpec(a.shape, lambda i: (0,) * a.ndim)
    cnt_spec = pl.BlockSpec((8, LANES), lambda i: (0, 0))
    return pl.pallas_call(
        functools.partial(_post_kernel, d_ssm=d_ssm),
        grid=(n // tm,),
        in_specs=[row(d), ssm_spec, row(y_att.shape[1]), full(w_glu_bf), full(b_glu), full(g_ssm),
                  full(g_att), full(w_out_bf), full(g_ffn), full(w_r_bf), full(b_r)],
        out_specs=[row(d), row(d // 2), row(LANES), cnt_spec],
        out_shape=[jax.ShapeDtypeStruct((n, d), F32), jax.ShapeDtypeStruct((n, d // 2), jnp.uint32),
                   jax.ShapeDtypeStruct((n, LANES), F32), jax.ShapeDtypeStruct((8, LANES), F32)],
        scratch_shapes=[pltpu.VMEM((8, LANES), F32)],
        compiler_params=_cparams("arbitrary"),
        name="post",
    )(x2, y_ssm_t, y_att, w_glu_bf, b_glu, g_ssm, g_att, w_out_bf, g_ffn, w_r_bf, b_r)


def _row_copy(src_ref, src_row, dst_ref, dst_row, sem):
    return pltpu.make_async_copy(src_ref.at[pl.ds(src_row, 1), :], dst_ref.at[pl.ds(dst_row, 1), :], sem)


def _dispatch_kernel(dest_ref, h_ref, buf_in_ref, buf_ref, sem, *, tm):
    del buf_in_ref

    def issue(g, _):
        for j in range(ROW_DMA_UNROLL):
            r = g * ROW_DMA_UNROLL + j
            _row_copy(h_ref, r, buf_ref, dest_ref[0, 0, 2 * r], sem).start()
            _row_copy(h_ref, r, buf_ref, dest_ref[0, 0, 2 * r + 1], sem).start()
        return 0

    lax.fori_loop(0, tm // ROW_DMA_UNROLL, issue, 0)
    for _ in range(2):
        pltpu.make_async_copy(h_ref, buf_ref.at[pl.ds(0, tm), :], sem).wait()


def _dispatch(dest, h2p, n_rows):
    n, w = h2p.shape
    tm = DISPATCH_TM
    dest3 = dest.reshape(n // tm, 1, 2 * tm)
    buf0 = jnp.zeros((n_rows, w), h2p.dtype)
    return pl.pallas_call(
        functools.partial(_dispatch_kernel, tm=tm),
        grid=(n // tm,),
        in_specs=[pl.BlockSpec((1, 1, 2 * tm), lambda i: (i, 0, 0), memory_space=pltpu.SMEM),
                  pl.BlockSpec((tm, w), lambda i: (i, 0)),
                  pl.BlockSpec(memory_space=pl.ANY)],
        out_specs=pl.BlockSpec(memory_space=pl.ANY),
        out_shape=jax.ShapeDtypeStruct((n_rows, w), h2p.dtype),
        scratch_shapes=[pltpu.SemaphoreType.DMA(())],
        input_output_aliases={2: 0},
        compiler_params=_cparams("arbitrary"),
        name="dispatch",
    )(dest3, h2p, buf0)


def _experts_kernel(blk_e_ref, used_ref, x_ref, wg_ref, wu_ref, wd_ref, o_ref, wg_bf, wu_bf, wd_bf):
    i = pl.program_id(0)

    @pl.when(jnp.logical_or(i == 0, blk_e_ref[i] != blk_e_ref[jnp.maximum(i - 1, 0)]))
    def _():
        wg_bf[...] = wg_ref[0].astype(BF16)
        wu_bf[...] = wu_ref[0].astype(BF16)
        wd_bf[...] = wd_ref[0].astype(BF16)

    @pl.when(i < used_ref[0])
    def _():
        xa, xb = _unpack_bf16_pairs(x_ref[...])
        half = xa.shape[1]
        gate = (jnp.dot(xa, wg_bf[:half, :], preferred_element_type=F32)
                + jnp.dot(xb, wg_bf[half:, :], preferred_element_type=F32))
        up = (jnp.dot(xa, wu_bf[:half, :], preferred_element_type=F32)
              + jnp.dot(xb, wu_bf[half:, :], preferred_element_type=F32))
        hid = (jax.nn.silu(gate) * up).astype(BF16)
        o_ref[...] = jnp.dot(hid, wd_bf[...], preferred_element_type=F32)

    @pl.when(i >= used_ref[0])
    def _():
        o_ref[...] = jnp.zeros_like(o_ref)


def _experts(blk_e, n_used, buf, w_gate, w_up, w_down):
    n_rows, w = buf.shape
    d = w_down.shape[2]
    wspec = lambda a: pl.BlockSpec((1,) + a.shape[1:], lambda i, be, nu: (be[i], 0, 0))
    grid_spec = pltpu.PrefetchScalarGridSpec(
        num_scalar_prefetch=2,
        grid=(n_rows // MOE_ROWS,),
        in_specs=[pl.BlockSpec((MOE_ROWS, w), lambda i, be, nu: (i, 0)), wspec(w_gate), wspec(w_up), wspec(w_down)],
        out_specs=pl.BlockSpec((MOE_ROWS, d), lambda i, be, nu: (i, 0)),
        scratch_shapes=[pltpu.VMEM(w_gate.shape[1:], BF16), pltpu.VMEM(w_up.shape[1:], BF16),
                        pltpu.VMEM(w_down.shape[1:], BF16)],
    )
    return pl.pallas_call(
        _experts_kernel,
        grid_spec=grid_spec,
        out_shape=jax.ShapeDtypeStruct((n_rows, d), F32),
        compiler_params=_cparams("arbitrary"),
        name="experts",
    )(blk_e, n_used, buf, w_gate, w_up, w_down)


def _combine_kernel(dest_ref, x1_ref, route_ref, eo_ref, o_ref, rows_ref, sem, *, tm):
    def issue(g, _):
        for j in range(ROW_DMA_UNROLL):
            r = g * ROW_DMA_UNROLL + j
            _row_copy(eo_ref, dest_ref[0, 0, 2 * r], rows_ref.at[0], r, sem).start()
            _row_copy(eo_ref, dest_ref[0, 0, 2 * r + 1], rows_ref.at[1], r, sem).start()
        return 0

    lax.fori_loop(0, tm // ROW_DMA_UNROLL, issue, 0)
    for s in range(2):
        pltpu.make_async_copy(eo_ref.at[pl.ds(0, tm), :], rows_ref.at[s], sem).wait()
    route = route_ref[...]
    o_ref[...] = x1_ref[...] + (route[:, 2:3] * rows_ref[0] + route[:, 3:4] * rows_ref[1])


def _combine(dest, x1, route, eo):
    n, d = x1.shape
    tm = COMBINE_TM
    dest3 = dest.reshape(n // tm, 1, 2 * tm)
    return pl.pallas_call(
        functools.partial(_combine_kernel, tm=tm),
        grid=(n // tm,),
        in_specs=[pl.BlockSpec((1, 1, 2 * tm), lambda i: (i, 0, 0), memory_space=pltpu.SMEM),
                  pl.BlockSpec((tm, d), lambda i: (i, 0)),
                  pl.BlockSpec((tm, LANES), lambda i: (i, 0)),
                  pl.BlockSpec(memory_space=pl.ANY)],
        out_specs=pl.BlockSpec((tm, d), lambda i: (i, 0)),
        out_shape=jax.ShapeDtypeStruct((n, d), F32),
        scratch_shapes=[pltpu.VMEM((2, tm, d), F32), pltpu.SemaphoreType.DMA(())],
        compiler_params=_cparams("arbitrary"),
        name="combine",
    )(dest3, x1, route, eo)


def _layer(x, g_mix, w_in, lam_re, lam_im, log_dt, b_re, b_im, c_re, c_im, d_skip, w_glu, b_glu, g_q, g_k,
           g_ssm_out, g_attn_out, w_out, g_ffn, w_rg, b_rg, w_re, b_re_router, w_gate, w_up, w_down):
    bsz, t_len, d = x.shape
    n = bsz * t_len
    d_ssm = w_glu.shape[0]
    d_att = g_attn_out.shape[0]
    n_heads = d_att // HEAD_DIM
    n_chunks = t_len // SSM_CHUNK
    x2 = x.reshape(n, d)

    u_t, q, k, v = _in_proj(x2, g_mix[None], w_in.astype(BF16), jnp.tile(g_q, n_heads)[None],
                            jnp.tile(g_k, n_heads)[None], d_ssm, d_att, bsz, t_len)
    tables = _s5_tables(lam_re, lam_im, log_dt, b_re, b_im, c_re, c_im, d_skip)
    y_ssm_t = _s5(u_t.reshape(n_chunks, SSM_CHUNK, bsz, d_ssm), tables).reshape(t_len, bsz * d_ssm)

    y_att = _attention(q, k, v, bsz, t_len)

    w_r = jnp.concatenate([w_rg, w_re.reshape(d, N_EXPERTS)], axis=1)
    w_r = jnp.pad(w_r, ((0, 0), (0, LANES - w_r.shape[1]))).astype(BF16)
    b_r = jnp.pad(jnp.concatenate([b_rg, b_re_router.reshape(N_EXPERTS)]), (0, LANES - ROUTER_LANE0 - N_EXPERTS))[None]
    x1, h2p, route, cnt = _post(x2, y_ssm_t, y_att, w_glu.astype(BF16), b_glu[None], g_ssm_out[None],
                                g_attn_out[None], w_out.astype(BF16), g_ffn[None], w_r, b_r)

    counts = cnt[0, ROUTER_LANE0:ROUTER_LANE0 + N_EXPERTS].astype(jnp.int32)
    pcounts = ((counts + MOE_ROWS - 1) // MOE_ROWS) * MOE_ROWS
    pends = jnp.cumsum(pcounts)
    pstarts = pends - pcounts
    expert = route[:, 0:2].astype(jnp.int32)
    e_ids = jnp.arange(N_EXPERTS, dtype=jnp.int32)
    start = jnp.sum(jnp.where(expert[..., None] == e_ids, pstarts, 0), axis=-1)
    dest = start + route[:, 4:6].astype(jnp.int32)
    n_rows = n * 2 + N_EXPERTS * MOE_ROWS
    n_blk = n_rows // MOE_ROWS
    blk_row0 = jnp.arange(n_blk, dtype=jnp.int32)[:, None] * MOE_ROWS
    blk_e = jnp.minimum(jnp.sum((pends[None, :] <= blk_row0).astype(jnp.int32), axis=1), N_EXPERTS - 1)
    n_used = (pends[-1:] // MOE_ROWS).astype(jnp.int32)

    buf = _dispatch(dest, h2p, n_rows)
    eo = _experts(blk_e, n_used, buf, w_gate, w_up, w_down)
    out = _combine(dest, x1, route, eo)
    return out.reshape(bsz, t_len, d)


def kernel(x, g_mix, w_in, ssm_lambda_re, ssm_lambda_im, ssm_log_dt, ssm_b_re, ssm_b_im, ssm_c_re, ssm_c_im, ssm_d, ssm_w_glu, ssm_b_glu, g_q, g_k, g_ssm_out, g_attn_out, w_out, g_ffn, w_router_group, b_router_group, w_router_expert, b_router_expert, w_gate, w_up, w_down):
    for l in range(g_mix.shape[0]):
        x = _layer(x, g_mix[l], w_in[l], ssm_lambda_re[l], ssm_lambda_im[l], ssm_log_dt[l], ssm_b_re[l],
                   ssm_b_im[l], ssm_c_re[l], ssm_c_im[l], ssm_d[l], ssm_w_glu[l], ssm_b_glu[l], g_q[l], g_k[l],
                   g_ssm_out[l], g_attn_out[l], w_out[l], g_ffn[l], w_router_group[l], b_router_group[l],
                   w_router_expert[l], b_router_expert[l], w_gate[l], w_up[l], w_down[l])
    return x
```

```python
import functools
import math

import jax
import jax.numpy as jnp
from jax import lax
from jax.experimental import pallas as pl
from jax.experimental.pallas import tpu as pltpu
from jax.experimental.pallas import tpu_sc as plsc

F32 = jnp.float32
BF16 = jnp.bfloat16
EPS = 1e-6

LANES = 128
VMEM_LIMIT_BYTES = 56 * 1024 * 1024

SSM_GROUP = 16
SSM_STATE = 64
SSM_CHUNK = 16
HEAD_DIM = 64
N_EXPERT_GROUPS = 4
EXPERTS_PER_GROUP = 8
N_EXPERTS = N_EXPERT_GROUPS * EXPERTS_PER_GROUP
ROUTER_LANE0 = N_EXPERT_GROUPS
MOE_ROWS = 512
ATT_SKIP = 110.0

S5_GPB = LANES // SSM_GROUP
S5_CHUNKS_PER_STEP = 8

IN_TM = 512
ATT_TILE = 256
POST_TM = 512
DISPATCH_TM = 512
COMBINE_TM = 256
ROW_DMA_UNROLL = 8
SC_CORES = 2
SC_SUBCORES = 16
SC_WINDOW = 64


def _cparams(*sem):
    return pltpu.CompilerParams(dimension_semantics=sem, vmem_limit_bytes=VMEM_LIMIT_BYTES)


def _lane_iota(shape):
    return lax.broadcasted_iota(jnp.int32, shape, len(shape) - 1)


def _head_rms(t, gain):
    outs = []
    for c in range(t.shape[1] // LANES):
        blk = t[:, c * LANES:(c + 1) * LANES]
        sq = blk * blk
        lo = _lane_iota(blk.shape) < HEAD_DIM
        s_lo = jnp.sum(jnp.where(lo, sq, 0.0), axis=-1, keepdims=True)
        s_hi = jnp.sum(jnp.where(lo, 0.0, sq), axis=-1, keepdims=True)
        inv = jnp.where(lo, lax.rsqrt(s_lo * (1.0 / HEAD_DIM) + EPS),
                        lax.rsqrt(s_hi * (1.0 / HEAD_DIM) + EPS))
        outs.append(blk * inv * gain[:, c * LANES:(c + 1) * LANES])
    return jnp.concatenate(outs, axis=-1)


def _in_proj_kernel(x_ref, g_ref, w_ref, gq_ref, gk_ref, u_ref, q_ref, k_ref, v_ref, *, d_ssm, d_att, scale):
    x = x_ref[...]
    inv = lax.rsqrt(jnp.mean(x * x, axis=-1, keepdims=True) + EPS)
    h = (x * inv * g_ref[...]).astype(BF16)
    proj = jnp.dot(h, w_ref[...], preferred_element_type=F32)
    u_ref[...] = proj[:, :d_ssm].astype(BF16)
    q = _head_rms(proj[:, d_ssm:d_ssm + d_att], gq_ref[...])
    k = _head_rms(proj[:, d_ssm + d_att:d_ssm + 2 * d_att], gk_ref[...])
    q_ref[...] = (q * scale).astype(BF16)
    k_ref[...] = k.astype(BF16)
    v_ref[...] = proj[:, d_ssm + 2 * d_att:].astype(BF16)


def _in_proj(x2, g_mix, w_in_bf, gq_t, gk_t, d_ssm, d_att, bsz, t_len):
    n, d = x2.shape
    tm = IN_TM
    nt = t_len // tm
    row = lambda w: pl.BlockSpec((tm, w), lambda b, t: (b * nt + t, 0))
    full = lambda a: pl.BlockSpec(a.shape, lambda b, t: (0,) * a.ndim)
    out_sd = jax.ShapeDtypeStruct((n, d_att), BF16)
    return pl.pallas_call(
        functools.partial(_in_proj_kernel, d_ssm=d_ssm, d_att=d_att, scale=1.0 / math.sqrt(HEAD_DIM)),
        grid=(bsz, nt),
        in_specs=[row(d), full(g_mix), full(w_in_bf), full(gq_t), full(gk_t)],
        out_specs=[pl.BlockSpec((tm, d_ssm), lambda b, t: (t, b)), row(d_att), row(d_att), row(d_att)],
        out_shape=[jax.ShapeDtypeStruct((t_len, bsz * d_ssm), BF16), out_sd, out_sd, out_sd],
        compiler_params=_cparams("parallel", "parallel"),
        name="in_proj",
    )(x2, g_mix, w_in_bf, gq_t, gk_t)


def _s5_tables(lam_re, lam_im, log_dt, b_re, b_im, c_re, c_im, d_skip):
    hp = lax.Precision.HIGHEST
    L = SSM_CHUNK
    g_n, p_n = lam_re.shape
    dt = jnp.exp(log_dt)[:, None]
    lr, li = lam_re, lam_im
    ls = jnp.arange(L + 1, dtype=F32)[:, None, None]
    mag = jnp.exp(lr * dt * ls)
    pr, pi = mag * jnp.cos(li * dt * ls), mag * jnp.sin(li * dt * ls)
    abar_r, abar_i = pr[1], pi[1]
    den = lr * lr + li * li
    nr, ni = abar_r - 1.0, abar_i
    coef_r = (nr * lr + ni * li) / den
    coef_i = (ni * lr - nr * li) / den
    bbr = coef_r[..., None] * b_re - coef_i[..., None] * b_im
    bbi = coef_r[..., None] * b_im + coef_i[..., None] * b_re
    wr = pr[..., None] * bbr - pi[..., None] * bbi
    wi = pr[..., None] * bbi + pi[..., None] * bbr
    kl = (jnp.einsum('gop,lgpi->lgoi', c_re, wr[:L], precision=hp)
          - jnp.einsum('gop,lgpi->lgoi', c_im, wi[:L], precision=hp))
    kl = kl.at[0].add(jax.vmap(jnp.diag)(d_skip))
    n_lb = g_n // S5_GPB
    eye = jnp.eye(S5_GPB, dtype=F32)
    blk = lambda a: a.reshape(a.shape[0], n_lb, S5_GPB, *a.shape[2:])
    kbd = jnp.einsum('xy,dlxoi->ldxiyo', eye, blk(kl)).reshape(n_lb, L, LANES, LANES)
    top = kbd.transpose(0, 2, 1, 3).reshape(n_lb, LANES, L * LANES)
    bot = jnp.concatenate([jnp.zeros((n_lb, LANES, LANES), F32), top[:, :, :-LANES]], axis=2)
    w0 = jnp.concatenate([top, bot], axis=1)
    bst_r = jnp.einsum('xy,slxpi->lsxiyp', eye, blk(wr[:L][::-1])).reshape(n_lb, L * LANES, S5_GPB * p_n)
    bst_i = jnp.einsum('xy,slxpi->lsxiyp', eye, blk(wi[:L][::-1])).reshape(n_lb, L * LANES, S5_GPB * p_n)
    p1r, p1i = pr[1:], pi[1:]
    cst_r = (c_re[None] * p1r[:, :, None, :] - c_im[None] * p1i[:, :, None, :])
    cst_i = -(c_re[None] * p1i[:, :, None, :] + c_im[None] * p1r[:, :, None, :])
    cst_r = jnp.einsum('xy,tlxop->lxptyo', eye, blk(cst_r)).reshape(n_lb, S5_GPB * p_n, L * LANES)
    cst_i = jnp.einsum('xy,tlxop->lxptyo', eye, blk(cst_i)).reshape(n_lb, S5_GPB * p_n, L * LANES)
    a_l = jnp.stack([pr[L].reshape(n_lb, S5_GPB * p_n), pi[L].reshape(n_lb, S5_GPB * p_n)], axis=1)
    return (w0.astype(BF16), bst_r.astype(BF16), bst_i.astype(BF16),
            cst_r.astype(BF16), cst_i.astype(BF16), a_l)


def _s5_kernel(u_ref, w0_ref, br_ref, bi_ref, cr_ref, ci_ref, a_ref, y_ref, hr_ref, hi_ref, acc_ref):
    n_chunks, L, bsz, _ = u_ref.shape
    rows = n_chunks * bsz

    @pl.when(pl.program_id(1) == 0)
    def _():
        hr_ref[...] = jnp.zeros_like(hr_ref)
        hi_ref[...] = jnp.zeros_like(hi_ref)

    us = [u_ref[:, s].reshape(rows, LANES) for s in range(L)]
    lhs = jnp.concatenate(us, axis=1)
    sin_r = jnp.dot(lhs, br_ref[0], preferred_element_type=F32)
    sin_i = jnp.dot(lhs, bi_ref[0], preferred_element_type=F32)
    ar = a_ref[0, 0:1, :]
    ai = a_ref[0, 1:2, :]
    hr, hi = hr_ref[...], hi_ref[...]
    prev_r, prev_i = [], []
    for c in range(n_chunks):
        prev_r.append(hr)
        prev_i.append(hi)
        sl = slice(c * bsz, (c + 1) * bsz)
        hr, hi = ar * hr - ai * hi + sin_r[sl], ar * hi + ai * hr + sin_i[sl]
    hr_ref[...] = hr
    hi_ref[...] = hi
    pr = jnp.concatenate(prev_r, axis=0).astype(BF16)
    pi = jnp.concatenate(prev_i, axis=0).astype(BF16)
    acc_ref[...] = (jnp.dot(pr, cr_ref[0], preferred_element_type=F32)
                    + jnp.dot(pi, ci_ref[0], preferred_element_type=F32))
    for p in range(L // 2):
        off = 2 * p * LANES
        pair = jnp.concatenate([us[2 * p], us[2 * p + 1]], axis=1)
        acc_ref[:, off:] += jnp.dot(pair, w0_ref[0, :, :L * LANES - off], preferred_element_type=F32)
    for t in range(L):
        y_ref[:, t] = acc_ref[:, t * LANES:(t + 1) * LANES].reshape(n_chunks, bsz, LANES)


def _s5(u4, tables):
    n_chunks, L, bsz, d_ssm = u4.shape
    w0, bst_r, bst_i, cst_r, cst_i, a_l = tables
    cb = S5_CHUNKS_PER_STEP
    data = pl.BlockSpec((cb, L, bsz, LANES), lambda lb, c: (c, 0, 0, lb))
    per_lb = lambda a: pl.BlockSpec((1,) + a.shape[1:], lambda lb, c: (lb,) + (0,) * (a.ndim - 1))
    state = pltpu.VMEM((bsz, a_l.shape[2]), F32)
    return pl.pallas_call(
        _s5_kernel,
        grid=(d_ssm // LANES, n_chunks // cb),
        in_specs=[data, per_lb(w0), per_lb(bst_r), per_lb(bst_i), per_lb(cst_r), per_lb(cst_i), per_lb(a_l)],
        out_specs=data,
        out_shape=jax.ShapeDtypeStruct(u4.shape, F32),
        scratch_shapes=[state, state, pltpu.VMEM((cb * bsz, L * LANES), F32)],
        compiler_params=_cparams("arbitrary", "arbitrary"),
        name="s5",
    )(u4, w0, bst_r, bst_i, cst_r, cst_i, a_l)


def _softplus(z):
    return jnp.maximum(z, 0.0) + jnp.log(1.0 + jnp.exp(-jnp.abs(z)))


def _att_tile(qm, k, v, tri, r_in, causal):
    z = lax.dot_general(qm, k, (((1,), (1,)), ((), ())), preferred_element_type=F32)
    sp = _softplus(z)
    if causal:
        rows = lax.broadcasted_iota(jnp.int32, z.shape, 0)
        cols = lax.broadcasted_iota(jnp.int32, z.shape, 1)
        keep = cols < rows
        sp_m = jnp.where(keep, sp, 0.0)
    else:
        sp_m = sp
    newer = jnp.dot(sp_m.astype(BF16), tri, preferred_element_type=F32)
    att = jnp.exp(z - sp - newer - r_in)
    if causal:
        att = jnp.where(keep, att, 0.0)
    pv = jnp.dot(att.astype(BF16), v, preferred_element_type=F32)
    return pv, newer[:, 0:1] + sp_m[:, 0:1]


def _attn_kernel(q_ref, k_ref, v_ref, o_ref, *, t_len, tile):
    nq = t_len // tile
    r_i = lax.broadcasted_iota(jnp.int32, (tile, tile), 0)
    c_i = lax.broadcasted_iota(jnp.int32, (tile, tile), 1)
    tri = jnp.where(r_i > c_i, 1.0, 0.0).astype(BF16)
    head0 = _lane_iota((tile, LANES)) < HEAD_DIM
    zero_r = jnp.zeros((tile, 1), F32)

    def q_tile(q0, j_older):
        q = q_ref[pl.ds(q0, tile), :]
        zq = jnp.zeros_like(q)
        qms = (jnp.where(head0, q, zq), jnp.where(head0, zq, q))
        k_d, v_d = k_ref[pl.ds(q0, tile), :], v_ref[pl.ds(q0, tile), :]
        if j_older is not None:
            k_p, v_p = k_ref[pl.ds(q0 - tile, tile), :], v_ref[pl.ds(q0 - tile, tile), :]
        accs, rs = [], []
        for qm in qms:
            acc, r = _att_tile(qm, k_d, v_d, tri, zero_r, causal=True)
            if j_older is not None:
                pv, dr = _att_tile(qm, k_p, v_p, tri, r, causal=False)
                acc, r = acc + pv, r + dr
            accs.append(acc)
            rs.append(r)
        if j_older is not None:
            def cond(c):
                j, _, _, r0, r1 = c
                return jnp.logical_and(j >= 0, jnp.min(jnp.minimum(r0, r1)) < ATT_SKIP)

            def body(c):
                j, a0, a1, r0, r1 = c
                k0 = pl.multiple_of(j * tile, tile)
                k_j, v_j = k_ref[pl.ds(k0, tile), :], v_ref[pl.ds(k0, tile), :]
                pv0, d0 = _att_tile(qms[0], k_j, v_j, tri, r0, causal=False)
                pv1, d1 = _att_tile(qms[1], k_j, v_j, tri, r1, causal=False)
                return j - 1, a0 + pv0, a1 + pv1, r0 + d0, r1 + d1

            _, a0, a1, _, _ = lax.while_loop(cond, body, (j_older, accs[0], accs[1], rs[0], rs[1]))
            accs = [a0, a1]
        o_ref[pl.ds(q0, tile), :] = jnp.where(head0, accs[0], accs[1])

    q_tile(0, None)

    def later(i, _):
        q_tile(pl.multiple_of(i * tile, tile), i - 2)
        return 0

    lax.fori_loop(1, nq, later, 0)


def _attention(q, k, v, bsz, t_len):
    n, d_att = q.shape
    spec = pl.BlockSpec((t_len, LANES), lambda b, p: (b, p))
    return pl.pallas_call(
        functools.partial(_attn_kernel, t_len=t_len, tile=ATT_TILE),
        grid=(bsz, d_att // LANES),
        in_specs=[spec, spec, spec],
        out_specs=spec,
        out_shape=jax.ShapeDtypeStruct((n, d_att), F32),
        compiler_params=_cparams("parallel", "parallel"),
        name="attn",
    )(q, k, v)


def _rms(t, gain):
    return t * lax.rsqrt(jnp.mean(t * t, axis=-1, keepdims=True) + EPS) * gain


def _gelu_tanh(y):
    return 0.5 * y * (1.0 + jnp.tanh(math.sqrt(2.0 / math.pi) * (y + 0.044715 * (y * y * y))))


def _pack_bf16_pairs(a, b):
    ua = pltpu.bitcast(a.astype(BF16).astype(F32), jnp.uint32)
    ub = pltpu.bitcast(b.astype(BF16).astype(F32), jnp.uint32)
    return ua | (ub >> 16)


def _unpack_bf16_pairs(w):
    a = pltpu.bitcast(w & jnp.uint32(0xFFFF0000), F32).astype(BF16)
    b = pltpu.bitcast(w << 16, F32).astype(BF16)
    return a, b


def _post_kernel(x_ref, ys_ref, ya_ref, wglu_ref, bglu_ref, gs_ref, ga_ref, wo_ref, gf_ref,
                 wr_ref, br_ref, x1_ref, h2_ref, route_ref, cnt_ref, run_ref, *, d_ssm):
    i = pl.program_id(0)

    @pl.when(i == 0)
    def _():
        run_ref[...] = jnp.zeros_like(run_ref)

    y = _gelu_tanh(ys_ref[...])
    gate = jnp.dot(y.astype(BF16), wglu_ref[...], preferred_element_type=F32) + bglu_ref[...]
    y = y * jax.nn.sigmoid(gate)
    m_s = _rms(y, gs_ref[...]).astype(BF16)
    m_a = _rms(ya_ref[...], ga_ref[...]).astype(BF16)
    mix = (jnp.dot(m_s, wo_ref[:d_ssm, :], preferred_element_type=F32)
           + jnp.dot(m_a, wo_ref[d_ssm:, :], preferred_element_type=F32))
    x1 = x_ref[...] + mix
    x1_ref[...] = x1
    h2 = _rms(x1, gf_ref[...])
    half = h2.shape[1] // 2
    h2_ref[...] = _pack_bf16_pairs(h2[:, :half], h2[:, half:])

    lg = jnp.dot(h2.astype(BF16), wr_ref[...], preferred_element_type=F32) + br_ref[...]
    tm = lg.shape[0]
    lane = _lane_iota(lg.shape).astype(F32)
    neg = -jnp.inf
    first = lambda hit: jnp.min(jnp.where(hit, lane, float(LANES)), axis=-1, keepdims=True)
    glog = jnp.where(lane < N_EXPERT_GROUPS, lg, neg)
    gmax = jnp.max(glog, axis=-1, keepdims=True)
    p_grp = 1.0 / jnp.sum(jnp.exp(glog - gmax), axis=-1, keepdims=True)
    grp = first(glog == gmax)
    e0 = ROUTER_LANE0 + grp * EXPERTS_PER_GROUP
    elog = jnp.where(jnp.logical_and(lane >= e0, lane < e0 + EXPERTS_PER_GROUP), lg, neg)
    m1 = jnp.max(elog, axis=-1, keepdims=True)
    i1 = first(elog == m1)
    elog2 = jnp.where(lane == i1, neg, elog)
    m2 = jnp.max(elog2, axis=-1, keepdims=True)
    i2 = first(elog2 == m2)
    e21 = jnp.exp(m2 - m1)
    g1 = p_grp * (1.0 / (1.0 + e21))
    g2 = p_grp * (e21 / (1.0 + e21))

    sel1 = lane == i1
    sel2 = lane == i2
    onehot = jnp.where(jnp.logical_or(sel1, sel2), 1.0, 0.0)
    r_i = lax.broadcasted_iota(jnp.int32, (tm, tm), 0)
    c_i = lax.broadcasted_iota(jnp.int32, (tm, tm), 1)
    lower = jnp.where(c_i < r_i, 1.0, 0.0).astype(BF16)
    before = jnp.dot(lower, onehot.astype(BF16), preferred_element_type=F32) + run_ref[0:1, :]
    rank1 = jnp.sum(jnp.where(sel1, before, 0.0), axis=-1, keepdims=True)
    rank2 = jnp.sum(jnp.where(sel2, before, 0.0), axis=-1, keepdims=True)
    run_ref[0:1, :] = run_ref[0:1, :] + jnp.sum(onehot, axis=0, keepdims=True)
    cnt_ref[...] = run_ref[...]

    fields = (i1 - ROUTER_LANE0, i2 - ROUTER_LANE0, g1, g2, rank1, rank2)
    route = jnp.zeros(lg.shape, F32)
    for pos, val in enumerate(fields):
        route = jnp.where(lane == pos, val, route)
    route_ref[...] = route


def _post(x2, y_ssm_t, y_att, w_glu_bf, b_glu, g_ssm, g_att, w_out_bf, g_ffn, w_r_bf, b_r):
    n, d = x2.shape
    d_ssm = w_glu_bf.shape[0]
    tm = POST_TM
    nt = y_ssm_t.shape[0] // tm
    row = lambda w: pl.BlockSpec((tm, w), lambda i: (i, 0))
    ssm_spec = pl.BlockSpec((tm, d_ssm), lambda i: (i % nt, i // nt))
    full = lambda a: pl.BlockSpec(a.shape, lambda i: (0,) * a.ndim)
    cnt_spec = pl.BlockSpec((8, LANES), lambda i: (0, 0))
    return pl.pallas_call(
        functools.partial(_post_kernel, d_ssm=d_ssm),
        grid=(n // tm,),
        in_specs=[row(d), ssm_spec, row(y_att.shape[1]), full(w_glu_bf), full(b_glu), full(g_ssm),
                  full(g_att), full(w_out_bf), full(g_ffn), full(w_r_bf), full(b_r)],
        out_specs=[row(d), row(d // 2), row(LANES), cnt_spec],
        out_shape=[jax.ShapeDtypeStruct((n, d), F32), jax.ShapeDtypeStruct((n, d // 2), jnp.uint32),
                   jax.ShapeDtypeStruct((n, LANES), F32), jax.ShapeDtypeStruct((8, LANES), F32)],
        scratch_shapes=[pltpu.VMEM((8, LANES), F32)],
        compiler_params=_cparams("arbitrary"),
        name="post",
    )(x2, y_ssm_t, y_att, w_glu_bf, b_glu, g_ssm, g_att, w_out_bf, g_ffn, w_r_bf, b_r)


def _row_copy(src_ref, src_row, dst_ref, dst_row, sem):
    return pltpu.make_async_copy(src_ref.at[pl.ds(src_row, 1), :], dst_ref.at[pl.ds(dst_row, 1), :], sem)


def _dispatch_kernel(dest_ref, h_ref, buf_in_ref, buf_ref, sem, *, tm):
    del buf_in_ref

    def issue(g, _):
        for j in range(ROW_DMA_UNROLL):
            r = g * ROW_DMA_UNROLL + j
            _row_copy(h_ref, r, buf_ref, dest_ref[0, 0, 2 * r], sem).start()
            _row_copy(h_ref, r, buf_ref, dest_ref[0, 0, 2 * r + 1], sem).start()
        return 0

    lax.fori_loop(0, tm // ROW_DMA_UNROLL, issue, 0)
    for _ in range(2):
        pltpu.make_async_copy(h_ref, buf_ref.at[pl.ds(0, tm), :], sem).wait()


def _dispatch(dest, h2p, n_rows):
    n, w = h2p.shape
    tm = DISPATCH_TM
    dest3 = dest.reshape(n // tm, 1, 2 * tm)
    buf0 = jnp.zeros((n_rows, w), h2p.dtype)
    return pl.pallas_call(
        functools.partial(_dispatch_kernel, tm=tm),
        grid=(n // tm,),
        in_specs=[pl.BlockSpec((1, 1, 2 * tm), lambda i: (i, 0, 0), memory_space=pltpu.SMEM),
                  pl.BlockSpec((tm, w), lambda i: (i, 0)),
                  pl.BlockSpec(memory_space=pl.ANY)],
        out_specs=pl.BlockSpec(memory_space=pl.ANY),
        out_shape=jax.ShapeDtypeStruct((n_rows, w), h2p.dtype),
        scratch_shapes=[pltpu.SemaphoreType.DMA(())],
        input_output_aliases={2: 0},
        compiler_params=_cparams("arbitrary"),
        name="dispatch",
    )(dest3, h2p, buf0)


def _dispatch_sc(dest, h2p, n_rows):
    n, w = h2p.shape
    workers = SC_CORES * SC_SUBCORES
    n_win = n // (workers * SC_WINDOW)
    d0 = dest[:, 0].reshape(n // SC_WINDOW, SC_WINDOW)
    d1 = dest[:, 1].reshape(n // SC_WINDOW, SC_WINDOW)
    mesh = plsc.VectorSubcoreMesh(core_axis_name="c", subcore_axis_name="s")

    def body(h_hbm, d0_hbm, d1_hbm, o_hbm, rows_v, i0_v, i1_v):
        wid = lax.axis_index("c") * SC_SUBCORES + lax.axis_index("s")

        @pl.loop(0, n_win)
        def _(j):
            blk = wid * n_win + j
            pltpu.sync_copy(h_hbm.at[pl.ds(blk * SC_WINDOW, SC_WINDOW)], rows_v)
            pltpu.sync_copy(d0_hbm.at[blk], i0_v)
            pltpu.sync_copy(d1_hbm.at[blk], i1_v)
            pltpu.sync_copy(rows_v, o_hbm.at[i0_v])
            pltpu.sync_copy(rows_v, o_hbm.at[i1_v])

    return pl.kernel(
        body,
        out_type=jax.ShapeDtypeStruct((n_rows, w), h2p.dtype),
        mesh=mesh,
        scratch_types=[pltpu.VMEM((SC_WINDOW, w), h2p.dtype), pltpu.VMEM((SC_WINDOW,), jnp.int32),
                       pltpu.VMEM((SC_WINDOW,), jnp.int32)],
        name="dispatch_sc",
    )(h2p, d0, d1)


def _experts_kernel(blk_e_ref, used_ref, x_ref, wg_ref, wu_ref, wd_ref, o_ref, wg_bf, wu_bf, wd_bf):
    i = pl.program_id(0)

    @pl.when(jnp.logical_or(i == 0, blk_e_ref[i] != blk_e_ref[jnp.maximum(i - 1, 0)]))
    def _():
        wg_bf[...] = wg_ref[0].astype(BF16)
        wu_bf[...] = wu_ref[0].astype(BF16)
        wd_bf[...] = wd_ref[0].astype(BF16)

    @pl.when(i < used_ref[0])
    def _():
        xa, xb = _unpack_bf16_pairs(x_ref[...])
        half = xa.shape[1]
        gate = (jnp.dot(xa, wg_bf[:half, :], preferred_element_type=F32)
                + jnp.dot(xb, wg_bf[half:, :], preferred_element_type=F32))
        up = (jnp.dot(xa, wu_bf[:half, :], preferred_element_type=F32)
              + jnp.dot(xb, wu_bf[half:, :], preferred_element_type=F32))
        hid = (jax.nn.silu(gate) * up).astype(BF16)
        o_ref[...] = jnp.dot(hid, wd_bf[...], preferred_element_type=F32)

    @pl.when(i >= used_ref[0])
    def _():
        o_ref[...] = jnp.zeros_like(o_ref)


def _experts(blk_e, n_used, buf, w_gate, w_up, w_down):
    n_rows, w = buf.shape
    d = w_down.shape[2]
    wspec = lambda a: pl.BlockSpec((1,) + a.shape[1:], lambda i, be, nu: (be[i], 0, 0))
    grid_spec = pltpu.PrefetchScalarGridSpec(
        num_scalar_prefetch=2,
        grid=(n_rows // MOE_ROWS,),
        in_specs=[pl.BlockSpec((MOE_ROWS, w), lambda i, be, nu: (i, 0)), wspec(w_gate), wspec(w_up), wspec(w_down)],
        out_specs=pl.BlockSpec((MOE_ROWS, d), lambda i, be, nu: (i, 0)),
        scratch_shapes=[pltpu.VMEM(w_gate.shape[1:], BF16), pltpu.VMEM(w_up.shape[1:], BF16),
                        pltpu.VMEM(w_down.shape[1:], BF16)],
    )
    return pl.pallas_call(
        _experts_kernel,
        grid_spec=grid_spec,
        out_shape=jax.ShapeDtypeStruct((n_rows, d), F32),
        compiler_params=_cparams("arbitrary"),
        name="experts",
    )(blk_e, n_used, buf, w_gate, w_up, w_down)


def _combine_kernel(dest_ref, x1_ref, route_ref, eo_ref, o_ref, rows_ref, sem, *, tm):
    def issue(g, _):
        for j in range(ROW_DMA_UNROLL):
            r = g * ROW_DMA_UNROLL + j
            _row_copy(eo_ref, dest_ref[0, 0, 2 * r], rows_ref.at[0], r, sem).start()
            _row_copy(eo_ref, dest_ref[0, 0, 2 * r + 1], rows_ref.at[1], r, sem).start()
        return 0

    lax.fori_loop(0, tm // ROW_DMA_UNROLL, issue, 0)
    for s in range(2):
        pltpu.make_async_copy(eo_ref.at[pl.ds(0, tm), :], rows_ref.at[s], sem).wait()
    route = route_ref[...]
    o_ref[...] = x1_ref[...] + (route[:, 2:3] * rows_ref[0] + route[:, 3:4] * rows_ref[1])


def _combine(dest, x1, route, eo):
    n, d = x1.shape
    tm = COMBINE_TM
    dest3 = dest.reshape(n // tm, 1, 2 * tm)
    return pl.pallas_call(
        functools.partial(_combine_kernel, tm=tm),
        grid=(n // tm,),
        in_specs=[pl.BlockSpec((1, 1, 2 * tm), lambda i: (i, 0, 0), memory_space=pltpu.SMEM),
                  pl.BlockSpec((tm, d), lambda i: (i, 0)),
                  pl.BlockSpec((tm, LANES), lambda i: (i, 0)),
                  pl.BlockSpec(memory_space=pl.ANY)],
        out_specs=pl.BlockSpec((tm, d), lambda i: (i, 0)),
        out_shape=jax.ShapeDtypeStruct((n, d), F32),
        scratch_shapes=[pltpu.VMEM((2, tm, d), F32), pltpu.SemaphoreType.DMA(())],
        compiler_params=_cparams("arbitrary"),
        name="combine",
    )(dest3, x1, route, eo)


def _layer(x, g_mix, w_in, lam_re, lam_im, log_dt, b_re, b_im, c_re, c_im, d_skip, w_glu, b_glu, g_q, g_k,
           g_ssm_out, g_attn_out, w_out, g_ffn, w_rg, b_rg, w_re, b_re_router, w_gate, w_up, w_down):
    bsz, t_len, d = x.shape
    n = bsz * t_len
    d_ssm = w_glu.shape[0]
    d_att = g_attn_out.shape[0]
    n_heads = d_att // HEAD_DIM
    n_chunks = t_len // SSM_CHUNK
    x2 = x.reshape(n, d)

    u_t, q, k, v = _in_proj(x2, g_mix[None], w_in.astype(BF16), jnp.tile(g_q, n_heads)[None],
                            jnp.tile(g_k, n_heads)[None], d_ssm, d_att, bsz, t_len)
    tables = _s5_tables(lam_re, lam_im, log_dt, b_re, b_im, c_re, c_im, d_skip)
    y_ssm_t = _s5(u_t.reshape(n_chunks, SSM_CHUNK, bsz, d_ssm), tables).reshape(t_len, bsz * d_ssm)

    y_att = _attention(q, k, v, bsz, t_len)

    w_r = jnp.concatenate([w_rg, w_re.reshape(d, N_EXPERTS)], axis=1)
    w_r = jnp.pad(w_r, ((0, 0), (0, LANES - w_r.shape[1]))).astype(BF16)
    b_r = jnp.pad(jnp.concatenate([b_rg, b_re_router.reshape(N_EXPERTS)]), (0, LANES - ROUTER_LANE0 - N_EXPERTS))[None]
    x1, h2p, route, cnt = _post(x2, y_ssm_t, y_att, w_glu.astype(BF16), b_glu[None], g_ssm_out[None],
                                g_attn_out[None], w_out.astype(BF16), g_ffn[None], w_r, b_r)

    counts = cnt[0, ROUTER_LANE0:ROUTER_LANE0 + N_EXPERTS].astype(jnp.int32)
    pcounts = ((counts + MOE_ROWS - 1) // MOE_ROWS) * MOE_ROWS
    pends = jnp.cumsum(pcounts)
    pstarts = pends - pcounts
    expert = route[:, 0:2].astype(jnp.int32)
    e_ids = jnp.arange(N_EXPERTS, dtype=jnp.int32)
    start = jnp.sum(jnp.where(expert[..., None] == e_ids, pstarts, 0), axis=-1)
    dest = start + route[:, 4:6].astype(jnp.int32)
    n_rows = n * 2 + N_EXPERTS * MOE_ROWS
    n_blk = n_rows // MOE_ROWS
    blk_row0 = jnp.arange(n_blk, dtype=jnp.int32)[:, None] * MOE_ROWS
    blk_e = jnp.minimum(jnp.sum((pends[None, :] <= blk_row0).astype(jnp.int32), axis=1), N_EXPERTS - 1)
    n_used = (pends[-1:] // MOE_ROWS).astype(jnp.int32)

    buf = _dispatch_sc(dest, h2p, n_rows)
    eo = _experts(blk_e, n_used, buf, w_gate, w_up, w_down)
    out = _combine(dest, x1, route, eo)
    return out.reshape(bsz, t_len, d)


def kernel(x, g_mix, w_in, ssm_lambda_re, ssm_lambda_im, ssm_log_dt, ssm_b_re, ssm_b_im, ssm_c_re, ssm_c_im, ssm_d, ssm_w_glu, ssm_b_glu, g_q, g_k, g_ssm_out, g_attn_out, w_out, g_ffn, w_router_group, b_router_group, w_router_expert, b_router_expert, w_gate, w_up, w_down):
    for l in range(g_mix.shape[0]):
        x = _layer(x, g_mix[l], w_in[l], ssm_lambda_re[l], ssm_lambda_im[l], ssm_log_dt[l], ssm_b_re[l],
                   ssm_b_im[l], ssm_c_re[l], ssm_c_im[l], ssm_d[l], ssm_w_glu[l], ssm_b_glu[l], g_q[l], g_k[l],
                   g_ssm_out[l], g_attn_out[l], w_out[l], g_ffn[l], w_router_group[l], b_router_group[l],
                   w_router_expert[l], b_router_expert[l], w_gate[l], w_up[l], w_down[l])
    return x
```

```python
import functools
import math

import jax
import jax.numpy as jnp
from jax import lax
from jax.experimental import pallas as pl
from jax.experimental.pallas import tpu as pltpu
from jax.experimental.pallas import tpu_sc as plsc

F32 = jnp.float32
BF16 = jnp.bfloat16
EPS = 1e-6

LANES = 128
VMEM_LIMIT_BYTES = 56 * 1024 * 1024

SSM_GROUP = 16
SSM_STATE = 64
SSM_CHUNK = 16
HEAD_DIM = 64
N_EXPERT_GROUPS = 4
EXPERTS_PER_GROUP = 8
N_EXPERTS = N_EXPERT_GROUPS * EXPERTS_PER_GROUP
ROUTER_LANE0 = N_EXPERT_GROUPS
MOE_ROWS = 512
ATT_SKIP = 110.0

S5_GPB = LANES // SSM_GROUP
S5_CHUNKS_PER_STEP = 8

IN_TM = 512
ATT_TILE = 256
POST_TM = 512
COMBINE_TM = 512
SC_CORES = 2
SC_SUBCORES = 16
SC_WINDOW = 64


def _cparams(*sem):
    return pltpu.CompilerParams(dimension_semantics=sem, vmem_limit_bytes=VMEM_LIMIT_BYTES)


def _lane_iota(shape):
    return lax.broadcasted_iota(jnp.int32, shape, len(shape) - 1)


def _head_rms(t, gain):
    outs = []
    for c in range(t.shape[1] // LANES):
        blk = t[:, c * LANES:(c + 1) * LANES]
        sq = blk * blk
        lo = _lane_iota(blk.shape) < HEAD_DIM
        s_lo = jnp.sum(jnp.where(lo, sq, 0.0), axis=-1, keepdims=True)
        s_hi = jnp.sum(jnp.where(lo, 0.0, sq), axis=-1, keepdims=True)
        inv = jnp.where(lo, lax.rsqrt(s_lo * (1.0 / HEAD_DIM) + EPS),
                        lax.rsqrt(s_hi * (1.0 / HEAD_DIM) + EPS))
        outs.append(blk * inv * gain[:, c * LANES:(c + 1) * LANES])
    return jnp.concatenate(outs, axis=-1)


def _in_proj_kernel(x_ref, g_ref, w_ref, gq_ref, gk_ref, u_ref, q_ref, k_ref, v_ref, *, d_ssm, d_att, scale):
    x = x_ref[...]
    inv = lax.rsqrt(jnp.mean(x * x, axis=-1, keepdims=True) + EPS)
    h = (x * inv * g_ref[...]).astype(BF16)
    proj = jnp.dot(h, w_ref[...], preferred_element_type=F32)
    u_ref[...] = proj[:, :d_ssm].astype(BF16)
    q = _head_rms(proj[:, d_ssm:d_ssm + d_att], gq_ref[...])
    k = _head_rms(proj[:, d_ssm + d_att:d_ssm + 2 * d_att], gk_ref[...])
    q_ref[...] = (q * scale).astype(BF16)
    k_ref[...] = k.astype(BF16)
    v_ref[...] = proj[:, d_ssm + 2 * d_att:].astype(BF16)


def _in_proj(x2, g_mix, w_in_bf, gq_t, gk_t, d_ssm, d_att, bsz, t_len):
    n, d = x2.shape
    tm = IN_TM
    nt = t_len // tm
    row = lambda w: pl.BlockSpec((tm, w), lambda b, t: (b * nt + t, 0))
    full = lambda a: pl.BlockSpec(a.shape, lambda b, t: (0,) * a.ndim)
    out_sd = jax.ShapeDtypeStruct((n, d_att), BF16)
    return pl.pallas_call(
        functools.partial(_in_proj_kernel, d_ssm=d_ssm, d_att=d_att, scale=1.0 / math.sqrt(HEAD_DIM)),
        grid=(bsz, nt),
        in_specs=[row(d), full(g_mix), full(w_in_bf), full(gq_t), full(gk_t)],
        out_specs=[pl.BlockSpec((tm, d_ssm), lambda b, t: (t, b)), row(d_att), row(d_att), row(d_att)],
        out_shape=[jax.ShapeDtypeStruct((t_len, bsz * d_ssm), BF16), out_sd, out_sd, out_sd],
        compiler_params=_cparams("parallel", "parallel"),
        name="in_proj",
    )(x2, g_mix, w_in_bf, gq_t, gk_t)


def _s5_tables(lam_re, lam_im, log_dt, b_re, b_im, c_re, c_im, d_skip):
    hp = lax.Precision.HIGHEST
    L = SSM_CHUNK
    g_n, p_n = lam_re.shape
    dt = jnp.exp(log_dt)[:, None]
    lr, li = lam_re, lam_im
    ls = jnp.arange(L + 1, dtype=F32)[:, None, None]
    mag = jnp.exp(lr * dt * ls)
    pr, pi = mag * jnp.cos(li * dt * ls), mag * jnp.sin(li * dt * ls)
    abar_r, abar_i = pr[1], pi[1]
    den = lr * lr + li * li
    nr, ni = abar_r - 1.0, abar_i
    coef_r = (nr * lr + ni * li) / den
    coef_i = (ni * lr - nr * li) / den
    bbr = coef_r[..., None] * b_re - coef_i[..., None] * b_im
    bbi = coef_r[..., None] * b_im + coef_i[..., None] * b_re
    wr = pr[..., None] * bbr - pi[..., None] * bbi
    wi = pr[..., None] * bbi + pi[..., None] * bbr
    kl = (jnp.einsum('gop,lgpi->lgoi', c_re, wr[:L], precision=hp)
          - jnp.einsum('gop,lgpi->lgoi', c_im, wi[:L], precision=hp))
    kl = kl.at[0].add(jax.vmap(jnp.diag)(d_skip))
    n_lb = g_n // S5_GPB
    eye = jnp.eye(S5_GPB, dtype=F32)
    blk = lambda a: a.reshape(a.shape[0], n_lb, S5_GPB, *a.shape[2:])
    kbd = jnp.einsum('xy,dlxoi->ldxiyo', eye, blk(kl)).reshape(n_lb, L, LANES, LANES)
    top = kbd.transpose(0, 2, 1, 3).reshape(n_lb, LANES, L * LANES)
    bot = jnp.concatenate([jnp.zeros((n_lb, LANES, LANES), F32), top[:, :, :-LANES]], axis=2)
    w0 = jnp.concatenate([top, bot], axis=1)
    bst_r = jnp.einsum('xy,slxpi->lsxiyp', eye, blk(wr[:L][::-1])).reshape(n_lb, L * LANES, S5_GPB * p_n)
    bst_i = jnp.einsum('xy,slxpi->lsxiyp', eye, blk(wi[:L][::-1])).reshape(n_lb, L * LANES, S5_GPB * p_n)
    p1r, p1i = pr[1:], pi[1:]
    cst_r = (c_re[None] * p1r[:, :, None, :] - c_im[None] * p1i[:, :, None, :])
    cst_i = -(c_re[None] * p1i[:, :, None, :] + c_im[None] * p1r[:, :, None, :])
    cst_r = jnp.einsum('xy,tlxop->lxptyo', eye, blk(cst_r)).reshape(n_lb, S5_GPB * p_n, L * LANES)
    cst_i = jnp.einsum('xy,tlxop->lxptyo', eye, blk(cst_i)).reshape(n_lb, S5_GPB * p_n, L * LANES)
    a_l = jnp.stack([pr[L].reshape(n_lb, S5_GPB * p_n), pi[L].reshape(n_lb, S5_GPB * p_n)], axis=1)
    return (w0.astype(BF16), bst_r.astype(BF16), bst_i.astype(BF16),
            cst_r.astype(BF16), cst_i.astype(BF16), a_l)


def _s5_kernel(u_ref, w0_ref, br_ref, bi_ref, cr_ref, ci_ref, a_ref, y_ref, hr_ref, hi_ref, acc_ref):
    n_chunks, L, bsz, _ = u_ref.shape
    rows = n_chunks * bsz

    @pl.when(pl.program_id(1) == 0)
    def _():
        hr_ref[...] = jnp.zeros_like(hr_ref)
        hi_ref[...] = jnp.zeros_like(hi_ref)

    us = [u_ref[:, s].reshape(rows, LANES) for s in range(L)]
    lhs = jnp.concatenate(us, axis=1)
    sin_r = jnp.dot(lhs, br_ref[0], preferred_element_type=F32)
    sin_i = jnp.dot(lhs, bi_ref[0], preferred_element_type=F32)
    ar = a_ref[0, 0:1, :]
    ai = a_ref[0, 1:2, :]
    hr, hi = hr_ref[...], hi_ref[...]
    prev_r, prev_i = [], []
    for c in range(n_chunks):
        prev_r.append(hr)
        prev_i.append(hi)
        sl = slice(c * bsz, (c + 1) * bsz)
        hr, hi = ar * hr - ai * hi + sin_r[sl], ar * hi + ai * hr + sin_i[sl]
    hr_ref[...] = hr
    hi_ref[...] = hi
    pr = jnp.concatenate(prev_r, axis=0).astype(BF16)
    pi = jnp.concatenate(prev_i, axis=0).astype(BF16)
    acc_ref[...] = (jnp.dot(pr, cr_ref[0], preferred_element_type=F32)
                    + jnp.dot(pi, ci_ref[0], preferred_element_type=F32))
    for p in range(L // 2):
        off = 2 * p * LANES
        pair = jnp.concatenate([us[2 * p], us[2 * p + 1]], axis=1)
        acc_ref[:, off:] += jnp.dot(pair, w0_ref[0, :, :L * LANES - off], preferred_element_type=F32)
    for t in range(L):
        y_ref[:, t] = acc_ref[:, t * LANES:(t + 1) * LANES].reshape(n_chunks, bsz, LANES)


def _s5(u4, tables):
    n_chunks, L, bsz, d_ssm = u4.shape
    w0, bst_r, bst_i, cst_r, cst_i, a_l = tables
    cb = S5_CHUNKS_PER_STEP
    data = pl.BlockSpec((cb, L, bsz, LANES), lambda lb, c: (c, 0, 0, lb))
    per_lb = lambda a: pl.BlockSpec((1,) + a.shape[1:], lambda lb, c: (lb,) + (0,) * (a.ndim - 1))
    state = pltpu.VMEM((bsz, a_l.shape[2]), F32)
    return pl.pallas_call(
        _s5_kernel,
        grid=(d_ssm // LANES, n_chunks // cb),
        in_specs=[data, per_lb(w0), per_lb(bst_r), per_lb(bst_i), per_lb(cst_r), per_lb(cst_i), per_lb(a_l)],
        out_specs=data,
        out_shape=jax.ShapeDtypeStruct(u4.shape, F32),
        scratch_shapes=[state, state, pltpu.VMEM((cb * bsz, L * LANES), F32)],
        compiler_params=_cparams("arbitrary", "arbitrary"),
        name="s5",
    )(u4, w0, bst_r, bst_i, cst_r, cst_i, a_l)


def _softplus(z):
    return jnp.maximum(z, 0.0) + jnp.log(1.0 + jnp.exp(-jnp.abs(z)))


def _att_tile(qm, k, v, tri, r_in, causal):
    z = lax.dot_general(qm, k, (((1,), (1,)), ((), ())), preferred_element_type=F32)
    sp = _softplus(z)
    if causal:
        rows = lax.broadcasted_iota(jnp.int32, z.shape, 0)
        cols = lax.broadcasted_iota(jnp.int32, z.shape, 1)
        keep = cols < rows
        sp_m = jnp.where(keep, sp, 0.0)
    else:
        sp_m = sp
    newer = jnp.dot(sp_m.astype(BF16), tri, preferred_element_type=F32)
    att = jnp.exp(z - sp - newer - r_in)
    if causal:
        att = jnp.where(keep, att, 0.0)
    pv = jnp.dot(att.astype(BF16), v, preferred_element_type=F32)
    return pv, newer[:, 0:1] + sp_m[:, 0:1]


def _attn_kernel(q_ref, k_ref, v_ref, o_ref, *, t_len, tile):
    nq = t_len // tile
    r_i = lax.broadcasted_iota(jnp.int32, (tile, tile), 0)
    c_i = lax.broadcasted_iota(jnp.int32, (tile, tile), 1)
    tri = jnp.where(r_i > c_i, 1.0, 0.0).astype(BF16)
    head0 = _lane_iota((tile, LANES)) < HEAD_DIM
    zero_r = jnp.zeros((tile, 1), F32)

    def q_tile(q0, j_older):
        q = q_ref[pl.ds(q0, tile), :]
        zq = jnp.zeros_like(q)
        qms = (jnp.where(head0, q, zq), jnp.where(head0, zq, q))
        k_d, v_d = k_ref[pl.ds(q0, tile), :], v_ref[pl.ds(q0, tile), :]
        if j_older is not None:
            k_p, v_p = k_ref[pl.ds(q0 - tile, tile), :], v_ref[pl.ds(q0 - tile, tile), :]
        accs, rs = [], []
        for qm in qms:
            acc, r = _att_tile(qm, k_d, v_d, tri, zero_r, causal=True)
            if j_older is not None:
                pv, dr = _att_tile(qm, k_p, v_p, tri, r, causal=False)
                acc, r = acc + pv, r + dr
            accs.append(acc)
            rs.append(r)
        if j_older is not None:
            def cond(c):
                j, _, _, r0, r1 = c
                return jnp.logical_and(j >= 0, jnp.min(jnp.minimum(r0, r1)) < ATT_SKIP)

            def body(c):
                j, a0, a1, r0, r1 = c
                k0 = pl.multiple_of(j * tile, tile)
                k_j, v_j = k_ref[pl.ds(k0, tile), :], v_ref[pl.ds(k0, tile), :]
                pv0, d0 = _att_tile(qms[0], k_j, v_j, tri, r0, causal=False)
                pv1, d1 = _att_tile(qms[1], k_j, v_j, tri, r1, causal=False)
                return j - 1, a0 + pv0, a1 + pv1, r0 + d0, r1 + d1

            _, a0, a1, _, _ = lax.while_loop(cond, body, (j_older, accs[0], accs[1], rs[0], rs[1]))
            accs = [a0, a1]
        o_ref[pl.ds(q0, tile), :] = jnp.where(head0, accs[0], accs[1])

    q_tile(0, None)

    def later(i, _):
        q_tile(pl.multiple_of(i * tile, tile), i - 2)
        return 0

    lax.fori_loop(1, nq, later, 0)


def _attention(q, k, v, bsz, t_len):
    n, d_att = q.shape
    spec = pl.BlockSpec((t_len, LANES), lambda b, p: (b, p))
    return pl.pallas_call(
        functools.partial(_attn_kernel, t_len=t_len, tile=ATT_TILE),
        grid=(bsz, d_att // LANES),
        in_specs=[spec, spec, spec],
        out_specs=spec,
        out_shape=jax.ShapeDtypeStruct((n, d_att), F32),
        compiler_params=_cparams("parallel", "parallel"),
        name="attn",
    )(q, k, v)


def _rms(t, gain):
    return t * lax.rsqrt(jnp.mean(t * t, axis=-1, keepdims=True) + EPS) * gain


def _gelu_tanh(y):
    return 0.5 * y * (1.0 + jnp.tanh(math.sqrt(2.0 / math.pi) * (y + 0.044715 * (y * y * y))))


def _pack_bf16_pairs(a, b):
    ua = pltpu.bitcast(a.astype(BF16).astype(F32), jnp.uint32)
    ub = pltpu.bitcast(b.astype(BF16).astype(F32), jnp.uint32)
    return ua | (ub >> 16)


def _unpack_bf16_pairs(w):
    a = pltpu.bitcast(w & jnp.uint32(0xFFFF0000), F32).astype(BF16)
    b = pltpu.bitcast(w << 16, F32).astype(BF16)
    return a, b


def _post_kernel(x_ref, ys_ref, ya_ref, wglu_ref, bglu_ref, gs_ref, ga_ref, wo_ref, gf_ref,
                 wr_ref, br_ref, x1_ref, h2_ref, route_ref, cnt_ref, run_ref, *, d_ssm):
    i = pl.program_id(0)

    @pl.when(i == 0)
    def _():
        run_ref[...] = jnp.zeros_like(run_ref)

    y = _gelu_tanh(ys_ref[...])
    gate = jnp.dot(y.astype(BF16), wglu_ref[...], preferred_element_type=F32) + bglu_ref[...]
    y = y * jax.nn.sigmoid(gate)
    m_s = _rms(y, gs_ref[...]).astype(BF16)
    m_a = _rms(ya_ref[...], ga_ref[...]).astype(BF16)
    mix = (jnp.dot(m_s, wo_ref[:d_ssm, :], preferred_element_type=F32)
           + jnp.dot(m_a, wo_ref[d_ssm:, :], preferred_element_type=F32))
    x1 = x_ref[...] + mix
    x1_ref[...] = x1
    h2 = _rms(x1, gf_ref[...])
    half = h2.shape[1] // 2
    h2_ref[...] = _pack_bf16_pairs(h2[:, :half], h2[:, half:])

    lg = jnp.dot(h2.astype(BF16), wr_ref[...], preferred_element_type=F32) + br_ref[...]
    tm = lg.shape[0]
    lane = _lane_iota(lg.shape).astype(F32)
    neg = -jnp.inf
    first = lambda hit: jnp.min(jnp.where(hit, lane, float(LANES)), axis=-1, keepdims=True)
    glog = jnp.where(lane < N_EXPERT_GROUPS, lg, neg)
    gmax = jnp.max(glog, axis=-1, keepdims=True)
    p_grp = 1.0 / jnp.sum(jnp.exp(glog - gmax), axis=-1, keepdims=True)
    grp = first(glog == gmax)
    e0 = ROUTER_LANE0 + grp * EXPERTS_PER_GROUP
    elog = jnp.where(jnp.logical_and(lane >= e0, lane < e0 + EXPERTS_PER_GROUP), lg, neg)
    m1 = jnp.max(elog, axis=-1, keepdims=True)
    i1 = first(elog == m1)
    elog2 = jnp.where(lane == i1, neg, elog)
    m2 = jnp.max(elog2, axis=-1, keepdims=True)
    i2 = first(elog2 == m2)
    e21 = jnp.exp(m2 - m1)
    g1 = p_grp * (1.0 / (1.0 + e21))
    g2 = p_grp * (e21 / (1.0 + e21))

    sel1 = lane == i1
    sel2 = lane == i2
    onehot = jnp.where(jnp.logical_or(sel1, sel2), 1.0, 0.0)
    r_i = lax.broadcasted_iota(jnp.int32, (tm, tm), 0)
    c_i = lax.broadcasted_iota(jnp.int32, (tm, tm), 1)
    lower = jnp.where(c_i < r_i, 1.0, 0.0).astype(BF16)
    before = jnp.dot(lower, onehot.astype(BF16), preferred_element_type=F32) + run_ref[0:1, :]
    rank1 = jnp.sum(jnp.where(sel1, before, 0.0), axis=-1, keepdims=True)
    rank2 = jnp.sum(jnp.where(sel2, before, 0.0), axis=-1, keepdims=True)
    run_ref[0:1, :] = run_ref[0:1, :] + jnp.sum(onehot, axis=0, keepdims=True)
    cnt_ref[...] = run_ref[...]

    fields = (i1 - ROUTER_LANE0, i2 - ROUTER_LANE0, g1, g2, rank1, rank2)
    route = jnp.zeros(lg.shape, F32)
    for pos, val in enumerate(fields):
        route = jnp.where(lane == pos, val, route)
    route_ref[...] = route


def _post(x2, y_ssm_t, y_att, w_glu_bf, b_glu, g_ssm, g_att, w_out_bf, g_ffn, w_r_bf, b_r):
    n, d = x2.shape
    d_ssm = w_glu_bf.shape[0]
    tm = POST_TM
    nt = y_ssm_t.shape[0] // tm
    row = lambda w: pl.BlockSpec((tm, w), lambda i: (i, 0))
    ssm_spec = pl.BlockSpec((tm, d_ssm), lambda i: (i % nt, i // nt))
    full = lambda a: pl.BlockSpec(a.shape, lambda i: (0,) * a.ndim)
    cnt_spec = pl.BlockSpec((8, LANES), lambda i: (0, 0))
    return pl.pallas_call(
        functools.partial(_post_kernel, d_ssm=d_ssm),
        grid=(n // tm,),
        in_specs=[row(d), ssm_spec, row(y_att.shape[1]), full(w_glu_bf), full(b_glu), full(g_ssm),
                  full(g_att), full(w_out_bf), full(g_ffn), full(w_r_bf), full(b_r)],
        out_specs=[row(d), row(d // 2), row(LANES), cnt_spec],
        out_shape=[jax.ShapeDtypeStruct((n, d), F32), jax.ShapeDtypeStruct((n, d // 2), jnp.uint32),
                   jax.ShapeDtypeStruct((n, LANES), F32), jax.ShapeDtypeStruct((8, LANES), F32)],
        scratch_shapes=[pltpu.VMEM((8, LANES), F32)],
        compiler_params=_cparams("arbitrary"),
        name="post",
    )(x2, y_ssm_t, y_att, w_glu_bf, b_glu, g_ssm, g_att, w_out_bf, g_ffn, w_r_bf, b_r)


def _dispatch_sc(dest, h2p, n_rows):
    n, w = h2p.shape
    workers = SC_CORES * SC_SUBCORES
    n_win = n // (workers * SC_WINDOW)
    d0 = dest[:, 0].reshape(n // SC_WINDOW, SC_WINDOW)
    d1 = dest[:, 1].reshape(n // SC_WINDOW, SC_WINDOW)
    mesh = plsc.VectorSubcoreMesh(core_axis_name="c", subcore_axis_name="s")

    def body(h_hbm, d0_hbm, d1_hbm, o_hbm, rows_v, i0_v, i1_v):
        wid = lax.axis_index("c") * SC_SUBCORES + lax.axis_index("s")

        @pl.loop(0, n_win)
        def _(j):
            blk = wid * n_win + j
            pltpu.sync_copy(h_hbm.at[pl.ds(blk * SC_WINDOW, SC_WINDOW)], rows_v)
            pltpu.sync_copy(d0_hbm.at[blk], i0_v)
            pltpu.sync_copy(d1_hbm.at[blk], i1_v)
            pltpu.sync_copy(rows_v, o_hbm.at[i0_v])
            pltpu.sync_copy(rows_v, o_hbm.at[i1_v])

    return pl.kernel(
        body,
        out_type=jax.ShapeDtypeStruct((n_rows, w), h2p.dtype),
        mesh=mesh,
        scratch_types=[pltpu.VMEM((SC_WINDOW, w), h2p.dtype), pltpu.VMEM((SC_WINDOW,), jnp.int32),
                       pltpu.VMEM((SC_WINDOW,), jnp.int32)],
        name="dispatch_sc",
    )(h2p, d0, d1)


def _experts_kernel(blk_e_ref, used_ref, x_ref, wg_ref, wu_ref, wd_ref, o_ref, wg_bf, wu_bf, wd_bf):
    i = pl.program_id(0)

    @pl.when(jnp.logical_or(i == 0, blk_e_ref[i] != blk_e_ref[jnp.maximum(i - 1, 0)]))
    def _():
        wg_bf[...] = wg_ref[0].astype(BF16)
        wu_bf[...] = wu_ref[0].astype(BF16)
        wd_bf[...] = wd_ref[0].astype(BF16)

    @pl.when(i < used_ref[0])
    def _():
        xa, xb = _unpack_bf16_pairs(x_ref[...])
        half = xa.shape[1]
        gate = (jnp.dot(xa, wg_bf[:half, :], preferred_element_type=F32)
                + jnp.dot(xb, wg_bf[half:, :], preferred_element_type=F32))
        up = (jnp.dot(xa, wu_bf[:half, :], preferred_element_type=F32)
              + jnp.dot(xb, wu_bf[half:, :], preferred_element_type=F32))
        hid = (jax.nn.silu(gate) * up).astype(BF16)
        o_ref[...] = jnp.dot(hid, wd_bf[...], preferred_element_type=F32)

    @pl.when(i >= used_ref[0])
    def _():
        o_ref[...] = jnp.zeros_like(o_ref)


def _experts(blk_e, n_used, buf, w_gate, w_up, w_down):
    n_rows, w = buf.shape
    d = w_down.shape[2]
    wspec = lambda a: pl.BlockSpec((1,) + a.shape[1:], lambda i, be, nu: (be[i], 0, 0))
    grid_spec = pltpu.PrefetchScalarGridSpec(
        num_scalar_prefetch=2,
        grid=(n_rows // MOE_ROWS,),
        in_specs=[pl.BlockSpec((MOE_ROWS, w), lambda i, be, nu: (i, 0)), wspec(w_gate), wspec(w_up), wspec(w_down)],
        out_specs=pl.BlockSpec((MOE_ROWS, d), lambda i, be, nu: (i, 0)),
        scratch_shapes=[pltpu.VMEM(w_gate.shape[1:], BF16), pltpu.VMEM(w_up.shape[1:], BF16),
                        pltpu.VMEM(w_down.shape[1:], BF16)],
    )
    return pl.pallas_call(
        _experts_kernel,
        grid_spec=grid_spec,
        out_shape=jax.ShapeDtypeStruct((n_rows, d), F32),
        compiler_params=_cparams("arbitrary"),
        name="experts",
    )(blk_e, n_used, buf, w_gate, w_up, w_down)


def _gather_sc(dest, eo):
    n_idx = dest.size
    d = eo.shape[1]
    workers = SC_CORES * SC_SUBCORES
    n_win = n_idx // (workers * SC_WINDOW)
    idx = dest.reshape(n_idx // SC_WINDOW, SC_WINDOW)
    mesh = plsc.VectorSubcoreMesh(core_axis_name="c", subcore_axis_name="s")

    def body(eo_hbm, idx_hbm, o_hbm, rows_v, i_v):
        wid = lax.axis_index("c") * SC_SUBCORES + lax.axis_index("s")

        @pl.loop(0, n_win)
        def _(j):
            blk = wid * n_win + j
            pltpu.sync_copy(idx_hbm.at[blk], i_v)
            pltpu.sync_copy(eo_hbm.at[i_v], rows_v)
            pltpu.sync_copy(rows_v, o_hbm.at[pl.ds(blk * SC_WINDOW, SC_WINDOW)])

    return pl.kernel(
        body,
        out_type=jax.ShapeDtypeStruct((n_idx, d), eo.dtype),
        mesh=mesh,
        scratch_types=[pltpu.VMEM((SC_WINDOW, d), eo.dtype), pltpu.VMEM((SC_WINDOW,), jnp.int32)],
        name="gather_sc",
    )(eo, idx)


def _combine_kernel(x1_ref, route_ref, rows_ref, o_ref):
    d = o_ref.shape[1]
    route = route_ref[...]
    o_ref[...] = x1_ref[...] + (route[:, 2:3] * rows_ref[:, :d] + route[:, 3:4] * rows_ref[:, d:])


def _combine(x1, route, rows2):
    n, d = x1.shape
    tm = COMBINE_TM
    return pl.pallas_call(
        _combine_kernel,
        grid=(n // tm,),
        in_specs=[pl.BlockSpec((tm, d), lambda i: (i, 0)),
                  pl.BlockSpec((tm, LANES), lambda i: (i, 0)),
                  pl.BlockSpec((tm, 2 * d), lambda i: (i, 0))],
        out_specs=pl.BlockSpec((tm, d), lambda i: (i, 0)),
        out_shape=jax.ShapeDtypeStruct((n, d), F32),
        compiler_params=_cparams("parallel"),
        name="combine",
    )(x1, route, rows2)


def _layer(x, g_mix, w_in, lam_re, lam_im, log_dt, b_re, b_im, c_re, c_im, d_skip, w_glu, b_glu, g_q, g_k,
           g_ssm_out, g_attn_out, w_out, g_ffn, w_rg, b_rg, w_re, b_re_router, w_gate, w_up, w_down):
    bsz, t_len, d = x.shape
    n = bsz * t_len
    d_ssm = w_glu.shape[0]
    d_att = g_attn_out.shape[0]
    n_heads = d_att // HEAD_DIM
    n_chunks = t_len // SSM_CHUNK
    x2 = x.reshape(n, d)

    u_t, q, k, v = _in_proj(x2, g_mix[None], w_in.astype(BF16), jnp.tile(g_q, n_heads)[None],
                            jnp.tile(g_k, n_heads)[None], d_ssm, d_att, bsz, t_len)
    tables = _s5_tables(lam_re, lam_im, log_dt, b_re, b_im, c_re, c_im, d_skip)
    y_ssm_t = _s5(u_t.reshape(n_chunks, SSM_CHUNK, bsz, d_ssm), tables).reshape(t_len, bsz * d_ssm)

    y_att = _attention(q, k, v, bsz, t_len)

    w_r = jnp.concatenate([w_rg, w_re.reshape(d, N_EXPERTS)], axis=1)
    w_r = jnp.pad(w_r, ((0, 0), (0, LANES - w_r.shape[1]))).astype(BF16)
    b_r = jnp.pad(jnp.concatenate([b_rg, b_re_router.reshape(N_EXPERTS)]), (0, LANES - ROUTER_LANE0 - N_EXPERTS))[None]
    x1, h2p, route, cnt = _post(x2, y_ssm_t, y_att, w_glu.astype(BF16), b_glu[None], g_ssm_out[None],
                                g_attn_out[None], w_out.astype(BF16), g_ffn[None], w_r, b_r)

    counts = cnt[0, ROUTER_LANE0:ROUTER_LANE0 + N_EXPERTS].astype(jnp.int32)
    pcounts = ((counts + MOE_ROWS - 1) // MOE_ROWS) * MOE_ROWS
    pends = jnp.cumsum(pcounts)
    pstarts = pends - pcounts
    expert = route[:, 0:2].astype(jnp.int32)
    e_ids = jnp.arange(N_EXPERTS, dtype=jnp.int32)
    start = jnp.sum(jnp.where(expert[..., None] == e_ids, pstarts, 0), axis=-1)
    dest = start + route[:, 4:6].astype(jnp.int32)
    n_rows = n * 2 + N_EXPERTS * MOE_ROWS
    n_blk = n_rows // MOE_ROWS
    blk_row0 = jnp.arange(n_blk, dtype=jnp.int32)[:, None] * MOE_ROWS
    blk_e = jnp.minimum(jnp.sum((pends[None, :] <= blk_row0).astype(jnp.int32), axis=1), N_EXPERTS - 1)
    n_used = (pends[-1:] // MOE_ROWS).astype(jnp.int32)

    buf = _dispatch_sc(dest, h2p, n_rows)
    eo = _experts(blk_e, n_used, buf, w_gate, w_up, w_down)
    out = _combine(x1, route, _gather_sc(dest, eo).reshape(n, 2 * d))
    return out.reshape(bsz, t_len, d)


def kernel(x, g_mix, w_in, ssm_lambda_re, ssm_lambda_im, ssm_log_dt, ssm_b_re, ssm_b_im, ssm_c_re, ssm_c_im, ssm_d, ssm_w_glu, ssm_b_glu, g_q, g_k, g_ssm_out, g_attn_out, w_out, g_ffn, w_router_group, b_router_group, w_router_expert, b_router_expert, w_gate, w_up, w_down):
    for l in range(g_mix.shape[0]):
        x = _layer(x, g_mix[l], w_in[l], ssm_lambda_re[l], ssm_lambda_im[l], ssm_log_dt[l], ssm_b_re[l],
                   ssm_b_im[l], ssm_c_re[l], ssm_c_im[l], ssm_d[l], ssm_w_glu[l], ssm_b_glu[l], g_q[l], g_k[l],
                   g_ssm_out[l], g_attn_out[l], w_out[l], g_ffn[l], w_router_group[l], b_router_group[l],
                   w_router_expert[l], b_router_expert[l], w_gate[l], w_up[l], w_down[l])
    return x
```

```python
import functools
import math

import jax
import jax.numpy as jnp
from jax import lax
from jax.experimental import pallas as pl
from jax.experimental.pallas import tpu as pltpu
from jax.experimental.pallas import tpu_sc as plsc

F32 = jnp.float32
BF16 = jnp.bfloat16
EPS = 1e-6

LANES = 128
VMEM_LIMIT_BYTES = 56 * 1024 * 1024

SSM_GROUP = 16
SSM_STATE = 64
SSM_CHUNK = 16
HEAD_DIM = 64
N_EXPERT_GROUPS = 4
EXPERTS_PER_GROUP = 8
N_EXPERTS = N_EXPERT_GROUPS * EXPERTS_PER_GROUP
ROUTER_LANE0 = N_EXPERT_GROUPS
MOE_ROWS = 512
ATT_SKIP = 110.0

S5_GPB = LANES // SSM_GROUP
S5_CHUNKS_PER_STEP = 8

IN_TM = 512
ATT_TILE = 256
ATT_KEYS_AHEAD = 256
POST_TM = 512
COMBINE_TM = 512
SC_CORES = 2
SC_SUBCORES = 16
SC_WINDOW = 64


def _cparams(*sem):
    return pltpu.CompilerParams(dimension_semantics=sem, vmem_limit_bytes=VMEM_LIMIT_BYTES)


def _lane_iota(shape):
    return lax.broadcasted_iota(jnp.int32, shape, len(shape) - 1)


def _head_rms(t, gain):
    outs = []
    for c in range(t.shape[1] // LANES):
        blk = t[:, c * LANES:(c + 1) * LANES]
        sq = blk * blk
        lo = _lane_iota(blk.shape) < HEAD_DIM
        s_lo = jnp.sum(jnp.where(lo, sq, 0.0), axis=-1, keepdims=True)
        s_hi = jnp.sum(jnp.where(lo, 0.0, sq), axis=-1, keepdims=True)
        inv = jnp.where(lo, lax.rsqrt(s_lo * (1.0 / HEAD_DIM) + EPS),
                        lax.rsqrt(s_hi * (1.0 / HEAD_DIM) + EPS))
        outs.append(blk * inv * gain[:, c * LANES:(c + 1) * LANES])
    return jnp.concatenate(outs, axis=-1)


def _in_proj_kernel(x_ref, g_ref, w_ref, gq_ref, gk_ref, u_ref, q_ref, k_ref, v_ref, *, d_ssm, d_att, scale):
    x = x_ref[...]
    inv = lax.rsqrt(jnp.mean(x * x, axis=-1, keepdims=True) + EPS)
    h = (x * inv * g_ref[...]).astype(BF16)
    proj = jnp.dot(h, w_ref[...], preferred_element_type=F32)
    u_ref[...] = proj[:, :d_ssm].astype(BF16)
    q = _head_rms(proj[:, d_ssm:d_ssm + d_att], gq_ref[...])
    k = _head_rms(proj[:, d_ssm + d_att:d_ssm + 2 * d_att], gk_ref[...])
    q_ref[...] = (q * scale).astype(BF16)
    k_ref[...] = k.astype(BF16)
    v_ref[...] = proj[:, d_ssm + 2 * d_att:].astype(BF16)


def _in_proj(x2, g_mix, w_in_bf, gq_t, gk_t, d_ssm, d_att, bsz, t_len):
    n, d = x2.shape
    tm = IN_TM
    nt = t_len // tm
    row = lambda w: pl.BlockSpec((tm, w), lambda b, t: (b * nt + t, 0))
    full = lambda a: pl.BlockSpec(a.shape, lambda b, t: (0,) * a.ndim)
    out_sd = jax.ShapeDtypeStruct((n, d_att), BF16)
    return pl.pallas_call(
        functools.partial(_in_proj_kernel, d_ssm=d_ssm, d_att=d_att, scale=1.0 / math.sqrt(HEAD_DIM)),
        grid=(bsz, nt),
        in_specs=[row(d), full(g_mix), full(w_in_bf), full(gq_t), full(gk_t)],
        out_specs=[pl.BlockSpec((tm, d_ssm), lambda b, t: (t, b)), row(d_att), row(d_att), row(d_att)],
        out_shape=[jax.ShapeDtypeStruct((t_len, bsz * d_ssm), BF16), out_sd, out_sd, out_sd],
        compiler_params=_cparams("parallel", "parallel"),
        name="in_proj",
    )(x2, g_mix, w_in_bf, gq_t, gk_t)


def _s5_tables(lam_re, lam_im, log_dt, b_re, b_im, c_re, c_im, d_skip):
    hp = lax.Precision.HIGHEST
    L = SSM_CHUNK
    g_n, p_n = lam_re.shape
    dt = jnp.exp(log_dt)[:, None]
    lr, li = lam_re, lam_im
    ls = jnp.arange(L + 1, dtype=F32)[:, None, None]
    mag = jnp.exp(lr * dt * ls)
    pr, pi = mag * jnp.cos(li * dt * ls), mag * jnp.sin(li * dt * ls)
    abar_r, abar_i = pr[1], pi[1]
    den = lr * lr + li * li
    nr, ni = abar_r - 1.0, abar_i
    coef_r = (nr * lr + ni * li) / den
    coef_i = (ni * lr - nr * li) / den
    bbr = coef_r[..., None] * b_re - coef_i[..., None] * b_im
    bbi = coef_r[..., None] * b_im + coef_i[..., None] * b_re
    wr = pr[..., None] * bbr - pi[..., None] * bbi
    wi = pr[..., None] * bbi + pi[..., None] * bbr
    kl = (jnp.einsum('gop,lgpi->lgoi', c_re, wr[:L], precision=hp)
          - jnp.einsum('gop,lgpi->lgoi', c_im, wi[:L], precision=hp))
    kl = kl.at[0].add(jax.vmap(jnp.diag)(d_skip))
    n_lb = g_n // S5_GPB
    eye = jnp.eye(S5_GPB, dtype=F32)
    blk = lambda a: a.reshape(a.shape[0], n_lb, S5_GPB, *a.shape[2:])
    kbd = jnp.einsum('xy,dlxoi->ldxiyo', eye, blk(kl)).reshape(n_lb, L, LANES, LANES)
    top = kbd.transpose(0, 2, 1, 3).reshape(n_lb, LANES, L * LANES)
    bot = jnp.concatenate([jnp.zeros((n_lb, LANES, LANES), F32), top[:, :, :-LANES]], axis=2)
    w0 = jnp.concatenate([top, bot], axis=1)
    bst_r = jnp.einsum('xy,slxpi->lsxiyp', eye, blk(wr[:L][::-1])).reshape(n_lb, L * LANES, S5_GPB * p_n)
    bst_i = jnp.einsum('xy,slxpi->lsxiyp', eye, blk(wi[:L][::-1])).reshape(n_lb, L * LANES, S5_GPB * p_n)
    p1r, p1i = pr[1:], pi[1:]
    cst_r = (c_re[None] * p1r[:, :, None, :] - c_im[None] * p1i[:, :, None, :])
    cst_i = -(c_re[None] * p1i[:, :, None, :] + c_im[None] * p1r[:, :, None, :])
    cst_r = jnp.einsum('xy,tlxop->lxptyo', eye, blk(cst_r)).reshape(n_lb, S5_GPB * p_n, L * LANES)
    cst_i = jnp.einsum('xy,tlxop->lxptyo', eye, blk(cst_i)).reshape(n_lb, S5_GPB * p_n, L * LANES)
    a_l = jnp.stack([pr[L].reshape(n_lb, S5_GPB * p_n), pi[L].reshape(n_lb, S5_GPB * p_n)], axis=1)
    return (w0.astype(BF16), bst_r.astype(BF16), bst_i.astype(BF16),
            cst_r.astype(BF16), cst_i.astype(BF16), a_l)


def _s5_kernel(u_ref, w0_ref, br_ref, bi_ref, cr_ref, ci_ref, a_ref, y_ref, hr_ref, hi_ref, acc_ref):
    n_chunks, L, bsz, _ = u_ref.shape
    rows = n_chunks * bsz

    @pl.when(pl.program_id(1) == 0)
    def _():
        hr_ref[...] = jnp.zeros_like(hr_ref)
        hi_ref[...] = jnp.zeros_like(hi_ref)

    us = [u_ref[:, s].reshape(rows, LANES) for s in range(L)]
    lhs = jnp.concatenate(us, axis=1)
    sin_r = jnp.dot(lhs, br_ref[0], preferred_element_type=F32)
    sin_i = jnp.dot(lhs, bi_ref[0], preferred_element_type=F32)
    ar = a_ref[0, 0:1, :]
    ai = a_ref[0, 1:2, :]
    hr, hi = hr_ref[...], hi_ref[...]
    prev_r, prev_i = [], []
    for c in range(n_chunks):
        prev_r.append(hr)
        prev_i.append(hi)
        sl = slice(c * bsz, (c + 1) * bsz)
        hr, hi = ar * hr - ai * hi + sin_r[sl], ar * hi + ai * hr + sin_i[sl]
    hr_ref[...] = hr
    hi_ref[...] = hi
    pr = jnp.concatenate(prev_r, axis=0).astype(BF16)
    pi = jnp.concatenate(prev_i, axis=0).astype(BF16)
    acc_ref[...] = (jnp.dot(pr, cr_ref[0], preferred_element_type=F32)
                    + jnp.dot(pi, ci_ref[0], preferred_element_type=F32))
    for p in range(L // 2):
        off = 2 * p * LANES
        pair = jnp.concatenate([us[2 * p], us[2 * p + 1]], axis=1)
        acc_ref[:, off:] += jnp.dot(pair, w0_ref[0, :, :L * LANES - off], preferred_element_type=F32)
    for t in range(L):
        y_ref[:, t] = acc_ref[:, t * LANES:(t + 1) * LANES].reshape(n_chunks, bsz, LANES)


def _s5(u4, tables):
    n_chunks, L, bsz, d_ssm = u4.shape
    w0, bst_r, bst_i, cst_r, cst_i, a_l = tables
    cb = S5_CHUNKS_PER_STEP
    data = pl.BlockSpec((cb, L, bsz, LANES), lambda lb, c: (c, 0, 0, lb))
    per_lb = lambda a: pl.BlockSpec((1,) + a.shape[1:], lambda lb, c: (lb,) + (0,) * (a.ndim - 1))
    state = pltpu.VMEM((bsz, a_l.shape[2]), F32)
    return pl.pallas_call(
        _s5_kernel,
        grid=(d_ssm // LANES, n_chunks // cb),
        in_specs=[data, per_lb(w0), per_lb(bst_r), per_lb(bst_i), per_lb(cst_r), per_lb(cst_i), per_lb(a_l)],
        out_specs=data,
        out_shape=jax.ShapeDtypeStruct(u4.shape, F32),
        scratch_shapes=[state, state, pltpu.VMEM((cb * bsz, L * LANES), F32)],
        compiler_params=_cparams("arbitrary", "arbitrary"),
        name="s5",
    )(u4, w0, bst_r, bst_i, cst_r, cst_i, a_l)


def _softplus(z):
    return jnp.maximum(z, 0.0) + jnp.log(1.0 + jnp.exp(-jnp.abs(z)))


def _att_tile(qm, k, v, tri, r_in, causal):
    z = lax.dot_general(qm, k, (((1,), (1,)), ((), ())), preferred_element_type=F32)
    sp = _softplus(z)
    if causal:
        rows = lax.broadcasted_iota(jnp.int32, z.shape, 0)
        cols = lax.broadcasted_iota(jnp.int32, z.shape, 1)
        keep = cols < rows
        sp_m = jnp.where(keep, sp, 0.0)
    else:
        sp_m = sp
    newer = jnp.dot(sp_m.astype(BF16), tri, preferred_element_type=F32)
    att = jnp.exp(z - sp - newer - r_in)
    if causal:
        att = jnp.where(keep, att, 0.0)
    pv = jnp.dot(att.astype(BF16), v, preferred_element_type=F32)
    return pv, newer[:, 0:1] + sp_m[:, 0:1]


def _attn_kernel(q_ref, k_ref, v_ref, o_ref, *, t_len, tile):
    nq = t_len // tile
    r_i = lax.broadcasted_iota(jnp.int32, (tile, tile), 0)
    c_i = lax.broadcasted_iota(jnp.int32, (tile, tile), 1)
    tri = jnp.where(r_i > c_i, 1.0, 0.0).astype(BF16)
    head0 = _lane_iota((tile, LANES)) < HEAD_DIM
    zero_r = jnp.zeros((tile, 1), F32)

    def q_tile(q0, n_prev, j_older):
        q = q_ref[pl.ds(q0, tile), :]
        zq = jnp.zeros_like(q)
        qms = (jnp.where(head0, q, zq), jnp.where(head0, zq, q))
        kvs = [(k_ref[pl.ds(q0 - p * tile, tile), :], v_ref[pl.ds(q0 - p * tile, tile), :])
               for p in range(n_prev + 1)]
        accs, rs = [], []
        for qm in qms:
            acc, r = _att_tile(qm, kvs[0][0], kvs[0][1], tri, zero_r, causal=True)
            for k_p, v_p in kvs[1:]:
                pv, dr = _att_tile(qm, k_p, v_p, tri, r, causal=False)
                acc, r = acc + pv, r + dr
            accs.append(acc)
            rs.append(r)
        if j_older is not None:
            def cond(c):
                j, _, _, r0, r1 = c
                return jnp.logical_and(j >= 0, jnp.min(jnp.minimum(r0, r1)) < ATT_SKIP)

            def body(c):
                j, a0, a1, r0, r1 = c
                k0 = pl.multiple_of(j * tile, tile)
                k_j, v_j = k_ref[pl.ds(k0, tile), :], v_ref[pl.ds(k0, tile), :]
                pv0, d0 = _att_tile(qms[0], k_j, v_j, tri, r0, causal=False)
                pv1, d1 = _att_tile(qms[1], k_j, v_j, tri, r1, causal=False)
                return j - 1, a0 + pv0, a1 + pv1, r0 + d0, r1 + d1

            _, a0, a1, _, _ = lax.while_loop(cond, body, (j_older, accs[0], accs[1], rs[0], rs[1]))
            accs = [a0, a1]
        o_ref[pl.ds(q0, tile), :] = jnp.where(head0, accs[0], accs[1])

    n_prev = ATT_KEYS_AHEAD // tile
    for i in range(n_prev):
        q_tile(i * tile, i, None)

    def later(i, _):
        q_tile(pl.multiple_of(i * tile, tile), n_prev, i - n_prev - 1)
        return 0

    lax.fori_loop(n_prev, nq, later, 0)


def _attention(q, k, v, bsz, t_len):
    n, d_att = q.shape
    spec = pl.BlockSpec((t_len, LANES), lambda b, p: (b, p))
    return pl.pallas_call(
        functools.partial(_attn_kernel, t_len=t_len, tile=ATT_TILE),
        grid=(bsz, d_att // LANES),
        in_specs=[spec, spec, spec],
        out_specs=spec,
        out_shape=jax.ShapeDtypeStruct((n, d_att), F32),
        compiler_params=_cparams("parallel", "parallel"),
        name="attn",
    )(q, k, v)


def _rms(t, gain):
    return t * lax.rsqrt(jnp.mean(t * t, axis=-1, keepdims=True) + EPS) * gain


def _gelu_tanh(y):
    return 0.5 * y * (1.0 + jnp.tanh(math.sqrt(2.0 / math.pi) * (y + 0.044715 * (y * y * y))))


def _pack_bf16_pairs(a, b):
    ua = pltpu.bitcast(a.astype(BF16).astype(F32), jnp.uint32)
    ub = pltpu.bitcast(b.astype(BF16).astype(F32), jnp.uint32)
    return ua | (ub >> 16)


def _unpack_bf16_pairs(w):
    return pltpu.bitcast(w & jnp.uint32(0xFFFF0000), F32), pltpu.bitcast(w << 16, F32)


def _post_kernel(x_ref, ys_ref, ya_ref, wglu_ref, bglu_ref, gs_ref, ga_ref, wo_ref, gf_ref,
                 wr_ref, br_ref, x1_ref, h2_ref, route_ref, cnt_ref, run_ref, *, d_ssm):
    i = pl.program_id(0)

    @pl.when(i == 0)
    def _():
        run_ref[...] = jnp.zeros_like(run_ref)

    y = _gelu_tanh(ys_ref[...])
    gate = jnp.dot(y.astype(BF16), wglu_ref[...], preferred_element_type=F32) + bglu_ref[...]
    y = y * jax.nn.sigmoid(gate)
    m_s = _rms(y, gs_ref[...]).astype(BF16)
    m_a = _rms(ya_ref[...], ga_ref[...]).astype(BF16)
    mix = (jnp.dot(m_s, wo_ref[:d_ssm, :], preferred_element_type=F32)
           + jnp.dot(m_a, wo_ref[d_ssm:, :], preferred_element_type=F32))
    x1 = x_ref[...] + mix
    x1_ref[...] = x1
    h2 = _rms(x1, gf_ref[...])
    half = h2.shape[1] // 2
    h2_ref[...] = _pack_bf16_pairs(h2[:, :half], h2[:, half:])

    lg = jnp.dot(h2.astype(BF16), wr_ref[...], preferred_element_type=F32) + br_ref[...]
    tm = lg.shape[0]
    lane = _lane_iota(lg.shape).astype(F32)
    neg = -jnp.inf
    first = lambda hit: jnp.min(jnp.where(hit, lane, float(LANES)), axis=-1, keepdims=True)
    glog = jnp.where(lane < N_EXPERT_GROUPS, lg, neg)
    gmax = jnp.max(glog, axis=-1, keepdims=True)
    p_grp = 1.0 / jnp.sum(jnp.exp(glog - gmax), axis=-1, keepdims=True)
    grp = first(glog == gmax)
    e0 = ROUTER_LANE0 + grp * EXPERTS_PER_GROUP
    elog = jnp.where(jnp.logical_and(lane >= e0, lane < e0 + EXPERTS_PER_GROUP), lg, neg)
    m1 = jnp.max(elog, axis=-1, keepdims=True)
    i1 = first(elog == m1)
    elog2 = jnp.where(lane == i1, neg, elog)
    m2 = jnp.max(elog2, axis=-1, keepdims=True)
    i2 = first(elog2 == m2)
    e21 = jnp.exp(m2 - m1)
    g1 = p_grp * (1.0 / (1.0 + e21))
    g2 = p_grp * (e21 / (1.0 + e21))

    sel1 = lane == i1
    sel2 = lane == i2
    onehot = jnp.where(jnp.logical_or(sel1, sel2), 1.0, 0.0)
    r_i = lax.broadcasted_iota(jnp.int32, (tm, tm), 0)
    c_i = lax.broadcasted_iota(jnp.int32, (tm, tm), 1)
    lower = jnp.where(c_i < r_i, 1.0, 0.0).astype(BF16)
    before = jnp.dot(lower, onehot.astype(BF16), preferred_element_type=F32) + run_ref[0:1, :]
    rank1 = jnp.sum(jnp.where(sel1, before, 0.0), axis=-1, keepdims=True)
    rank2 = jnp.sum(jnp.where(sel2, before, 0.0), axis=-1, keepdims=True)
    run_ref[0:1, :] = run_ref[0:1, :] + jnp.sum(onehot, axis=0, keepdims=True)
    cnt_ref[...] = run_ref[...]

    fields = (i1 - ROUTER_LANE0, i2 - ROUTER_LANE0, g1, g2, rank1, rank2)
    route = jnp.zeros(lg.shape, F32)
    for pos, val in enumerate(fields):
        route = jnp.where(lane == pos, val, route)
    route_ref[...] = route


def _post(x2, y_ssm_t, y_att, w_glu_bf, b_glu, g_ssm, g_att, w_out_bf, g_ffn, w_r_bf, b_r):
    n, d = x2.shape
    d_ssm = w_glu_bf.shape[0]
    tm = POST_TM
    nt = y_ssm_t.shape[0] // tm
    row = lambda w: pl.BlockSpec((tm, w), lambda i: (i, 0))
    ssm_spec = pl.BlockSpec((tm, d_ssm), lambda i: (i % nt, i // nt))
    full = lambda a: pl.BlockSpec(a.shape, lambda i: (0,) * a.ndim)
    cnt_spec = pl.BlockSpec((8, LANES), lambda i: (0, 0))
    return pl.pallas_call(
        functools.partial(_post_kernel, d_ssm=d_ssm),
        grid=(n // tm,),
        in_specs=[row(d), ssm_spec, row(y_att.shape[1]), full(w_glu_bf), full(b_glu), full(g_ssm),
                  full(g_att), full(w_out_bf), full(g_ffn), full(w_r_bf), full(b_r)],
        out_specs=[row(d), row(d // 2), row(LANES), cnt_spec],
        out_shape=[jax.ShapeDtypeStruct((n, d), F32), jax.ShapeDtypeStruct((n, d // 2), jnp.uint32),
                   jax.ShapeDtypeStruct((n, LANES), F32), jax.ShapeDtypeStruct((8, LANES), F32)],
        scratch_shapes=[pltpu.VMEM((8, LANES), F32)],
        compiler_params=_cparams("arbitrary"),
        name="post",
    )(x2, y_ssm_t, y_att, w_glu_bf, b_glu, g_ssm, g_att, w_out_bf, g_ffn, w_r_bf, b_r)


def _dispatch_sc(dest, h2p, n_rows):
    n, w = h2p.shape
    workers = SC_CORES * SC_SUBCORES
    n_win = n // (workers * SC_WINDOW)
    d0 = dest[:, 0].reshape(n // SC_WINDOW, SC_WINDOW)
    d1 = dest[:, 1].reshape(n // SC_WINDOW, SC_WINDOW)
    mesh = plsc.VectorSubcoreMesh(core_axis_name="c", subcore_axis_name="s")

    def body(h_hbm, d0_hbm, d1_hbm, o_hbm, rows_v, i0_v, i1_v):
        wid = lax.axis_index("c") * SC_SUBCORES + lax.axis_index("s")

        @pl.loop(0, n_win)
        def _(j):
            blk = wid * n_win + j
            pltpu.sync_copy(h_hbm.at[pl.ds(blk * SC_WINDOW, SC_WINDOW)], rows_v)
            pltpu.sync_copy(d0_hbm.at[blk], i0_v)
            pltpu.sync_copy(d1_hbm.at[blk], i1_v)
            pltpu.sync_copy(rows_v, o_hbm.at[i0_v])
            pltpu.sync_copy(rows_v, o_hbm.at[i1_v])

    return pl.kernel(
        body,
        out_type=jax.ShapeDtypeStruct((n_rows, w), h2p.dtype),
        mesh=mesh,
        scratch_types=[pltpu.VMEM((SC_WINDOW, w), h2p.dtype), pltpu.VMEM((SC_WINDOW,), jnp.int32),
                       pltpu.VMEM((SC_WINDOW,), jnp.int32)],
        name="dispatch_sc",
    )(h2p, d0, d1)


def _experts_kernel(blk_e_ref, used_ref, x_ref, wg_ref, wu_ref, wd_ref, o_ref, wg_bf, wu_bf, wd_bf):
    i = pl.program_id(0)

    @pl.when(jnp.logical_or(i == 0, blk_e_ref[i] != blk_e_ref[jnp.maximum(i - 1, 0)]))
    def _():
        wg_bf[...] = wg_ref[0].astype(BF16)
        wu_bf[...] = wu_ref[0].astype(BF16)
        wd_bf[...] = wd_ref[0].astype(BF16)

    @pl.when(i < used_ref[0])
    def _():
        xa, xb = (t.astype(BF16) for t in _unpack_bf16_pairs(x_ref[...]))
        half = xa.shape[1]
        gate = (jnp.dot(xa, wg_bf[:half, :], preferred_element_type=F32)
                + jnp.dot(xb, wg_bf[half:, :], preferred_element_type=F32))
        up = (jnp.dot(xa, wu_bf[:half, :], preferred_element_type=F32)
              + jnp.dot(xb, wu_bf[half:, :], preferred_element_type=F32))
        hid = (jax.nn.silu(gate) * up).astype(BF16)
        out = jnp.dot(hid, wd_bf[...], preferred_element_type=F32)
        o_ref[...] = _pack_bf16_pairs(out[:, :half], out[:, half:])

    @pl.when(i >= used_ref[0])
    def _():
        o_ref[...] = jnp.zeros_like(o_ref)


def _experts(blk_e, n_used, buf, w_gate, w_up, w_down):
    n_rows, w = buf.shape
    d = w_down.shape[2]
    wspec = lambda a: pl.BlockSpec((1,) + a.shape[1:], lambda i, be, nu: (be[i], 0, 0))
    grid_spec = pltpu.PrefetchScalarGridSpec(
        num_scalar_prefetch=2,
        grid=(n_rows // MOE_ROWS,),
        in_specs=[pl.BlockSpec((MOE_ROWS, w), lambda i, be, nu: (i, 0)), wspec(w_gate), wspec(w_up), wspec(w_down)],
        out_specs=pl.BlockSpec((MOE_ROWS, d // 2), lambda i, be, nu: (i, 0)),
        scratch_shapes=[pltpu.VMEM(w_gate.shape[1:], BF16), pltpu.VMEM(w_up.shape[1:], BF16),
                        pltpu.VMEM(w_down.shape[1:], BF16)],
    )
    return pl.pallas_call(
        _experts_kernel,
        grid_spec=grid_spec,
        out_shape=jax.ShapeDtypeStruct((n_rows, d // 2), jnp.uint32),
        compiler_params=_cparams("arbitrary"),
        name="experts",
    )(blk_e, n_used, buf, w_gate, w_up, w_down)


def _gather_sc(dest, eo):
    n = dest.shape[0]
    w = eo.shape[1]
    workers = SC_CORES * SC_SUBCORES
    n_win = n // (workers * SC_WINDOW)
    d0 = dest[:, 0].reshape(n // SC_WINDOW, SC_WINDOW)
    d1 = dest[:, 1].reshape(n // SC_WINDOW, SC_WINDOW)
    mesh = plsc.VectorSubcoreMesh(core_axis_name="c", subcore_axis_name="s")

    def body(eo_hbm, d0_hbm, d1_hbm, o0_hbm, o1_hbm, rows_v, i_v):
        wid = lax.axis_index("c") * SC_SUBCORES + lax.axis_index("s")

        @pl.loop(0, n_win)
        def _(j):
            blk = wid * n_win + j
            for d_hbm, o_hbm in ((d0_hbm, o0_hbm), (d1_hbm, o1_hbm)):
                pltpu.sync_copy(d_hbm.at[blk], i_v)
                pltpu.sync_copy(eo_hbm.at[i_v], rows_v)
                pltpu.sync_copy(rows_v, o_hbm.at[pl.ds(blk * SC_WINDOW, SC_WINDOW)])

    out_sd = jax.ShapeDtypeStruct((n, w), eo.dtype)
    return pl.kernel(
        body,
        out_type=(out_sd, out_sd),
        mesh=mesh,
        scratch_types=[pltpu.VMEM((SC_WINDOW, w), eo.dtype), pltpu.VMEM((SC_WINDOW,), jnp.int32)],
        name="gather_sc",
    )(eo, d0, d1)


def _combine_kernel(x1_ref, route_ref, r0_ref, r1_ref, o_ref):
    route = route_ref[...]
    row0 = jnp.concatenate(_unpack_bf16_pairs(r0_ref[...]), axis=1)
    row1 = jnp.concatenate(_unpack_bf16_pairs(r1_ref[...]), axis=1)
    o_ref[...] = x1_ref[...] + (route[:, 2:3] * row0 + route[:, 3:4] * row1)


def _combine(x1, route, rows0, rows1):
    n, d = x1.shape
    tm = COMBINE_TM
    row = lambda w: pl.BlockSpec((tm, w), lambda i: (i, 0))
    return pl.pallas_call(
        _combine_kernel,
        grid=(n // tm,),
        in_specs=[row(d), row(LANES), row(d // 2), row(d // 2)],
        out_specs=row(d),
        out_shape=jax.ShapeDtypeStruct((n, d), F32),
        compiler_params=_cparams("parallel"),
        name="combine",
    )(x1, route, rows0, rows1)


def _layer(x, g_mix, w_in, lam_re, lam_im, log_dt, b_re, b_im, c_re, c_im, d_skip, w_glu, b_glu, g_q, g_k,
           g_ssm_out, g_attn_out, w_out, g_ffn, w_rg, b_rg, w_re, b_re_router, w_gate, w_up, w_down):
    bsz, t_len, d = x.shape
    n = bsz * t_len
    d_ssm = w_glu.shape[0]
    d_att = g_attn_out.shape[0]
    n_heads = d_att // HEAD_DIM
    n_chunks = t_len // SSM_CHUNK
    x2 = x.reshape(n, d)

    u_t, q, k, v = _in_proj(x2, g_mix[None], w_in.astype(BF16), jnp.tile(g_q, n_heads)[None],
                            jnp.tile(g_k, n_heads)[None], d_ssm, d_att, bsz, t_len)
    tables = _s5_tables(lam_re, lam_im, log_dt, b_re, b_im, c_re, c_im, d_skip)
    y_ssm_t = _s5(u_t.reshape(n_chunks, SSM_CHUNK, bsz, d_ssm), tables).reshape(t_len, bsz * d_ssm)

    y_att = _attention(q, k, v, bsz, t_len)

    w_r = jnp.concatenate([w_rg, w_re.reshape(d, N_EXPERTS)], axis=1)
    w_r = jnp.pad(w_r, ((0, 0), (0, LANES - w_r.shape[1]))).astype(BF16)
    b_r = jnp.pad(jnp.concatenate([b_rg, b_re_router.reshape(N_EXPERTS)]), (0, LANES - ROUTER_LANE0 - N_EXPERTS))[None]
    x1, h2p, route, cnt = _post(x2, y_ssm_t, y_att, w_glu.astype(BF16), b_glu[None], g_ssm_out[None],
                                g_attn_out[None], w_out.astype(BF16), g_ffn[None], w_r, b_r)

    counts = cnt[0, ROUTER_LANE0:ROUTER_LANE0 + N_EXPERTS].astype(jnp.int32)
    pcounts = ((counts + MOE_ROWS - 1) // MOE_ROWS) * MOE_ROWS
    pends = jnp.cumsum(pcounts)
    pstarts = pends - pcounts
    expert = route[:, 0:2].astype(jnp.int32)
    e_ids = jnp.arange(N_EXPERTS, dtype=jnp.int32)
    start = jnp.sum(jnp.where(expert[..., None] == e_ids, pstarts, 0), axis=-1)
    dest = start + route[:, 4:6].astype(jnp.int32)
    n_rows = n * 2 + N_EXPERTS * MOE_ROWS
    n_blk = n_rows // MOE_ROWS
    blk_row0 = jnp.arange(n_blk, dtype=jnp.int32)[:, None] * MOE_ROWS
    blk_e = jnp.minimum(jnp.sum((pends[None, :] <= blk_row0).astype(jnp.int32), axis=1), N_EXPERTS - 1)
    n_used = (pends[-1:] // MOE_ROWS).astype(jnp.int32)

    buf = _dispatch_sc(dest, h2p, n_rows)
    eo = _experts(blk_e, n_used, buf, w_gate, w_up, w_down)
    out = _combine(x1, route, *_gather_sc(dest, eo))
    return out.reshape(bsz, t_len, d)


def kernel(x, g_mix, w_in, ssm_lambda_re, ssm_lambda_im, ssm_log_dt, ssm_b_re, ssm_b_im, ssm_c_re, ssm_c_im, ssm_d, ssm_w_glu, ssm_b_glu, g_q, g_k, g_ssm_out, g_attn_out, w_out, g_ffn, w_router_group, b_router_group, w_router_expert, b_router_expert, w_gate, w_up, w_down):
    for l in range(g_mix.shape[0]):
        x = _layer(x, g_mix[l], w_in[l], ssm_lambda_re[l], ssm_lambda_im[l], ssm_log_dt[l], ssm_b_re[l],
                   ssm_b_im[l], ssm_c_re[l], ssm_c_im[l], ssm_d[l], ssm_w_glu[l], ssm_b_glu[l], g_q[l], g_k[l],
                   g_ssm_out[l], g_attn_out[l], w_out[l], g_ffn[l], w_router_group[l], b_router_group[l],
                   w_router_expert[l], b_router_expert[l], w_gate[l], w_up[l], w_down[l])
    return x
```

```python
import functools
import math

import jax
import jax.numpy as jnp
from jax import lax
from jax.experimental import pallas as pl
from jax.experimental.pallas import tpu as pltpu
from jax.experimental.pallas import tpu_sc as plsc

F32 = jnp.float32
BF16 = jnp.bfloat16
EPS = 1e-6

LANES = 128
VMEM_LIMIT_BYTES = 56 * 1024 * 1024

SSM_GROUP = 16
SSM_STATE = 64
SSM_CHUNK = 16
HEAD_DIM = 64
N_EXPERT_GROUPS = 4
EXPERTS_PER_GROUP = 8
N_EXPERTS = N_EXPERT_GROUPS * EXPERTS_PER_GROUP
ROUTER_LANE0 = N_EXPERT_GROUPS
MOE_ROWS = 512
ATT_SKIP = 110.0

S5_GPB = LANES // SSM_GROUP
S5_CHUNKS_PER_STEP = 8

IN_TM = 512
ATT_TILE = 256
ATT_KEYS_AHEAD = 256
ATT_TILES_PER_STEP = 2
POST_TM = 512
COMBINE_TM = 512
SC_CORES = 2
SC_SUBCORES = 16
SC_WINDOW = 64


def _cparams(*sem):
    return pltpu.CompilerParams(dimension_semantics=sem, vmem_limit_bytes=VMEM_LIMIT_BYTES)


def _lane_iota(shape):
    return lax.broadcasted_iota(jnp.int32, shape, len(shape) - 1)


def _head_rms(t, gain):
    outs = []
    for c in range(t.shape[1] // LANES):
        blk = t[:, c * LANES:(c + 1) * LANES]
        sq = blk * blk
        lo = _lane_iota(blk.shape) < HEAD_DIM
        s_lo = jnp.sum(jnp.where(lo, sq, 0.0), axis=-1, keepdims=True)
        s_hi = jnp.sum(jnp.where(lo, 0.0, sq), axis=-1, keepdims=True)
        inv = jnp.where(lo, lax.rsqrt(s_lo * (1.0 / HEAD_DIM) + EPS),
                        lax.rsqrt(s_hi * (1.0 / HEAD_DIM) + EPS))
        outs.append(blk * inv * gain[:, c * LANES:(c + 1) * LANES])
    return jnp.concatenate(outs, axis=-1)


def _in_proj_kernel(x_ref, g_ref, w_ref, gq_ref, gk_ref, u_ref, q_ref, k_ref, v_ref, *, d_ssm, d_att, scale):
    x = x_ref[...]
    inv = lax.rsqrt(jnp.mean(x * x, axis=-1, keepdims=True) + EPS)
    h = (x * inv * g_ref[...]).astype(BF16)
    proj = jnp.dot(h, w_ref[...], preferred_element_type=F32)
    u_ref[...] = proj[:, :d_ssm].astype(BF16)
    q = _head_rms(proj[:, d_ssm:d_ssm + d_att], gq_ref[...])
    k = _head_rms(proj[:, d_ssm + d_att:d_ssm + 2 * d_att], gk_ref[...])
    q_ref[...] = (q * scale).astype(BF16)
    k_ref[...] = k.astype(BF16)
    v_ref[...] = proj[:, d_ssm + 2 * d_att:].astype(BF16)


def _in_proj(x2, g_mix, w_in_bf, gq_t, gk_t, d_ssm, d_att, bsz, t_len):
    n, d = x2.shape
    tm = IN_TM
    nt = t_len // tm
    row = lambda w: pl.BlockSpec((tm, w), lambda b, t: (b * nt + t, 0))
    full = lambda a: pl.BlockSpec(a.shape, lambda b, t: (0,) * a.ndim)
    out_sd = jax.ShapeDtypeStruct((n, d_att), BF16)
    return pl.pallas_call(
        functools.partial(_in_proj_kernel, d_ssm=d_ssm, d_att=d_att, scale=1.0 / math.sqrt(HEAD_DIM)),
        grid=(bsz, nt),
        in_specs=[row(d), full(g_mix), full(w_in_bf), full(gq_t), full(gk_t)],
        out_specs=[pl.BlockSpec((tm, d_ssm), lambda b, t: (t, b)), row(d_att), row(d_att), row(d_att)],
        out_shape=[jax.ShapeDtypeStruct((t_len, bsz * d_ssm), BF16), out_sd, out_sd, out_sd],
        compiler_params=_cparams("parallel", "parallel"),
        name="in_proj",
    )(x2, g_mix, w_in_bf, gq_t, gk_t)


def _s5_tables(lam_re, lam_im, log_dt, b_re, b_im, c_re, c_im, d_skip):
    hp = lax.Precision.HIGHEST
    L = SSM_CHUNK
    g_n, p_n = lam_re.shape
    dt = jnp.exp(log_dt)[:, None]
    lr, li = lam_re, lam_im
    ls = jnp.arange(L + 1, dtype=F32)[:, None, None]
    mag = jnp.exp(lr * dt * ls)
    pr, pi = mag * jnp.cos(li * dt * ls), mag * jnp.sin(li * dt * ls)
    abar_r, abar_i = pr[1], pi[1]
    den = lr * lr + li * li
    nr, ni = abar_r - 1.0, abar_i
    coef_r = (nr * lr + ni * li) / den
    coef_i = (ni * lr - nr * li) / den
    bbr = coef_r[..., None] * b_re - coef_i[..., None] * b_im
    bbi = coef_r[..., None] * b_im + coef_i[..., None] * b_re
    wr = pr[..., None] * bbr - pi[..., None] * bbi
    wi = pr[..., None] * bbi + pi[..., None] * bbr
    kl = (jnp.einsum('gop,lgpi->lgoi', c_re, wr[:L], precision=hp)
          - jnp.einsum('gop,lgpi->lgoi', c_im, wi[:L], precision=hp))
    kl = kl.at[0].add(jax.vmap(jnp.diag)(d_skip))
    n_lb = g_n // S5_GPB
    sg = SSM_GROUP
    kc = kl.transpose(1, 3, 0, 2).reshape(n_lb, S5_GPB * sg, L * sg)
    b_rows = lambda w: (w[:L][::-1].reshape(L, n_lb, S5_GPB, p_n, sg).transpose(1, 0, 2, 4, 3)
                        .reshape(n_lb, L * S5_GPB * sg, p_n))
    p1r, p1i = pr[1:], pi[1:]
    cst_r = (c_re[None] * p1r[:, :, None, :] - c_im[None] * p1i[:, :, None, :])
    cst_i = -(c_re[None] * p1i[:, :, None, :] + c_im[None] * p1r[:, :, None, :])
    c_rows = lambda c: c.transpose(1, 3, 0, 2).reshape(n_lb, S5_GPB * p_n, L * sg)
    a_l = jnp.stack([pr[L].reshape(n_lb, S5_GPB * p_n), pi[L].reshape(n_lb, S5_GPB * p_n)], axis=1)
    return (kc.astype(BF16), b_rows(wr).astype(BF16), b_rows(wi).astype(BF16),
            c_rows(cst_r).astype(BF16), c_rows(cst_i).astype(BF16), a_l)


def _s5_expand_tables(kc_ref, bcr_ref, bci_ref, ccr_ref, cci_ref, w0_ref, br_ref, bi_ref, cr_ref, ci_ref):
    sg, gpb = SSM_GROUP, S5_GPB
    n_lo = w0_ref.shape[1]
    p_n = bcr_ref.shape[2]

    def iotas(shape):
        return lax.broadcasted_iota(jnp.int32, shape, 0), lax.broadcasted_iota(jnp.int32, shape, 1)

    r, c = iotas((kc_ref.shape[2], n_lo))
    spread_o = jnp.where(jnp.logical_and(r // sg == c // LANES, r % sg == c % sg), 1.0, 0.0).astype(BF16)
    r, c = iotas((p_n, gpb * p_n))
    spread_p = jnp.where(r == c % p_n, 1.0, 0.0).astype(BF16)

    def expand(compact, spread, row_group, col_group):
        full = jnp.dot(compact, spread, preferred_element_type=F32)
        r, c = iotas(full.shape)
        return jnp.where(row_group(r) == col_group(c), full, 0.0).astype(BF16)

    lane_group = lambda c: (c % LANES) // sg
    top = expand(kc_ref[0], spread_o, lambda r: r // sg, lane_group)
    w0_ref[:LANES, :] = top
    w0_ref[LANES:, :LANES] = jnp.zeros((LANES, LANES), BF16)
    w0_ref[LANES:, LANES:] = top[:, :n_lo - LANES]
    state_group = lambda c: c // p_n
    br_ref[...] = expand(bcr_ref[0], spread_p, lambda r: (r // sg) % gpb, state_group)
    bi_ref[...] = expand(bci_ref[0], spread_p, lambda r: (r // sg) % gpb, state_group)
    cr_ref[...] = expand(ccr_ref[0], spread_o, lambda r: r // p_n, lane_group)
    ci_ref[...] = expand(cci_ref[0], spread_o, lambda r: r // p_n, lane_group)


def _s5_kernel(u_ref, kc_ref, bcr_ref, bci_ref, ccr_ref, cci_ref, a_ref, y_ref,
               hr_ref, hi_ref, acc_ref, w0_ref, br_ref, bi_ref, cr_ref, ci_ref):
    n_chunks, L, bsz, _ = u_ref.shape
    rows = n_chunks * bsz

    @pl.when(pl.program_id(1) == 0)
    def _():
        hr_ref[...] = jnp.zeros_like(hr_ref)
        hi_ref[...] = jnp.zeros_like(hi_ref)
        _s5_expand_tables(kc_ref, bcr_ref, bci_ref, ccr_ref, cci_ref, w0_ref, br_ref, bi_ref, cr_ref, ci_ref)

    us = [u_ref[:, s].reshape(rows, LANES) for s in range(L)]
    lhs = jnp.concatenate(us, axis=1)
    sin_r = jnp.dot(lhs, br_ref[...], preferred_element_type=F32)
    sin_i = jnp.dot(lhs, bi_ref[...], preferred_element_type=F32)
    ar = a_ref[0, 0:1, :]
    ai = a_ref[0, 1:2, :]
    hr, hi = hr_ref[...], hi_ref[...]
    prev_r, prev_i = [], []
    for c in range(n_chunks):
        prev_r.append(hr)
        prev_i.append(hi)
        sl = slice(c * bsz, (c + 1) * bsz)
        hr, hi = ar * hr - ai * hi + sin_r[sl], ar * hi + ai * hr + sin_i[sl]
    hr_ref[...] = hr
    hi_ref[...] = hi
    pr = jnp.concatenate(prev_r, axis=0).astype(BF16)
    pi = jnp.concatenate(prev_i, axis=0).astype(BF16)
    acc_ref[...] = (jnp.dot(pr, cr_ref[...], preferred_element_type=F32)
                    + jnp.dot(pi, ci_ref[...], preferred_element_type=F32))
    for p in range(L // 2):
        off = 2 * p * LANES
        pair = jnp.concatenate([us[2 * p], us[2 * p + 1]], axis=1)
        acc_ref[:, off:] += jnp.dot(pair, w0_ref[:, :L * LANES - off], preferred_element_type=F32)
    for t in range(L):
        y_ref[:, t] = acc_ref[:, t * LANES:(t + 1) * LANES].reshape(n_chunks, bsz, LANES)


def _s5(u4, tables):
    n_chunks, L, bsz, d_ssm = u4.shape
    a_l = tables[-1]
    cb = S5_CHUNKS_PER_STEP
    data = pl.BlockSpec((cb, L, bsz, LANES), lambda lb, c: (c, 0, 0, lb))
    per_lb = lambda a: pl.BlockSpec((1,) + a.shape[1:], lambda lb, c: (lb,) + (0,) * (a.ndim - 1))
    n_state = a_l.shape[2]
    state = pltpu.VMEM((bsz, n_state), F32)
    n_lo = L * LANES
    return pl.pallas_call(
        _s5_kernel,
        grid=(d_ssm // LANES, n_chunks // cb),
        in_specs=[data] + [per_lb(t) for t in tables],
        out_specs=data,
        out_shape=jax.ShapeDtypeStruct(u4.shape, F32),
        scratch_shapes=[state, state, pltpu.VMEM((cb * bsz, n_lo), F32),
                        pltpu.VMEM((2 * LANES, n_lo), BF16),
                        pltpu.VMEM((n_lo, n_state), BF16), pltpu.VMEM((n_lo, n_state), BF16),
                        pltpu.VMEM((n_state, n_lo), BF16), pltpu.VMEM((n_state, n_lo), BF16)],
        compiler_params=_cparams("arbitrary", "arbitrary"),
        name="s5",
    )(u4, *tables)


def _softplus(z):
    return jnp.maximum(z, 0.0) + jnp.log(1.0 + jnp.exp(-jnp.abs(z)))


def _att_tile(qm, k, v, tri, r_in, causal):
    z = lax.dot_general(qm, k, (((1,), (1,)), ((), ())), preferred_element_type=F32)
    sp = _softplus(z)
    if causal:
        rows = lax.broadcasted_iota(jnp.int32, z.shape, 0)
        cols = lax.broadcasted_iota(jnp.int32, z.shape, 1)
        keep = cols < rows
        sp_m = jnp.where(keep, sp, 0.0)
    else:
        sp_m = sp
    newer = jnp.dot(sp_m.astype(BF16), tri, preferred_element_type=F32)
    att = jnp.exp(z - sp - newer - r_in)
    if causal:
        att = jnp.where(keep, att, 0.0)
    pv = jnp.dot(att.astype(BF16), v, preferred_element_type=F32)
    return pv, newer[:, 0:1] + sp_m[:, 0:1]


def _attn_kernel(q_ref, k_ref, v_ref, o_ref, *, t_len, tile):
    nq = t_len // tile
    r_i = lax.broadcasted_iota(jnp.int32, (tile, tile), 0)
    c_i = lax.broadcasted_iota(jnp.int32, (tile, tile), 1)
    tri = jnp.where(r_i > c_i, 1.0, 0.0).astype(BF16)
    head0 = _lane_iota((tile, LANES)) < HEAD_DIM
    zero_r = jnp.zeros((tile, 1), F32)

    def near(q0, n_prev):
        q = q_ref[pl.ds(q0, tile), :]
        zq = jnp.zeros_like(q)
        qms = (jnp.where(head0, q, zq), jnp.where(head0, zq, q))
        kvs = [(k_ref[pl.ds(q0 - p * tile, tile), :], v_ref[pl.ds(q0 - p * tile, tile), :])
               for p in range(n_prev + 1)]
        accs, rs = [], []
        for qm in qms:
            acc, r = _att_tile(qm, kvs[0][0], kvs[0][1], tri, zero_r, causal=True)
            for k_p, v_p in kvs[1:]:
                pv, dr = _att_tile(qm, k_p, v_p, tri, r, causal=False)
                acc, r = acc + pv, r + dr
            accs.append(acc)
            rs.append(r)
        return qms, accs, rs

    def far_and_store(q0, qms, accs, rs, j_older):
        if j_older is not None:
            def cond(c):
                j, _, _, r0, r1 = c
                return jnp.logical_and(j >= 0, jnp.min(jnp.minimum(r0, r1)) < ATT_SKIP)

            def body(c):
                j, a0, a1, r0, r1 = c
                k0 = pl.multiple_of(j * tile, tile)
                k_j, v_j = k_ref[pl.ds(k0, tile), :], v_ref[pl.ds(k0, tile), :]
                pv0, d0 = _att_tile(qms[0], k_j, v_j, tri, r0, causal=False)
                pv1, d1 = _att_tile(qms[1], k_j, v_j, tri, r1, causal=False)
                return j - 1, a0 + pv0, a1 + pv1, r0 + d0, r1 + d1

            _, a0, a1, _, _ = lax.while_loop(cond, body, (j_older, accs[0], accs[1], rs[0], rs[1]))
            accs = [a0, a1]
        o_ref[pl.ds(q0, tile), :] = jnp.where(head0, accs[0], accs[1])

    n_prev = ATT_KEYS_AHEAD // tile
    group = ATT_TILES_PER_STEP
    first = n_prev + (nq - n_prev) % group
    for i in range(first):
        far_and_store(i * tile, *near(i * tile, min(i, n_prev)), None if i <= n_prev else i - n_prev - 1)

    def later(g, _):
        tiles = [first + g * group + t for t in range(group)]
        parts = [near(pl.multiple_of(i * tile, tile), n_prev) for i in tiles]
        for i, part in zip(tiles, parts):
            far_and_store(pl.multiple_of(i * tile, tile), *part, i - n_prev - 1)
        return 0

    lax.fori_loop(0, (nq - first) // group, later, 0)


def _attention(q, k, v, bsz, t_len):
    n, d_att = q.shape
    spec = pl.BlockSpec((t_len, LANES), lambda b, p: (b, p))
    return pl.pallas_call(
        functools.partial(_attn_kernel, t_len=t_len, tile=ATT_TILE),
        grid=(bsz, d_att // LANES),
        in_specs=[spec, spec, spec],
        out_specs=spec,
        out_shape=jax.ShapeDtypeStruct((n, d_att), F32),
        compiler_params=_cparams("parallel", "parallel"),
        name="attn",
    )(q, k, v)


def _rms(t, gain):
    return t * lax.rsqrt(jnp.mean(t * t, axis=-1, keepdims=True) + EPS) * gain


def _gelu_tanh(y):
    return 0.5 * y * (1.0 + jnp.tanh(math.sqrt(2.0 / math.pi) * (y + 0.044715 * (y * y * y))))


def _pack_bf16_pairs(a, b):
    ua = pltpu.bitcast(a.astype(BF16).astype(F32), jnp.uint32)
    ub = pltpu.bitcast(b.astype(BF16).astype(F32), jnp.uint32)
    return ua | (ub >> 16)


def _unpack_bf16_pairs(w):
    return pltpu.bitcast(w & jnp.uint32(0xFFFF0000), F32), pltpu.bitcast(w << 16, F32)


def _post_kernel(x_ref, ys_ref, ya_ref, wglu_ref, bglu_ref, gs_ref, ga_ref, wo_ref, gf_ref,
                 wr_ref, br_ref, x1_ref, h2_ref, route_ref, cnt_ref, run_ref, *, d_ssm):
    i = pl.program_id(0)

    @pl.when(i == 0)
    def _():
        run_ref[...] = jnp.zeros_like(run_ref)

    y = _gelu_tanh(ys_ref[...])
    gate = jnp.dot(y.astype(BF16), wglu_ref[...], preferred_element_type=F32) + bglu_ref[...]
    y = y * jax.nn.sigmoid(gate)
    m_s = _rms(y, gs_ref[...]).astype(BF16)
    m_a = _rms(ya_ref[...], ga_ref[...]).astype(BF16)
    mix = (jnp.dot(m_s, wo_ref[:d_ssm, :], preferred_element_type=F32)
           + jnp.dot(m_a, wo_ref[d_ssm:, :], preferred_element_type=F32))
    x1 = x_ref[...] + mix
    x1_ref[...] = x1
    h2 = _rms(x1, gf_ref[...])
    half = h2.shape[1] // 2
    h2_ref[...] = _pack_bf16_pairs(h2[:, :half], h2[:, half:])

    lg = jnp.dot(h2.astype(BF16), wr_ref[...], preferred_element_type=F32) + br_ref[...]
    tm = lg.shape[0]
    lane = _lane_iota(lg.shape).astype(F32)
    neg = -jnp.inf
    first = lambda hit: jnp.min(jnp.where(hit, lane, float(LANES)), axis=-1, keepdims=True)
    glog = jnp.where(lane < N_EXPERT_GROUPS, lg, neg)
    gmax = jnp.max(glog, axis=-1, keepdims=True)
    p_grp = 1.0 / jnp.sum(jnp.exp(glog - gmax), axis=-1, keepdims=True)
    grp = first(glog == gmax)
    e0 = ROUTER_LANE0 + grp * EXPERTS_PER_GROUP
    elog = jnp.where(jnp.logical_and(lane >= e0, lane < e0 + EXPERTS_PER_GROUP), lg, neg)
    m1 = jnp.max(elog, axis=-1, keepdims=True)
    i1 = first(elog == m1)
    elog2 = jnp.where(lane == i1, neg, elog)
    m2 = jnp.max(elog2, axis=-1, keepdims=True)
    i2 = first(elog2 == m2)
    e21 = jnp.exp(m2 - m1)
    g1 = p_grp * (1.0 / (1.0 + e21))
    g2 = p_grp * (e21 / (1.0 + e21))

    sel1 = lane == i1
    sel2 = lane == i2
    onehot = jnp.where(jnp.logical_or(sel1, sel2), 1.0, 0.0)
    r_i = lax.broadcasted_iota(jnp.int32, (tm, tm), 0)
    c_i = lax.broadcasted_iota(jnp.int32, (tm, tm), 1)
    lower = jnp.where(c_i < r_i, 1.0, 0.0).astype(BF16)
    before = jnp.dot(lower, onehot.astype(BF16), preferred_element_type=F32) + run_ref[0:1, :]
    rank1 = jnp.sum(jnp.where(sel1, before, 0.0), axis=-1, keepdims=True)
    rank2 = jnp.sum(jnp.where(sel2, before, 0.0), axis=-1, keepdims=True)
    run_ref[0:1, :] = run_ref[0:1, :] + jnp.sum(onehot, axis=0, keepdims=True)
    cnt_ref[...] = run_ref[...]

    fields = (i1 - ROUTER_LANE0, i2 - ROUTER_LANE0, g1, g2, rank1, rank2)
    route = jnp.zeros(lg.shape, F32)
    for pos, val in enumerate(fields):
        route = jnp.where(lane == pos, val, route)
    route_ref[...] = route


def _post(x2, y_ssm_t, y_att, w_glu_bf, b_glu, g_ssm, g_att, w_out_bf, g_ffn, w_r_bf, b_r):
    n, d = x2.shape
    d_ssm = w_glu_bf.shape[0]
    tm = POST_TM
    nt = y_ssm_t.shape[0] // tm
    row = lambda w: pl.BlockSpec((tm, w), lambda i: (i, 0))
    ssm_spec = pl.BlockSpec((tm, d_ssm), lambda i: (i % nt, i // nt))
    full = lambda a: pl.BlockSpec(a.shape, lambda i: (0,) * a.ndim)
    cnt_spec = pl.BlockSpec((8, LANES), lambda i: (0, 0))
    return pl.pallas_call(
        functools.partial(_post_kernel, d_ssm=d_ssm),
        grid=(n // tm,),
        in_specs=[row(d), ssm_spec, row(y_att.shape[1]), full(w_glu_bf), full(b_glu), full(g_ssm),
                  full(g_att), full(w_out_bf), full(g_ffn), full(w_r_bf), full(b_r)],
        out_specs=[row(d), row(d // 2), row(LANES), cnt_spec],
        out_shape=[jax.ShapeDtypeStruct((n, d), F32), jax.ShapeDtypeStruct((n, d // 2), jnp.uint32),
                   jax.ShapeDtypeStruct((n, LANES), F32), jax.ShapeDtypeStruct((8, LANES), F32)],
        scratch_shapes=[pltpu.VMEM((8, LANES), F32)],
        compiler_params=_cparams("arbitrary"),
        name="post",
    )(x2, y_ssm_t, y_att, w_glu_bf, b_glu, g_ssm, g_att, w_out_bf, g_ffn, w_r_bf, b_r)


def _dispatch_sc(dest, h2p, n_rows):
    n, w = h2p.shape
    workers = SC_CORES * SC_SUBCORES
    n_win = n // (workers * SC_WINDOW)
    d0 = dest[:, 0].reshape(n // SC_WINDOW, SC_WINDOW)
    d1 = dest[:, 1].reshape(n // SC_WINDOW, SC_WINDOW)
    mesh = plsc.VectorSubcoreMesh(core_axis_name="c", subcore_axis_name="s")

    def body(h_hbm, d0_hbm, d1_hbm, o_hbm, rows_v, i0_v, i1_v):
        wid = lax.axis_index("c") * SC_SUBCORES + lax.axis_index("s")

        @pl.loop(0, n_win)
        def _(j):
            blk = wid * n_win + j
            pltpu.sync_copy(h_hbm.at[pl.ds(blk * SC_WINDOW, SC_WINDOW)], rows_v)
            pltpu.sync_copy(d0_hbm.at[blk], i0_v)
            pltpu.sync_copy(d1_hbm.at[blk], i1_v)
            pltpu.sync_copy(rows_v, o_hbm.at[i0_v])
            pltpu.sync_copy(rows_v, o_hbm.at[i1_v])

    return pl.kernel(
        body,
        out_type=jax.ShapeDtypeStruct((n_rows, w), h2p.dtype),
        mesh=mesh,
        scratch_types=[pltpu.VMEM((SC_WINDOW, w), h2p.dtype), pltpu.VMEM((SC_WINDOW,), jnp.int32),
                       pltpu.VMEM((SC_WINDOW,), jnp.int32)],
        name="dispatch_sc",
    )(h2p, d0, d1)


def _experts_kernel(blk_e_ref, used_ref, x_ref, wg_ref, wu_ref, wd_ref, o_ref, wg_bf, wu_bf, wd_bf):
    i = pl.program_id(0)

    @pl.when(jnp.logical_or(i == 0, blk_e_ref[i] != blk_e_ref[jnp.maximum(i - 1, 0)]))
    def _():
        wg_bf[...] = wg_ref[0].astype(BF16)
        wu_bf[...] = wu_ref[0].astype(BF16)
        wd_bf[...] = wd_ref[0].astype(BF16)

    @pl.when(i < used_ref[0])
    def _():
        xa, xb = (t.astype(BF16) for t in _unpack_bf16_pairs(x_ref[...]))
        half = xa.shape[1]
        gate = (jnp.dot(xa, wg_bf[:half, :], preferred_element_type=F32)
                + jnp.dot(xb, wg_bf[half:, :], preferred_element_type=F32))
        up = (jnp.dot(xa, wu_bf[:half, :], preferred_element_type=F32)
              + jnp.dot(xb, wu_bf[half:, :], preferred_element_type=F32))
        hid = (jax.nn.silu(gate) * up).astype(BF16)
        out = jnp.dot(hid, wd_bf[...], preferred_element_type=F32)
        o_ref[...] = _pack_bf16_pairs(out[:, :half], out[:, half:])

    @pl.when(i >= used_ref[0])
    def _():
        o_ref[...] = jnp.zeros_like(o_ref)


def _experts(blk_e, n_used, buf, w_gate, w_up, w_down):
    n_rows, w = buf.shape
    d = w_down.shape[2]
    wspec = lambda a: pl.BlockSpec((1,) + a.shape[1:], lambda i, be, nu: (be[i], 0, 0))
    grid_spec = pltpu.PrefetchScalarGridSpec(
        num_scalar_prefetch=2,
        grid=(n_rows // MOE_ROWS,),
        in_specs=[pl.BlockSpec((MOE_ROWS, w), lambda i, be, nu: (i, 0)), wspec(w_gate), wspec(w_up), wspec(w_down)],
        out_specs=pl.BlockSpec((MOE_ROWS, d // 2), lambda i, be, nu: (i, 0)),
        scratch_shapes=[pltpu.VMEM(w_gate.shape[1:], BF16), pltpu.VMEM(w_up.shape[1:], BF16),
                        pltpu.VMEM(w_down.shape[1:], BF16)],
    )
    return pl.pallas_call(
        _experts_kernel,
        grid_spec=grid_spec,
        out_shape=jax.ShapeDtypeStruct((n_rows, d // 2), jnp.uint32),
        compiler_params=_cparams("arbitrary"),
        name="experts",
    )(blk_e, n_used, buf, w_gate, w_up, w_down)


def _gather_sc(dest, eo):
    n = dest.shape[0]
    w = eo.shape[1]
    workers = SC_CORES * SC_SUBCORES
    n_win = n // (workers * SC_WINDOW)
    d0 = dest[:, 0].reshape(n // SC_WINDOW, SC_WINDOW)
    d1 = dest[:, 1].reshape(n // SC_WINDOW, SC_WINDOW)
    mesh = plsc.VectorSubcoreMesh(core_axis_name="c", subcore_axis_name="s")

    def body(eo_hbm, d0_hbm, d1_hbm, o0_hbm, o1_hbm, rows_v, i_v):
        wid = lax.axis_index("c") * SC_SUBCORES + lax.axis_index("s")

        @pl.loop(0, n_win)
        def _(j):
            blk = wid * n_win + j
            for d_hbm, o_hbm in ((d0_hbm, o0_hbm), (d1_hbm, o1_hbm)):
                pltpu.sync_copy(d_hbm.at[blk], i_v)
                pltpu.sync_copy(eo_hbm.at[i_v], rows_v)
                pltpu.sync_copy(rows_v, o_hbm.at[pl.ds(blk * SC_WINDOW, SC_WINDOW)])

    out_sd = jax.ShapeDtypeStruct((n, w), eo.dtype)
    return pl.kernel(
        body,
        out_type=(out_sd, out_sd),
        mesh=mesh,
        scratch_types=[pltpu.VMEM((SC_WINDOW, w), eo.dtype), pltpu.VMEM((SC_WINDOW,), jnp.int32)],
        name="gather_sc",
    )(eo, d0, d1)


def _combine_kernel(x1_ref, route_ref, r0_ref, r1_ref, o_ref):
    route = route_ref[...]
    row0 = jnp.concatenate(_unpack_bf16_pairs(r0_ref[...]), axis=1)
    row1 = jnp.concatenate(_unpack_bf16_pairs(r1_ref[...]), axis=1)
    o_ref[...] = x1_ref[...] + (route[:, 2:3] * row0 + route[:, 3:4] * row1)


def _combine(x1, route, rows0, rows1):
    n, d = x1.shape
    tm = COMBINE_TM
    row = lambda w: pl.BlockSpec((tm, w), lambda i: (i, 0))
    return pl.pallas_call(
        _combine_kernel,
        grid=(n // tm,),
        in_specs=[row(d), row(LANES), row(d // 2), row(d // 2)],
        out_specs=row(d),
        out_shape=jax.ShapeDtypeStruct((n, d), F32),
        compiler_params=_cparams("parallel"),
        name="combine",
    )(x1, route, rows0, rows1)


def _layer(x, g_mix, w_in, lam_re, lam_im, log_dt, b_re, b_im, c_re, c_im, d_skip, w_glu, b_glu, g_q, g_k,
           g_ssm_out, g_attn_out, w_out, g_ffn, w_rg, b_rg, w_re, b_re_router, w_gate, w_up, w_down):
    bsz, t_len, d = x.shape
    n = bsz * t_len
    d_ssm = w_glu.shape[0]
    d_att = g_attn_out.shape[0]
    n_heads = d_att // HEAD_DIM
    n_chunks = t_len // SSM_CHUNK
    x2 = x.reshape(n, d)

    u_t, q, k, v = _in_proj(x2, g_mix[None], w_in.astype(BF16), jnp.tile(g_q, n_heads)[None],
                            jnp.tile(g_k, n_heads)[None], d_ssm, d_att, bsz, t_len)
    tables = _s5_tables(lam_re, lam_im, log_dt, b_re, b_im, c_re, c_im, d_skip)
    y_ssm_t = _s5(u_t.reshape(n_chunks, SSM_CHUNK, bsz, d_ssm), tables).reshape(t_len, bsz * d_ssm)

    y_att = _attention(q, k, v, bsz, t_len)

    w_r = jnp.concatenate([w_rg, w_re.reshape(d, N_EXPERTS)], axis=1)
    w_r = jnp.pad(w_r, ((0, 0), (0, LANES - w_r.shape[1]))).astype(BF16)
    b_r = jnp.pad(jnp.concatenate([b_rg, b_re_router.reshape(N_EXPERTS)]), (0, LANES - ROUTER_LANE0 - N_EXPERTS))[None]
    x1, h2p, route, cnt = _post(x2, y_ssm_t, y_att, w_glu.astype(BF16), b_glu[None], g_ssm_out[None],
                                g_attn_out[None], w_out.astype(BF16), g_ffn[None], w_r, b_r)

    counts = cnt[0, ROUTER_LANE0:ROUTER_LANE0 + N_EXPERTS].astype(jnp.int32)
    pcounts = ((counts + MOE_ROWS - 1) // MOE_ROWS) * MOE_ROWS
    pends = jnp.cumsum(pcounts)
    pstarts = pends - pcounts
    expert = route[:, 0:2].astype(jnp.int32)
    e_ids = jnp.arange(N_EXPERTS, dtype=jnp.int32)
    start = jnp.sum(jnp.where(expert[..., None] == e_ids, pstarts, 0), axis=-1)
    dest = start + route[:, 4:6].astype(jnp.int32)
    n_rows = n * 2 + N_EXPERTS * MOE_ROWS
    n_blk = n_rows // MOE_ROWS
    blk_row0 = jnp.arange(n_blk, dtype=jnp.int32)[:, None] * MOE_ROWS
    blk_e = jnp.minimum(jnp.sum((pends[None, :] <= blk_row0).astype(jnp.int32), axis=1), N_EXPERTS - 1)
    n_used = (pends[-1:] // MOE_ROWS).astype(jnp.int32)

    buf = _dispatch_sc(dest, h2p, n_rows)
    eo = _experts(blk_e, n_used, buf, w_gate, w_up, w_down)
    out = _combine(x1, route, *_gather_sc(dest, eo))
    return out.reshape(bsz, t_len, d)


def kernel(x, g_mix, w_in, ssm_lambda_re, ssm_lambda_im, ssm_log_dt, ssm_b_re, ssm_b_im, ssm_c_re, ssm_c_im, ssm_d, ssm_w_glu, ssm_b_glu, g_q, g_k, g_ssm_out, g_attn_out, w_out, g_ffn, w_router_group, b_router_group, w_router_expert, b_router_expert, w_gate, w_up, w_down):
    for l in range(g_mix.shape[0]):
        x = _layer(x, g_mix[l], w_in[l], ssm_lambda_re[l], ssm_lambda_im[l], ssm_log_dt[l], ssm_b_re[l],
                   ssm_b_im[l], ssm_c_re[l], ssm_c_im[l], ssm_d[l], ssm_w_glu[l], ssm_b_glu[l], g_q[l], g_k[l],
                   g_ssm_out[l], g_attn_out[l], w_out[l], g_ffn[l], w_router_group[l], b_router_group[l],
                   w_router_expert[l], b_router_expert[l], w_gate[l], w_up[l], w_down[l])
    return x
```

```python
import functools
import math

import jax
import jax.numpy as jnp
from jax import lax
from jax.experimental import pallas as pl
from jax.experimental.pallas import tpu as pltpu
from jax.experimental.pallas import tpu_sc as plsc

F32 = jnp.float32
BF16 = jnp.bfloat16
EPS = 1e-6

LANES = 128
VMEM_LIMIT_BYTES = 56 * 1024 * 1024

SSM_GROUP = 16
SSM_STATE = 64
SSM_CHUNK = 16
HEAD_DIM = 64
N_EXPERT_GROUPS = 4
EXPERTS_PER_GROUP = 8
N_EXPERTS = N_EXPERT_GROUPS * EXPERTS_PER_GROUP
ROUTER_LANE0 = N_EXPERT_GROUPS
MOE_ROWS = 512
MOE_PARTS = 2
ATT_SKIP = 110.0

S5_GPB = LANES // SSM_GROUP
S5_CHUNKS_PER_STEP = 8

IN_TM = 512
ATT_TILE = 256
ATT_KEYS_AHEAD = 256
ATT_TILES_PER_STEP = 2
POST_TM = 512
COMBINE_TM = 512
SC_CORES = 2
SC_SUBCORES = 16
SC_WINDOW = 64


def _cparams(*sem):
    return pltpu.CompilerParams(dimension_semantics=sem, vmem_limit_bytes=VMEM_LIMIT_BYTES)


def _lane_iota(shape):
    return lax.broadcasted_iota(jnp.int32, shape, len(shape) - 1)


def _head_rms(t, gain):
    outs = []
    for c in range(t.shape[1] // LANES):
        blk = t[:, c * LANES:(c + 1) * LANES]
        sq = blk * blk
        lo = _lane_iota(blk.shape) < HEAD_DIM
        s_lo = jnp.sum(jnp.where(lo, sq, 0.0), axis=-1, keepdims=True)
        s_hi = jnp.sum(jnp.where(lo, 0.0, sq), axis=-1, keepdims=True)
        inv = jnp.where(lo, lax.rsqrt(s_lo * (1.0 / HEAD_DIM) + EPS),
                        lax.rsqrt(s_hi * (1.0 / HEAD_DIM) + EPS))
        outs.append(blk * inv * gain[:, c * LANES:(c + 1) * LANES])
    return jnp.concatenate(outs, axis=-1)


def _in_proj_kernel(x_ref, g_ref, w_ref, gq_ref, gk_ref, u_ref, q_ref, k_ref, v_ref, *, d_ssm, d_att, scale):
    x = x_ref[...]
    inv = lax.rsqrt(jnp.mean(x * x, axis=-1, keepdims=True) + EPS)
    h = (x * inv * g_ref[...]).astype(BF16)
    proj = jnp.dot(h, w_ref[...], preferred_element_type=F32)
    u_ref[...] = proj[:, :d_ssm].astype(BF16)
    q = _head_rms(proj[:, d_ssm:d_ssm + d_att], gq_ref[...])
    k = _head_rms(proj[:, d_ssm + d_att:d_ssm + 2 * d_att], gk_ref[...])
    q_ref[...] = (q * scale).astype(BF16)
    k_ref[...] = k.astype(BF16)
    v_ref[...] = proj[:, d_ssm + 2 * d_att:].astype(BF16)


def _in_proj(x2, g_mix, w_in_bf, gq_t, gk_t, d_ssm, d_att, bsz, t_len):
    n, d = x2.shape
    tm = IN_TM
    nt = t_len // tm
    row = lambda w: pl.BlockSpec((tm, w), lambda b, t: (b * nt + t, 0))
    full = lambda a: pl.BlockSpec(a.shape, lambda b, t: (0,) * a.ndim)
    out_sd = jax.ShapeDtypeStruct((n, d_att), BF16)
    return pl.pallas_call(
        functools.partial(_in_proj_kernel, d_ssm=d_ssm, d_att=d_att, scale=1.0 / math.sqrt(HEAD_DIM)),
        grid=(bsz, nt),
        in_specs=[row(d), full(g_mix), full(w_in_bf), full(gq_t), full(gk_t)],
        out_specs=[pl.BlockSpec((tm, d_ssm), lambda b, t: (t, b)), row(d_att), row(d_att), row(d_att)],
        out_shape=[jax.ShapeDtypeStruct((t_len, bsz * d_ssm), BF16), out_sd, out_sd, out_sd],
        compiler_params=_cparams("parallel", "parallel"),
        name="in_proj",
    )(x2, g_mix, w_in_bf, gq_t, gk_t)


def _s5_tables(lam_re, lam_im, log_dt, b_re, b_im, c_re, c_im, d_skip):
    hp = lax.Precision.HIGHEST
    L = SSM_CHUNK
    g_n, p_n = lam_re.shape
    dt = jnp.exp(log_dt)[:, None]
    lr, li = lam_re, lam_im
    ls = jnp.arange(L + 1, dtype=F32)[:, None, None]
    mag = jnp.exp(lr * dt * ls)
    pr, pi = mag * jnp.cos(li * dt * ls), mag * jnp.sin(li * dt * ls)
    abar_r, abar_i = pr[1], pi[1]
    den = lr * lr + li * li
    nr, ni = abar_r - 1.0, abar_i
    coef_r = (nr * lr + ni * li) / den
    coef_i = (ni * lr - nr * li) / den
    bbr = coef_r[..., None] * b_re - coef_i[..., None] * b_im
    bbi = coef_r[..., None] * b_im + coef_i[..., None] * b_re
    wr = pr[..., None] * bbr - pi[..., None] * bbi
    wi = pr[..., None] * bbi + pi[..., None] * bbr
    kl = (jnp.einsum('gop,lgpi->lgoi', c_re, wr[:L], precision=hp)
          - jnp.einsum('gop,lgpi->lgoi', c_im, wi[:L], precision=hp))
    kl = kl.at[0].add(jax.vmap(jnp.diag)(d_skip))
    n_lb = g_n // S5_GPB
    sg = SSM_GROUP
    kc = kl.transpose(1, 3, 0, 2).reshape(n_lb, S5_GPB * sg, L * sg)
    b_rows = lambda w: (w[:L][::-1].reshape(L, n_lb, S5_GPB, p_n, sg).transpose(1, 0, 2, 4, 3)
                        .reshape(n_lb, L * S5_GPB * sg, p_n))
    p1r, p1i = pr[1:], pi[1:]
    cst_r = (c_re[None] * p1r[:, :, None, :] - c_im[None] * p1i[:, :, None, :])
    cst_i = -(c_re[None] * p1i[:, :, None, :] + c_im[None] * p1r[:, :, None, :])
    c_rows = lambda c: c.transpose(1, 3, 0, 2).reshape(n_lb, S5_GPB * p_n, L * sg)
    a_l = jnp.stack([pr[L].reshape(n_lb, S5_GPB * p_n), pi[L].reshape(n_lb, S5_GPB * p_n)], axis=1)
    return (kc.astype(BF16), b_rows(wr).astype(BF16), b_rows(wi).astype(BF16),
            c_rows(cst_r).astype(BF16), c_rows(cst_i).astype(BF16), a_l)


def _s5_expand_tables(kc_ref, bcr_ref, bci_ref, ccr_ref, cci_ref, w0_ref, br_ref, bi_ref, cr_ref, ci_ref):
    sg, gpb = SSM_GROUP, S5_GPB
    n_lo = w0_ref.shape[1]
    p_n = bcr_ref.shape[2]

    def iotas(shape):
        return lax.broadcasted_iota(jnp.int32, shape, 0), lax.broadcasted_iota(jnp.int32, shape, 1)

    r, c = iotas((kc_ref.shape[2], n_lo))
    spread_o = jnp.where(jnp.logical_and(r // sg == c // LANES, r % sg == c % sg), 1.0, 0.0).astype(BF16)
    r, c = iotas((p_n, gpb * p_n))
    spread_p = jnp.where(r == c % p_n, 1.0, 0.0).astype(BF16)

    def expand(compact, spread, row_group, col_group):
        full = jnp.dot(compact, spread, preferred_element_type=F32)
        r, c = iotas(full.shape)
        return jnp.where(row_group(r) == col_group(c), full, 0.0).astype(BF16)

    lane_group = lambda c: (c % LANES) // sg
    top = expand(kc_ref[0], spread_o, lambda r: r // sg, lane_group)
    w0_ref[:LANES, :] = top
    w0_ref[LANES:, :LANES] = jnp.zeros((LANES, LANES), BF16)
    w0_ref[LANES:, LANES:] = top[:, :n_lo - LANES]
    state_group = lambda c: c // p_n
    br_ref[...] = expand(bcr_ref[0], spread_p, lambda r: (r // sg) % gpb, state_group)
    bi_ref[...] = expand(bci_ref[0], spread_p, lambda r: (r // sg) % gpb, state_group)
    cr_ref[...] = expand(ccr_ref[0], spread_o, lambda r: r // p_n, lane_group)
    ci_ref[...] = expand(cci_ref[0], spread_o, lambda r: r // p_n, lane_group)


def _s5_kernel(u_ref, kc_ref, bcr_ref, bci_ref, ccr_ref, cci_ref, a_ref, y_ref,
               hr_ref, hi_ref, acc_ref, w0_ref, br_ref, bi_ref, cr_ref, ci_ref):
    n_chunks, L, bsz, _ = u_ref.shape
    rows = n_chunks * bsz

    @pl.when(pl.program_id(1) == 0)
    def _():
        hr_ref[...] = jnp.zeros_like(hr_ref)
        hi_ref[...] = jnp.zeros_like(hi_ref)
        _s5_expand_tables(kc_ref, bcr_ref, bci_ref, ccr_ref, cci_ref, w0_ref, br_ref, bi_ref, cr_ref, ci_ref)

    us = [u_ref[:, s].reshape(rows, LANES) for s in range(L)]
    lhs = jnp.concatenate(us, axis=1)
    sin_r = jnp.dot(lhs, br_ref[...], preferred_element_type=F32)
    sin_i = jnp.dot(lhs, bi_ref[...], preferred_element_type=F32)
    ar = a_ref[0, 0:1, :]
    ai = a_ref[0, 1:2, :]
    hr, hi = hr_ref[...], hi_ref[...]
    prev_r, prev_i = [], []
    for c in range(n_chunks):
        prev_r.append(hr)
        prev_i.append(hi)
        sl = slice(c * bsz, (c + 1) * bsz)
        hr, hi = ar * hr - ai * hi + sin_r[sl], ar * hi + ai * hr + sin_i[sl]
    hr_ref[...] = hr
    hi_ref[...] = hi
    pr = jnp.concatenate(prev_r, axis=0).astype(BF16)
    pi = jnp.concatenate(prev_i, axis=0).astype(BF16)
    acc_ref[...] = (jnp.dot(pr, cr_ref[...], preferred_element_type=F32)
                    + jnp.dot(pi, ci_ref[...], preferred_element_type=F32))
    for p in range(L // 2):
        off = 2 * p * LANES
        pair = jnp.concatenate([us[2 * p], us[2 * p + 1]], axis=1)
        acc_ref[:, off:] += jnp.dot(pair, w0_ref[:, :L * LANES - off], preferred_element_type=F32)
    for t in range(L):
        y_ref[:, t] = acc_ref[:, t * LANES:(t + 1) * LANES].reshape(n_chunks, bsz, LANES)


def _s5(u4, tables):
    n_chunks, L, bsz, d_ssm = u4.shape
    a_l = tables[-1]
    cb = S5_CHUNKS_PER_STEP
    data = pl.BlockSpec((cb, L, bsz, LANES), lambda lb, c: (c, 0, 0, lb))
    per_lb = lambda a: pl.BlockSpec((1,) + a.shape[1:], lambda lb, c: (lb,) + (0,) * (a.ndim - 1))
    n_state = a_l.shape[2]
    state = pltpu.VMEM((bsz, n_state), F32)
    n_lo = L * LANES
    return pl.pallas_call(
        _s5_kernel,
        grid=(d_ssm // LANES, n_chunks // cb),
        in_specs=[data] + [per_lb(t) for t in tables],
        out_specs=data,
        out_shape=jax.ShapeDtypeStruct(u4.shape, F32),
        scratch_shapes=[state, state, pltpu.VMEM((cb * bsz, n_lo), F32),
                        pltpu.VMEM((2 * LANES, n_lo), BF16),
                        pltpu.VMEM((n_lo, n_state), BF16), pltpu.VMEM((n_lo, n_state), BF16),
                        pltpu.VMEM((n_state, n_lo), BF16), pltpu.VMEM((n_state, n_lo), BF16)],
        compiler_params=_cparams("arbitrary", "arbitrary"),
        name="s5",
    )(u4, *tables)


def _softplus(z):
    return jnp.maximum(z, 0.0) + jnp.log(1.0 + jnp.exp(-jnp.abs(z)))


def _att_tile(qm, k, v, tri, r_in, causal):
    z = lax.dot_general(qm, k, (((1,), (1,)), ((), ())), preferred_element_type=F32)
    sp = _softplus(z)
    if causal:
        rows = lax.broadcasted_iota(jnp.int32, z.shape, 0)
        cols = lax.broadcasted_iota(jnp.int32, z.shape, 1)
        keep = cols < rows
        sp_m = jnp.where(keep, sp, 0.0)
    else:
        sp_m = sp
    newer = jnp.dot(sp_m.astype(BF16), tri, preferred_element_type=F32)
    att = jnp.exp(z - sp - newer - r_in)
    if causal:
        att = jnp.where(keep, att, 0.0)
    pv = jnp.dot(att.astype(BF16), v, preferred_element_type=F32)
    return pv, newer[:, 0:1] + sp_m[:, 0:1]


def _attn_kernel(q_ref, k_ref, v_ref, o_ref, *, t_len, tile):
    nq = t_len // tile
    r_i = lax.broadcasted_iota(jnp.int32, (tile, tile), 0)
    c_i = lax.broadcasted_iota(jnp.int32, (tile, tile), 1)
    tri = jnp.where(r_i > c_i, 1.0, 0.0).astype(BF16)
    head0 = _lane_iota((tile, LANES)) < HEAD_DIM
    zero_r = jnp.zeros((tile, 1), F32)

    def near(q0, n_prev):
        q = q_ref[pl.ds(q0, tile), :]
        zq = jnp.zeros_like(q)
        qms = (jnp.where(head0, q, zq), jnp.where(head0, zq, q))
        kvs = [(k_ref[pl.ds(q0 - p * tile, tile), :], v_ref[pl.ds(q0 - p * tile, tile), :])
               for p in range(n_prev + 1)]
        accs, rs = [], []
        for qm in qms:
            acc, r = _att_tile(qm, kvs[0][0], kvs[0][1], tri, zero_r, causal=True)
            for k_p, v_p in kvs[1:]:
                pv, dr = _att_tile(qm, k_p, v_p, tri, r, causal=False)
                acc, r = acc + pv, r + dr
            accs.append(acc)
            rs.append(r)
        return qms, accs, rs

    def far_and_store(q0, qms, accs, rs, j_older):
        if j_older is not None:
            def cond(c):
                j, _, _, r0, r1 = c
                return jnp.logical_and(j >= 0, jnp.min(jnp.minimum(r0, r1)) < ATT_SKIP)

            def body(c):
                j, a0, a1, r0, r1 = c
                k0 = pl.multiple_of(j * tile, tile)
                k_j, v_j = k_ref[pl.ds(k0, tile), :], v_ref[pl.ds(k0, tile), :]
                pv0, d0 = _att_tile(qms[0], k_j, v_j, tri, r0, causal=False)
                pv1, d1 = _att_tile(qms[1], k_j, v_j, tri, r1, causal=False)
                return j - 1, a0 + pv0, a1 + pv1, r0 + d0, r1 + d1

            _, a0, a1, _, _ = lax.while_loop(cond, body, (j_older, accs[0], accs[1], rs[0], rs[1]))
            accs = [a0, a1]
        o_ref[pl.ds(q0, tile), :] = jnp.where(head0, accs[0], accs[1])

    n_prev = ATT_KEYS_AHEAD // tile
    group = ATT_TILES_PER_STEP
    first = n_prev + (nq - n_prev) % group
    for i in range(first):
        far_and_store(i * tile, *near(i * tile, min(i, n_prev)), None if i <= n_prev else i - n_prev - 1)

    def later(g, _):
        tiles = [first + g * group + t for t in range(group)]
        parts = [near(pl.multiple_of(i * tile, tile), n_prev) for i in tiles]
        for i, part in zip(tiles, parts):
            far_and_store(pl.multiple_of(i * tile, tile), *part, i - n_prev - 1)
        return 0

    lax.fori_loop(0, (nq - first) // group, later, 0)


def _attention(q, k, v, bsz, t_len):
    n, d_att = q.shape
    spec = pl.BlockSpec((t_len, LANES), lambda b, p: (b, p))
    return pl.pallas_call(
        functools.partial(_attn_kernel, t_len=t_len, tile=ATT_TILE),
        grid=(bsz, d_att // LANES),
        in_specs=[spec, spec, spec],
        out_specs=spec,
        out_shape=jax.ShapeDtypeStruct((n, d_att), F32),
        compiler_params=_cparams("parallel", "parallel"),
        name="attn",
    )(q, k, v)


def _rms(t, gain):
    return t * lax.rsqrt(jnp.mean(t * t, axis=-1, keepdims=True) + EPS) * gain


def _gelu_tanh(y):
    return 0.5 * y * (1.0 + jnp.tanh(math.sqrt(2.0 / math.pi) * (y + 0.044715 * (y * y * y))))


def _pack_bf16_pairs(a, b):
    ua = pltpu.bitcast(a.astype(BF16).astype(F32), jnp.uint32)
    ub = pltpu.bitcast(b.astype(BF16).astype(F32), jnp.uint32)
    return ua | (ub >> 16)


def _unpack_bf16_pairs(w):
    return pltpu.bitcast(w & jnp.uint32(0xFFFF0000), F32), pltpu.bitcast(w << 16, F32)


def _post_kernel(x_ref, ys_ref, ya_ref, wglu_ref, bglu_ref, gs_ref, ga_ref, wo_ref, gf_ref,
                 wr_ref, br_ref, x1_ref, h2_ref, route_ref, cnt_ref, run_ref, *, d_ssm):
    i = pl.program_id(0)

    @pl.when(i == 0)
    def _():
        run_ref[...] = jnp.zeros_like(run_ref)

    y = _gelu_tanh(ys_ref[...])
    gate = jnp.dot(y.astype(BF16), wglu_ref[...], preferred_element_type=F32) + bglu_ref[...]
    y = y * jax.nn.sigmoid(gate)
    m_s = _rms(y, gs_ref[...]).astype(BF16)
    m_a = _rms(ya_ref[...], ga_ref[...]).astype(BF16)
    mix = (jnp.dot(m_s, wo_ref[:d_ssm, :], preferred_element_type=F32)
           + jnp.dot(m_a, wo_ref[d_ssm:, :], preferred_element_type=F32))
    x1 = x_ref[...] + mix
    x1_ref[...] = x1
    h2 = _rms(x1, gf_ref[...])
    half = h2.shape[1] // 2
    h2_ref[...] = _pack_bf16_pairs(h2[:, :half], h2[:, half:])

    lg = jnp.dot(h2.astype(BF16), wr_ref[...], preferred_element_type=F32) + br_ref[...]
    tm = lg.shape[0]
    lane = _lane_iota(lg.shape).astype(F32)
    neg = -jnp.inf
    first = lambda hit: jnp.min(jnp.where(hit, lane, float(LANES)), axis=-1, keepdims=True)
    glog = jnp.where(lane < N_EXPERT_GROUPS, lg, neg)
    gmax = jnp.max(glog, axis=-1, keepdims=True)
    p_grp = 1.0 / jnp.sum(jnp.exp(glog - gmax), axis=-1, keepdims=True)
    grp = first(glog == gmax)
    e0 = ROUTER_LANE0 + grp * EXPERTS_PER_GROUP
    elog = jnp.where(jnp.logical_and(lane >= e0, lane < e0 + EXPERTS_PER_GROUP), lg, neg)
    m1 = jnp.max(elog, axis=-1, keepdims=True)
    i1 = first(elog == m1)
    elog2 = jnp.where(lane == i1, neg, elog)
    m2 = jnp.max(elog2, axis=-1, keepdims=True)
    i2 = first(elog2 == m2)
    e21 = jnp.exp(m2 - m1)
    g1 = p_grp * (1.0 / (1.0 + e21))
    g2 = p_grp * (e21 / (1.0 + e21))

    sel1 = lane == i1
    sel2 = lane == i2
    onehot = jnp.where(jnp.logical_or(sel1, sel2), 1.0, 0.0)
    r_i = lax.broadcasted_iota(jnp.int32, (tm, tm), 0)
    c_i = lax.broadcasted_iota(jnp.int32, (tm, tm), 1)
    lower = jnp.where(c_i < r_i, 1.0, 0.0).astype(BF16)
    before = jnp.dot(lower, onehot.astype(BF16), preferred_element_type=F32) + run_ref[0:1, :]
    rank1 = jnp.sum(jnp.where(sel1, before, 0.0), axis=-1, keepdims=True)
    rank2 = jnp.sum(jnp.where(sel2, before, 0.0), axis=-1, keepdims=True)
    run_ref[0:1, :] = run_ref[0:1, :] + jnp.sum(onehot, axis=0, keepdims=True)
    cnt_ref[...] = run_ref[...]

    fields = (i1 - ROUTER_LANE0, i2 - ROUTER_LANE0, g1, g2, rank1, rank2)
    route = jnp.zeros(lg.shape, F32)
    for pos, val in enumerate(fields):
        route = jnp.where(lane == pos, val, route)
    route_ref[...] = route


def _post(x2, y_ssm_t, y_att, w_glu_bf, b_glu, g_ssm, g_att, w_out_bf, g_ffn, w_r_bf, b_r, part):
    n, d = x2.shape
    d_ssm = w_glu_bf.shape[0]
    tm = POST_TM
    nt = y_ssm_t.shape[0] // tm
    steps = n // (tm * MOE_PARTS)
    i0 = part * steps
    row_in = lambda w: pl.BlockSpec((tm, w), lambda i: (i0 + i, 0))
    row = lambda w: pl.BlockSpec((tm, w), lambda i: (i, 0))
    ssm_spec = pl.BlockSpec((tm, d_ssm), lambda i: ((i0 + i) % nt, (i0 + i) // nt))
    full = lambda a: pl.BlockSpec(a.shape, lambda i: (0,) * a.ndim)
    cnt_spec = pl.BlockSpec((8, LANES), lambda i: (0, 0))
    n = n // MOE_PARTS
    return pl.pallas_call(
        functools.partial(_post_kernel, d_ssm=d_ssm),
        grid=(steps,),
        in_specs=[row_in(d), ssm_spec, row_in(y_att.shape[1]), full(w_glu_bf), full(b_glu), full(g_ssm),
                  full(g_att), full(w_out_bf), full(g_ffn), full(w_r_bf), full(b_r)],
        out_specs=[row(d), row(d // 2), row(LANES), cnt_spec],
        out_shape=[jax.ShapeDtypeStruct((n, d), F32), jax.ShapeDtypeStruct((n, d // 2), jnp.uint32),
                   jax.ShapeDtypeStruct((n, LANES), F32), jax.ShapeDtypeStruct((8, LANES), F32)],
        scratch_shapes=[pltpu.VMEM((8, LANES), F32)],
        compiler_params=_cparams("arbitrary"),
        name="post",
    )(x2, y_ssm_t, y_att, w_glu_bf, b_glu, g_ssm, g_att, w_out_bf, g_ffn, w_r_bf, b_r)


def _dispatch_sc(dest, h2p, n_rows):
    n, w = h2p.shape
    workers = SC_CORES * SC_SUBCORES
    n_win = n // (workers * SC_WINDOW)
    d0 = dest[:, 0].reshape(n // SC_WINDOW, SC_WINDOW)
    d1 = dest[:, 1].reshape(n // SC_WINDOW, SC_WINDOW)
    mesh = plsc.VectorSubcoreMesh(core_axis_name="c", subcore_axis_name="s")

    def body(h_hbm, d0_hbm, d1_hbm, o_hbm, rows_v, i0_v, i1_v):
        wid = lax.axis_index("c") * SC_SUBCORES + lax.axis_index("s")

        @pl.loop(0, n_win)
        def _(j):
            blk = wid * n_win + j
            pltpu.sync_copy(h_hbm.at[pl.ds(blk * SC_WINDOW, SC_WINDOW)], rows_v)
            pltpu.sync_copy(d0_hbm.at[blk], i0_v)
            pltpu.sync_copy(d1_hbm.at[blk], i1_v)
            pltpu.sync_copy(rows_v, o_hbm.at[i0_v])
            pltpu.sync_copy(rows_v, o_hbm.at[i1_v])

    return pl.kernel(
        body,
        out_type=jax.ShapeDtypeStruct((n_rows, w), h2p.dtype),
        mesh=mesh,
        scratch_types=[pltpu.VMEM((SC_WINDOW, w), h2p.dtype), pltpu.VMEM((SC_WINDOW,), jnp.int32),
                       pltpu.VMEM((SC_WINDOW,), jnp.int32)],
        name="dispatch_sc",
    )(h2p, d0, d1)


def _experts_kernel(blk_e_ref, used_ref, x_ref, wg_ref, wu_ref, wd_ref, o_ref, wg_bf, wu_bf, wd_bf):
    i = pl.program_id(0)

    @pl.when(jnp.logical_or(i == 0, blk_e_ref[i] != blk_e_ref[jnp.maximum(i - 1, 0)]))
    def _():
        wg_bf[...] = wg_ref[0].astype(BF16)
        wu_bf[...] = wu_ref[0].astype(BF16)
        wd_bf[...] = wd_ref[0].astype(BF16)

    @pl.when(i < used_ref[0])
    def _():
        xa, xb = (t.astype(BF16) for t in _unpack_bf16_pairs(x_ref[...]))
        half = xa.shape[1]
        gate = (jnp.dot(xa, wg_bf[:half, :], preferred_element_type=F32)
                + jnp.dot(xb, wg_bf[half:, :], preferred_element_type=F32))
        up = (jnp.dot(xa, wu_bf[:half, :], preferred_element_type=F32)
              + jnp.dot(xb, wu_bf[half:, :], preferred_element_type=F32))
        hid = (jax.nn.silu(gate) * up).astype(BF16)
        out = jnp.dot(hid, wd_bf[...], preferred_element_type=F32)
        o_ref[...] = _pack_bf16_pairs(out[:, :half], out[:, half:])

    @pl.when(i >= used_ref[0])
    def _():
        o_ref[...] = jnp.zeros_like(o_ref)


def _experts(blk_e, n_used, buf, w_gate, w_up, w_down):
    n_rows, w = buf.shape
    d = w_down.shape[2]
    wspec = lambda a: pl.BlockSpec((1,) + a.shape[1:], lambda i, be, nu: (be[i], 0, 0))
    grid_spec = pltpu.PrefetchScalarGridSpec(
        num_scalar_prefetch=2,
        grid=(n_rows // MOE_ROWS,),
        in_specs=[pl.BlockSpec((MOE_ROWS, w), lambda i, be, nu: (i, 0)), wspec(w_gate), wspec(w_up), wspec(w_down)],
        out_specs=pl.BlockSpec((MOE_ROWS, d // 2), lambda i, be, nu: (i, 0)),
        scratch_shapes=[pltpu.VMEM(w_gate.shape[1:], BF16), pltpu.VMEM(w_up.shape[1:], BF16),
                        pltpu.VMEM(w_down.shape[1:], BF16)],
    )
    return pl.pallas_call(
        _experts_kernel,
        grid_spec=grid_spec,
        out_shape=jax.ShapeDtypeStruct((n_rows, d // 2), jnp.uint32),
        compiler_params=_cparams("arbitrary"),
        name="experts",
    )(blk_e, n_used, buf, w_gate, w_up, w_down)


def _gather_sc(dest, eo):
    n = dest.shape[0]
    w = eo.shape[1]
    workers = SC_CORES * SC_SUBCORES
    n_win = n // (workers * SC_WINDOW)
    d0 = dest[:, 0].reshape(n // SC_WINDOW, SC_WINDOW)
    d1 = dest[:, 1].reshape(n // SC_WINDOW, SC_WINDOW)
    mesh = plsc.VectorSubcoreMesh(core_axis_name="c", subcore_axis_name="s")

    def body(eo_hbm, d0_hbm, d1_hbm, o0_hbm, o1_hbm, rows_v, i_v):
        wid = lax.axis_index("c") * SC_SUBCORES + lax.axis_index("s")

        @pl.loop(0, n_win)
        def _(j):
            blk = wid * n_win + j
            for d_hbm, o_hbm in ((d0_hbm, o0_hbm), (d1_hbm, o1_hbm)):
                pltpu.sync_copy(d_hbm.at[blk], i_v)
                pltpu.sync_copy(eo_hbm.at[i_v], rows_v)
                pltpu.sync_copy(rows_v, o_hbm.at[pl.ds(blk * SC_WINDOW, SC_WINDOW)])

    out_sd = jax.ShapeDtypeStruct((n, w), eo.dtype)
    return pl.kernel(
        body,
        out_type=(out_sd, out_sd),
        mesh=mesh,
        scratch_types=[pltpu.VMEM((SC_WINDOW, w), eo.dtype), pltpu.VMEM((SC_WINDOW,), jnp.int32)],
        name="gather_sc",
    )(eo, d0, d1)


def _combine_kernel(x1_ref, route_ref, r0_ref, r1_ref, *rest):
    o_ref = rest[-1]
    route = route_ref[...]
    row0 = jnp.concatenate(_unpack_bf16_pairs(r0_ref[...]), axis=1)
    row1 = jnp.concatenate(_unpack_bf16_pairs(r1_ref[...]), axis=1)
    o_ref[...] = x1_ref[...] + (route[:, 2:3] * row0 + route[:, 3:4] * row1)


def _combine(x1, route, rows0, rows1, part, out_prev):
    n_slice, d = x1.shape
    tm = COMBINE_TM
    steps = n_slice // tm
    row = lambda w: pl.BlockSpec((tm, w), lambda i: (i, 0))
    in_specs = [row(d), row(LANES), row(d // 2), row(d // 2)]
    args = [x1, route, rows0, rows1]
    aliases = {}
    if out_prev is not None:
        in_specs.append(pl.BlockSpec(memory_space=pl.ANY))
        args.append(out_prev)
        aliases = {len(args) - 1: 0}
    return pl.pallas_call(
        _combine_kernel,
        grid=(steps,),
        in_specs=in_specs,
        out_specs=pl.BlockSpec((tm, d), lambda i: (part * steps + i, 0)),
        out_shape=jax.ShapeDtypeStruct((n_slice * MOE_PARTS, d), F32),
        input_output_aliases=aliases,
        compiler_params=_cparams("parallel"),
        name="combine",
    )(*args)


def _layer(x, g_mix, w_in, lam_re, lam_im, log_dt, b_re, b_im, c_re, c_im, d_skip, w_glu, b_glu, g_q, g_k,
           g_ssm_out, g_attn_out, w_out, g_ffn, w_rg, b_rg, w_re, b_re_router, w_gate, w_up, w_down):
    bsz, t_len, d = x.shape
    n = bsz * t_len
    d_ssm = w_glu.shape[0]
    d_att = g_attn_out.shape[0]
    n_heads = d_att // HEAD_DIM
    n_chunks = t_len // SSM_CHUNK
    x2 = x.reshape(n, d)

    u_t, q, k, v = _in_proj(x2, g_mix[None], w_in.astype(BF16), jnp.tile(g_q, n_heads)[None],
                            jnp.tile(g_k, n_heads)[None], d_ssm, d_att, bsz, t_len)
    tables = _s5_tables(lam_re, lam_im, log_dt, b_re, b_im, c_re, c_im, d_skip)
    y_ssm_t = _s5(u_t.reshape(n_chunks, SSM_CHUNK, bsz, d_ssm), tables).reshape(t_len, bsz * d_ssm)

    y_att = _attention(q, k, v, bsz, t_len)

    w_r = jnp.concatenate([w_rg, w_re.reshape(d, N_EXPERTS)], axis=1)
    w_r = jnp.pad(w_r, ((0, 0), (0, LANES - w_r.shape[1]))).astype(BF16)
    b_r = jnp.pad(jnp.concatenate([b_rg, b_re_router.reshape(N_EXPERTS)]), (0, LANES - ROUTER_LANE0 - N_EXPERTS))[None]
    w_glu_bf, w_out_bf = w_glu.astype(BF16), w_out.astype(BF16)
    n_slice = n // MOE_PARTS
    n_rows = n_slice * 2 + N_EXPERTS * MOE_ROWS
    n_blk = n_rows // MOE_ROWS
    out = None
    for part in range(MOE_PARTS):
        x1, h2p, route, cnt = _post(x2, y_ssm_t, y_att, w_glu_bf, b_glu[None], g_ssm_out[None],
                                    g_attn_out[None], w_out_bf, g_ffn[None], w_r, b_r, part)
        counts = cnt[0, ROUTER_LANE0:ROUTER_LANE0 + N_EXPERTS].astype(jnp.int32)
        pcounts = ((counts + MOE_ROWS - 1) // MOE_ROWS) * MOE_ROWS
        pends = jnp.cumsum(pcounts)
        pstarts = pends - pcounts
        expert = route[:, 0:2].astype(jnp.int32)
        e_ids = jnp.arange(N_EXPERTS, dtype=jnp.int32)
        start = jnp.sum(jnp.where(expert[..., None] == e_ids, pstarts, 0), axis=-1)
        dest = start + route[:, 4:6].astype(jnp.int32)
        blk_row0 = jnp.arange(n_blk, dtype=jnp.int32)[:, None] * MOE_ROWS
        blk_e = jnp.minimum(jnp.sum((pends[None, :] <= blk_row0).astype(jnp.int32), axis=1), N_EXPERTS - 1)
        n_used = (pends[-1:] // MOE_ROWS).astype(jnp.int32)

        buf = _dispatch_sc(dest, h2p, n_rows)
        eo = _experts(blk_e, n_used, buf, w_gate, w_up, w_down)
        out = _combine(x1, route, *_gather_sc(dest, eo), part, out)
    return out.reshape(bsz, t_len, d)


def kernel(x, g_mix, w_in, ssm_lambda_re, ssm_lambda_im, ssm_log_dt, ssm_b_re, ssm_b_im, ssm_c_re, ssm_c_im, ssm_d, ssm_w_glu, ssm_b_glu, g_q, g_k, g_ssm_out, g_attn_out, w_out, g_ffn, w_router_group, b_router_group, w_router_expert, b_router_expert, w_gate, w_up, w_down):
    for l in range(g_mix.shape[0]):
        x = _layer(x, g_mix[l], w_in[l], ssm_lambda_re[l], ssm_lambda_im[l], ssm_log_dt[l], ssm_b_re[l],
                   ssm_b_im[l], ssm_c_re[l], ssm_c_im[l], ssm_d[l], ssm_w_glu[l], ssm_b_glu[l], g_q[l], g_k[l],
                   g_ssm_out[l], g_attn_out[l], w_out[l], g_ffn[l], w_router_group[l], b_router_group[l],
                   w_router_expert[l], b_router_expert[l], w_gate[l], w_up[l], w_down[l])
    return x
```

```python
import functools
import math

import jax
import jax.numpy as jnp
from jax import lax
from jax.experimental import pallas as pl
from jax.experimental.pallas import tpu as pltpu
from jax.experimental.pallas import tpu_sc as plsc

F32 = jnp.float32
BF16 = jnp.bfloat16
EPS = 1e-6

LANES = 128
VMEM_LIMIT_BYTES = 56 * 1024 * 1024

SSM_GROUP = 16
SSM_STATE = 64
SSM_CHUNK = 16
HEAD_DIM = 64
N_EXPERT_GROUPS = 4
EXPERTS_PER_GROUP = 8
N_EXPERTS = N_EXPERT_GROUPS * EXPERTS_PER_GROUP
ROUTER_LANE0 = N_EXPERT_GROUPS
MOE_ROWS = 512
MOE_PARTS = 2
ATT_SKIP = 110.0

S5_GPB = LANES // SSM_GROUP
S5_CHUNKS_PER_STEP = 8

IN_TM = 512
ATT_TILE = 256
ATT_KEYS_AHEAD = 256
ATT_TILES_PER_STEP = 2
POST_TM = 512
COMBINE_TM = 512
SC_CORES = 2
SC_SUBCORES = 16
SC_WINDOW = 64


def _cparams(*sem):
    return pltpu.CompilerParams(dimension_semantics=sem, vmem_limit_bytes=VMEM_LIMIT_BYTES)


def _lane_iota(shape):
    return lax.broadcasted_iota(jnp.int32, shape, len(shape) - 1)


def _head_rms(t, gain):
    outs = []
    for c in range(t.shape[1] // LANES):
        blk = t[:, c * LANES:(c + 1) * LANES]
        sq = blk * blk
        lo = _lane_iota(blk.shape) < HEAD_DIM
        s_lo = jnp.sum(jnp.where(lo, sq, 0.0), axis=-1, keepdims=True)
        s_hi = jnp.sum(jnp.where(lo, 0.0, sq), axis=-1, keepdims=True)
        inv = jnp.where(lo, lax.rsqrt(s_lo * (1.0 / HEAD_DIM) + EPS),
                        lax.rsqrt(s_hi * (1.0 / HEAD_DIM) + EPS))
        outs.append(blk * inv * gain[:, c * LANES:(c + 1) * LANES])
    return jnp.concatenate(outs, axis=-1)


def _in_proj_kernel(x_ref, g_ref, w_ref, gq_ref, gk_ref, u_ref, q_ref, k_ref, v_ref, *, d_ssm, d_att, scale):
    x = x_ref[...]
    inv = lax.rsqrt(jnp.mean(x * x, axis=-1, keepdims=True) + EPS)
    h = (x * inv * g_ref[...]).astype(BF16)
    proj = jnp.dot(h, w_ref[...], preferred_element_type=F32)
    u_ref[...] = proj[:, :d_ssm].astype(BF16)
    q = _head_rms(proj[:, d_ssm:d_ssm + d_att], gq_ref[...])
    k = _head_rms(proj[:, d_ssm + d_att:d_ssm + 2 * d_att], gk_ref[...])
    q_ref[...] = (q * scale).astype(BF16)
    k_ref[...] = k.astype(BF16)
    v_ref[...] = proj[:, d_ssm + 2 * d_att:].astype(BF16)


def _in_proj(x2, g_mix, w_in_bf, gq_t, gk_t, d_ssm, d_att, bsz, t_len):
    n, d = x2.shape
    tm = IN_TM
    nt = t_len // tm
    row = lambda w: pl.BlockSpec((tm, w), lambda b, t: (b * nt + t, 0))
    full = lambda a: pl.BlockSpec(a.shape, lambda b, t: (0,) * a.ndim)
    out_sd = jax.ShapeDtypeStruct((n, d_att), BF16)
    return pl.pallas_call(
        functools.partial(_in_proj_kernel, d_ssm=d_ssm, d_att=d_att, scale=1.0 / math.sqrt(HEAD_DIM)),
        grid=(bsz, nt),
        in_specs=[row(d), full(g_mix), full(w_in_bf), full(gq_t), full(gk_t)],
        out_specs=[pl.BlockSpec((tm, d_ssm), lambda b, t: (t, b)), row(d_att), row(d_att), row(d_att)],
        out_shape=[jax.ShapeDtypeStruct((t_len, bsz * d_ssm), BF16), out_sd, out_sd, out_sd],
        compiler_params=_cparams("parallel", "parallel"),
        name="in_proj",
    )(x2, g_mix, w_in_bf, gq_t, gk_t)


def _s5_tables(lam_re, lam_im, log_dt, b_re, b_im, c_re, c_im, d_skip):
    hp = lax.Precision.HIGHEST
    L = SSM_CHUNK
    g_n, p_n = lam_re.shape
    dt = jnp.exp(log_dt)[:, None]
    lr, li = lam_re, lam_im
    ls = jnp.arange(L + 1, dtype=F32)[:, None, None]
    mag = jnp.exp(lr * dt * ls)
    pr, pi = mag * jnp.cos(li * dt * ls), mag * jnp.sin(li * dt * ls)
    abar_r, abar_i = pr[1], pi[1]
    den = lr * lr + li * li
    nr, ni = abar_r - 1.0, abar_i
    coef_r = (nr * lr + ni * li) / den
    coef_i = (ni * lr - nr * li) / den
    bbr = coef_r[..., None] * b_re - coef_i[..., None] * b_im
    bbi = coef_r[..., None] * b_im + coef_i[..., None] * b_re
    wr = pr[..., None] * bbr - pi[..., None] * bbi
    wi = pr[..., None] * bbi + pi[..., None] * bbr
    kl = (jnp.einsum('gop,lgpi->lgoi', c_re, wr[:L], precision=hp)
          - jnp.einsum('gop,lgpi->lgoi', c_im, wi[:L], precision=hp))
    kl = kl.at[0].add(jax.vmap(jnp.diag)(d_skip))
    n_lb = g_n // S5_GPB
    sg = SSM_GROUP
    kc = kl.transpose(1, 3, 0, 2).reshape(n_lb, S5_GPB * sg, L * sg)
    b_rows = lambda w: (w[:L][::-1].reshape(L, n_lb, S5_GPB, p_n, sg).transpose(1, 0, 2, 4, 3)
                        .reshape(n_lb, L * S5_GPB * sg, p_n))
    p1r, p1i = pr[1:], pi[1:]
    cst_r = (c_re[None] * p1r[:, :, None, :] - c_im[None] * p1i[:, :, None, :])
    cst_i = -(c_re[None] * p1i[:, :, None, :] + c_im[None] * p1r[:, :, None, :])
    c_rows = lambda c: c.transpose(1, 3, 0, 2).reshape(n_lb, S5_GPB * p_n, L * sg)
    a_l = jnp.stack([pr[L].reshape(n_lb, S5_GPB * p_n), pi[L].reshape(n_lb, S5_GPB * p_n)], axis=1)
    return (kc.astype(BF16), b_rows(wr).astype(BF16), b_rows(wi).astype(BF16),
            c_rows(cst_r).astype(BF16), c_rows(cst_i).astype(BF16), a_l)


def _s5_expand_tables(kc_ref, bcr_ref, bci_ref, ccr_ref, cci_ref, w0_ref, br_ref, bi_ref, cr_ref, ci_ref):
    sg, gpb = SSM_GROUP, S5_GPB
    n_lo = w0_ref.shape[1]
    p_n = bcr_ref.shape[2]

    def iotas(shape):
        return lax.broadcasted_iota(jnp.int32, shape, 0), lax.broadcasted_iota(jnp.int32, shape, 1)

    r, c = iotas((kc_ref.shape[2], n_lo))
    spread_o = jnp.where(jnp.logical_and(r // sg == c // LANES, r % sg == c % sg), 1.0, 0.0).astype(BF16)
    r, c = iotas((p_n, gpb * p_n))
    spread_p = jnp.where(r == c % p_n, 1.0, 0.0).astype(BF16)

    def expand(compact, spread, row_group, col_group):
        full = jnp.dot(compact, spread, preferred_element_type=F32)
        r, c = iotas(full.shape)
        return jnp.where(row_group(r) == col_group(c), full, 0.0).astype(BF16)

    lane_group = lambda c: (c % LANES) // sg
    top = expand(kc_ref[0], spread_o, lambda r: r // sg, lane_group)
    w0_ref[:LANES, :] = top
    w0_ref[LANES:, :LANES] = jnp.zeros((LANES, LANES), BF16)
    w0_ref[LANES:, LANES:] = top[:, :n_lo - LANES]
    state_group = lambda c: c // p_n
    br_ref[...] = expand(bcr_ref[0], spread_p, lambda r: (r // sg) % gpb, state_group)
    bi_ref[...] = expand(bci_ref[0], spread_p, lambda r: (r // sg) % gpb, state_group)
    cr_ref[...] = expand(ccr_ref[0], spread_o, lambda r: r // p_n, lane_group)
    ci_ref[...] = expand(cci_ref[0], spread_o, lambda r: r // p_n, lane_group)


def _s5_kernel(u_ref, kc_ref, bcr_ref, bci_ref, ccr_ref, cci_ref, a_ref, y_ref,
               hr_ref, hi_ref, acc_ref, w0_ref, br_ref, bi_ref, cr_ref, ci_ref):
    n_chunks, L, bsz, _ = u_ref.shape
    rows = n_chunks * bsz

    @pl.when(pl.program_id(1) == 0)
    def _():
        hr_ref[...] = jnp.zeros_like(hr_ref)
        hi_ref[...] = jnp.zeros_like(hi_ref)
        _s5_expand_tables(kc_ref, bcr_ref, bci_ref, ccr_ref, cci_ref, w0_ref, br_ref, bi_ref, cr_ref, ci_ref)

    us = [u_ref[:, s].reshape(rows, LANES) for s in range(L)]
    lhs = jnp.concatenate(us, axis=1)
    sin_r = jnp.dot(lhs, br_ref[...], preferred_element_type=F32)
    sin_i = jnp.dot(lhs, bi_ref[...], preferred_element_type=F32)
    ar = a_ref[0, 0:1, :]
    ai = a_ref[0, 1:2, :]
    hr, hi = hr_ref[...], hi_ref[...]
    prev_r, prev_i = [], []
    for c in range(n_chunks):
        prev_r.append(hr)
        prev_i.append(hi)
        sl = slice(c * bsz, (c + 1) * bsz)
        hr, hi = ar * hr - ai * hi + sin_r[sl], ar * hi + ai * hr + sin_i[sl]
    hr_ref[...] = hr
    hi_ref[...] = hi
    pr = jnp.concatenate(prev_r, axis=0).astype(BF16)
    pi = jnp.concatenate(prev_i, axis=0).astype(BF16)
    acc_ref[...] = (jnp.dot(pr, cr_ref[...], preferred_element_type=F32)
                    + jnp.dot(pi, ci_ref[...], preferred_element_type=F32))
    for p in range(L // 2):
        off = 2 * p * LANES
        pair = jnp.concatenate([us[2 * p], us[2 * p + 1]], axis=1)
        acc_ref[:, off:] += jnp.dot(pair, w0_ref[:, :L * LANES - off], preferred_element_type=F32)
    for t in range(L):
        y_ref[:, t] = acc_ref[:, t * LANES:(t + 1) * LANES].reshape(n_chunks, bsz, LANES)


def _s5(u4, tables):
    n_chunks, L, bsz, d_ssm = u4.shape
    a_l = tables[-1]
    cb = S5_CHUNKS_PER_STEP
    data = pl.BlockSpec((cb, L, bsz, LANES), lambda lb, c: (c, 0, 0, lb))
    per_lb = lambda a: pl.BlockSpec((1,) + a.shape[1:], lambda lb, c: (lb,) + (0,) * (a.ndim - 1))
    n_state = a_l.shape[2]
    state = pltpu.VMEM((bsz, n_state), F32)
    n_lo = L * LANES
    return pl.pallas_call(
        _s5_kernel,
        grid=(d_ssm // LANES, n_chunks // cb),
        in_specs=[data] + [per_lb(t) for t in tables],
        out_specs=data,
        out_shape=jax.ShapeDtypeStruct(u4.shape, F32),
        scratch_shapes=[state, state, pltpu.VMEM((cb * bsz, n_lo), F32),
                        pltpu.VMEM((2 * LANES, n_lo), BF16),
                        pltpu.VMEM((n_lo, n_state), BF16), pltpu.VMEM((n_lo, n_state), BF16),
                        pltpu.VMEM((n_state, n_lo), BF16), pltpu.VMEM((n_state, n_lo), BF16)],
        compiler_params=_cparams("arbitrary", "arbitrary"),
        name="s5",
    )(u4, *tables)


def _softplus(z):
    return jnp.maximum(z, 0.0) + jnp.log(1.0 + jnp.exp(-jnp.abs(z)))


def _att_tile(qm, k, v, tri, r_in, causal):
    z = lax.dot_general(qm, k, (((1,), (1,)), ((), ())), preferred_element_type=F32)
    sp = _softplus(z)
    if causal:
        rows = lax.broadcasted_iota(jnp.int32, z.shape, 0)
        cols = lax.broadcasted_iota(jnp.int32, z.shape, 1)
        keep = cols < rows
        sp_m = jnp.where(keep, sp, 0.0)
    else:
        sp_m = sp
    newer = jnp.dot(sp_m.astype(BF16), tri, preferred_element_type=F32)
    att = jnp.exp(z - sp - newer - r_in)
    if causal:
        att = jnp.where(keep, att, 0.0)
    pv = jnp.dot(att.astype(BF16), v, preferred_element_type=F32)
    return pv, newer[:, 0:1] + sp_m[:, 0:1]


def _attn_kernel(q_ref, k_ref, v_ref, o_ref, *, t_len, tile):
    nq = t_len // tile
    r_i = lax.broadcasted_iota(jnp.int32, (tile, tile), 0)
    c_i = lax.broadcasted_iota(jnp.int32, (tile, tile), 1)
    tri = jnp.where(r_i > c_i, 1.0, 0.0).astype(BF16)
    head0 = _lane_iota((tile, LANES)) < HEAD_DIM
    zero_r = jnp.zeros((tile, 1), F32)

    def near(q0, n_prev):
        q = q_ref[pl.ds(q0, tile), :]
        zq = jnp.zeros_like(q)
        qms = (jnp.where(head0, q, zq), jnp.where(head0, zq, q))
        kvs = [(k_ref[pl.ds(q0 - p * tile, tile), :], v_ref[pl.ds(q0 - p * tile, tile), :])
               for p in range(n_prev + 1)]
        accs, rs = [], []
        for qm in qms:
            acc, r = _att_tile(qm, kvs[0][0], kvs[0][1], tri, zero_r, causal=True)
            for k_p, v_p in kvs[1:]:
                pv, dr = _att_tile(qm, k_p, v_p, tri, r, causal=False)
                acc, r = acc + pv, r + dr
            accs.append(acc)
            rs.append(r)
        return qms, accs, rs

    def far_and_store(q0, qms, accs, rs, j_older):
        if j_older is not None:
            def cond(c):
                j, _, _, r0, r1 = c
                return jnp.logical_and(j >= 0, jnp.min(jnp.minimum(r0, r1)) < ATT_SKIP)

            def body(c):
                j, a0, a1, r0, r1 = c
                k0 = pl.multiple_of(j * tile, tile)
                k_j, v_j = k_ref[pl.ds(k0, tile), :], v_ref[pl.ds(k0, tile), :]
                pv0, d0 = _att_tile(qms[0], k_j, v_j, tri, r0, causal=False)
                pv1, d1 = _att_tile(qms[1], k_j, v_j, tri, r1, causal=False)
                return j - 1, a0 + pv0, a1 + pv1, r0 + d0, r1 + d1

            _, a0, a1, _, _ = lax.while_loop(cond, body, (j_older, accs[0], accs[1], rs[0], rs[1]))
            accs = [a0, a1]
        o_ref[pl.ds(q0, tile), :] = jnp.where(head0, accs[0], accs[1])

    n_prev = ATT_KEYS_AHEAD // tile
    group = ATT_TILES_PER_STEP
    first = n_prev + (nq - n_prev) % group
    for i in range(first):
        far_and_store(i * tile, *near(i * tile, min(i, n_prev)), None if i <= n_prev else i - n_prev - 1)

    def later(g, _):
        tiles = [first + g * group + t for t in range(group)]
        parts = [near(pl.multiple_of(i * tile, tile), n_prev) for i in tiles]
        for i, part in zip(tiles, parts):
            far_and_store(pl.multiple_of(i * tile, tile), *part, i - n_prev - 1)
        return 0

    lax.fori_loop(0, (nq - first) // group, later, 0)


def _attention(q, k, v, bsz, t_len):
    n, d_att = q.shape
    spec = pl.BlockSpec((t_len, LANES), lambda b, p: (b, p))
    return pl.pallas_call(
        functools.partial(_attn_kernel, t_len=t_len, tile=ATT_TILE),
        grid=(bsz, d_att // LANES),
        in_specs=[spec, spec, spec],
        out_specs=spec,
        out_shape=jax.ShapeDtypeStruct((n, d_att), F32),
        compiler_params=_cparams("parallel", "parallel"),
        name="attn",
    )(q, k, v)


def _rms(t, gain):
    return t * lax.rsqrt(jnp.mean(t * t, axis=-1, keepdims=True) + EPS) * gain


def _gelu_tanh(y):
    return 0.5 * y * (1.0 + jnp.tanh(math.sqrt(2.0 / math.pi) * (y + 0.044715 * (y * y * y))))


def _pack_bf16_pairs(a, b):
    ua = pltpu.bitcast(a.astype(BF16).astype(F32), jnp.uint32)
    ub = pltpu.bitcast(b.astype(BF16).astype(F32), jnp.uint32)
    return ua | (ub >> 16)


def _unpack_bf16_pairs(w):
    return pltpu.bitcast(w & jnp.uint32(0xFFFF0000), F32), pltpu.bitcast(w << 16, F32)


def _post_kernel(x_ref, ys_ref, ya_ref, wglu_ref, bglu_ref, gs_ref, ga_ref, wo_ref, gf_ref,
                 wr_ref, br_ref, x1_ref, h2_ref, route_ref, cnt_ref, run_ref, *, d_ssm):
    i = pl.program_id(0)

    @pl.when(i == 0)
    def _():
        run_ref[...] = jnp.zeros_like(run_ref)

    y = _gelu_tanh(ys_ref[...])
    gate = jnp.dot(y.astype(BF16), wglu_ref[...], preferred_element_type=F32) + bglu_ref[...]
    y = y * jax.nn.sigmoid(gate)
    m_s = _rms(y, gs_ref[...]).astype(BF16)
    m_a = _rms(ya_ref[...], ga_ref[...]).astype(BF16)
    mix = (jnp.dot(m_s, wo_ref[:d_ssm, :], preferred_element_type=F32)
           + jnp.dot(m_a, wo_ref[d_ssm:, :], preferred_element_type=F32))
    x1 = x_ref[...] + mix
    x1_ref[...] = x1
    h2 = _rms(x1, gf_ref[...])
    half = h2.shape[1] // 2
    h2_ref[...] = _pack_bf16_pairs(h2[:, :half], h2[:, half:])

    lg = jnp.dot(h2.astype(BF16), wr_ref[...], preferred_element_type=F32) + br_ref[...]
    tm = lg.shape[0]
    lane = _lane_iota(lg.shape).astype(F32)
    neg = -jnp.inf
    first = lambda hit: jnp.min(jnp.where(hit, lane, float(LANES)), axis=-1, keepdims=True)
    glog = jnp.where(lane < N_EXPERT_GROUPS, lg, neg)
    gmax = jnp.max(glog, axis=-1, keepdims=True)
    p_grp = 1.0 / jnp.sum(jnp.exp(glog - gmax), axis=-1, keepdims=True)
    grp = first(glog == gmax)
    e0 = ROUTER_LANE0 + grp * EXPERTS_PER_GROUP
    elog = jnp.where(jnp.logical_and(lane >= e0, lane < e0 + EXPERTS_PER_GROUP), lg, neg)
    m1 = jnp.max(elog, axis=-1, keepdims=True)
    i1 = first(elog == m1)
    elog2 = jnp.where(lane == i1, neg, elog)
    m2 = jnp.max(elog2, axis=-1, keepdims=True)
    i2 = first(elog2 == m2)
    e21 = jnp.exp(m2 - m1)
    g1 = p_grp * (1.0 / (1.0 + e21))
    g2 = p_grp * (e21 / (1.0 + e21))

    sel1 = lane == i1
    sel2 = lane == i2
    onehot = jnp.where(jnp.logical_or(sel1, sel2), 1.0, 0.0)
    r_i = lax.broadcasted_iota(jnp.int32, (tm, tm), 0)
    c_i = lax.broadcasted_iota(jnp.int32, (tm, tm), 1)
    lower = jnp.where(c_i < r_i, 1.0, 0.0).astype(BF16)
    before = jnp.dot(lower, onehot.astype(BF16), preferred_element_type=F32) + run_ref[0:1, :]
    rank1 = jnp.sum(jnp.where(sel1, before, 0.0), axis=-1, keepdims=True)
    rank2 = jnp.sum(jnp.where(sel2, before, 0.0), axis=-1, keepdims=True)
    run_ref[0:1, :] = run_ref[0:1, :] + jnp.sum(onehot, axis=0, keepdims=True)
    cnt_ref[...] = run_ref[...]

    fields = (i1 - ROUTER_LANE0, i2 - ROUTER_LANE0, g1, g2, rank1, rank2)
    route = jnp.zeros(lg.shape, F32)
    for pos, val in enumerate(fields):
        route = jnp.where(lane == pos, val, route)
    route_ref[...] = route


def _post(x2, y_ssm_t, y_att, w_glu_bf, b_glu, g_ssm, g_att, w_out_bf, g_ffn, w_r_bf, b_r, part):
    n, d = x2.shape
    d_ssm = w_glu_bf.shape[0]
    tm = POST_TM
    nt = y_ssm_t.shape[0] // tm
    steps = n // (tm * MOE_PARTS)
    i0 = part * steps
    row_in = lambda w: pl.BlockSpec((tm, w), lambda i: (i0 + i, 0))
    row = lambda w: pl.BlockSpec((tm, w), lambda i: (i, 0))
    ssm_spec = pl.BlockSpec((tm, d_ssm), lambda i: ((i0 + i) % nt, (i0 + i) // nt))
    full = lambda a: pl.BlockSpec(a.shape, lambda i: (0,) * a.ndim)
    cnt_spec = pl.BlockSpec((8, LANES), lambda i: (0, 0))
    n = n // MOE_PARTS
    return pl.pallas_call(
        functools.partial(_post_kernel, d_ssm=d_ssm),
        grid=(steps,),
        in_specs=[row_in(d), ssm_spec, row_in(y_att.shape[1]), full(w_glu_bf), full(b_glu), full(g_ssm),
                  full(g_att), full(w_out_bf), full(g_ffn), full(w_r_bf), full(b_r)],
        out_specs=[row(d), row(d // 2), row(LANES), cnt_spec],
        out_shape=[jax.ShapeDtypeStruct((n, d), F32), jax.ShapeDtypeStruct((n, d // 2), jnp.uint32),
                   jax.ShapeDtypeStruct((n, LANES), F32), jax.ShapeDtypeStruct((8, LANES), F32)],
        scratch_shapes=[pltpu.VMEM((8, LANES), F32)],
        compiler_params=_cparams("arbitrary"),
        name="post",
    )(x2, y_ssm_t, y_att, w_glu_bf, b_glu, g_ssm, g_att, w_out_bf, g_ffn, w_r_bf, b_r)


def _dispatch_sc(dest, h2p, n_rows):
    n, w = h2p.shape
    workers = SC_CORES * SC_SUBCORES
    n_win = n // (workers * SC_WINDOW)
    d0 = dest[:, 0].reshape(n // SC_WINDOW, SC_WINDOW)
    d1 = dest[:, 1].reshape(n // SC_WINDOW, SC_WINDOW)
    mesh = plsc.VectorSubcoreMesh(core_axis_name="c", subcore_axis_name="s")

    def body(h_hbm, d0_hbm, d1_hbm, o_hbm, rows_v, i0_v, i1_v):
        wid = lax.axis_index("c") * SC_SUBCORES + lax.axis_index("s")

        @pl.loop(0, n_win)
        def _(j):
            blk = wid * n_win + j
            pltpu.sync_copy(h_hbm.at[pl.ds(blk * SC_WINDOW, SC_WINDOW)], rows_v)
            pltpu.sync_copy(d0_hbm.at[blk], i0_v)
            pltpu.sync_copy(d1_hbm.at[blk], i1_v)
            pltpu.sync_copy(rows_v, o_hbm.at[i0_v])
            pltpu.sync_copy(rows_v, o_hbm.at[i1_v])

    return pl.kernel(
        body,
        out_type=jax.ShapeDtypeStruct((n_rows, w), h2p.dtype),
        mesh=mesh,
        scratch_types=[pltpu.VMEM((SC_WINDOW, w), h2p.dtype), pltpu.VMEM((SC_WINDOW,), jnp.int32),
                       pltpu.VMEM((SC_WINDOW,), jnp.int32)],
        name="dispatch_sc",
    )(h2p, d0, d1)


def _experts_kernel(blk_e_ref, next_e_ref, used_ref, x_ref, wg_hbm, wu_hbm, wd_hbm, o_ref,
                    wg_f32, wu_f32, wd_f32, wg_bf, wu_bf, wd_bf, slot_ref, sems):
    i = pl.program_id(0)
    staged = ((wg_hbm, wg_f32, wg_bf), (wu_hbm, wu_f32, wu_bf), (wd_hbm, wd_f32, wd_bf))

    def fetch(expert, slot):
        return [pltpu.make_async_copy(hbm.at[expert], f32.at[slot], sems.at[slot, k])
                for k, (hbm, f32, _) in enumerate(staged)]

    @pl.when(i == 0)
    def _():
        slot_ref[0] = 0
        for cp in fetch(blk_e_ref[0], 0):
            cp.start()

    @pl.when(jnp.logical_or(i == 0, blk_e_ref[i] != blk_e_ref[jnp.maximum(i - 1, 0)]))
    def _():
        slot = slot_ref[0]
        for cp in fetch(blk_e_ref[i], slot):
            cp.wait()
        for _, f32, bf in staged:
            bf[...] = f32[slot].astype(BF16)
        nxt = next_e_ref[i]

        @pl.when(nxt >= 0)
        def _():
            for cp in fetch(nxt, 1 - slot):
                cp.start()

        slot_ref[0] = 1 - slot

    @pl.when(i < used_ref[0])
    def _():
        xa, xb = (t.astype(BF16) for t in _unpack_bf16_pairs(x_ref[...]))
        half = xa.shape[1]
        gate = (jnp.dot(xa, wg_bf[:half, :], preferred_element_type=F32)
                + jnp.dot(xb, wg_bf[half:, :], preferred_element_type=F32))
        up = (jnp.dot(xa, wu_bf[:half, :], preferred_element_type=F32)
              + jnp.dot(xb, wu_bf[half:, :], preferred_element_type=F32))
        hid = (jax.nn.silu(gate) * up).astype(BF16)
        out = jnp.dot(hid, wd_bf[...], preferred_element_type=F32)
        o_ref[...] = _pack_bf16_pairs(out[:, :half], out[:, half:])

    @pl.when(i >= used_ref[0])
    def _():
        o_ref[...] = jnp.zeros_like(o_ref)


def _experts(blk_e, next_e, n_used, buf, w_gate, w_up, w_down):
    n_rows, w = buf.shape
    d = w_down.shape[2]
    hbm = pl.BlockSpec(memory_space=pl.ANY)
    weights = (w_gate, w_up, w_down)
    grid_spec = pltpu.PrefetchScalarGridSpec(
        num_scalar_prefetch=3,
        grid=(n_rows // MOE_ROWS,),
        in_specs=[pl.BlockSpec((MOE_ROWS, w), lambda i, be, ne, nu: (i, 0)), hbm, hbm, hbm],
        out_specs=pl.BlockSpec((MOE_ROWS, d // 2), lambda i, be, ne, nu: (i, 0)),
        scratch_shapes=([pltpu.VMEM((2,) + a.shape[1:], F32) for a in weights]
                        + [pltpu.VMEM(a.shape[1:], BF16) for a in weights]
                        + [pltpu.SMEM((1,), jnp.int32), pltpu.SemaphoreType.DMA((2, len(weights)))]),
    )
    return pl.pallas_call(
        _experts_kernel,
        grid_spec=grid_spec,
        out_shape=jax.ShapeDtypeStruct((n_rows, d // 2), jnp.uint32),
        compiler_params=_cparams("arbitrary"),
        name="experts",
    )(blk_e, next_e, n_used, buf, w_gate, w_up, w_down)


def _gather_sc(dest, eo):
    n = dest.shape[0]
    w = eo.shape[1]
    workers = SC_CORES * SC_SUBCORES
    n_win = n // (workers * SC_WINDOW)
    d0 = dest[:, 0].reshape(n // SC_WINDOW, SC_WINDOW)
    d1 = dest[:, 1].reshape(n // SC_WINDOW, SC_WINDOW)
    mesh = plsc.VectorSubcoreMesh(core_axis_name="c", subcore_axis_name="s")

    def body(eo_hbm, d0_hbm, d1_hbm, o0_hbm, o1_hbm, rows_v, i_v):
        wid = lax.axis_index("c") * SC_SUBCORES + lax.axis_index("s")

        @pl.loop(0, n_win)
        def _(j):
            blk = wid * n_win + j
            for d_hbm, o_hbm in ((d0_hbm, o0_hbm), (d1_hbm, o1_hbm)):
                pltpu.sync_copy(d_hbm.at[blk], i_v)
                pltpu.sync_copy(eo_hbm.at[i_v], rows_v)
                pltpu.sync_copy(rows_v, o_hbm.at[pl.ds(blk * SC_WINDOW, SC_WINDOW)])

    out_sd = jax.ShapeDtypeStruct((n, w), eo.dtype)
    return pl.kernel(
        body,
        out_type=(out_sd, out_sd),
        mesh=mesh,
        scratch_types=[pltpu.VMEM((SC_WINDOW, w), eo.dtype), pltpu.VMEM((SC_WINDOW,), jnp.int32)],
        name="gather_sc",
    )(eo, d0, d1)


def _combine_kernel(x1_ref, route_ref, r0_ref, r1_ref, *rest):
    o_ref = rest[-1]
    route = route_ref[...]
    row0 = jnp.concatenate(_unpack_bf16_pairs(r0_ref[...]), axis=1)
    row1 = jnp.concatenate(_unpack_bf16_pairs(r1_ref[...]), axis=1)
    o_ref[...] = x1_ref[...] + (route[:, 2:3] * row0 + route[:, 3:4] * row1)


def _combine(x1, route, rows0, rows1, part, out_prev):
    n_slice, d = x1.shape
    tm = COMBINE_TM
    steps = n_slice // tm
    row = lambda w: pl.BlockSpec((tm, w), lambda i: (i, 0))
    in_specs = [row(d), row(LANES), row(d // 2), row(d // 2)]
    args = [x1, route, rows0, rows1]
    aliases = {}
    if out_prev is not None:
        in_specs.append(pl.BlockSpec(memory_space=pl.ANY))
        args.append(out_prev)
        aliases = {len(args) - 1: 0}
    return pl.pallas_call(
        _combine_kernel,
        grid=(steps,),
        in_specs=in_specs,
        out_specs=pl.BlockSpec((tm, d), lambda i: (part * steps + i, 0)),
        out_shape=jax.ShapeDtypeStruct((n_slice * MOE_PARTS, d), F32),
        input_output_aliases=aliases,
        compiler_params=_cparams("parallel"),
        name="combine",
    )(*args)


def _layer(x, g_mix, w_in, lam_re, lam_im, log_dt, b_re, b_im, c_re, c_im, d_skip, w_glu, b_glu, g_q, g_k,
           g_ssm_out, g_attn_out, w_out, g_ffn, w_rg, b_rg, w_re, b_re_router, w_gate, w_up, w_down):
    bsz, t_len, d = x.shape
    n = bsz * t_len
    d_ssm = w_glu.shape[0]
    d_att = g_attn_out.shape[0]
    n_heads = d_att // HEAD_DIM
    n_chunks = t_len // SSM_CHUNK
    x2 = x.reshape(n, d)

    u_t, q, k, v = _in_proj(x2, g_mix[None], w_in.astype(BF16), jnp.tile(g_q, n_heads)[None],
                            jnp.tile(g_k, n_heads)[None], d_ssm, d_att, bsz, t_len)
    tables = _s5_tables(lam_re, lam_im, log_dt, b_re, b_im, c_re, c_im, d_skip)
    y_ssm_t = _s5(u_t.reshape(n_chunks, SSM_CHUNK, bsz, d_ssm), tables).reshape(t_len, bsz * d_ssm)

    y_att = _attention(q, k, v, bsz, t_len)

    w_r = jnp.concatenate([w_rg, w_re.reshape(d, N_EXPERTS)], axis=1)
    w_r = jnp.pad(w_r, ((0, 0), (0, LANES - w_r.shape[1]))).astype(BF16)
    b_r = jnp.pad(jnp.concatenate([b_rg, b_re_router.reshape(N_EXPERTS)]), (0, LANES - ROUTER_LANE0 - N_EXPERTS))[None]
    w_glu_bf, w_out_bf = w_glu.astype(BF16), w_out.astype(BF16)
    n_slice = n // MOE_PARTS
    n_rows = n_slice * 2 + N_EXPERTS * MOE_ROWS
    n_blk = n_rows // MOE_ROWS
    out = None
    for part in range(MOE_PARTS):
        x1, h2p, route, cnt = _post(x2, y_ssm_t, y_att, w_glu_bf, b_glu[None], g_ssm_out[None],
                                    g_attn_out[None], w_out_bf, g_ffn[None], w_r, b_r, part)
        counts = cnt[0, ROUTER_LANE0:ROUTER_LANE0 + N_EXPERTS].astype(jnp.int32)
        pcounts = ((counts + MOE_ROWS - 1) // MOE_ROWS) * MOE_ROWS
        pends = jnp.cumsum(pcounts)
        pstarts = pends - pcounts
        expert = route[:, 0:2].astype(jnp.int32)
        e_ids = jnp.arange(N_EXPERTS, dtype=jnp.int32)
        start = jnp.sum(jnp.where(expert[..., None] == e_ids, pstarts, 0), axis=-1)
        dest = start + route[:, 4:6].astype(jnp.int32)
        blk_row0 = jnp.arange(n_blk, dtype=jnp.int32)[:, None] * MOE_ROWS
        blk_e = jnp.minimum(jnp.sum((pends[None, :] <= blk_row0).astype(jnp.int32), axis=1), N_EXPERTS - 1)
        n_used = (pends[-1:] // MOE_ROWS).astype(jnp.int32)
        present = jnp.any(blk_e[:, None] == e_ids, axis=0)
        later = jnp.logical_and(present[None, :], e_ids[None, :] > blk_e[:, None])
        next_e = jnp.min(jnp.where(later, e_ids[None, :], N_EXPERTS), axis=1)
        next_e = jnp.where(next_e == N_EXPERTS, -1, next_e).astype(jnp.int32)

        buf = _dispatch_sc(dest, h2p, n_rows)
        eo = _experts(blk_e, next_e, n_used, buf, w_gate, w_up, w_down)
        out = _combine(x1, route, *_gather_sc(dest, eo), part, out)
    return out.reshape(bsz, t_len, d)


def kernel(x, g_mix, w_in, ssm_lambda_re, ssm_lambda_im, ssm_log_dt, ssm_b_re, ssm_b_im, ssm_c_re, ssm_c_im, ssm_d, ssm_w_glu, ssm_b_glu, g_q, g_k, g_ssm_out, g_attn_out, w_out, g_ffn, w_router_group, b_router_group, w_router_expert, b_router_expert, w_gate, w_up, w_down):
    for l in range(g_mix.shape[0]):
        x = _layer(x, g_mix[l], w_in[l], ssm_lambda_re[l], ssm_lambda_im[l], ssm_log_dt[l], ssm_b_re[l],
                   ssm_b_im[l], ssm_c_re[l], ssm_c_im[l], ssm_d[l], ssm_w_glu[l], ssm_b_glu[l], g_q[l], g_k[l],
                   g_ssm_out[l], g_attn_out[l], w_out[l], g_ffn[l], w_router_group[l], b_router_group[l],
                   w_router_expert[l], b_router_expert[l], w_gate[l], w_up[l], w_down[l])
    return x
```

```python
import functools
import math

import jax
import jax.numpy as jnp
from jax import lax
from jax.experimental import pallas as pl
from jax.experimental.pallas import tpu as pltpu
from jax.experimental.pallas import tpu_sc as plsc

F32 = jnp.float32
BF16 = jnp.bfloat16
EPS = 1e-6

LANES = 128
VMEM_LIMIT_BYTES = 56 * 1024 * 1024

SSM_GROUP = 16
SSM_STATE = 64
SSM_CHUNK = 16
HEAD_DIM = 64
N_EXPERT_GROUPS = 4
EXPERTS_PER_GROUP = 8
N_EXPERTS = N_EXPERT_GROUPS * EXPERTS_PER_GROUP
ROUTER_LANE0 = N_EXPERT_GROUPS
ROUTE_FIELDS = 8
MOE_ROWS = 512
MOE_PARTS = 2
ATT_SKIP = 110.0

S5_GPB = LANES // SSM_GROUP
S5_CHUNKS_PER_STEP = 8

IN_TM = 512
ATT_TILE = 256
ATT_KEYS_AHEAD = 256
ATT_TILES_PER_STEP = 2
POST_TM = 512
COMBINE_TM = 512
SC_CORES = 2
SC_SUBCORES = 16
SC_WINDOW = 64


def _cparams(*sem):
    return pltpu.CompilerParams(dimension_semantics=sem, vmem_limit_bytes=VMEM_LIMIT_BYTES)


def _lane_iota(shape):
    return lax.broadcasted_iota(jnp.int32, shape, len(shape) - 1)


def _head_rms(t, gain):
    outs = []
    for c in range(t.shape[1] // LANES):
        blk = t[:, c * LANES:(c + 1) * LANES]
        sq = blk * blk
        lo = _lane_iota(blk.shape) < HEAD_DIM
        s_lo = jnp.sum(jnp.where(lo, sq, 0.0), axis=-1, keepdims=True)
        s_hi = jnp.sum(jnp.where(lo, 0.0, sq), axis=-1, keepdims=True)
        inv = jnp.where(lo, lax.rsqrt(s_lo * (1.0 / HEAD_DIM) + EPS),
                        lax.rsqrt(s_hi * (1.0 / HEAD_DIM) + EPS))
        outs.append(blk * inv * gain[:, c * LANES:(c + 1) * LANES])
    return jnp.concatenate(outs, axis=-1)


def _in_proj_kernel(x_ref, g_ref, w_ref, gq_ref, gk_ref, u_ref, q_ref, k_ref, v_ref, *, d_ssm, d_att, scale):
    x = x_ref[...]
    inv = lax.rsqrt(jnp.mean(x * x, axis=-1, keepdims=True) + EPS)
    h = (x * inv * g_ref[...]).astype(BF16)
    proj = jnp.dot(h, w_ref[...], preferred_element_type=F32)
    u_ref[...] = proj[:, :d_ssm].astype(BF16)
    q = _head_rms(proj[:, d_ssm:d_ssm + d_att], gq_ref[...])
    k = _head_rms(proj[:, d_ssm + d_att:d_ssm + 2 * d_att], gk_ref[...])
    q_ref[...] = (q * scale).astype(BF16)
    k_ref[...] = k.astype(BF16)
    v_ref[...] = proj[:, d_ssm + 2 * d_att:].astype(BF16)


def _in_proj(x2, g_mix, w_in_bf, gq_t, gk_t, d_ssm, d_att, bsz, t_len):
    n, d = x2.shape
    tm = IN_TM
    nt = t_len // tm
    row = lambda w: pl.BlockSpec((tm, w), lambda b, t: (b * nt + t, 0))
    full = lambda a: pl.BlockSpec(a.shape, lambda b, t: (0,) * a.ndim)
    out_sd = jax.ShapeDtypeStruct((n, d_att), BF16)
    return pl.pallas_call(
        functools.partial(_in_proj_kernel, d_ssm=d_ssm, d_att=d_att, scale=1.0 / math.sqrt(HEAD_DIM)),
        grid=(bsz, nt),
        in_specs=[row(d), full(g_mix), full(w_in_bf), full(gq_t), full(gk_t)],
        out_specs=[pl.BlockSpec((tm, d_ssm), lambda b, t: (t, b)), row(d_att), row(d_att), row(d_att)],
        out_shape=[jax.ShapeDtypeStruct((t_len, bsz * d_ssm), BF16), out_sd, out_sd, out_sd],
        compiler_params=_cparams("parallel", "parallel"),
        name="in_proj",
    )(x2, g_mix, w_in_bf, gq_t, gk_t)


def _s5_tables(lam_re, lam_im, log_dt, b_re, b_im, c_re, c_im, d_skip):
    hp = lax.Precision.HIGHEST
    L = SSM_CHUNK
    g_n, p_n = lam_re.shape
    dt = jnp.exp(log_dt)[:, None]
    lr, li = lam_re, lam_im
    ls = jnp.arange(L + 1, dtype=F32)[:, None, None]
    mag = jnp.exp(lr * dt * ls)
    pr, pi = mag * jnp.cos(li * dt * ls), mag * jnp.sin(li * dt * ls)
    abar_r, abar_i = pr[1], pi[1]
    den = lr * lr + li * li
    nr, ni = abar_r - 1.0, abar_i
    coef_r = (nr * lr + ni * li) / den
    coef_i = (ni * lr - nr * li) / den
    bbr = coef_r[..., None] * b_re - coef_i[..., None] * b_im
    bbi = coef_r[..., None] * b_im + coef_i[..., None] * b_re
    wr = pr[..., None] * bbr - pi[..., None] * bbi
    wi = pr[..., None] * bbi + pi[..., None] * bbr
    kl = (jnp.einsum('gop,lgpi->lgoi', c_re, wr[:L], precision=hp)
          - jnp.einsum('gop,lgpi->lgoi', c_im, wi[:L], precision=hp))
    kl = kl.at[0].add(jax.vmap(jnp.diag)(d_skip))
    n_lb = g_n // S5_GPB
    sg = SSM_GROUP
    kc = kl.transpose(1, 3, 0, 2).reshape(n_lb, S5_GPB * sg, L * sg)
    b_rows = lambda w: (w[:L][::-1].reshape(L, n_lb, S5_GPB, p_n, sg).transpose(1, 0, 2, 4, 3)
                        .reshape(n_lb, L * S5_GPB * sg, p_n))
    p1r, p1i = pr[1:], pi[1:]
    cst_r = (c_re[None] * p1r[:, :, None, :] - c_im[None] * p1i[:, :, None, :])
    cst_i = -(c_re[None] * p1i[:, :, None, :] + c_im[None] * p1r[:, :, None, :])
    c_rows = lambda c: c.transpose(1, 3, 0, 2).reshape(n_lb, S5_GPB * p_n, L * sg)
    a_l = jnp.stack([pr[L].reshape(n_lb, S5_GPB * p_n), pi[L].reshape(n_lb, S5_GPB * p_n)], axis=1)
    return (kc.astype(BF16), b_rows(wr).astype(BF16), b_rows(wi).astype(BF16),
            c_rows(cst_r).astype(BF16), c_rows(cst_i).astype(BF16), a_l)


def _s5_expand_tables(kc_ref, bcr_ref, bci_ref, ccr_ref, cci_ref, w0_ref, br_ref, bi_ref, cr_ref, ci_ref):
    sg, gpb = SSM_GROUP, S5_GPB
    n_lo = w0_ref.shape[1]
    p_n = bcr_ref.shape[2]

    def iotas(shape):
        return lax.broadcasted_iota(jnp.int32, shape, 0), lax.broadcasted_iota(jnp.int32, shape, 1)

    r, c = iotas((kc_ref.shape[2], n_lo))
    spread_o = jnp.where(jnp.logical_and(r // sg == c // LANES, r % sg == c % sg), 1.0, 0.0).astype(BF16)
    r, c = iotas((p_n, gpb * p_n))
    spread_p = jnp.where(r == c % p_n, 1.0, 0.0).astype(BF16)

    def expand(compact, spread, row_group, col_group):
        full = jnp.dot(compact, spread, preferred_element_type=F32)
        r, c = iotas(full.shape)
        return jnp.where(row_group(r) == col_group(c), full, 0.0).astype(BF16)

    lane_group = lambda c: (c % LANES) // sg
    top = expand(kc_ref[0], spread_o, lambda r: r // sg, lane_group)
    w0_ref[:LANES, :] = top
    w0_ref[LANES:, :LANES] = jnp.zeros((LANES, LANES), BF16)
    w0_ref[LANES:, LANES:] = top[:, :n_lo - LANES]
    state_group = lambda c: c // p_n
    br_ref[...] = expand(bcr_ref[0], spread_p, lambda r: (r // sg) % gpb, state_group)
    bi_ref[...] = expand(bci_ref[0], spread_p, lambda r: (r // sg) % gpb, state_group)
    cr_ref[...] = expand(ccr_ref[0], spread_o, lambda r: r // p_n, lane_group)
    ci_ref[...] = expand(cci_ref[0], spread_o, lambda r: r // p_n, lane_group)


def _s5_kernel(u_ref, kc_ref, bcr_ref, bci_ref, ccr_ref, cci_ref, a_ref, y_ref,
               hr_ref, hi_ref, acc_ref, w0_ref, br_ref, bi_ref, cr_ref, ci_ref):
    n_chunks, L, bsz, _ = u_ref.shape
    rows = n_chunks * bsz

    @pl.when(pl.program_id(1) == 0)
    def _():
        hr_ref[...] = jnp.zeros_like(hr_ref)
        hi_ref[...] = jnp.zeros_like(hi_ref)
        _s5_expand_tables(kc_ref, bcr_ref, bci_ref, ccr_ref, cci_ref, w0_ref, br_ref, bi_ref, cr_ref, ci_ref)

    us = [u_ref[:, s].reshape(rows, LANES) for s in range(L)]
    lhs = jnp.concatenate(us, axis=1)
    sin_r = jnp.dot(lhs, br_ref[...], preferred_element_type=F32)
    sin_i = jnp.dot(lhs, bi_ref[...], preferred_element_type=F32)
    ar = a_ref[0, 0:1, :]
    ai = a_ref[0, 1:2, :]
    hr, hi = hr_ref[...], hi_ref[...]
    prev_r, prev_i = [], []
    for c in range(n_chunks):
        prev_r.append(hr)
        prev_i.append(hi)
        sl = slice(c * bsz, (c + 1) * bsz)
        hr, hi = ar * hr - ai * hi + sin_r[sl], ar * hi + ai * hr + sin_i[sl]
    hr_ref[...] = hr
    hi_ref[...] = hi
    pr = jnp.concatenate(prev_r, axis=0).astype(BF16)
    pi = jnp.concatenate(prev_i, axis=0).astype(BF16)
    acc_ref[...] = (jnp.dot(pr, cr_ref[...], preferred_element_type=F32)
                    + jnp.dot(pi, ci_ref[...], preferred_element_type=F32))
    for p in range(L // 2):
        off = 2 * p * LANES
        pair = jnp.concatenate([us[2 * p], us[2 * p + 1]], axis=1)
        acc_ref[:, off:] += jnp.dot(pair, w0_ref[:, :L * LANES - off], preferred_element_type=F32)
    for t in range(L):
        y_ref[:, t] = acc_ref[:, t * LANES:(t + 1) * LANES].reshape(n_chunks, bsz, LANES)


def _s5(u4, tables):
    n_chunks, L, bsz, d_ssm = u4.shape
    a_l = tables[-1]
    cb = S5_CHUNKS_PER_STEP
    data = pl.BlockSpec((cb, L, bsz, LANES), lambda lb, c: (c, 0, 0, lb))
    per_lb = lambda a: pl.BlockSpec((1,) + a.shape[1:], lambda lb, c: (lb,) + (0,) * (a.ndim - 1))
    n_state = a_l.shape[2]
    state = pltpu.VMEM((bsz, n_state), F32)
    n_lo = L * LANES
    return pl.pallas_call(
        _s5_kernel,
        grid=(d_ssm // LANES, n_chunks // cb),
        in_specs=[data] + [per_lb(t) for t in tables],
        out_specs=data,
        out_shape=jax.ShapeDtypeStruct(u4.shape, F32),
        scratch_shapes=[state, state, pltpu.VMEM((cb * bsz, n_lo), F32),
                        pltpu.VMEM((2 * LANES, n_lo), BF16),
                        pltpu.VMEM((n_lo, n_state), BF16), pltpu.VMEM((n_lo, n_state), BF16),
                        pltpu.VMEM((n_state, n_lo), BF16), pltpu.VMEM((n_state, n_lo), BF16)],
        compiler_params=_cparams("arbitrary", "arbitrary"),
        name="s5",
    )(u4, *tables)


def _softplus(z):
    return jnp.maximum(z, 0.0) + jnp.log(1.0 + jnp.exp(-jnp.abs(z)))


def _att_tile(qm, k, v, tri, r_in, causal):
    z = lax.dot_general(qm, k, (((1,), (1,)), ((), ())), preferred_element_type=F32)
    sp = _softplus(z)
    if causal:
        rows = lax.broadcasted_iota(jnp.int32, z.shape, 0)
        cols = lax.broadcasted_iota(jnp.int32, z.shape, 1)
        keep = cols < rows
        sp_m = jnp.where(keep, sp, 0.0)
    else:
        sp_m = sp
    newer = jnp.dot(sp_m.astype(BF16), tri, preferred_element_type=F32)
    att = jnp.exp(z - sp - newer - r_in)
    if causal:
        att = jnp.where(keep, att, 0.0)
    pv = jnp.dot(att.astype(BF16), v, preferred_element_type=F32)
    return pv, newer[:, 0:1] + sp_m[:, 0:1]


def _attn_kernel(q_ref, k_ref, v_ref, o_ref, *, t_len, tile):
    nq = t_len // tile
    r_i = lax.broadcasted_iota(jnp.int32, (tile, tile), 0)
    c_i = lax.broadcasted_iota(jnp.int32, (tile, tile), 1)
    tri = jnp.where(r_i > c_i, 1.0, 0.0).astype(BF16)
    head0 = _lane_iota((tile, LANES)) < HEAD_DIM
    zero_r = jnp.zeros((tile, 1), F32)

    def near(q0, n_prev):
        q = q_ref[pl.ds(q0, tile), :]
        zq = jnp.zeros_like(q)
        qms = (jnp.where(head0, q, zq), jnp.where(head0, zq, q))
        kvs = [(k_ref[pl.ds(q0 - p * tile, tile), :], v_ref[pl.ds(q0 - p * tile, tile), :])
               for p in range(n_prev + 1)]
        accs, rs = [], []
        for qm in qms:
            acc, r = _att_tile(qm, kvs[0][0], kvs[0][1], tri, zero_r, causal=True)
            for k_p, v_p in kvs[1:]:
                pv, dr = _att_tile(qm, k_p, v_p, tri, r, causal=False)
                acc, r = acc + pv, r + dr
            accs.append(acc)
            rs.append(r)
        return qms, accs, rs

    def far_and_store(q0, qms, accs, rs, j_older):
        if j_older is not None:
            def cond(c):
                j, _, _, r0, r1 = c
                return jnp.logical_and(j >= 0, jnp.min(jnp.minimum(r0, r1)) < ATT_SKIP)

            def body(c):
                j, a0, a1, r0, r1 = c
                k0 = pl.multiple_of(j * tile, tile)
                k_j, v_j = k_ref[pl.ds(k0, tile), :], v_ref[pl.ds(k0, tile), :]
                pv0, d0 = _att_tile(qms[0], k_j, v_j, tri, r0, causal=False)
                pv1, d1 = _att_tile(qms[1], k_j, v_j, tri, r1, causal=False)
                return j - 1, a0 + pv0, a1 + pv1, r0 + d0, r1 + d1

            _, a0, a1, _, _ = lax.while_loop(cond, body, (j_older, accs[0], accs[1], rs[0], rs[1]))
            accs = [a0, a1]
        o_ref[pl.ds(q0, tile), :] = jnp.where(head0, accs[0], accs[1])

    n_prev = ATT_KEYS_AHEAD // tile
    group = ATT_TILES_PER_STEP
    first = n_prev + (nq - n_prev) % group
    for i in range(first):
        far_and_store(i * tile, *near(i * tile, min(i, n_prev)), None if i <= n_prev else i - n_prev - 1)

    def later(g, _):
        tiles = [first + g * group + t for t in range(group)]
        parts = [near(pl.multiple_of(i * tile, tile), n_prev) for i in tiles]
        for i, part in zip(tiles, parts):
            far_and_store(pl.multiple_of(i * tile, tile), *part, i - n_prev - 1)
        return 0

    lax.fori_loop(0, (nq - first) // group, later, 0)


def _attention(q, k, v, bsz, t_len):
    n, d_att = q.shape
    spec = pl.BlockSpec((t_len, LANES), lambda b, p: (b, p))
    return pl.pallas_call(
        functools.partial(_attn_kernel, t_len=t_len, tile=ATT_TILE),
        grid=(bsz, d_att // LANES),
        in_specs=[spec, spec, spec],
        out_specs=spec,
        out_shape=jax.ShapeDtypeStruct((n, d_att), F32),
        compiler_params=_cparams("parallel", "parallel"),
        name="attn",
    )(q, k, v)


def _rms(t, gain):
    return t * lax.rsqrt(jnp.mean(t * t, axis=-1, keepdims=True) + EPS) * gain


def _gelu_tanh(y):
    return 0.5 * y * (1.0 + jnp.tanh(math.sqrt(2.0 / math.pi) * (y + 0.044715 * (y * y * y))))


def _pack_bf16_pairs(a, b):
    ua = pltpu.bitcast(a.astype(BF16).astype(F32), jnp.uint32)
    ub = pltpu.bitcast(b.astype(BF16).astype(F32), jnp.uint32)
    return ua | (ub >> 16)


def _unpack_bf16_pairs(w):
    return pltpu.bitcast(w & jnp.uint32(0xFFFF0000), F32), pltpu.bitcast(w << 16, F32)


def _post_kernel(x_ref, ys_ref, ya_ref, wglu_ref, bglu_ref, gs_ref, ga_ref, wo_ref, gf_ref,
                 wr_ref, br_ref, x1_ref, h2_ref, route_ref, route_t_ref, cnt_ref, run_ref, *, d_ssm):
    i = pl.program_id(0)

    @pl.when(i == 0)
    def _():
        run_ref[...] = jnp.zeros_like(run_ref)

    y = _gelu_tanh(ys_ref[...])
    gate = jnp.dot(y.astype(BF16), wglu_ref[...], preferred_element_type=F32) + bglu_ref[...]
    y = y * jax.nn.sigmoid(gate)
    m_s = _rms(y, gs_ref[...]).astype(BF16)
    m_a = _rms(ya_ref[...], ga_ref[...]).astype(BF16)
    mix = (jnp.dot(m_s, wo_ref[:d_ssm, :], preferred_element_type=F32)
           + jnp.dot(m_a, wo_ref[d_ssm:, :], preferred_element_type=F32))
    x1 = x_ref[...] + mix
    x1_ref[...] = x1
    h2 = _rms(x1, gf_ref[...])
    half = h2.shape[1] // 2
    h2_ref[...] = _pack_bf16_pairs(h2[:, :half], h2[:, half:])

    lg = jnp.dot(h2.astype(BF16), wr_ref[...], preferred_element_type=F32) + br_ref[...]
    tm = lg.shape[0]
    lane = _lane_iota(lg.shape).astype(F32)
    neg = -jnp.inf
    first = lambda hit: jnp.min(jnp.where(hit, lane, float(LANES)), axis=-1, keepdims=True)
    glog = jnp.where(lane < N_EXPERT_GROUPS, lg, neg)
    gmax = jnp.max(glog, axis=-1, keepdims=True)
    p_grp = 1.0 / jnp.sum(jnp.exp(glog - gmax), axis=-1, keepdims=True)
    grp = first(glog == gmax)
    e0 = ROUTER_LANE0 + grp * EXPERTS_PER_GROUP
    elog = jnp.where(jnp.logical_and(lane >= e0, lane < e0 + EXPERTS_PER_GROUP), lg, neg)
    m1 = jnp.max(elog, axis=-1, keepdims=True)
    i1 = first(elog == m1)
    elog2 = jnp.where(lane == i1, neg, elog)
    m2 = jnp.max(elog2, axis=-1, keepdims=True)
    i2 = first(elog2 == m2)
    e21 = jnp.exp(m2 - m1)
    g1 = p_grp * (1.0 / (1.0 + e21))
    g2 = p_grp * (e21 / (1.0 + e21))

    sel1 = lane == i1
    sel2 = lane == i2
    onehot = jnp.where(jnp.logical_or(sel1, sel2), 1.0, 0.0)
    r_i = lax.broadcasted_iota(jnp.int32, (tm, tm), 0)
    c_i = lax.broadcasted_iota(jnp.int32, (tm, tm), 1)
    lower = jnp.where(c_i < r_i, 1.0, 0.0).astype(BF16)
    before = jnp.dot(lower, onehot.astype(BF16), preferred_element_type=F32) + run_ref[0:1, :]
    rank1 = jnp.sum(jnp.where(sel1, before, 0.0), axis=-1, keepdims=True)
    rank2 = jnp.sum(jnp.where(sel2, before, 0.0), axis=-1, keepdims=True)
    run_ref[0:1, :] = run_ref[0:1, :] + jnp.sum(onehot, axis=0, keepdims=True)
    cnt_ref[...] = run_ref[...]

    fields = (i1 - ROUTER_LANE0, i2 - ROUTER_LANE0, g1, g2, rank1, rank2)
    route = jnp.zeros(lg.shape, F32)
    for pos, val in enumerate(fields):
        route = jnp.where(lane == pos, val, route)
    route_ref[...] = route
    route_t_ref[...] = route.T[:ROUTE_FIELDS, :]


def _post(x2, y_ssm_t, y_att, w_glu_bf, b_glu, g_ssm, g_att, w_out_bf, g_ffn, w_r_bf, b_r, part):
    n, d = x2.shape
    d_ssm = w_glu_bf.shape[0]
    tm = POST_TM
    nt = y_ssm_t.shape[0] // tm
    steps = n // (tm * MOE_PARTS)
    i0 = part * steps
    row_in = lambda w: pl.BlockSpec((tm, w), lambda i: (i0 + i, 0))
    row = lambda w: pl.BlockSpec((tm, w), lambda i: (i, 0))
    ssm_spec = pl.BlockSpec((tm, d_ssm), lambda i: ((i0 + i) % nt, (i0 + i) // nt))
    full = lambda a: pl.BlockSpec(a.shape, lambda i: (0,) * a.ndim)
    cnt_spec = pl.BlockSpec((8, LANES), lambda i: (0, 0))
    n = n // MOE_PARTS
    return pl.pallas_call(
        functools.partial(_post_kernel, d_ssm=d_ssm),
        grid=(steps,),
        in_specs=[row_in(d), ssm_spec, row_in(y_att.shape[1]), full(w_glu_bf), full(b_glu), full(g_ssm),
                  full(g_att), full(w_out_bf), full(g_ffn), full(w_r_bf), full(b_r)],
        out_specs=[row(d), row(d // 2), row(LANES), pl.BlockSpec((ROUTE_FIELDS, tm), lambda i: (0, i)), cnt_spec],
        out_shape=[jax.ShapeDtypeStruct((n, d), F32), jax.ShapeDtypeStruct((n, d // 2), jnp.uint32),
                   jax.ShapeDtypeStruct((n, LANES), F32), jax.ShapeDtypeStruct((ROUTE_FIELDS, n), F32),
                   jax.ShapeDtypeStruct((8, LANES), F32)],
        scratch_shapes=[pltpu.VMEM((8, LANES), F32)],
        compiler_params=_cparams("arbitrary"),
        name="post",
    )(x2, y_ssm_t, y_att, w_glu_bf, b_glu, g_ssm, g_att, w_out_bf, g_ffn, w_r_bf, b_r)


def _dispatch_sc(d0, d1, h2p, n_rows):
    n, w = h2p.shape
    workers = SC_CORES * SC_SUBCORES
    n_win = n // (workers * SC_WINDOW)
    mesh = plsc.VectorSubcoreMesh(core_axis_name="c", subcore_axis_name="s")

    def body(h_hbm, d0_hbm, d1_hbm, o_hbm, rows_v, i0_v, i1_v):
        wid = lax.axis_index("c") * SC_SUBCORES + lax.axis_index("s")

        @pl.loop(0, n_win)
        def _(j):
            blk = wid * n_win + j
            pltpu.sync_copy(h_hbm.at[pl.ds(blk * SC_WINDOW, SC_WINDOW)], rows_v)
            pltpu.sync_copy(d0_hbm.at[blk], i0_v)
            pltpu.sync_copy(d1_hbm.at[blk], i1_v)
            pltpu.sync_copy(rows_v, o_hbm.at[i0_v])
            pltpu.sync_copy(rows_v, o_hbm.at[i1_v])

    return pl.kernel(
        body,
        out_type=jax.ShapeDtypeStruct((n_rows, w), h2p.dtype),
        mesh=mesh,
        scratch_types=[pltpu.VMEM((SC_WINDOW, w), h2p.dtype), pltpu.VMEM((SC_WINDOW,), jnp.int32),
                       pltpu.VMEM((SC_WINDOW,), jnp.int32)],
        name="dispatch_sc",
    )(h2p, d0, d1)


def _experts_kernel(blk_e_ref, next_e_ref, used_ref, x_ref, wg_hbm, wu_hbm, wd_hbm, o_ref,
                    wg_f32, wu_f32, wd_f32, wg_bf, wu_bf, wd_bf, slot_ref, sems):
    i = pl.program_id(0)
    staged = ((wg_hbm, wg_f32, wg_bf), (wu_hbm, wu_f32, wu_bf), (wd_hbm, wd_f32, wd_bf))

    def fetch(expert, slot):
        return [pltpu.make_async_copy(hbm.at[expert], f32.at[slot], sems.at[slot, k])
                for k, (hbm, f32, _) in enumerate(staged)]

    @pl.when(i == 0)
    def _():
        slot_ref[0] = 0
        for cp in fetch(blk_e_ref[0], 0):
            cp.start()

    @pl.when(jnp.logical_or(i == 0, blk_e_ref[i] != blk_e_ref[jnp.maximum(i - 1, 0)]))
    def _():
        slot = slot_ref[0]
        for cp in fetch(blk_e_ref[i], slot):
            cp.wait()
        for _, f32, bf in staged:
            bf[...] = f32[slot].astype(BF16)
        nxt = next_e_ref[i]

        @pl.when(nxt >= 0)
        def _():
            for cp in fetch(nxt, 1 - slot):
                cp.start()

        slot_ref[0] = 1 - slot

    @pl.when(i < used_ref[0])
    def _():
        xa, xb = (t.astype(BF16) for t in _unpack_bf16_pairs(x_ref[...]))
        half = xa.shape[1]
        gate = (jnp.dot(xa, wg_bf[:half, :], preferred_element_type=F32)
                + jnp.dot(xb, wg_bf[half:, :], preferred_element_type=F32))
        up = (jnp.dot(xa, wu_bf[:half, :], preferred_element_type=F32)
              + jnp.dot(xb, wu_bf[half:, :], preferred_element_type=F32))
        hid = (jax.nn.silu(gate) * up).astype(BF16)
        out = jnp.dot(hid, wd_bf[...], preferred_element_type=F32)
        o_ref[...] = _pack_bf16_pairs(out[:, :half], out[:, half:])

    @pl.when(i >= used_ref[0])
    def _():
        o_ref[...] = jnp.zeros_like(o_ref)


def _experts(blk_e, next_e, n_used, buf, w_gate, w_up, w_down):
    n_rows, w = buf.shape
    d = w_down.shape[2]
    hbm = pl.BlockSpec(memory_space=pl.ANY)
    weights = (w_gate, w_up, w_down)
    grid_spec = pltpu.PrefetchScalarGridSpec(
        num_scalar_prefetch=3,
        grid=(n_rows // MOE_ROWS,),
        in_specs=[pl.BlockSpec((MOE_ROWS, w), lambda i, be, ne, nu: (i, 0)), hbm, hbm, hbm],
        out_specs=pl.BlockSpec((MOE_ROWS, d // 2), lambda i, be, ne, nu: (i, 0)),
        scratch_shapes=([pltpu.VMEM((2,) + a.shape[1:], F32) for a in weights]
                        + [pltpu.VMEM(a.shape[1:], BF16) for a in weights]
                        + [pltpu.SMEM((1,), jnp.int32), pltpu.SemaphoreType.DMA((2, len(weights)))]),
    )
    return pl.pallas_call(
        _experts_kernel,
        grid_spec=grid_spec,
        out_shape=jax.ShapeDtypeStruct((n_rows, d // 2), jnp.uint32),
        compiler_params=_cparams("arbitrary"),
        name="experts",
    )(blk_e, next_e, n_used, buf, w_gate, w_up, w_down)


def _gather_sc(d0, d1, eo):
    n = d0.size
    w = eo.shape[1]
    workers = SC_CORES * SC_SUBCORES
    n_win = n // (workers * SC_WINDOW)
    mesh = plsc.VectorSubcoreMesh(core_axis_name="c", subcore_axis_name="s")

    def body(eo_hbm, d0_hbm, d1_hbm, o0_hbm, o1_hbm, rows_v, i_v):
        wid = lax.axis_index("c") * SC_SUBCORES + lax.axis_index("s")

        @pl.loop(0, n_win)
        def _(j):
            blk = wid * n_win + j
            for d_hbm, o_hbm in ((d0_hbm, o0_hbm), (d1_hbm, o1_hbm)):
                pltpu.sync_copy(d_hbm.at[blk], i_v)
                pltpu.sync_copy(eo_hbm.at[i_v], rows_v)
                pltpu.sync_copy(rows_v, o_hbm.at[pl.ds(blk * SC_WINDOW, SC_WINDOW)])

    out_sd = jax.ShapeDtypeStruct((n, w), eo.dtype)
    return pl.kernel(
        body,
        out_type=(out_sd, out_sd),
        mesh=mesh,
        scratch_types=[pltpu.VMEM((SC_WINDOW, w), eo.dtype), pltpu.VMEM((SC_WINDOW,), jnp.int32)],
        name="gather_sc",
    )(eo, d0, d1)


def _combine_kernel(x1_ref, route_ref, r0_ref, r1_ref, *rest):
    o_ref = rest[-1]
    route = route_ref[...]
    row0 = jnp.concatenate(_unpack_bf16_pairs(r0_ref[...]), axis=1)
    row1 = jnp.concatenate(_unpack_bf16_pairs(r1_ref[...]), axis=1)
    o_ref[...] = x1_ref[...] + (route[:, 2:3] * row0 + route[:, 3:4] * row1)


def _combine(x1, route, rows0, rows1, part, out_prev):
    n_slice, d = x1.shape
    tm = COMBINE_TM
    steps = n_slice // tm
    row = lambda w: pl.BlockSpec((tm, w), lambda i: (i, 0))
    in_specs = [row(d), row(LANES), row(d // 2), row(d // 2)]
    args = [x1, route, rows0, rows1]
    aliases = {}
    if out_prev is not None:
        in_specs.append(pl.BlockSpec(memory_space=pl.ANY))
        args.append(out_prev)
        aliases = {len(args) - 1: 0}
    return pl.pallas_call(
        _combine_kernel,
        grid=(steps,),
        in_specs=in_specs,
        out_specs=pl.BlockSpec((tm, d), lambda i: (part * steps + i, 0)),
        out_shape=jax.ShapeDtypeStruct((n_slice * MOE_PARTS, d), F32),
        input_output_aliases=aliases,
        compiler_params=_cparams("parallel"),
        name="combine",
    )(*args)


def _layer(x, g_mix, w_in, lam_re, lam_im, log_dt, b_re, b_im, c_re, c_im, d_skip, w_glu, b_glu, g_q, g_k,
           g_ssm_out, g_attn_out, w_out, g_ffn, w_rg, b_rg, w_re, b_re_router, w_gate, w_up, w_down):
    bsz, t_len, d = x.shape
    n = bsz * t_len
    d_ssm = w_glu.shape[0]
    d_att = g_attn_out.shape[0]
    n_heads = d_att // HEAD_DIM
    n_chunks = t_len // SSM_CHUNK
    x2 = x.reshape(n, d)

    u_t, q, k, v = _in_proj(x2, g_mix[None], w_in.astype(BF16), jnp.tile(g_q, n_heads)[None],
                            jnp.tile(g_k, n_heads)[None], d_ssm, d_att, bsz, t_len)
    tables = _s5_tables(lam_re, lam_im, log_dt, b_re, b_im, c_re, c_im, d_skip)
    y_ssm_t = _s5(u_t.reshape(n_chunks, SSM_CHUNK, bsz, d_ssm), tables).reshape(t_len, bsz * d_ssm)

    y_att = _attention(q, k, v, bsz, t_len)

    w_r = jnp.concatenate([w_rg, w_re.reshape(d, N_EXPERTS)], axis=1)
    w_r = jnp.pad(w_r, ((0, 0), (0, LANES - w_r.shape[1]))).astype(BF16)
    b_r = jnp.pad(jnp.concatenate([b_rg, b_re_router.reshape(N_EXPERTS)]), (0, LANES - ROUTER_LANE0 - N_EXPERTS))[None]
    w_glu_bf, w_out_bf = w_glu.astype(BF16), w_out.astype(BF16)
    n_slice = n // MOE_PARTS
    n_rows = n_slice * 2 + N_EXPERTS * MOE_ROWS
    n_blk = n_rows // MOE_ROWS
    out = None
    for part in range(MOE_PARTS):
        x1, h2p, route, route_t, cnt = _post(x2, y_ssm_t, y_att, w_glu_bf, b_glu[None], g_ssm_out[None],
                                             g_attn_out[None], w_out_bf, g_ffn[None], w_r, b_r, part)
        counts = cnt[0, ROUTER_LANE0:ROUTER_LANE0 + N_EXPERTS].astype(jnp.int32)
        pcounts = ((counts + MOE_ROWS - 1) // MOE_ROWS) * MOE_ROWS
        pends = jnp.cumsum(pcounts)
        pstarts = pends - pcounts
        e_ids = jnp.arange(N_EXPERTS, dtype=jnp.int32)
        dests = []
        for k in range(2):
            expert = route_t[k].astype(jnp.int32)
            start = jnp.sum(jnp.where(expert[:, None] == e_ids, pstarts, 0), axis=-1)
            dest = start + route_t[4 + k].astype(jnp.int32)
            dests.append(dest.reshape(n_slice // SC_WINDOW, SC_WINDOW))
        blk_row0 = jnp.arange(n_blk, dtype=jnp.int32)[:, None] * MOE_ROWS
        blk_e = jnp.minimum(jnp.sum((pends[None, :] <= blk_row0).astype(jnp.int32), axis=1), N_EXPERTS - 1)
        n_used = (pends[-1:] // MOE_ROWS).astype(jnp.int32)
        present = jnp.any(blk_e[:, None] == e_ids, axis=0)
        later = jnp.logical_and(present[None, :], e_ids[None, :] > blk_e[:, None])
        next_e = jnp.min(jnp.where(later, e_ids[None, :], N_EXPERTS), axis=1)
        next_e = jnp.where(next_e == N_EXPERTS, -1, next_e).astype(jnp.int32)

        buf = _dispatch_sc(*dests, h2p, n_rows)
        eo = _experts(blk_e, next_e, n_used, buf, w_gate, w_up, w_down)
        out = _combine(x1, route, *_gather_sc(*dests, eo), part, out)
    return out.reshape(bsz, t_len, d)


def kernel(x, g_mix, w_in, ssm_lambda_re, ssm_lambda_im, ssm_log_dt, ssm_b_re, ssm_b_im, ssm_c_re, ssm_c_im, ssm_d, ssm_w_glu, ssm_b_glu, g_q, g_k, g_ssm_out, g_attn_out, w_out, g_ffn, w_router_group, b_router_group, w_router_expert, b_router_expert, w_gate, w_up, w_down):
    for l in range(g_mix.shape[0]):
        x = _layer(x, g_mix[l], w_in[l], ssm_lambda_re[l], ssm_lambda_im[l], ssm_log_dt[l], ssm_b_re[l],
                   ssm_b_im[l], ssm_c_re[l], ssm_c_im[l], ssm_d[l], ssm_w_glu[l], ssm_b_glu[l], g_q[l], g_k[l],
                   g_ssm_out[l], g_attn_out[l], w_out[l], g_ffn[l], w_router_group[l], b_router_group[l],
                   w_router_expert[l], b_router_expert[l], w_gate[l], w_up[l], w_down[l])
    return x
```

```python
import functools
import math

import jax
import jax.numpy as jnp
from jax import lax
from jax.experimental import pallas as pl
from jax.experimental.pallas import tpu as pltpu
from jax.experimental.pallas import tpu_sc as plsc

F32 = jnp.float32
BF16 = jnp.bfloat16
EPS = 1e-6

LANES = 128
VMEM_LIMIT_BYTES = 56 * 1024 * 1024

SSM_GROUP = 16
SSM_STATE = 64
SSM_CHUNK = 16
HEAD_DIM = 64
N_EXPERT_GROUPS = 4
EXPERTS_PER_GROUP = 8
N_EXPERTS = N_EXPERT_GROUPS * EXPERTS_PER_GROUP
ROUTER_LANE0 = N_EXPERT_GROUPS
ROUTE_FIELDS = 8
MOE_ROWS = 512
MOE_PARTS = 2
ATT_SKIP = 110.0

S5_GPB = LANES // SSM_GROUP
S5_CHUNKS_PER_STEP = 8

IN_TM = 512
ATT_TILE = 256
ATT_KEYS_AHEAD = 256
ATT_TILES_PER_STEP = 2
POST_TM = 512
COMBINE_TM = 512
SC_CORES = 2
SC_SUBCORES = 16
SC_WINDOW = 64


def _cparams(*sem):
    return pltpu.CompilerParams(dimension_semantics=sem, vmem_limit_bytes=VMEM_LIMIT_BYTES)


def _lane_iota(shape):
    return lax.broadcasted_iota(jnp.int32, shape, len(shape) - 1)


def _head_rms(t, gain):
    outs = []
    for c in range(t.shape[1] // LANES):
        blk = t[:, c * LANES:(c + 1) * LANES]
        sq = blk * blk
        lo = _lane_iota(blk.shape) < HEAD_DIM
        s_lo = jnp.sum(jnp.where(lo, sq, 0.0), axis=-1, keepdims=True)
        s_hi = jnp.sum(jnp.where(lo, 0.0, sq), axis=-1, keepdims=True)
        inv = jnp.where(lo, lax.rsqrt(s_lo * (1.0 / HEAD_DIM) + EPS),
                        lax.rsqrt(s_hi * (1.0 / HEAD_DIM) + EPS))
        outs.append(blk * inv * gain[:, c * LANES:(c + 1) * LANES])
    return jnp.concatenate(outs, axis=-1)


def _in_proj_kernel(x_ref, g_ref, w_ref, gq_ref, gk_ref, u_ref, q_ref, k_ref, v_ref, *, d_ssm, d_att, scale):
    x = x_ref[...]
    inv = lax.rsqrt(jnp.mean(x * x, axis=-1, keepdims=True) + EPS)
    h = (x * inv * g_ref[...]).astype(BF16)
    proj = jnp.dot(h, w_ref[...], preferred_element_type=F32)
    u_ref[...] = proj[:, :d_ssm].astype(BF16)
    q = _head_rms(proj[:, d_ssm:d_ssm + d_att], gq_ref[...])
    k = _head_rms(proj[:, d_ssm + d_att:d_ssm + 2 * d_att], gk_ref[...])
    q_ref[...] = (q * scale).astype(BF16)
    k_ref[...] = k.astype(BF16)
    v_ref[...] = proj[:, d_ssm + 2 * d_att:].astype(BF16)


def _in_proj(x2, g_mix, w_in_bf, gq_t, gk_t, d_ssm, d_att, bsz, t_len):
    n, d = x2.shape
    tm = IN_TM
    nt = t_len // tm
    row = lambda w: pl.BlockSpec((tm, w), lambda b, t: (b * nt + t, 0))
    full = lambda a: pl.BlockSpec(a.shape, lambda b, t: (0,) * a.ndim)
    out_sd = jax.ShapeDtypeStruct((n, d_att), BF16)
    return pl.pallas_call(
        functools.partial(_in_proj_kernel, d_ssm=d_ssm, d_att=d_att, scale=1.0 / math.sqrt(HEAD_DIM)),
        grid=(bsz, nt),
        in_specs=[row(d), full(g_mix), full(w_in_bf), full(gq_t), full(gk_t)],
        out_specs=[pl.BlockSpec((tm, d_ssm), lambda b, t: (t, b)), row(d_att), row(d_att), row(d_att)],
        out_shape=[jax.ShapeDtypeStruct((t_len, bsz * d_ssm), BF16), out_sd, out_sd, out_sd],
        compiler_params=_cparams("parallel", "parallel"),
        name="in_proj",
    )(x2, g_mix, w_in_bf, gq_t, gk_t)


def _s5_tables(lam_re, lam_im, log_dt, b_re, b_im, c_re, c_im, d_skip):
    hp = lax.Precision.HIGHEST
    L = SSM_CHUNK
    g_n, p_n = lam_re.shape
    dt = jnp.exp(log_dt)[:, None]
    lr, li = lam_re, lam_im
    ls = jnp.arange(L + 1, dtype=F32)[:, None, None]
    mag = jnp.exp(lr * dt * ls)
    pr, pi = mag * jnp.cos(li * dt * ls), mag * jnp.sin(li * dt * ls)
    abar_r, abar_i = pr[1], pi[1]
    den = lr * lr + li * li
    nr, ni = abar_r - 1.0, abar_i
    coef_r = (nr * lr + ni * li) / den
    coef_i = (ni * lr - nr * li) / den
    bbr = coef_r[..., None] * b_re - coef_i[..., None] * b_im
    bbi = coef_r[..., None] * b_im + coef_i[..., None] * b_re
    wr = pr[..., None] * bbr - pi[..., None] * bbi
    wi = pr[..., None] * bbi + pi[..., None] * bbr
    kl = (jnp.einsum('gop,lgpi->lgoi', c_re, wr[:L], precision=hp)
          - jnp.einsum('gop,lgpi->lgoi', c_im, wi[:L], precision=hp))
    kl = kl.at[0].add(jax.vmap(jnp.diag)(d_skip))
    n_lb = g_n // S5_GPB
    sg = SSM_GROUP
    kc = kl.transpose(1, 3, 0, 2).reshape(n_lb, S5_GPB * sg, L * sg)
    b_rows = lambda w: (w[:L][::-1].reshape(L, n_lb, S5_GPB, p_n, sg).transpose(1, 0, 2, 4, 3)
                        .reshape(n_lb, L * S5_GPB * sg, p_n))
    p1r, p1i = pr[1:], pi[1:]
    cst_r = (c_re[None] * p1r[:, :, None, :] - c_im[None] * p1i[:, :, None, :])
    cst_i = -(c_re[None] * p1i[:, :, None, :] + c_im[None] * p1r[:, :, None, :])
    c_rows = lambda c: c.transpose(1, 3, 0, 2).reshape(n_lb, S5_GPB * p_n, L * sg)
    a_l = jnp.stack([pr[L].reshape(n_lb, S5_GPB * p_n), pi[L].reshape(n_lb, S5_GPB * p_n)], axis=1)
    return (kc.astype(BF16), b_rows(wr).astype(BF16), b_rows(wi).astype(BF16),
            c_rows(cst_r).astype(BF16), c_rows(cst_i).astype(BF16), a_l)


def _s5_expand_tables(kc_ref, bcr_ref, bci_ref, ccr_ref, cci_ref, w0_ref, br_ref, bi_ref, cr_ref, ci_ref):
    sg, gpb = SSM_GROUP, S5_GPB
    n_lo = w0_ref.shape[1]
    p_n = bcr_ref.shape[2]

    def iotas(shape):
        return lax.broadcasted_iota(jnp.int32, shape, 0), lax.broadcasted_iota(jnp.int32, shape, 1)

    r, c = iotas((kc_ref.shape[2], n_lo))
    spread_o = jnp.where(jnp.logical_and(r // sg == c // LANES, r % sg == c % sg), 1.0, 0.0).astype(BF16)
    r, c = iotas((p_n, gpb * p_n))
    spread_p = jnp.where(r == c % p_n, 1.0, 0.0).astype(BF16)

    def expand(compact, spread, row_group, col_group):
        full = jnp.dot(compact, spread, preferred_element_type=F32)
        r, c = iotas(full.shape)
        return jnp.where(row_group(r) == col_group(c), full, 0.0).astype(BF16)

    lane_group = lambda c: (c % LANES) // sg
    top = expand(kc_ref[0], spread_o, lambda r: r // sg, lane_group)
    w0_ref[:LANES, :] = top
    w0_ref[LANES:, :LANES] = jnp.zeros((LANES, LANES), BF16)
    w0_ref[LANES:, LANES:] = top[:, :n_lo - LANES]
    state_group = lambda c: c // p_n
    br_ref[...] = expand(bcr_ref[0], spread_p, lambda r: (r // sg) % gpb, state_group)
    bi_ref[...] = expand(bci_ref[0], spread_p, lambda r: (r // sg) % gpb, state_group)
    cr_ref[...] = expand(ccr_ref[0], spread_o, lambda r: r // p_n, lane_group)
    ci_ref[...] = expand(cci_ref[0], spread_o, lambda r: r // p_n, lane_group)


def _s5_kernel(u_ref, kc_ref, bcr_ref, bci_ref, ccr_ref, cci_ref, a_ref, y_ref,
               hr_ref, hi_ref, acc_ref, w0_ref, br_ref, bi_ref, cr_ref, ci_ref):
    n_chunks, L, bsz, _ = u_ref.shape
    rows = n_chunks * bsz

    @pl.when(pl.program_id(1) == 0)
    def _():
        hr_ref[...] = jnp.zeros_like(hr_ref)
        hi_ref[...] = jnp.zeros_like(hi_ref)
        _s5_expand_tables(kc_ref, bcr_ref, bci_ref, ccr_ref, cci_ref, w0_ref, br_ref, bi_ref, cr_ref, ci_ref)

    us = [u_ref[:, s].reshape(rows, LANES) for s in range(L)]
    lhs = jnp.concatenate(us, axis=1)
    sin_r = jnp.dot(lhs, br_ref[...], preferred_element_type=F32)
    sin_i = jnp.dot(lhs, bi_ref[...], preferred_element_type=F32)
    ar = a_ref[0, 0:1, :]
    ai = a_ref[0, 1:2, :]
    hr, hi = hr_ref[...], hi_ref[...]
    prev_r, prev_i = [], []
    for c in range(n_chunks):
        prev_r.append(hr)
        prev_i.append(hi)
        sl = slice(c * bsz, (c + 1) * bsz)
        hr, hi = ar * hr - ai * hi + sin_r[sl], ar * hi + ai * hr + sin_i[sl]
    hr_ref[...] = hr
    hi_ref[...] = hi
    pr = jnp.concatenate(prev_r, axis=0).astype(BF16)
    pi = jnp.concatenate(prev_i, axis=0).astype(BF16)
    acc_ref[...] = (jnp.dot(pr, cr_ref[...], preferred_element_type=F32)
                    + jnp.dot(pi, ci_ref[...], preferred_element_type=F32))
    for p in range(L // 2):
        off = 2 * p * LANES
        pair = jnp.concatenate([us[2 * p], us[2 * p + 1]], axis=1)
        acc_ref[:, off:] += jnp.dot(pair, w0_ref[:, :L * LANES - off], preferred_element_type=F32)
    for t in range(L):
        y_ref[:, t] = acc_ref[:, t * LANES:(t + 1) * LANES].reshape(n_chunks, bsz, LANES)


def _s5(u4, tables):
    n_chunks, L, bsz, d_ssm = u4.shape
    a_l = tables[-1]
    cb = S5_CHUNKS_PER_STEP
    data = pl.BlockSpec((cb, L, bsz, LANES), lambda lb, c: (c, 0, 0, lb))
    per_lb = lambda a: pl.BlockSpec((1,) + a.shape[1:], lambda lb, c: (lb,) + (0,) * (a.ndim - 1))
    n_state = a_l.shape[2]
    state = pltpu.VMEM((bsz, n_state), F32)
    n_lo = L * LANES
    return pl.pallas_call(
        _s5_kernel,
        grid=(d_ssm // LANES, n_chunks // cb),
        in_specs=[data] + [per_lb(t) for t in tables],
        out_specs=data,
        out_shape=jax.ShapeDtypeStruct(u4.shape, F32),
        scratch_shapes=[state, state, pltpu.VMEM((cb * bsz, n_lo), F32),
                        pltpu.VMEM((2 * LANES, n_lo), BF16),
                        pltpu.VMEM((n_lo, n_state), BF16), pltpu.VMEM((n_lo, n_state), BF16),
                        pltpu.VMEM((n_state, n_lo), BF16), pltpu.VMEM((n_state, n_lo), BF16)],
        compiler_params=_cparams("arbitrary", "arbitrary"),
        name="s5",
    )(u4, *tables)


def _softplus(z):
    return jnp.maximum(z, 0.0) + jnp.log(1.0 + jnp.exp(-jnp.abs(z)))


def _att_tiles(q2, kvs, tri, r_in, causal_first):
    tq = q2.shape[0] // 2
    zs = [lax.dot_general(q2, k, (((1,), (1,)), ((), ())), preferred_element_type=F32) for k, _ in kvs]
    sps = [_softplus(z) for z in zs]
    sp_ms = list(sps)
    if causal_first:
        rows = lax.broadcasted_iota(jnp.int32, zs[0].shape, 0)
        cols = lax.broadcasted_iota(jnp.int32, zs[0].shape, 1)
        keep = cols < jnp.where(rows >= tq, rows - tq, rows)
        sp_ms[0] = jnp.where(keep, sps[0], 0.0)
    stacked = jnp.concatenate(sp_ms, axis=0).astype(BF16)
    newer_all = jnp.dot(stacked, tri, preferred_element_type=F32)
    pv, total = None, r_in
    for p, (_, v) in enumerate(kvs):
        newer = newer_all[p * 2 * tq:(p + 1) * 2 * tq]
        att = jnp.exp(zs[p] - sps[p] - newer - total)
        if causal_first and p == 0:
            att = jnp.where(keep, att, 0.0)
        part = jnp.dot(att.astype(BF16), v, preferred_element_type=F32)
        pv = part if pv is None else pv + part
        total = total + (newer[:, 0:1] + sp_ms[p][:, 0:1])
    return pv, total - r_in


def _attn_kernel(q_ref, k_ref, v_ref, o_ref, *, t_len, tile):
    nq = t_len // tile
    r_i = lax.broadcasted_iota(jnp.int32, (tile, tile), 0)
    c_i = lax.broadcasted_iota(jnp.int32, (tile, tile), 1)
    tri = jnp.where(r_i > c_i, 1.0, 0.0).astype(BF16)
    head0 = _lane_iota((tile, LANES)) < HEAD_DIM
    zero_r = jnp.zeros((2 * tile, 1), F32)

    def near(q0, n_prev):
        q = q_ref[pl.ds(q0, tile), :]
        zq = jnp.zeros_like(q)
        q2 = jnp.concatenate([jnp.where(head0, q, zq), jnp.where(head0, zq, q)], axis=0)
        kvs = [(k_ref[pl.ds(q0 - p * tile, tile), :], v_ref[pl.ds(q0 - p * tile, tile), :])
               for p in range(n_prev + 1)]
        acc, r = _att_tiles(q2, kvs, tri, zero_r, causal_first=True)
        return q2, acc, r

    def far_and_store(q0, q2, acc, r, j_older):
        if j_older is not None:
            def cond(c):
                j, _, r_c = c
                return jnp.logical_and(j >= 0, jnp.min(r_c) < ATT_SKIP)

            def body(c):
                j, a_c, r_c = c
                k0 = pl.multiple_of(j * tile, tile)
                pv, dr = _att_tiles(q2, [(k_ref[pl.ds(k0, tile), :], v_ref[pl.ds(k0, tile), :])], tri, r_c,
                                    causal_first=False)
                return j - 1, a_c + pv, r_c + dr

            _, acc, _ = lax.while_loop(cond, body, (j_older, acc, r))
        o_ref[pl.ds(q0, tile), :] = jnp.where(head0, acc[:tile], acc[tile:])

    n_prev = ATT_KEYS_AHEAD // tile
    group = ATT_TILES_PER_STEP
    first = n_prev + (nq - n_prev) % group
    for i in range(first):
        far_and_store(i * tile, *near(i * tile, min(i, n_prev)), None if i <= n_prev else i - n_prev - 1)

    def later(g, _):
        tiles = [first + g * group + t for t in range(group)]
        parts = [near(pl.multiple_of(i * tile, tile), n_prev) for i in tiles]
        for i, part in zip(tiles, parts):
            far_and_store(pl.multiple_of(i * tile, tile), *part, i - n_prev - 1)
        return 0

    lax.fori_loop(0, (nq - first) // group, later, 0)


def _attention(q, k, v, bsz, t_len):
    n, d_att = q.shape
    spec = pl.BlockSpec((t_len, LANES), lambda b, p: (b, p))
    return pl.pallas_call(
        functools.partial(_attn_kernel, t_len=t_len, tile=ATT_TILE),
        grid=(bsz, d_att // LANES),
        in_specs=[spec, spec, spec],
        out_specs=spec,
        out_shape=jax.ShapeDtypeStruct((n, d_att), F32),
        compiler_params=_cparams("parallel", "parallel"),
        name="attn",
    )(q, k, v)


def _rms(t, gain):
    return t * lax.rsqrt(jnp.mean(t * t, axis=-1, keepdims=True) + EPS) * gain


def _gelu_tanh(y):
    return 0.5 * y * (1.0 + jnp.tanh(math.sqrt(2.0 / math.pi) * (y + 0.044715 * (y * y * y))))


def _pack_bf16_pairs(a, b):
    ua = pltpu.bitcast(a.astype(BF16).astype(F32), jnp.uint32)
    ub = pltpu.bitcast(b.astype(BF16).astype(F32), jnp.uint32)
    return ua | (ub >> 16)


def _unpack_bf16_pairs(w):
    return pltpu.bitcast(w & jnp.uint32(0xFFFF0000), F32), pltpu.bitcast(w << 16, F32)


def _post_kernel(x_ref, ys_ref, ya_ref, wglu_ref, bglu_ref, gs_ref, ga_ref, wo_ref, gf_ref,
                 wr_ref, br_ref, x1_ref, h2_ref, route_ref, route_t_ref, cnt_ref, run_ref, *, d_ssm):
    i = pl.program_id(0)

    @pl.when(i == 0)
    def _():
        run_ref[...] = jnp.zeros_like(run_ref)

    y = _gelu_tanh(ys_ref[...])
    gate = jnp.dot(y.astype(BF16), wglu_ref[...], preferred_element_type=F32) + bglu_ref[...]
    y = y * jax.nn.sigmoid(gate)
    m_s = _rms(y, gs_ref[...]).astype(BF16)
    m_a = _rms(ya_ref[...], ga_ref[...]).astype(BF16)
    mix = (jnp.dot(m_s, wo_ref[:d_ssm, :], preferred_element_type=F32)
           + jnp.dot(m_a, wo_ref[d_ssm:, :], preferred_element_type=F32))
    x1 = x_ref[...] + mix
    x1_ref[...] = x1
    h2 = _rms(x1, gf_ref[...])
    half = h2.shape[1] // 2
    h2_ref[...] = _pack_bf16_pairs(h2[:, :half], h2[:, half:])

    lg = jnp.dot(h2.astype(BF16), wr_ref[...], preferred_element_type=F32) + br_ref[...]
    tm = lg.shape[0]
    lane = _lane_iota(lg.shape).astype(F32)
    neg = -jnp.inf
    first = lambda hit: jnp.min(jnp.where(hit, lane, float(LANES)), axis=-1, keepdims=True)
    glog = jnp.where(lane < N_EXPERT_GROUPS, lg, neg)
    gmax = jnp.max(glog, axis=-1, keepdims=True)
    p_grp = 1.0 / jnp.sum(jnp.exp(glog - gmax), axis=-1, keepdims=True)
    grp = first(glog == gmax)
    e0 = ROUTER_LANE0 + grp * EXPERTS_PER_GROUP
    elog = jnp.where(jnp.logical_and(lane >= e0, lane < e0 + EXPERTS_PER_GROUP), lg, neg)
    m1 = jnp.max(elog, axis=-1, keepdims=True)
    i1 = first(elog == m1)
    elog2 = jnp.where(lane == i1, neg, elog)
    m2 = jnp.max(elog2, axis=-1, keepdims=True)
    i2 = first(elog2 == m2)
    e21 = jnp.exp(m2 - m1)
    g1 = p_grp * (1.0 / (1.0 + e21))
    g2 = p_grp * (e21 / (1.0 + e21))

    sel1 = lane == i1
    sel2 = lane == i2
    onehot = jnp.where(jnp.logical_or(sel1, sel2), 1.0, 0.0)
    r_i = lax.broadcasted_iota(jnp.int32, (tm, tm), 0)
    c_i = lax.broadcasted_iota(jnp.int32, (tm, tm), 1)
    lower = jnp.where(c_i < r_i, 1.0, 0.0).astype(BF16)
    before = jnp.dot(lower, onehot.astype(BF16), preferred_element_type=F32) + run_ref[0:1, :]
    rank1 = jnp.sum(jnp.where(sel1, before, 0.0), axis=-1, keepdims=True)
    rank2 = jnp.sum(jnp.where(sel2, before, 0.0), axis=-1, keepdims=True)
    run_ref[0:1, :] = run_ref[0:1, :] + jnp.sum(onehot, axis=0, keepdims=True)
    cnt_ref[...] = run_ref[...]

    fields = (i1 - ROUTER_LANE0, i2 - ROUTER_LANE0, g1, g2, rank1, rank2)
    route = jnp.zeros(lg.shape, F32)
    for pos, val in enumerate(fields):
        route = jnp.where(lane == pos, val, route)
    route_ref[...] = route
    route_t_ref[...] = route.T[:ROUTE_FIELDS, :]


def _post(x2, y_ssm_t, y_att, w_glu_bf, b_glu, g_ssm, g_att, w_out_bf, g_ffn, w_r_bf, b_r, part):
    n, d = x2.shape
    d_ssm = w_glu_bf.shape[0]
    tm = POST_TM
    nt = y_ssm_t.shape[0] // tm
    steps = n // (tm * MOE_PARTS)
    i0 = part * steps
    row_in = lambda w: pl.BlockSpec((tm, w), lambda i: (i0 + i, 0))
    row = lambda w: pl.BlockSpec((tm, w), lambda i: (i, 0))
    ssm_spec = pl.BlockSpec((tm, d_ssm), lambda i: ((i0 + i) % nt, (i0 + i) // nt))
    full = lambda a: pl.BlockSpec(a.shape, lambda i: (0,) * a.ndim)
    cnt_spec = pl.BlockSpec((8, LANES), lambda i: (0, 0))
    n = n // MOE_PARTS
    return pl.pallas_call(
        functools.partial(_post_kernel, d_ssm=d_ssm),
        grid=(steps,),
        in_specs=[row_in(d), ssm_spec, row_in(y_att.shape[1]), full(w_glu_bf), full(b_glu), full(g_ssm),
                  full(g_att), full(w_out_bf), full(g_ffn), full(w_r_bf), full(b_r)],
        out_specs=[row(d), row(d // 2), row(LANES), pl.BlockSpec((ROUTE_FIELDS, tm), lambda i: (0, i)), cnt_spec],
        out_shape=[jax.ShapeDtypeStruct((n, d), F32), jax.ShapeDtypeStruct((n, d // 2), jnp.uint32),
                   jax.ShapeDtypeStruct((n, LANES), F32), jax.ShapeDtypeStruct((ROUTE_FIELDS, n), F32),
                   jax.ShapeDtypeStruct((8, LANES), F32)],
        scratch_shapes=[pltpu.VMEM((8, LANES), F32)],
        compiler_params=_cparams("arbitrary"),
        name="post",
    )(x2, y_ssm_t, y_att, w_glu_bf, b_glu, g_ssm, g_att, w_out_bf, g_ffn, w_r_bf, b_r)


def _dispatch_sc(d0, d1, h2p, n_rows):
    n, w = h2p.shape
    workers = SC_CORES * SC_SUBCORES
    n_win = n // (workers * SC_WINDOW)
    mesh = plsc.VectorSubcoreMesh(core_axis_name="c", subcore_axis_name="s")

    def body(h_hbm, d0_hbm, d1_hbm, o_hbm, rows_v, i0_v, i1_v):
        wid = lax.axis_index("c") * SC_SUBCORES + lax.axis_index("s")

        @pl.loop(0, n_win)
        def _(j):
            blk = wid * n_win + j
            pltpu.sync_copy(h_hbm.at[pl.ds(blk * SC_WINDOW, SC_WINDOW)], rows_v)
            pltpu.sync_copy(d0_hbm.at[blk], i0_v)
            pltpu.sync_copy(d1_hbm.at[blk], i1_v)
            pltpu.sync_copy(rows_v, o_hbm.at[i0_v])
            pltpu.sync_copy(rows_v, o_hbm.at[i1_v])

    return pl.kernel(
        body,
        out_type=jax.ShapeDtypeStruct((n_rows, w), h2p.dtype),
        mesh=mesh,
        scratch_types=[pltpu.VMEM((SC_WINDOW, w), h2p.dtype), pltpu.VMEM((SC_WINDOW,), jnp.int32),
                       pltpu.VMEM((SC_WINDOW,), jnp.int32)],
        name="dispatch_sc",
    )(h2p, d0, d1)


def _experts_kernel(blk_e_ref, next_e_ref, used_ref, x_ref, wg_hbm, wu_hbm, wd_hbm, o_ref,
                    wg_f32, wu_f32, wd_f32, wg_bf, wu_bf, wd_bf, slot_ref, sems):
    i = pl.program_id(0)
    staged = ((wg_hbm, wg_f32, wg_bf), (wu_hbm, wu_f32, wu_bf), (wd_hbm, wd_f32, wd_bf))

    def fetch(expert, slot):
        return [pltpu.make_async_copy(hbm.at[expert], f32.at[slot], sems.at[slot, k])
                for k, (hbm, f32, _) in enumerate(staged)]

    @pl.when(i == 0)
    def _():
        slot_ref[0] = 0
        for cp in fetch(blk_e_ref[0], 0):
            cp.start()

    @pl.when(jnp.logical_or(i == 0, blk_e_ref[i] != blk_e_ref[jnp.maximum(i - 1, 0)]))
    def _():
        slot = slot_ref[0]
        for cp in fetch(blk_e_ref[i], slot):
            cp.wait()
        for _, f32, bf in staged:
            bf[...] = f32[slot].astype(BF16)
        nxt = next_e_ref[i]

        @pl.when(nxt >= 0)
        def _():
            for cp in fetch(nxt, 1 - slot):
                cp.start()

        slot_ref[0] = 1 - slot

    @pl.when(i < used_ref[0])
    def _():
        xa, xb = (t.astype(BF16) for t in _unpack_bf16_pairs(x_ref[...]))
        half = xa.shape[1]
        gate = (jnp.dot(xa, wg_bf[:half, :], preferred_element_type=F32)
                + jnp.dot(xb, wg_bf[half:, :], preferred_element_type=F32))
        up = (jnp.dot(xa, wu_bf[:half, :], preferred_element_type=F32)
              + jnp.dot(xb, wu_bf[half:, :], preferred_element_type=F32))
        hid = (jax.nn.silu(gate) * up).astype(BF16)
        out = jnp.dot(hid, wd_bf[...], preferred_element_type=F32)
        o_ref[...] = _pack_bf16_pairs(out[:, :half], out[:, half:])

    @pl.when(i >= used_ref[0])
    def _():
        o_ref[...] = jnp.zeros_like(o_ref)


def _experts(blk_e, next_e, n_used, buf, w_gate, w_up, w_down):
    n_rows, w = buf.shape
    d = w_down.shape[2]
    hbm = pl.BlockSpec(memory_space=pl.ANY)
    weights = (w_gate, w_up, w_down)
    grid_spec = pltpu.PrefetchScalarGridSpec(
        num_scalar_prefetch=3,
        grid=(n_rows // MOE_ROWS,),
        in_specs=[pl.BlockSpec((MOE_ROWS, w), lambda i, be, ne, nu: (i, 0)), hbm, hbm, hbm],
        out_specs=pl.BlockSpec((MOE_ROWS, d // 2), lambda i, be, ne, nu: (i, 0)),
        scratch_shapes=([pltpu.VMEM((2,) + a.shape[1:], F32) for a in weights]
                        + [pltpu.VMEM(a.shape[1:], BF16) for a in weights]
                        + [pltpu.SMEM((1,), jnp.int32), pltpu.SemaphoreType.DMA((2, len(weights)))]),
    )
    return pl.pallas_call(
        _experts_kernel,
        grid_spec=grid_spec,
        out_shape=jax.ShapeDtypeStruct((n_rows, d // 2), jnp.uint32),
        compiler_params=_cparams("arbitrary"),
        name="experts",
    )(blk_e, next_e, n_used, buf, w_gate, w_up, w_down)


def _gather_sc(d0, d1, eo):
    n = d0.size
    w = eo.shape[1]
    workers = SC_CORES * SC_SUBCORES
    n_win = n // (workers * SC_WINDOW)
    mesh = plsc.VectorSubcoreMesh(core_axis_name="c", subcore_axis_name="s")

    def body(eo_hbm, d0_hbm, d1_hbm, o0_hbm, o1_hbm, rows_v, i_v):
        wid = lax.axis_index("c") * SC_SUBCORES + lax.axis_index("s")

        @pl.loop(0, n_win)
        def _(j):
            blk = wid * n_win + j
            for d_hbm, o_hbm in ((d0_hbm, o0_hbm), (d1_hbm, o1_hbm)):
                pltpu.sync_copy(d_hbm.at[blk], i_v)
                pltpu.sync_copy(eo_hbm.at[i_v], rows_v)
                pltpu.sync_copy(rows_v, o_hbm.at[pl.ds(blk * SC_WINDOW, SC_WINDOW)])

    out_sd = jax.ShapeDtypeStruct((n, w), eo.dtype)
    return pl.kernel(
        body,
        out_type=(out_sd, out_sd),
        mesh=mesh,
        scratch_types=[pltpu.VMEM((SC_WINDOW, w), eo.dtype), pltpu.VMEM((SC_WINDOW,), jnp.int32)],
        name="gather_sc",
    )(eo, d0, d1)


def _combine_kernel(x1_ref, route_ref, r0_ref, r1_ref, *rest):
    o_ref = rest[-1]
    route = route_ref[...]
    row0 = jnp.concatenate(_unpack_bf16_pairs(r0_ref[...]), axis=1)
    row1 = jnp.concatenate(_unpack_bf16_pairs(r1_ref[...]), axis=1)
    o_ref[...] = x1_ref[...] + (route[:, 2:3] * row0 + route[:, 3:4] * row1)


def _combine(x1, route, rows0, rows1, part, out_prev):
    n_slice, d = x1.shape
    tm = COMBINE_TM
    steps = n_slice // tm
    row = lambda w: pl.BlockSpec((tm, w), lambda i: (i, 0))
    in_specs = [row(d), row(LANES), row(d // 2), row(d // 2)]
    args = [x1, route, rows0, rows1]
    aliases = {}
    if out_prev is not None:
        in_specs.append(pl.BlockSpec(memory_space=pl.ANY))
        args.append(out_prev)
        aliases = {len(args) - 1: 0}
    return pl.pallas_call(
        _combine_kernel,
        grid=(steps,),
        in_specs=in_specs,
        out_specs=pl.BlockSpec((tm, d), lambda i: (part * steps + i, 0)),
        out_shape=jax.ShapeDtypeStruct((n_slice * MOE_PARTS, d), F32),
        input_output_aliases=aliases,
        compiler_params=_cparams("parallel"),
        name="combine",
    )(*args)


def _layer(x, g_mix, w_in, lam_re, lam_im, log_dt, b_re, b_im, c_re, c_im, d_skip, w_glu, b_glu, g_q, g_k,
           g_ssm_out, g_attn_out, w_out, g_ffn, w_rg, b_rg, w_re, b_re_router, w_gate, w_up, w_down):
    bsz, t_len, d = x.shape
    n = bsz * t_len
    d_ssm = w_glu.shape[0]
    d_att = g_attn_out.shape[0]
    n_heads = d_att // HEAD_DIM
    n_chunks = t_len // SSM_CHUNK
    x2 = x.reshape(n, d)

    u_t, q, k, v = _in_proj(x2, g_mix[None], w_in.astype(BF16), jnp.tile(g_q, n_heads)[None],
                            jnp.tile(g_k, n_heads)[None], d_ssm, d_att, bsz, t_len)
    tables = _s5_tables(lam_re, lam_im, log_dt, b_re, b_im, c_re, c_im, d_skip)
    y_ssm_t = _s5(u_t.reshape(n_chunks, SSM_CHUNK, bsz, d_ssm), tables).reshape(t_len, bsz * d_ssm)

    y_att = _attention(q, k, v, bsz, t_len)

    w_r = jnp.concatenate([w_rg, w_re.reshape(d, N_EXPERTS)], axis=1)
    w_r = jnp.pad(w_r, ((0, 0), (0, LANES - w_r.shape[1]))).astype(BF16)
    b_r = jnp.pad(jnp.concatenate([b_rg, b_re_router.reshape(N_EXPERTS)]), (0, LANES - ROUTER_LANE0 - N_EXPERTS))[None]
    w_glu_bf, w_out_bf = w_glu.astype(BF16), w_out.astype(BF16)
    n_slice = n // MOE_PARTS
    n_rows = n_slice * 2 + N_EXPERTS * MOE_ROWS
    n_blk = n_rows // MOE_ROWS
    out = None
    for part in range(MOE_PARTS):
        x1, h2p, route, route_t, cnt = _post(x2, y_ssm_t, y_att, w_glu_bf, b_glu[None], g_ssm_out[None],
                                             g_attn_out[None], w_out_bf, g_ffn[None], w_r, b_r, part)
        counts = cnt[0, ROUTER_LANE0:ROUTER_LANE0 + N_EXPERTS].astype(jnp.int32)
        pcounts = ((counts + MOE_ROWS - 1) // MOE_ROWS) * MOE_ROWS
        pends = jnp.cumsum(pcounts)
        pstarts = pends - pcounts
        e_ids = jnp.arange(N_EXPERTS, dtype=jnp.int32)
        dests = []
        for k in range(2):
            expert = route_t[k].astype(jnp.int32)
            start = jnp.sum(jnp.where(expert[:, None] == e_ids, pstarts, 0), axis=-1)
            dest = start + route_t[4 + k].astype(jnp.int32)
            dests.append(dest.reshape(n_slice // SC_WINDOW, SC_WINDOW))
        blk_row0 = jnp.arange(n_blk, dtype=jnp.int32)[:, None] * MOE_ROWS
        blk_e = jnp.minimum(jnp.sum((pends[None, :] <= blk_row0).astype(jnp.int32), axis=1), N_EXPERTS - 1)
        n_used = (pends[-1:] // MOE_ROWS).astype(jnp.int32)
        present = jnp.any(blk_e[:, None] == e_ids, axis=0)
        later = jnp.logical_and(present[None, :], e_ids[None, :] > blk_e[:, None])
        next_e = jnp.min(jnp.where(later, e_ids[None, :], N_EXPERTS), axis=1)
        next_e = jnp.where(next_e == N_EXPERTS, -1, next_e).astype(jnp.int32)

        buf = _dispatch_sc(*dests, h2p, n_rows)
        eo = _experts(blk_e, next_e, n_used, buf, w_gate, w_up, w_down)
        out = _combine(x1, route, *_gather_sc(*dests, eo), part, out)
    return out.reshape(bsz, t_len, d)


def kernel(x, g_mix, w_in, ssm_lambda_re, ssm_lambda_im, ssm_log_dt, ssm_b_re, ssm_b_im, ssm_c_re, ssm_c_im, ssm_d, ssm_w_glu, ssm_b_glu, g_q, g_k, g_ssm_out, g_attn_out, w_out, g_ffn, w_router_group, b_router_group, w_router_expert, b_router_expert, w_gate, w_up, w_down):
    for l in range(g_mix.shape[0]):
        x = _layer(x, g_mix[l], w_in[l], ssm_lambda_re[l], ssm_lambda_im[l], ssm_log_dt[l], ssm_b_re[l],
                   ssm_b_im[l], ssm_c_re[l], ssm_c_im[l], ssm_d[l], ssm_w_glu[l], ssm_b_glu[l], g_q[l], g_k[l],
                   g_ssm_out[l], g_attn_out[l], w_out[l], g_ffn[l], w_router_group[l], b_router_group[l],
                   w_router_expert[l], b_router_expert[l], w_gate[l], w_up[l], w_down[l])
    return x
```

```python
import functools
import math

import jax
import jax.numpy as jnp
from jax import lax
from jax.experimental import pallas as pl
from jax.experimental.pallas import tpu as pltpu
from jax.experimental.pallas import tpu_sc as plsc

F32 = jnp.float32
BF16 = jnp.bfloat16
EPS = 1e-6

LANES = 128
VMEM_LIMIT_BYTES = 56 * 1024 * 1024

SSM_GROUP = 16
SSM_STATE = 64
SSM_CHUNK = 16
HEAD_DIM = 64
N_EXPERT_GROUPS = 4
EXPERTS_PER_GROUP = 8
N_EXPERTS = N_EXPERT_GROUPS * EXPERTS_PER_GROUP
ROUTER_LANE0 = N_EXPERT_GROUPS
ROUTE_FIELDS = 8
MOE_ROWS = 512
MOE_PARTS = 2
ATT_SKIP = 110.0

S5_GPB = LANES // SSM_GROUP
S5_CHUNKS_PER_STEP = 8

IN_TM = 512
ATT_TILE = 256
ATT_KEYS_AHEAD = 256
ATT_TILES_PER_STEP = 2
POST_TM = 512
POST_ROW_GROUPS = 2
COMBINE_TM = 512
SC_CORES = 2
SC_SUBCORES = 16
SC_WINDOW = 64


def _cparams(*sem):
    return pltpu.CompilerParams(dimension_semantics=sem, vmem_limit_bytes=VMEM_LIMIT_BYTES)


def _lane_iota(shape):
    return lax.broadcasted_iota(jnp.int32, shape, len(shape) - 1)


def _head_rms(t, gain):
    outs = []
    for c in range(t.shape[1] // LANES):
        blk = t[:, c * LANES:(c + 1) * LANES]
        sq = blk * blk
        lo = _lane_iota(blk.shape) < HEAD_DIM
        s_lo = jnp.sum(jnp.where(lo, sq, 0.0), axis=-1, keepdims=True)
        s_hi = jnp.sum(jnp.where(lo, 0.0, sq), axis=-1, keepdims=True)
        inv = jnp.where(lo, lax.rsqrt(s_lo * (1.0 / HEAD_DIM) + EPS),
                        lax.rsqrt(s_hi * (1.0 / HEAD_DIM) + EPS))
        outs.append(blk * inv * gain[:, c * LANES:(c + 1) * LANES])
    return jnp.concatenate(outs, axis=-1)


def _in_proj_kernel(x_ref, g_ref, w_ref, gq_ref, gk_ref, u_ref, q_ref, k_ref, v_ref, *, d_ssm, d_att, scale):
    x = x_ref[...]
    inv = lax.rsqrt(jnp.mean(x * x, axis=-1, keepdims=True) + EPS)
    h = (x * inv * g_ref[...]).astype(BF16)
    proj = jnp.dot(h, w_ref[...], preferred_element_type=F32)
    u_ref[...] = proj[:, :d_ssm].astype(BF16)
    q = _head_rms(proj[:, d_ssm:d_ssm + d_att], gq_ref[...])
    k = _head_rms(proj[:, d_ssm + d_att:d_ssm + 2 * d_att], gk_ref[...])
    q_ref[...] = (q * scale).astype(BF16)
    k_ref[...] = k.astype(BF16)
    v_ref[...] = proj[:, d_ssm + 2 * d_att:].astype(BF16)


def _in_proj(x2, g_mix, w_in_bf, gq_t, gk_t, d_ssm, d_att, bsz, t_len):
    n, d = x2.shape
    tm = IN_TM
    nt = t_len // tm
    row = lambda w: pl.BlockSpec((tm, w), lambda b, t: (b * nt + t, 0))
    full = lambda a: pl.BlockSpec(a.shape, lambda b, t: (0,) * a.ndim)
    out_sd = jax.ShapeDtypeStruct((n, d_att), BF16)
    return pl.pallas_call(
        functools.partial(_in_proj_kernel, d_ssm=d_ssm, d_att=d_att, scale=1.0 / math.sqrt(HEAD_DIM)),
        grid=(bsz, nt),
        in_specs=[row(d), full(g_mix), full(w_in_bf), full(gq_t), full(gk_t)],
        out_specs=[pl.BlockSpec((tm, d_ssm), lambda b, t: (t, b)), row(d_att), row(d_att), row(d_att)],
        out_shape=[jax.ShapeDtypeStruct((t_len, bsz * d_ssm), BF16), out_sd, out_sd, out_sd],
        compiler_params=_cparams("parallel", "parallel"),
        name="in_proj",
    )(x2, g_mix, w_in_bf, gq_t, gk_t)


def _s5_tables(lam_re, lam_im, log_dt, b_re, b_im, c_re, c_im, d_skip):
    hp = lax.Precision.HIGHEST
    L = SSM_CHUNK
    g_n, p_n = lam_re.shape
    dt = jnp.exp(log_dt)[:, None]
    lr, li = lam_re, lam_im
    ls = jnp.arange(L + 1, dtype=F32)[:, None, None]
    mag = jnp.exp(lr * dt * ls)
    pr, pi = mag * jnp.cos(li * dt * ls), mag * jnp.sin(li * dt * ls)
    abar_r, abar_i = pr[1], pi[1]
    den = lr * lr + li * li
    nr, ni = abar_r - 1.0, abar_i
    coef_r = (nr * lr + ni * li) / den
    coef_i = (ni * lr - nr * li) / den
    bbr = coef_r[..., None] * b_re - coef_i[..., None] * b_im
    bbi = coef_r[..., None] * b_im + coef_i[..., None] * b_re
    wr = pr[..., None] * bbr - pi[..., None] * bbi
    wi = pr[..., None] * bbi + pi[..., None] * bbr
    kl = (jnp.einsum('gop,lgpi->lgoi', c_re, wr[:L], precision=hp)
          - jnp.einsum('gop,lgpi->lgoi', c_im, wi[:L], precision=hp))
    kl = kl.at[0].add(jax.vmap(jnp.diag)(d_skip))
    n_lb = g_n // S5_GPB
    sg = SSM_GROUP
    kc = kl.transpose(1, 3, 0, 2).reshape(n_lb, S5_GPB * sg, L * sg)
    b_rows = lambda w: (w[:L][::-1].reshape(L, n_lb, S5_GPB, p_n, sg).transpose(1, 0, 2, 4, 3)
                        .reshape(n_lb, L * S5_GPB * sg, p_n))
    p1r, p1i = pr[1:], pi[1:]
    cst_r = (c_re[None] * p1r[:, :, None, :] - c_im[None] * p1i[:, :, None, :])
    cst_i = -(c_re[None] * p1i[:, :, None, :] + c_im[None] * p1r[:, :, None, :])
    c_rows = lambda c: c.transpose(1, 3, 0, 2).reshape(n_lb, S5_GPB * p_n, L * sg)
    a_l = jnp.stack([pr[L].reshape(n_lb, S5_GPB * p_n), pi[L].reshape(n_lb, S5_GPB * p_n)], axis=1)
    return (kc.astype(BF16), b_rows(wr).astype(BF16), b_rows(wi).astype(BF16),
            c_rows(cst_r).astype(BF16), c_rows(cst_i).astype(BF16), a_l)


def _s5_expand_tables(kc_ref, bcr_ref, bci_ref, ccr_ref, cci_ref, w0_ref, br_ref, bi_ref, cr_ref, ci_ref):
    sg, gpb = SSM_GROUP, S5_GPB
    n_lo = w0_ref.shape[1]
    p_n = bcr_ref.shape[2]

    def iotas(shape):
        return lax.broadcasted_iota(jnp.int32, shape, 0), lax.broadcasted_iota(jnp.int32, shape, 1)

    r, c = iotas((kc_ref.shape[2], n_lo))
    spread_o = jnp.where(jnp.logical_and(r // sg == c // LANES, r % sg == c % sg), 1.0, 0.0).astype(BF16)
    r, c = iotas((p_n, gpb * p_n))
    spread_p = jnp.where(r == c % p_n, 1.0, 0.0).astype(BF16)

    def expand(compact, spread, row_group, col_group):
        full = jnp.dot(compact, spread, preferred_element_type=F32)
        r, c = iotas(full.shape)
        return jnp.where(row_group(r) == col_group(c), full, 0.0).astype(BF16)

    lane_group = lambda c: (c % LANES) // sg
    top = expand(kc_ref[0], spread_o, lambda r: r // sg, lane_group)
    w0_ref[:LANES, :] = top
    w0_ref[LANES:, :LANES] = jnp.zeros((LANES, LANES), BF16)
    w0_ref[LANES:, LANES:] = top[:, :n_lo - LANES]
    state_group = lambda c: c // p_n
    br_ref[...] = expand(bcr_ref[0], spread_p, lambda r: (r // sg) % gpb, state_group)
    bi_ref[...] = expand(bci_ref[0], spread_p, lambda r: (r // sg) % gpb, state_group)
    cr_ref[...] = expand(ccr_ref[0], spread_o, lambda r: r // p_n, lane_group)
    ci_ref[...] = expand(cci_ref[0], spread_o, lambda r: r // p_n, lane_group)


def _s5_kernel(u_ref, kc_ref, bcr_ref, bci_ref, ccr_ref, cci_ref, a_ref, y_ref,
               hr_ref, hi_ref, acc_ref, w0_ref, br_ref, bi_ref, cr_ref, ci_ref):
    n_chunks, L, bsz, _ = u_ref.shape
    rows = n_chunks * bsz

    @pl.when(pl.program_id(1) == 0)
    def _():
        hr_ref[...] = jnp.zeros_like(hr_ref)
        hi_ref[...] = jnp.zeros_like(hi_ref)
        _s5_expand_tables(kc_ref, bcr_ref, bci_ref, ccr_ref, cci_ref, w0_ref, br_ref, bi_ref, cr_ref, ci_ref)

    us = [u_ref[:, s].reshape(rows, LANES) for s in range(L)]
    lhs = jnp.concatenate(us, axis=1)
    sin_r = jnp.dot(lhs, br_ref[...], preferred_element_type=F32)
    sin_i = jnp.dot(lhs, bi_ref[...], preferred_element_type=F32)
    ar = a_ref[0, 0:1, :]
    ai = a_ref[0, 1:2, :]
    hr, hi = hr_ref[...], hi_ref[...]
    prev_r, prev_i = [], []
    for c in range(n_chunks):
        prev_r.append(hr)
        prev_i.append(hi)
        sl = slice(c * bsz, (c + 1) * bsz)
        hr, hi = ar * hr - ai * hi + sin_r[sl], ar * hi + ai * hr + sin_i[sl]
    hr_ref[...] = hr
    hi_ref[...] = hi
    pr = jnp.concatenate(prev_r, axis=0).astype(BF16)
    pi = jnp.concatenate(prev_i, axis=0).astype(BF16)
    acc_ref[...] = (jnp.dot(pr, cr_ref[...], preferred_element_type=F32)
                    + jnp.dot(pi, ci_ref[...], preferred_element_type=F32))
    for p in range(L // 2):
        off = 2 * p * LANES
        pair = jnp.concatenate([us[2 * p], us[2 * p + 1]], axis=1)
        acc_ref[:, off:] += jnp.dot(pair, w0_ref[:, :L * LANES - off], preferred_element_type=F32)
    for t in range(L):
        y_ref[:, t] = acc_ref[:, t * LANES:(t + 1) * LANES].reshape(n_chunks, bsz, LANES)


def _s5(u4, tables):
    n_chunks, L, bsz, d_ssm = u4.shape
    a_l = tables[-1]
    cb = S5_CHUNKS_PER_STEP
    data = pl.BlockSpec((cb, L, bsz, LANES), lambda lb, c: (c, 0, 0, lb))
    per_lb = lambda a: pl.BlockSpec((1,) + a.shape[1:], lambda lb, c: (lb,) + (0,) * (a.ndim - 1))
    n_state = a_l.shape[2]
    state = pltpu.VMEM((bsz, n_state), F32)
    n_lo = L * LANES
    return pl.pallas_call(
        _s5_kernel,
        grid=(d_ssm // LANES, n_chunks // cb),
        in_specs=[data] + [per_lb(t) for t in tables],
        out_specs=data,
        out_shape=jax.ShapeDtypeStruct(u4.shape, F32),
        scratch_shapes=[state, state, pltpu.VMEM((cb * bsz, n_lo), F32),
                        pltpu.VMEM((2 * LANES, n_lo), BF16),
                        pltpu.VMEM((n_lo, n_state), BF16), pltpu.VMEM((n_lo, n_state), BF16),
                        pltpu.VMEM((n_state, n_lo), BF16), pltpu.VMEM((n_state, n_lo), BF16)],
        compiler_params=_cparams("arbitrary", "arbitrary"),
        name="s5",
    )(u4, *tables)


def _softplus(z):
    return jnp.maximum(z, 0.0) + jnp.log(1.0 + jnp.exp(-jnp.abs(z)))


def _att_tiles(q2s, kvss, tri, r_ins, causal_first):
    tq = q2s[0].shape[0] // 2
    n_t = len(q2s)
    contract_last = (((1,), (1,)), ((), ()))
    zs = [[lax.dot_general(q2, k, contract_last, preferred_element_type=F32) for k, _ in kvs]
          for q2, kvs in zip(q2s, kvss)]
    sps = [[_softplus(z) for z in row] for row in zs]
    sp_ms = [list(row) for row in sps]
    if causal_first:
        rows = lax.broadcasted_iota(jnp.int32, zs[0][0].shape, 0)
        cols = lax.broadcasted_iota(jnp.int32, zs[0][0].shape, 1)
        keep = cols < jnp.where(rows >= tq, rows - tq, rows)
        for t in range(n_t):
            sp_ms[t][0] = jnp.where(keep, sps[t][0], 0.0)
    stacked = jnp.concatenate([blk for row in sp_ms for blk in row], axis=0).astype(BF16)
    newer_all = jnp.dot(stacked, tri, preferred_element_type=F32)
    pvs, totals = [None] * n_t, list(r_ins)
    base = 0
    offsets = []
    for row in sp_ms:
        offsets.append(base)
        base += len(row) * 2 * tq
    for p in range(max(len(kvs) for kvs in kvss)):
        for t in range(n_t):
            if p >= len(kvss[t]):
                continue
            newer = newer_all[offsets[t] + p * 2 * tq:offsets[t] + (p + 1) * 2 * tq]
            att = jnp.exp(zs[t][p] - sps[t][p] - newer - totals[t])
            if causal_first and p == 0:
                att = jnp.where(keep, att, 0.0)
            part = jnp.dot(att.astype(BF16), kvss[t][p][1], preferred_element_type=F32)
            pvs[t] = part if pvs[t] is None else pvs[t] + part
            totals[t] = totals[t] + (newer[:, 0:1] + sp_ms[t][p][:, 0:1])
    return [(pv, total - r_in) for pv, total, r_in in zip(pvs, totals, r_ins)]


def _attn_kernel(q_ref, k_ref, v_ref, o_ref, *, t_len, tile):
    nq = t_len // tile
    r_i = lax.broadcasted_iota(jnp.int32, (tile, tile), 0)
    c_i = lax.broadcasted_iota(jnp.int32, (tile, tile), 1)
    tri = jnp.where(r_i > c_i, 1.0, 0.0).astype(BF16)
    head0 = _lane_iota((tile, LANES)) < HEAD_DIM
    zero_r = jnp.zeros((2 * tile, 1), F32)

    def near(q0s, n_prev):
        q2s, kvss = [], []
        for q0 in q0s:
            q = q_ref[pl.ds(q0, tile), :]
            zq = jnp.zeros_like(q)
            q2s.append(jnp.concatenate([jnp.where(head0, q, zq), jnp.where(head0, zq, q)], axis=0))
            kvss.append([(k_ref[pl.ds(q0 - p * tile, tile), :], v_ref[pl.ds(q0 - p * tile, tile), :])
                         for p in range(n_prev + 1)])
        outs = _att_tiles(q2s, kvss, tri, [zero_r] * len(q0s), causal_first=True)
        return [(q2, acc, r) for q2, (acc, r) in zip(q2s, outs)]

    def far_and_store(q0, q2, acc, r, j_older):
        if j_older is not None:
            def cond(c):
                j, _, r_c = c
                return jnp.logical_and(j >= 0, jnp.min(r_c) < ATT_SKIP)

            def body(c):
                j, a_c, r_c = c
                k0 = pl.multiple_of(j * tile, tile)
                [(pv, dr)] = _att_tiles([q2], [[(k_ref[pl.ds(k0, tile), :], v_ref[pl.ds(k0, tile), :])]], tri,
                                        [r_c], causal_first=False)
                return j - 1, a_c + pv, r_c + dr

            _, acc, _ = lax.while_loop(cond, body, (j_older, acc, r))
        o_ref[pl.ds(q0, tile), :] = jnp.where(head0, acc[:tile], acc[tile:])

    n_prev = ATT_KEYS_AHEAD // tile
    group = ATT_TILES_PER_STEP
    first = n_prev + (nq - n_prev) % group
    for i in range(first):
        [part] = near([i * tile], min(i, n_prev))
        far_and_store(i * tile, *part, None if i <= n_prev else i - n_prev - 1)

    def later(g, _):
        tiles = [first + g * group + t for t in range(group)]
        q0s = [pl.multiple_of(i * tile, tile) for i in tiles]
        for i, q0, part in zip(tiles, q0s, near(q0s, n_prev)):
            far_and_store(q0, *part, i - n_prev - 1)
        return 0

    lax.fori_loop(0, (nq - first) // group, later, 0)


def _attention(q, k, v, bsz, t_len):
    n, d_att = q.shape
    spec = pl.BlockSpec((t_len, LANES), lambda b, p: (b, p))
    return pl.pallas_call(
        functools.partial(_attn_kernel, t_len=t_len, tile=ATT_TILE),
        grid=(bsz, d_att // LANES),
        in_specs=[spec, spec, spec],
        out_specs=spec,
        out_shape=jax.ShapeDtypeStruct((n, d_att), F32),
        compiler_params=_cparams("parallel", "parallel"),
        name="attn",
    )(q, k, v)


def _rms(t, gain):
    return t * lax.rsqrt(jnp.mean(t * t, axis=-1, keepdims=True) + EPS) * gain


def _gelu_tanh(y):
    return 0.5 * y * (1.0 + jnp.tanh(math.sqrt(2.0 / math.pi) * (y + 0.044715 * (y * y * y))))


def _pack_bf16_pairs(a, b):
    ua = pltpu.bitcast(a.astype(BF16).astype(F32), jnp.uint32)
    ub = pltpu.bitcast(b.astype(BF16).astype(F32), jnp.uint32)
    return ua | (ub >> 16)


def _unpack_bf16_pairs(w):
    return pltpu.bitcast(w & jnp.uint32(0xFFFF0000), F32), pltpu.bitcast(w << 16, F32)


def _post_kernel(x_ref, ys_ref, ya_ref, wglu_ref, bglu_ref, gs_ref, ga_ref, wo_ref, gf_ref,
                 wr_ref, br_ref, x1_ref, h2_ref, route_ref, route_t_ref, cnt_ref, run_ref, *, d_ssm):
    i = pl.program_id(0)

    @pl.when(i == 0)
    def _():
        run_ref[...] = jnp.zeros_like(run_ref)

    tm = x_ref.shape[0]
    groups = [pl.ds(g * (tm // POST_ROW_GROUPS), tm // POST_ROW_GROUPS) for g in range(POST_ROW_GROUPS)]
    ys = [_gelu_tanh(ys_ref[rows, :]) for rows in groups]
    gates = [jnp.dot(y.astype(BF16), wglu_ref[...], preferred_element_type=F32) + bglu_ref[...] for y in ys]
    m_as = [_rms(ya_ref[rows, :], ga_ref[...]).astype(BF16) for rows in groups]
    m_ss = [_rms(y * jax.nn.sigmoid(gate), gs_ref[...]).astype(BF16) for y, gate in zip(ys, gates)]
    mixes = [jnp.dot(m_s, wo_ref[:d_ssm, :], preferred_element_type=F32)
             + jnp.dot(m_a, wo_ref[d_ssm:, :], preferred_element_type=F32) for m_s, m_a in zip(m_ss, m_as)]
    x1s = [x_ref[rows, :] + mix for rows, mix in zip(groups, mixes)]
    h2s = [_rms(x1, gf_ref[...]) for x1 in x1s]
    logits = [jnp.dot(h2.astype(BF16), wr_ref[...], preferred_element_type=F32) for h2 in h2s]
    half = x_ref.shape[1] // 2
    for rows, x1, h2 in zip(groups, x1s, h2s):
        x1_ref[rows, :] = x1
        h2_ref[rows, :] = _pack_bf16_pairs(h2[:, :half], h2[:, half:])
    lg = jnp.concatenate(logits, axis=0) + br_ref[...]
    lane = _lane_iota(lg.shape).astype(F32)
    neg = -jnp.inf
    first = lambda hit: jnp.min(jnp.where(hit, lane, float(LANES)), axis=-1, keepdims=True)
    glog = jnp.where(lane < N_EXPERT_GROUPS, lg, neg)
    gmax = jnp.max(glog, axis=-1, keepdims=True)
    p_grp = 1.0 / jnp.sum(jnp.exp(glog - gmax), axis=-1, keepdims=True)
    grp = first(glog == gmax)
    e0 = ROUTER_LANE0 + grp * EXPERTS_PER_GROUP
    elog = jnp.where(jnp.logical_and(lane >= e0, lane < e0 + EXPERTS_PER_GROUP), lg, neg)
    m1 = jnp.max(elog, axis=-1, keepdims=True)
    i1 = first(elog == m1)
    elog2 = jnp.where(lane == i1, neg, elog)
    m2 = jnp.max(elog2, axis=-1, keepdims=True)
    i2 = first(elog2 == m2)
    e21 = jnp.exp(m2 - m1)
    g1 = p_grp * (1.0 / (1.0 + e21))
    g2 = p_grp * (e21 / (1.0 + e21))

    sel1 = lane == i1
    sel2 = lane == i2
    onehot = jnp.where(jnp.logical_or(sel1, sel2), 1.0, 0.0)
    r_i = lax.broadcasted_iota(jnp.int32, (tm, tm), 0)
    c_i = lax.broadcasted_iota(jnp.int32, (tm, tm), 1)
    lower = jnp.where(c_i < r_i, 1.0, 0.0).astype(BF16)
    before = jnp.dot(lower, onehot.astype(BF16), preferred_element_type=F32) + run_ref[0:1, :]
    rank1 = jnp.sum(jnp.where(sel1, before, 0.0), axis=-1, keepdims=True)
    rank2 = jnp.sum(jnp.where(sel2, before, 0.0), axis=-1, keepdims=True)
    run_ref[0:1, :] = run_ref[0:1, :] + jnp.sum(onehot, axis=0, keepdims=True)
    cnt_ref[...] = run_ref[...]

    fields = (i1 - ROUTER_LANE0, i2 - ROUTER_LANE0, g1, g2, rank1, rank2)
    route = jnp.zeros(lg.shape, F32)
    for pos, val in enumerate(fields):
        route = jnp.where(lane == pos, val, route)
    route_ref[...] = route
    route_t_ref[...] = route.T[:ROUTE_FIELDS, :]


def _post(x2, y_ssm_t, y_att, w_glu_bf, b_glu, g_ssm, g_att, w_out_bf, g_ffn, w_r_bf, b_r, part):
    n, d = x2.shape
    d_ssm = w_glu_bf.shape[0]
    tm = POST_TM
    nt = y_ssm_t.shape[0] // tm
    steps = n // (tm * MOE_PARTS)
    i0 = part * steps
    row_in = lambda w: pl.BlockSpec((tm, w), lambda i: (i0 + i, 0))
    row = lambda w: pl.BlockSpec((tm, w), lambda i: (i, 0))
    ssm_spec = pl.BlockSpec((tm, d_ssm), lambda i: ((i0 + i) % nt, (i0 + i) // nt))
    full = lambda a: pl.BlockSpec(a.shape, lambda i: (0,) * a.ndim)
    cnt_spec = pl.BlockSpec((8, LANES), lambda i: (0, 0))
    n = n // MOE_PARTS
    return pl.pallas_call(
        functools.partial(_post_kernel, d_ssm=d_ssm),
        grid=(steps,),
        in_specs=[row_in(d), ssm_spec, row_in(y_att.shape[1]), full(w_glu_bf), full(b_glu), full(g_ssm),
                  full(g_att), full(w_out_bf), full(g_ffn), full(w_r_bf), full(b_r)],
        out_specs=[row(d), row(d // 2), row(LANES), pl.BlockSpec((ROUTE_FIELDS, tm), lambda i: (0, i)), cnt_spec],
        out_shape=[jax.ShapeDtypeStruct((n, d), F32), jax.ShapeDtypeStruct((n, d // 2), jnp.uint32),
                   jax.ShapeDtypeStruct((n, LANES), F32), jax.ShapeDtypeStruct((ROUTE_FIELDS, n), F32),
                   jax.ShapeDtypeStruct((8, LANES), F32)],
        scratch_shapes=[pltpu.VMEM((8, LANES), F32)],
        compiler_params=_cparams("arbitrary"),
        name="post",
    )(x2, y_ssm_t, y_att, w_glu_bf, b_glu, g_ssm, g_att, w_out_bf, g_ffn, w_r_bf, b_r)


def _dispatch_sc(d0, d1, h2p, n_rows):
    n, w = h2p.shape
    workers = SC_CORES * SC_SUBCORES
    n_win = n // (workers * SC_WINDOW)
    mesh = plsc.VectorSubcoreMesh(core_axis_name="c", subcore_axis_name="s")

    def body(h_hbm, d0_hbm, d1_hbm, o_hbm, rows_v, i0_v, i1_v):
        wid = lax.axis_index("c") * SC_SUBCORES + lax.axis_index("s")

        @pl.loop(0, n_win)
        def _(j):
            blk = wid * n_win + j
            pltpu.sync_copy(h_hbm.at[pl.ds(blk * SC_WINDOW, SC_WINDOW)], rows_v)
            pltpu.sync_copy(d0_hbm.at[blk], i0_v)
            pltpu.sync_copy(d1_hbm.at[blk], i1_v)
            pltpu.sync_copy(rows_v, o_hbm.at[i0_v])
            pltpu.sync_copy(rows_v, o_hbm.at[i1_v])

    return pl.kernel(
        body,
        out_type=jax.ShapeDtypeStruct((n_rows, w), h2p.dtype),
        mesh=mesh,
        scratch_types=[pltpu.VMEM((SC_WINDOW, w), h2p.dtype), pltpu.VMEM((SC_WINDOW,), jnp.int32),
                       pltpu.VMEM((SC_WINDOW,), jnp.int32)],
        name="dispatch_sc",
    )(h2p, d0, d1)


def _experts_kernel(blk_e_ref, next_e_ref, used_ref, x_ref, wg_hbm, wu_hbm, wd_hbm, o_ref,
                    wg_f32, wu_f32, wd_f32, wg_bf, wu_bf, wd_bf, slot_ref, sems):
    i = pl.program_id(0)
    staged = ((wg_hbm, wg_f32, wg_bf), (wu_hbm, wu_f32, wu_bf), (wd_hbm, wd_f32, wd_bf))

    def fetch(expert, slot):
        return [pltpu.make_async_copy(hbm.at[expert], f32.at[slot], sems.at[slot, k])
                for k, (hbm, f32, _) in enumerate(staged)]

    @pl.when(i == 0)
    def _():
        slot_ref[0] = 0
        for cp in fetch(blk_e_ref[0], 0):
            cp.start()

    @pl.when(jnp.logical_or(i == 0, blk_e_ref[i] != blk_e_ref[jnp.maximum(i - 1, 0)]))
    def _():
        slot = slot_ref[0]
        for cp in fetch(blk_e_ref[i], slot):
            cp.wait()
        for _, f32, bf in staged:
            bf[...] = f32[slot].astype(BF16)
        nxt = next_e_ref[i]

        @pl.when(nxt >= 0)
        def _():
            for cp in fetch(nxt, 1 - slot):
                cp.start()

        slot_ref[0] = 1 - slot

    @pl.when(i < used_ref[0])
    def _():
        xa, xb = (t.astype(BF16) for t in _unpack_bf16_pairs(x_ref[...]))
        half = xa.shape[1]
        gate = (jnp.dot(xa, wg_bf[:half, :], preferred_element_type=F32)
                + jnp.dot(xb, wg_bf[half:, :], preferred_element_type=F32))
        up = (jnp.dot(xa, wu_bf[:half, :], preferred_element_type=F32)
              + jnp.dot(xb, wu_bf[half:, :], preferred_element_type=F32))
        hid = (jax.nn.silu(gate) * up).astype(BF16)
        out = jnp.dot(hid, wd_bf[...], preferred_element_type=F32)
        o_ref[...] = _pack_bf16_pairs(out[:, :half], out[:, half:])

    @pl.when(i >= used_ref[0])
    def _():
        o_ref[...] = jnp.zeros_like(o_ref)


def _experts(blk_e, next_e, n_used, buf, w_gate, w_up, w_down):
    n_rows, w = buf.shape
    d = w_down.shape[2]
    hbm = pl.BlockSpec(memory_space=pl.ANY)
    weights = (w_gate, w_up, w_down)
    grid_spec = pltpu.PrefetchScalarGridSpec(
        num_scalar_prefetch=3,
        grid=(n_rows // MOE_ROWS,),
        in_specs=[pl.BlockSpec((MOE_ROWS, w), lambda i, be, ne, nu: (i, 0)), hbm, hbm, hbm],
        out_specs=pl.BlockSpec((MOE_ROWS, d // 2), lambda i, be, ne, nu: (i, 0)),
        scratch_shapes=([pltpu.VMEM((2,) + a.shape[1:], F32) for a in weights]
                        + [pltpu.VMEM(a.shape[1:], BF16) for a in weights]
                        + [pltpu.SMEM((1,), jnp.int32), pltpu.SemaphoreType.DMA((2, len(weights)))]),
    )
    return pl.pallas_call(
        _experts_kernel,
        grid_spec=grid_spec,
        out_shape=jax.ShapeDtypeStruct((n_rows, d // 2), jnp.uint32),
        compiler_params=_cparams("arbitrary"),
        name="experts",
    )(blk_e, next_e, n_used, buf, w_gate, w_up, w_down)


def _gather_sc(d0, d1, eo):
    n = d0.size
    w = eo.shape[1]
    workers = SC_CORES * SC_SUBCORES
    n_win = n // (workers * SC_WINDOW)
    mesh = plsc.VectorSubcoreMesh(core_axis_name="c", subcore_axis_name="s")

    def body(eo_hbm, d0_hbm, d1_hbm, o0_hbm, o1_hbm, rows_v, i_v):
        wid = lax.axis_index("c") * SC_SUBCORES + lax.axis_index("s")

        @pl.loop(0, n_win)
        def _(j):
            blk = wid * n_win + j
            for d_hbm, o_hbm in ((d0_hbm, o0_hbm), (d1_hbm, o1_hbm)):
                pltpu.sync_copy(d_hbm.at[blk], i_v)
                pltpu.sync_copy(eo_hbm.at[i_v], rows_v)
                pltpu.sync_copy(rows_v, o_hbm.at[pl.ds(blk * SC_WINDOW, SC_WINDOW)])

    out_sd = jax.ShapeDtypeStruct((n, w), eo.dtype)
    return pl.kernel(
        body,
        out_type=(out_sd, out_sd),
        mesh=mesh,
        scratch_types=[pltpu.VMEM((SC_WINDOW, w), eo.dtype), pltpu.VMEM((SC_WINDOW,), jnp.int32)],
        name="gather_sc",
    )(eo, d0, d1)


def _combine_kernel(x1_ref, route_ref, r0_ref, r1_ref, *rest):
    o_ref = rest[-1]
    route = route_ref[...]
    row0 = jnp.concatenate(_unpack_bf16_pairs(r0_ref[...]), axis=1)
    row1 = jnp.concatenate(_unpack_bf16_pairs(r1_ref[...]), axis=1)
    o_ref[...] = x1_ref[...] + (route[:, 2:3] * row0 + route[:, 3:4] * row1)


def _combine(x1, route, rows0, rows1, part, out_prev):
    n_slice, d = x1.shape
    tm = COMBINE_TM
    steps = n_slice // tm
    row = lambda w: pl.BlockSpec((tm, w), lambda i: (i, 0))
    in_specs = [row(d), row(LANES), row(d // 2), row(d // 2)]
    args = [x1, route, rows0, rows1]
    aliases = {}
    if out_prev is not None:
        in_specs.append(pl.BlockSpec(memory_space=pl.ANY))
        args.append(out_prev)
        aliases = {len(args) - 1: 0}
    return pl.pallas_call(
        _combine_kernel,
        grid=(steps,),
        in_specs=in_specs,
        out_specs=pl.BlockSpec((tm, d), lambda i: (part * steps + i, 0)),
        out_shape=jax.ShapeDtypeStruct((n_slice * MOE_PARTS, d), F32),
        input_output_aliases=aliases,
        compiler_params=_cparams("parallel"),
        name="combine",
    )(*args)


def _layer(x, g_mix, w_in, lam_re, lam_im, log_dt, b_re, b_im, c_re, c_im, d_skip, w_glu, b_glu, g_q, g_k,
           g_ssm_out, g_attn_out, w_out, g_ffn, w_rg, b_rg, w_re, b_re_router, w_gate, w_up, w_down):
    bsz, t_len, d = x.shape
    n = bsz * t_len
    d_ssm = w_glu.shape[0]
    d_att = g_attn_out.shape[0]
    n_heads = d_att // HEAD_DIM
    n_chunks = t_len // SSM_CHUNK
    x2 = x.reshape(n, d)

    u_t, q, k, v = _in_proj(x2, g_mix[None], w_in.astype(BF16), jnp.tile(g_q, n_heads)[None],
                            jnp.tile(g_k, n_heads)[None], d_ssm, d_att, bsz, t_len)
    tables = _s5_tables(lam_re, lam_im, log_dt, b_re, b_im, c_re, c_im, d_skip)
    y_ssm_t = _s5(u_t.reshape(n_chunks, SSM_CHUNK, bsz, d_ssm), tables).reshape(t_len, bsz * d_ssm)

    y_att = _attention(q, k, v, bsz, t_len)

    w_r = jnp.concatenate([w_rg, w_re.reshape(d, N_EXPERTS)], axis=1)
    w_r = jnp.pad(w_r, ((0, 0), (0, LANES - w_r.shape[1]))).astype(BF16)
    b_r = jnp.pad(jnp.concatenate([b_rg, b_re_router.reshape(N_EXPERTS)]), (0, LANES - ROUTER_LANE0 - N_EXPERTS))[None]
    w_glu_bf, w_out_bf = w_glu.astype(BF16), w_out.astype(BF16)
    n_slice = n // MOE_PARTS
    n_rows = n_slice * 2 + N_EXPERTS * MOE_ROWS
    n_blk = n_rows // MOE_ROWS
    out = None
    for part in range(MOE_PARTS):
        x1, h2p, route, route_t, cnt = _post(x2, y_ssm_t, y_att, w_glu_bf, b_glu[None], g_ssm_out[None],
                                             g_attn_out[None], w_out_bf, g_ffn[None], w_r, b_r, part)
        counts = cnt[0, ROUTER_LANE0:ROUTER_LANE0 + N_EXPERTS].astype(jnp.int32)
        pcounts = ((counts + MOE_ROWS - 1) // MOE_ROWS) * MOE_ROWS
        pends = jnp.cumsum(pcounts)
        pstarts = pends - pcounts
        e_ids = jnp.arange(N_EXPERTS, dtype=jnp.int32)
        dests = []
        for k in range(2):
            expert = route_t[k].astype(jnp.int32)
            start = jnp.sum(jnp.where(expert[:, None] == e_ids, pstarts, 0), axis=-1)
            dest = start + route_t[4 + k].astype(jnp.int32)
            dests.append(dest.reshape(n_slice // SC_WINDOW, SC_WINDOW))
        blk_row0 = jnp.arange(n_blk, dtype=jnp.int32)[:, None] * MOE_ROWS
        blk_e = jnp.minimum(jnp.sum((pends[None, :] <= blk_row0).astype(jnp.int32), axis=1), N_EXPERTS - 1)
        n_used = (pends[-1:] // MOE_ROWS).astype(jnp.int32)
        present = jnp.any(blk_e[:, None] == e_ids, axis=0)
        later = jnp.logical_and(present[None, :], e_ids[None, :] > blk_e[:, None])
        next_e = jnp.min(jnp.where(later, e_ids[None, :], N_EXPERTS), axis=1)
        next_e = jnp.where(next_e == N_EXPERTS, -1, next_e).astype(jnp.int32)

        buf = _dispatch_sc(*dests, h2p, n_rows)
        eo = _experts(blk_e, next_e, n_used, buf, w_gate, w_up, w_down)
        out = _combine(x1, route, *_gather_sc(*dests, eo), part, out)
    return out.reshape(bsz, t_len, d)


def kernel(x, g_mix, w_in, ssm_lambda_re, ssm_lambda_im, ssm_log_dt, ssm_b_re, ssm_b_im, ssm_c_re, ssm_c_im, ssm_d, ssm_w_glu, ssm_b_glu, g_q, g_k, g_ssm_out, g_attn_out, w_out, g_ffn, w_router_group, b_router_group, w_router_expert, b_router_expert, w_gate, w_up, w_down):
    for l in range(g_mix.shape[0]):
        x = _layer(x, g_mix[l], w_in[l], ssm_lambda_re[l], ssm_lambda_im[l], ssm_log_dt[l], ssm_b_re[l],
                   ssm_b_im[l], ssm_c_re[l], ssm_c_im[l], ssm_d[l], ssm_w_glu[l], ssm_b_glu[l], g_q[l], g_k[l],
                   g_ssm_out[l], g_attn_out[l], w_out[l], g_ffn[l], w_router_group[l], b_router_group[l],
                   w_router_expert[l], b_router_expert[l], w_gate[l], w_up[l], w_down[l])
    return x
```

```python
import functools
import math

import jax
import jax.numpy as jnp
from jax import lax
from jax.experimental import pallas as pl
from jax.experimental.pallas import tpu as pltpu
from jax.experimental.pallas import tpu_sc as plsc

F32 = jnp.float32
BF16 = jnp.bfloat16
EPS = 1e-6

LANES = 128
VMEM_LIMIT_BYTES = 56 * 1024 * 1024

SSM_GROUP = 16
SSM_STATE = 64
SSM_CHUNK = 16
HEAD_DIM = 64
N_EXPERT_GROUPS = 4
EXPERTS_PER_GROUP = 8
N_EXPERTS = N_EXPERT_GROUPS * EXPERTS_PER_GROUP
ROUTER_LANE0 = N_EXPERT_GROUPS
ROUTE_FIELDS = 8
MOE_ROWS = 512
MOE_PARTS = 2
ATT_SKIP = 110.0

S5_GPB = LANES // SSM_GROUP
S5_CHUNKS_PER_STEP = 8

IN_TM = 512
IN_BATCHES = 8
ATT_TILE = 256
ATT_KEYS_AHEAD = 256
ATT_TILES_PER_STEP = 2
POST_TM = 512
POST_ROW_GROUPS = 2
COMBINE_TM = 512
SC_CORES = 2
SC_SUBCORES = 16
SC_WINDOW = 64


def _cparams(*sem):
    return pltpu.CompilerParams(dimension_semantics=sem, vmem_limit_bytes=VMEM_LIMIT_BYTES)


def _lane_iota(shape):
    return lax.broadcasted_iota(jnp.int32, shape, len(shape) - 1)


def _head_rms(t, gain):
    outs = []
    for c in range(t.shape[1] // LANES):
        blk = t[:, c * LANES:(c + 1) * LANES]
        sq = blk * blk
        lo = _lane_iota(blk.shape) < HEAD_DIM
        s_lo = jnp.sum(jnp.where(lo, sq, 0.0), axis=-1, keepdims=True)
        s_hi = jnp.sum(jnp.where(lo, 0.0, sq), axis=-1, keepdims=True)
        inv = jnp.where(lo, lax.rsqrt(s_lo * (1.0 / HEAD_DIM) + EPS),
                        lax.rsqrt(s_hi * (1.0 / HEAD_DIM) + EPS))
        outs.append(blk * inv * gain[:, c * LANES:(c + 1) * LANES])
    return jnp.concatenate(outs, axis=-1)


def _in_proj_kernel(x_ref, g_ref, w_ref, gq_ref, gk_ref, u_ref, q_ref, k_ref, v_ref, *, d_ssm, d_att, scale):
    nb, tt, d = x_ref.shape
    x = x_ref[...].reshape(nb * tt, d)
    inv = lax.rsqrt(jnp.mean(x * x, axis=-1, keepdims=True) + EPS)
    h = (x * inv * g_ref[...]).astype(BF16)
    proj = jnp.dot(h, w_ref[...], preferred_element_type=F32)
    u_ref[...] = pltpu.einshape("btc->tbc", proj[:, :d_ssm].astype(BF16).reshape(nb, tt, d_ssm))
    q = _head_rms(proj[:, d_ssm:d_ssm + d_att], gq_ref[...])
    k = _head_rms(proj[:, d_ssm + d_att:d_ssm + 2 * d_att], gk_ref[...])
    q_ref[...] = (q * scale).astype(BF16).reshape(nb, tt, d_att)
    k_ref[...] = k.astype(BF16).reshape(nb, tt, d_att)
    v_ref[...] = proj[:, d_ssm + 2 * d_att:].astype(BF16).reshape(nb, tt, d_att)


def _in_proj(x, g_mix, w_in_bf, gq_t, gk_t, d_ssm, d_att):
    bsz, t_len, d = x.shape
    nb = IN_BATCHES
    tt = IN_TM // nb
    tok = lambda w: pl.BlockSpec((nb, tt, w), lambda b, t: (b, t, 0))
    full = lambda a: pl.BlockSpec(a.shape, lambda b, t: (0,) * a.ndim)
    out_sd = jax.ShapeDtypeStruct((bsz, t_len, d_att), BF16)
    return pl.pallas_call(
        functools.partial(_in_proj_kernel, d_ssm=d_ssm, d_att=d_att, scale=1.0 / math.sqrt(HEAD_DIM)),
        grid=(bsz // nb, t_len // tt),
        in_specs=[tok(d), full(g_mix), full(w_in_bf), full(gq_t), full(gk_t)],
        out_specs=[pl.BlockSpec((tt, nb, d_ssm), lambda b, t: (t, b, 0)), tok(d_att), tok(d_att), tok(d_att)],
        out_shape=[jax.ShapeDtypeStruct((t_len, bsz, d_ssm), BF16), out_sd, out_sd, out_sd],
        compiler_params=_cparams("parallel", "parallel"),
        name="in_proj",
    )(x, g_mix, w_in_bf, gq_t, gk_t)


def _s5_tables(lam_re, lam_im, log_dt, b_re, b_im, c_re, c_im, d_skip):
    hp = lax.Precision.HIGHEST
    L = SSM_CHUNK
    g_n, p_n = lam_re.shape
    dt = jnp.exp(log_dt)[:, None]
    lr, li = lam_re, lam_im
    ls = jnp.arange(L + 1, dtype=F32)[:, None, None]
    mag = jnp.exp(lr * dt * ls)
    pr, pi = mag * jnp.cos(li * dt * ls), mag * jnp.sin(li * dt * ls)
    abar_r, abar_i = pr[1], pi[1]
    den = lr * lr + li * li
    nr, ni = abar_r - 1.0, abar_i
    coef_r = (nr * lr + ni * li) / den
    coef_i = (ni * lr - nr * li) / den
    bbr = coef_r[..., None] * b_re - coef_i[..., None] * b_im
    bbi = coef_r[..., None] * b_im + coef_i[..., None] * b_re
    wr = pr[..., None] * bbr - pi[..., None] * bbi
    wi = pr[..., None] * bbi + pi[..., None] * bbr
    kl = (jnp.einsum('gop,lgpi->lgoi', c_re, wr[:L], precision=hp)
          - jnp.einsum('gop,lgpi->lgoi', c_im, wi[:L], precision=hp))
    kl = kl.at[0].add(jax.vmap(jnp.diag)(d_skip))
    n_lb = g_n // S5_GPB
    sg = SSM_GROUP
    kc = kl.transpose(1, 3, 0, 2).reshape(n_lb, S5_GPB * sg, L * sg)
    b_rows = lambda w: (w[:L][::-1].reshape(L, n_lb, S5_GPB, p_n, sg).transpose(1, 0, 2, 4, 3)
                        .reshape(n_lb, L * S5_GPB * sg, p_n))
    p1r, p1i = pr[1:], pi[1:]
    cst_r = (c_re[None] * p1r[:, :, None, :] - c_im[None] * p1i[:, :, None, :])
    cst_i = -(c_re[None] * p1i[:, :, None, :] + c_im[None] * p1r[:, :, None, :])
    c_rows = lambda c: c.transpose(1, 3, 0, 2).reshape(n_lb, S5_GPB * p_n, L * sg)
    a_l = jnp.stack([pr[L].reshape(n_lb, S5_GPB * p_n), pi[L].reshape(n_lb, S5_GPB * p_n)], axis=1)
    return (kc.astype(BF16), b_rows(wr).astype(BF16), b_rows(wi).astype(BF16),
            c_rows(cst_r).astype(BF16), c_rows(cst_i).astype(BF16), a_l)


def _s5_expand_tables(kc_ref, bcr_ref, bci_ref, ccr_ref, cci_ref, w0_ref, br_ref, bi_ref, cr_ref, ci_ref):
    sg, gpb = SSM_GROUP, S5_GPB
    n_lo = w0_ref.shape[1]
    p_n = bcr_ref.shape[2]

    def iotas(shape):
        return lax.broadcasted_iota(jnp.int32, shape, 0), lax.broadcasted_iota(jnp.int32, shape, 1)

    r, c = iotas((kc_ref.shape[2], n_lo))
    spread_o = jnp.where(jnp.logical_and(r // sg == c // LANES, r % sg == c % sg), 1.0, 0.0).astype(BF16)
    r, c = iotas((p_n, gpb * p_n))
    spread_p = jnp.where(r == c % p_n, 1.0, 0.0).astype(BF16)

    def expand(compact, spread, row_group, col_group):
        full = jnp.dot(compact, spread, preferred_element_type=F32)
        r, c = iotas(full.shape)
        return jnp.where(row_group(r) == col_group(c), full, 0.0).astype(BF16)

    lane_group = lambda c: (c % LANES) // sg
    top = expand(kc_ref[0], spread_o, lambda r: r // sg, lane_group)
    w0_ref[:LANES, :] = top
    w0_ref[LANES:, :LANES] = jnp.zeros((LANES, LANES), BF16)
    w0_ref[LANES:, LANES:] = top[:, :n_lo - LANES]
    state_group = lambda c: c // p_n
    br_ref[...] = expand(bcr_ref[0], spread_p, lambda r: (r // sg) % gpb, state_group)
    bi_ref[...] = expand(bci_ref[0], spread_p, lambda r: (r // sg) % gpb, state_group)
    cr_ref[...] = expand(ccr_ref[0], spread_o, lambda r: r // p_n, lane_group)
    ci_ref[...] = expand(cci_ref[0], spread_o, lambda r: r // p_n, lane_group)


def _s5_kernel(u_ref, kc_ref, bcr_ref, bci_ref, ccr_ref, cci_ref, a_ref, y_ref,
               hr_ref, hi_ref, acc_ref, w0_ref, br_ref, bi_ref, cr_ref, ci_ref):
    n_chunks, L, bsz, _ = u_ref.shape
    rows = n_chunks * bsz

    @pl.when(pl.program_id(1) == 0)
    def _():
        hr_ref[...] = jnp.zeros_like(hr_ref)
        hi_ref[...] = jnp.zeros_like(hi_ref)
        _s5_expand_tables(kc_ref, bcr_ref, bci_ref, ccr_ref, cci_ref, w0_ref, br_ref, bi_ref, cr_ref, ci_ref)

    us = [u_ref[:, s].reshape(rows, LANES) for s in range(L)]
    lhs = jnp.concatenate(us, axis=1)
    sin_r = jnp.dot(lhs, br_ref[...], preferred_element_type=F32)
    sin_i = jnp.dot(lhs, bi_ref[...], preferred_element_type=F32)
    ar = a_ref[0, 0:1, :]
    ai = a_ref[0, 1:2, :]
    hr, hi = hr_ref[...], hi_ref[...]
    prev_r, prev_i = [], []
    for c in range(n_chunks):
        prev_r.append(hr)
        prev_i.append(hi)
        sl = slice(c * bsz, (c + 1) * bsz)
        hr, hi = ar * hr - ai * hi + sin_r[sl], ar * hi + ai * hr + sin_i[sl]
    hr_ref[...] = hr
    hi_ref[...] = hi
    pr = jnp.concatenate(prev_r, axis=0).astype(BF16)
    pi = jnp.concatenate(prev_i, axis=0).astype(BF16)
    acc_ref[...] = (jnp.dot(pr, cr_ref[...], preferred_element_type=F32)
                    + jnp.dot(pi, ci_ref[...], preferred_element_type=F32))
    for p in range(L // 2):
        off = 2 * p * LANES
        pair = jnp.concatenate([us[2 * p], us[2 * p + 1]], axis=1)
        acc_ref[:, off:] += jnp.dot(pair, w0_ref[:, :L * LANES - off], preferred_element_type=F32)
    for t in range(L):
        y_ref[:, t] = acc_ref[:, t * LANES:(t + 1) * LANES].reshape(n_chunks, bsz, LANES)


def _s5(u4, tables):
    n_chunks, L, bsz, d_ssm = u4.shape
    a_l = tables[-1]
    cb = S5_CHUNKS_PER_STEP
    data = pl.BlockSpec((cb, L, bsz, LANES), lambda lb, c: (c, 0, 0, lb))
    per_lb = lambda a: pl.BlockSpec((1,) + a.shape[1:], lambda lb, c: (lb,) + (0,) * (a.ndim - 1))
    n_state = a_l.shape[2]
    state = pltpu.VMEM((bsz, n_state), F32)
    n_lo = L * LANES
    return pl.pallas_call(
        _s5_kernel,
        grid=(d_ssm // LANES, n_chunks // cb),
        in_specs=[data] + [per_lb(t) for t in tables],
        out_specs=data,
        out_shape=jax.ShapeDtypeStruct(u4.shape, F32),
        scratch_shapes=[state, state, pltpu.VMEM((cb * bsz, n_lo), F32),
                        pltpu.VMEM((2 * LANES, n_lo), BF16),
                        pltpu.VMEM((n_lo, n_state), BF16), pltpu.VMEM((n_lo, n_state), BF16),
                        pltpu.VMEM((n_state, n_lo), BF16), pltpu.VMEM((n_state, n_lo), BF16)],
        compiler_params=_cparams("arbitrary", "arbitrary"),
        name="s5",
    )(u4, *tables)


def _softplus(z):
    return jnp.maximum(z, 0.0) + jnp.log(1.0 + jnp.exp(-jnp.abs(z)))


def _att_tiles(q2s, kvss, tri, r_ins, causal_first):
    tq = q2s[0].shape[0] // 2
    n_t = len(q2s)
    contract_last = (((1,), (1,)), ((), ()))
    zs = [[lax.dot_general(q2, k, contract_last, preferred_element_type=F32) for k, _ in kvs]
          for q2, kvs in zip(q2s, kvss)]
    sps = [[_softplus(z) for z in row] for row in zs]
    sp_ms = [list(row) for row in sps]
    if causal_first:
        rows = lax.broadcasted_iota(jnp.int32, zs[0][0].shape, 0)
        cols = lax.broadcasted_iota(jnp.int32, zs[0][0].shape, 1)
        keep = cols < jnp.where(rows >= tq, rows - tq, rows)
        for t in range(n_t):
            sp_ms[t][0] = jnp.where(keep, sps[t][0], 0.0)
    stacked = jnp.concatenate([blk for row in sp_ms for blk in row], axis=0).astype(BF16)
    newer_all = jnp.dot(stacked, tri, preferred_element_type=F32)
    pvs, totals = [None] * n_t, list(r_ins)
    base = 0
    offsets = []
    for row in sp_ms:
        offsets.append(base)
        base += len(row) * 2 * tq
    for p in range(max(len(kvs) for kvs in kvss)):
        for t in range(n_t):
            if p >= len(kvss[t]):
                continue
            newer = newer_all[offsets[t] + p * 2 * tq:offsets[t] + (p + 1) * 2 * tq]
            att = jnp.exp(zs[t][p] - sps[t][p] - newer - totals[t])
            if causal_first and p == 0:
                att = jnp.where(keep, att, 0.0)
            part = jnp.dot(att.astype(BF16), kvss[t][p][1], preferred_element_type=F32)
            pvs[t] = part if pvs[t] is None else pvs[t] + part
            totals[t] = totals[t] + (newer[:, 0:1] + sp_ms[t][p][:, 0:1])
    return [(pv, total - r_in) for pv, total, r_in in zip(pvs, totals, r_ins)]


def _attn_kernel(q_ref, k_ref, v_ref, o_ref, *, t_len, tile):
    nq = t_len // tile
    r_i = lax.broadcasted_iota(jnp.int32, (tile, tile), 0)
    c_i = lax.broadcasted_iota(jnp.int32, (tile, tile), 1)
    tri = jnp.where(r_i > c_i, 1.0, 0.0).astype(BF16)
    head0 = _lane_iota((tile, LANES)) < HEAD_DIM
    zero_r = jnp.zeros((2 * tile, 1), F32)

    def near(q0s, n_prev):
        q2s, kvss = [], []
        for q0 in q0s:
            q = q_ref[pl.ds(q0, tile), :]
            zq = jnp.zeros_like(q)
            q2s.append(jnp.concatenate([jnp.where(head0, q, zq), jnp.where(head0, zq, q)], axis=0))
            kvss.append([(k_ref[pl.ds(q0 - p * tile, tile), :], v_ref[pl.ds(q0 - p * tile, tile), :])
                         for p in range(n_prev + 1)])
        outs = _att_tiles(q2s, kvss, tri, [zero_r] * len(q0s), causal_first=True)
        return [(q2, acc, r) for q2, (acc, r) in zip(q2s, outs)]

    def far_and_store(q0, q2, acc, r, j_older):
        if j_older is not None:
            def cond(c):
                j, _, r_c = c
                return jnp.logical_and(j >= 0, jnp.min(r_c) < ATT_SKIP)

            def body(c):
                j, a_c, r_c = c
                k0 = pl.multiple_of(j * tile, tile)
                [(pv, dr)] = _att_tiles([q2], [[(k_ref[pl.ds(k0, tile), :], v_ref[pl.ds(k0, tile), :])]], tri,
                                        [r_c], causal_first=False)
                return j - 1, a_c + pv, r_c + dr

            _, acc, _ = lax.while_loop(cond, body, (j_older, acc, r))
        o_ref[pl.ds(q0, tile), :] = jnp.where(head0, acc[:tile], acc[tile:])

    n_prev = ATT_KEYS_AHEAD // tile
    group = ATT_TILES_PER_STEP
    first = n_prev + (nq - n_prev) % group
    for i in range(first):
        [part] = near([i * tile], min(i, n_prev))
        far_and_store(i * tile, *part, None if i <= n_prev else i - n_prev - 1)

    def later(g, _):
        tiles = [first + g * group + t for t in range(group)]
        q0s = [pl.multiple_of(i * tile, tile) for i in tiles]
        for i, q0, part in zip(tiles, q0s, near(q0s, n_prev)):
            far_and_store(q0, *part, i - n_prev - 1)
        return 0

    lax.fori_loop(0, (nq - first) // group, later, 0)


def _attention(q, k, v, bsz, t_len):
    n, d_att = q.shape
    spec = pl.BlockSpec((t_len, LANES), lambda b, p: (b, p))
    return pl.pallas_call(
        functools.partial(_attn_kernel, t_len=t_len, tile=ATT_TILE),
        grid=(bsz, d_att // LANES),
        in_specs=[spec, spec, spec],
        out_specs=spec,
        out_shape=jax.ShapeDtypeStruct((n, d_att), F32),
        compiler_params=_cparams("parallel", "parallel"),
        name="attn",
    )(q, k, v)


def _rms(t, gain):
    return t * lax.rsqrt(jnp.mean(t * t, axis=-1, keepdims=True) + EPS) * gain


def _gelu_tanh(y):
    return 0.5 * y * (1.0 + jnp.tanh(math.sqrt(2.0 / math.pi) * (y + 0.044715 * (y * y * y))))


def _pack_bf16_pairs(a, b):
    ua = pltpu.bitcast(a.astype(BF16).astype(F32), jnp.uint32)
    ub = pltpu.bitcast(b.astype(BF16).astype(F32), jnp.uint32)
    return ua | (ub >> 16)


def _unpack_bf16_pairs(w):
    return pltpu.bitcast(w & jnp.uint32(0xFFFF0000), F32), pltpu.bitcast(w << 16, F32)


def _post_kernel(x_ref, ys_ref, ya_ref, wglu_ref, bglu_ref, gs_ref, ga_ref, wo_ref, gf_ref,
                 wr_ref, br_ref, x1_ref, h2_ref, route_ref, route_t_ref, cnt_ref, run_ref, *, d_ssm):
    i = pl.program_id(0)

    @pl.when(i == 0)
    def _():
        run_ref[...] = jnp.zeros_like(run_ref)

    tm = x_ref.shape[0]
    groups = [pl.ds(g * (tm // POST_ROW_GROUPS), tm // POST_ROW_GROUPS) for g in range(POST_ROW_GROUPS)]
    ys = [_gelu_tanh(ys_ref[rows, :]) for rows in groups]
    gates = [jnp.dot(y.astype(BF16), wglu_ref[...], preferred_element_type=F32) + bglu_ref[...] for y in ys]
    m_as = [_rms(ya_ref[rows, :], ga_ref[...]).astype(BF16) for rows in groups]
    m_ss = [_rms(y * jax.nn.sigmoid(gate), gs_ref[...]).astype(BF16) for y, gate in zip(ys, gates)]
    mixes = [jnp.dot(m_s, wo_ref[:d_ssm, :], preferred_element_type=F32)
             + jnp.dot(m_a, wo_ref[d_ssm:, :], preferred_element_type=F32) for m_s, m_a in zip(m_ss, m_as)]
    x1s = [x_ref[rows, :] + mix for rows, mix in zip(groups, mixes)]
    h2s = [_rms(x1, gf_ref[...]) for x1 in x1s]
    logits = [jnp.dot(h2.astype(BF16), wr_ref[...], preferred_element_type=F32) for h2 in h2s]
    half = x_ref.shape[1] // 2
    for rows, x1, h2 in zip(groups, x1s, h2s):
        x1_ref[rows, :] = x1
        h2_ref[rows, :] = _pack_bf16_pairs(h2[:, :half], h2[:, half:])
    lg = jnp.concatenate(logits, axis=0) + br_ref[...]
    lane = _lane_iota(lg.shape).astype(F32)
    neg = -jnp.inf
    first = lambda hit: jnp.min(jnp.where(hit, lane, float(LANES)), axis=-1, keepdims=True)
    glog = jnp.where(lane < N_EXPERT_GROUPS, lg, neg)
    gmax = jnp.max(glog, axis=-1, keepdims=True)
    p_grp = 1.0 / jnp.sum(jnp.exp(glog - gmax), axis=-1, keepdims=True)
    grp = first(glog == gmax)
    e0 = ROUTER_LANE0 + grp * EXPERTS_PER_GROUP
    elog = jnp.where(jnp.logical_and(lane >= e0, lane < e0 + EXPERTS_PER_GROUP), lg, neg)
    m1 = jnp.max(elog, axis=-1, keepdims=True)
    i1 = first(elog == m1)
    elog2 = jnp.where(lane == i1, neg, elog)
    m2 = jnp.max(elog2, axis=-1, keepdims=True)
    i2 = first(elog2 == m2)
    e21 = jnp.exp(m2 - m1)
    g1 = p_grp * (1.0 / (1.0 + e21))
    g2 = p_grp * (e21 / (1.0 + e21))

    sel1 = lane == i1
    sel2 = lane == i2
    onehot = jnp.where(jnp.logical_or(sel1, sel2), 1.0, 0.0)
    r_i = lax.broadcasted_iota(jnp.int32, (tm, tm), 0)
    c_i = lax.broadcasted_iota(jnp.int32, (tm, tm), 1)
    lower = jnp.where(c_i < r_i, 1.0, 0.0).astype(BF16)
    before = jnp.dot(lower, onehot.astype(BF16), preferred_element_type=F32) + run_ref[0:1, :]
    rank1 = jnp.sum(jnp.where(sel1, before, 0.0), axis=-1, keepdims=True)
    rank2 = jnp.sum(jnp.where(sel2, before, 0.0), axis=-1, keepdims=True)
    run_ref[0:1, :] = run_ref[0:1, :] + jnp.sum(onehot, axis=0, keepdims=True)
    cnt_ref[...] = run_ref[...]

    fields = (i1 - ROUTER_LANE0, i2 - ROUTER_LANE0, g1, g2, rank1, rank2)
    route = jnp.zeros(lg.shape, F32)
    for pos, val in enumerate(fields):
        route = jnp.where(lane == pos, val, route)
    route_ref[...] = route
    route_t_ref[...] = route.T[:ROUTE_FIELDS, :]


def _post(x2, y_ssm_t, y_att, w_glu_bf, b_glu, g_ssm, g_att, w_out_bf, g_ffn, w_r_bf, b_r, part):
    n, d = x2.shape
    d_ssm = w_glu_bf.shape[0]
    tm = POST_TM
    nt = y_ssm_t.shape[0] // tm
    steps = n // (tm * MOE_PARTS)
    i0 = part * steps
    row_in = lambda w: pl.BlockSpec((tm, w), lambda i: (i0 + i, 0))
    row = lambda w: pl.BlockSpec((tm, w), lambda i: (i, 0))
    ssm_spec = pl.BlockSpec((tm, d_ssm), lambda i: ((i0 + i) % nt, (i0 + i) // nt))
    full = lambda a: pl.BlockSpec(a.shape, lambda i: (0,) * a.ndim)
    cnt_spec = pl.BlockSpec((8, LANES), lambda i: (0, 0))
    n = n // MOE_PARTS
    return pl.pallas_call(
        functools.partial(_post_kernel, d_ssm=d_ssm),
        grid=(steps,),
        in_specs=[row_in(d), ssm_spec, row_in(y_att.shape[1]), full(w_glu_bf), full(b_glu), full(g_ssm),
                  full(g_att), full(w_out_bf), full(g_ffn), full(w_r_bf), full(b_r)],
        out_specs=[row(d), row(d // 2), row(LANES), pl.BlockSpec((ROUTE_FIELDS, tm), lambda i: (0, i)), cnt_spec],
        out_shape=[jax.ShapeDtypeStruct((n, d), F32), jax.ShapeDtypeStruct((n, d // 2), jnp.uint32),
                   jax.ShapeDtypeStruct((n, LANES), F32), jax.ShapeDtypeStruct((ROUTE_FIELDS, n), F32),
                   jax.ShapeDtypeStruct((8, LANES), F32)],
        scratch_shapes=[pltpu.VMEM((8, LANES), F32)],
        compiler_params=_cparams("arbitrary"),
        name="post",
    )(x2, y_ssm_t, y_att, w_glu_bf, b_glu, g_ssm, g_att, w_out_bf, g_ffn, w_r_bf, b_r)


def _dispatch_sc(d0, d1, h2p, n_rows):
    n, w = h2p.shape
    workers = SC_CORES * SC_SUBCORES
    n_win = n // (workers * SC_WINDOW)
    mesh = plsc.VectorSubcoreMesh(core_axis_name="c", subcore_axis_name="s")

    def body(h_hbm, d0_hbm, d1_hbm, o_hbm, rows_v, i0_v, i1_v):
        wid = lax.axis_index("c") * SC_SUBCORES + lax.axis_index("s")

        @pl.loop(0, n_win)
        def _(j):
            blk = wid * n_win + j
            pltpu.sync_copy(h_hbm.at[pl.ds(blk * SC_WINDOW, SC_WINDOW)], rows_v)
            pltpu.sync_copy(d0_hbm.at[blk], i0_v)
            pltpu.sync_copy(d1_hbm.at[blk], i1_v)
            pltpu.sync_copy(rows_v, o_hbm.at[i0_v])
            pltpu.sync_copy(rows_v, o_hbm.at[i1_v])

    return pl.kernel(
        body,
        out_type=jax.ShapeDtypeStruct((n_rows, w), h2p.dtype),
        mesh=mesh,
        scratch_types=[pltpu.VMEM((SC_WINDOW, w), h2p.dtype), pltpu.VMEM((SC_WINDOW,), jnp.int32),
                       pltpu.VMEM((SC_WINDOW,), jnp.int32)],
        name="dispatch_sc",
    )(h2p, d0, d1)


def _experts_kernel(blk_e_ref, next_e_ref, used_ref, x_ref, wg_hbm, wu_hbm, wd_hbm, o_ref,
                    wg_f32, wu_f32, wd_f32, wg_bf, wu_bf, wd_bf, slot_ref, sems):
    i = pl.program_id(0)
    staged = ((wg_hbm, wg_f32, wg_bf), (wu_hbm, wu_f32, wu_bf), (wd_hbm, wd_f32, wd_bf))

    def fetch(expert, slot):
        return [pltpu.make_async_copy(hbm.at[expert], f32.at[slot], sems.at[slot, k])
                for k, (hbm, f32, _) in enumerate(staged)]

    @pl.when(i == 0)
    def _():
        slot_ref[0] = 0
        for cp in fetch(blk_e_ref[0], 0):
            cp.start()

    @pl.when(jnp.logical_or(i == 0, blk_e_ref[i] != blk_e_ref[jnp.maximum(i - 1, 0)]))
    def _():
        slot = slot_ref[0]
        for cp in fetch(blk_e_ref[i], slot):
            cp.wait()
        for _, f32, bf in staged:
            bf[...] = f32[slot].astype(BF16)
        nxt = next_e_ref[i]

        @pl.when(nxt >= 0)
        def _():
            for cp in fetch(nxt, 1 - slot):
                cp.start()

        slot_ref[0] = 1 - slot

    @pl.when(i < used_ref[0])
    def _():
        xa, xb = (t.astype(BF16) for t in _unpack_bf16_pairs(x_ref[...]))
        half = xa.shape[1]
        gate = (jnp.dot(xa, wg_bf[:half, :], preferred_element_type=F32)
                + jnp.dot(xb, wg_bf[half:, :], preferred_element_type=F32))
        up = (jnp.dot(xa, wu_bf[:half, :], preferred_element_type=F32)
              + jnp.dot(xb, wu_bf[half:, :], preferred_element_type=F32))
        hid = (jax.nn.silu(gate) * up).astype(BF16)
        out = jnp.dot(hid, wd_bf[...], preferred_element_type=F32)
        o_ref[...] = _pack_bf16_pairs(out[:, :half], out[:, half:])

    @pl.when(i >= used_ref[0])
    def _():
        o_ref[...] = jnp.zeros_like(o_ref)


def _experts(blk_e, next_e, n_used, buf, w_gate, w_up, w_down):
    n_rows, w = buf.shape
    d = w_down.shape[2]
    hbm = pl.BlockSpec(memory_space=pl.ANY)
    weights = (w_gate, w_up, w_down)
    grid_spec = pltpu.PrefetchScalarGridSpec(
        num_scalar_prefetch=3,
        grid=(n_rows // MOE_ROWS,),
        in_specs=[pl.BlockSpec((MOE_ROWS, w), lambda i, be, ne, nu: (i, 0)), hbm, hbm, hbm],
        out_specs=pl.BlockSpec((MOE_ROWS, d // 2), lambda i, be, ne, nu: (i, 0)),
        scratch_shapes=([pltpu.VMEM((2,) + a.shape[1:], F32) for a in weights]
                        + [pltpu.VMEM(a.shape[1:], BF16) for a in weights]
                        + [pltpu.SMEM((1,), jnp.int32), pltpu.SemaphoreType.DMA((2, len(weights)))]),
    )
    return pl.pallas_call(
        _experts_kernel,
        grid_spec=grid_spec,
        out_shape=jax.ShapeDtypeStruct((n_rows, d // 2), jnp.uint32),
        compiler_params=_cparams("arbitrary"),
        name="experts",
    )(blk_e, next_e, n_used, buf, w_gate, w_up, w_down)


def _gather_sc(d0, d1, eo):
    n = d0.size
    w = eo.shape[1]
    workers = SC_CORES * SC_SUBCORES
    n_win = n // (workers * SC_WINDOW)
    mesh = plsc.VectorSubcoreMesh(core_axis_name="c", subcore_axis_name="s")

    def body(eo_hbm, d0_hbm, d1_hbm, o0_hbm, o1_hbm, rows_v, i_v):
        wid = lax.axis_index("c") * SC_SUBCORES + lax.axis_index("s")

        @pl.loop(0, n_win)
        def _(j):
            blk = wid * n_win + j
            for d_hbm, o_hbm in ((d0_hbm, o0_hbm), (d1_hbm, o1_hbm)):
                pltpu.sync_copy(d_hbm.at[blk], i_v)
                pltpu.sync_copy(eo_hbm.at[i_v], rows_v)
                pltpu.sync_copy(rows_v, o_hbm.at[pl.ds(blk * SC_WINDOW, SC_WINDOW)])

    out_sd = jax.ShapeDtypeStruct((n, w), eo.dtype)
    return pl.kernel(
        body,
        out_type=(out_sd, out_sd),
        mesh=mesh,
        scratch_types=[pltpu.VMEM((SC_WINDOW, w), eo.dtype), pltpu.VMEM((SC_WINDOW,), jnp.int32)],
        name="gather_sc",
    )(eo, d0, d1)


def _combine_kernel(x1_ref, route_ref, r0_ref, r1_ref, *rest):
    o_ref = rest[-1]
    route = route_ref[...]
    row0 = jnp.concatenate(_unpack_bf16_pairs(r0_ref[...]), axis=1)
    row1 = jnp.concatenate(_unpack_bf16_pairs(r1_ref[...]), axis=1)
    o_ref[...] = x1_ref[...] + (route[:, 2:3] * row0 + route[:, 3:4] * row1)


def _combine(x1, route, rows0, rows1, part, out_prev):
    n_slice, d = x1.shape
    tm = COMBINE_TM
    steps = n_slice // tm
    row = lambda w: pl.BlockSpec((tm, w), lambda i: (i, 0))
    in_specs = [row(d), row(LANES), row(d // 2), row(d // 2)]
    args = [x1, route, rows0, rows1]
    aliases = {}
    if out_prev is not None:
        in_specs.append(pl.BlockSpec(memory_space=pl.ANY))
        args.append(out_prev)
        aliases = {len(args) - 1: 0}
    return pl.pallas_call(
        _combine_kernel,
        grid=(steps,),
        in_specs=in_specs,
        out_specs=pl.BlockSpec((tm, d), lambda i: (part * steps + i, 0)),
        out_shape=jax.ShapeDtypeStruct((n_slice * MOE_PARTS, d), F32),
        input_output_aliases=aliases,
        compiler_params=_cparams("parallel"),
        name="combine",
    )(*args)


def _layer(x, g_mix, w_in, lam_re, lam_im, log_dt, b_re, b_im, c_re, c_im, d_skip, w_glu, b_glu, g_q, g_k,
           g_ssm_out, g_attn_out, w_out, g_ffn, w_rg, b_rg, w_re, b_re_router, w_gate, w_up, w_down):
    bsz, t_len, d = x.shape
    n = bsz * t_len
    d_ssm = w_glu.shape[0]
    d_att = g_attn_out.shape[0]
    n_heads = d_att // HEAD_DIM
    n_chunks = t_len // SSM_CHUNK
    x2 = x.reshape(n, d)

    u_t, q, k, v = _in_proj(x, g_mix[None], w_in.astype(BF16), jnp.tile(g_q, n_heads)[None],
                            jnp.tile(g_k, n_heads)[None], d_ssm, d_att)
    tables = _s5_tables(lam_re, lam_im, log_dt, b_re, b_im, c_re, c_im, d_skip)
    y_ssm_t = _s5(u_t.reshape(n_chunks, SSM_CHUNK, bsz, d_ssm), tables).reshape(t_len, bsz * d_ssm)

    y_att = _attention(q.reshape(n, d_att), k.reshape(n, d_att), v.reshape(n, d_att), bsz, t_len)

    w_r = jnp.concatenate([w_rg, w_re.reshape(d, N_EXPERTS)], axis=1)
    w_r = jnp.pad(w_r, ((0, 0), (0, LANES - w_r.shape[1]))).astype(BF16)
    b_r = jnp.pad(jnp.concatenate([b_rg, b_re_router.reshape(N_EXPERTS)]), (0, LANES - ROUTER_LANE0 - N_EXPERTS))[None]
    w_glu_bf, w_out_bf = w_glu.astype(BF16), w_out.astype(BF16)
    n_slice = n // MOE_PARTS
    n_rows = n_slice * 2 + N_EXPERTS * MOE_ROWS
    n_blk = n_rows // MOE_ROWS
    out = None
    for part in range(MOE_PARTS):
        x1, h2p, route, route_t, cnt = _post(x2, y_ssm_t, y_att, w_glu_bf, b_glu[None], g_ssm_out[None],
                                             g_attn_out[None], w_out_bf, g_ffn[None], w_r, b_r, part)
        counts = cnt[0, ROUTER_LANE0:ROUTER_LANE0 + N_EXPERTS].astype(jnp.int32)
        pcounts = ((counts + MOE_ROWS - 1) // MOE_ROWS) * MOE_ROWS
        pends = jnp.cumsum(pcounts)
        pstarts = pends - pcounts
        e_ids = jnp.arange(N_EXPERTS, dtype=jnp.int32)
        dests = []
        for k in range(2):
            expert = route_t[k].astype(jnp.int32)
            start = jnp.sum(jnp.where(expert[:, None] == e_ids, pstarts, 0), axis=-1)
            dest = start + route_t[4 + k].astype(jnp.int32)
            dests.append(dest.reshape(n_slice // SC_WINDOW, SC_WINDOW))
        blk_row0 = jnp.arange(n_blk, dtype=jnp.int32)[:, None] * MOE_ROWS
        blk_e = jnp.minimum(jnp.sum((pends[None, :] <= blk_row0).astype(jnp.int32), axis=1), N_EXPERTS - 1)
        n_used = (pends[-1:] // MOE_ROWS).astype(jnp.int32)
        present = jnp.any(blk_e[:, None] == e_ids, axis=0)
        later = jnp.logical_and(present[None, :], e_ids[None, :] > blk_e[:, None])
        next_e = jnp.min(jnp.where(later, e_ids[None, :], N_EXPERTS), axis=1)
        next_e = jnp.where(next_e == N_EXPERTS, -1, next_e).astype(jnp.int32)

        buf = _dispatch_sc(*dests, h2p, n_rows)
        eo = _experts(blk_e, next_e, n_used, buf, w_gate, w_up, w_down)
        out = _combine(x1, route, *_gather_sc(*dests, eo), part, out)
    return out.reshape(bsz, t_len, d)


def kernel(x, g_mix, w_in, ssm_lambda_re, ssm_lambda_im, ssm_log_dt, ssm_b_re, ssm_b_im, ssm_c_re, ssm_c_im, ssm_d, ssm_w_glu, ssm_b_glu, g_q, g_k, g_ssm_out, g_attn_out, w_out, g_ffn, w_router_group, b_router_group, w_router_expert, b_router_expert, w_gate, w_up, w_down):
    for l in range(g_mix.shape[0]):
        x = _layer(x, g_mix[l], w_in[l], ssm_lambda_re[l], ssm_lambda_im[l], ssm_log_dt[l], ssm_b_re[l],
                   ssm_b_im[l], ssm_c_re[l], ssm_c_im[l], ssm_d[l], ssm_w_glu[l], ssm_b_glu[l], g_q[l], g_k[l],
                   g_ssm_out[l], g_attn_out[l], w_out[l], g_ffn[l], w_router_group[l], b_router_group[l],
                   w_router_expert[l], b_router_expert[l], w_gate[l], w_up[l], w_down[l])
    return x
```

```python
import functools
import math

import jax
import jax.numpy as jnp
from jax import lax
from jax.experimental import pallas as pl
from jax.experimental.pallas import tpu as pltpu
from jax.experimental.pallas import tpu_sc as plsc

F32 = jnp.float32
BF16 = jnp.bfloat16
EPS = 1e-6

LANES = 128
VMEM_LIMIT_BYTES = 56 * 1024 * 1024

SSM_GROUP = 16
SSM_STATE = 64
SSM_CHUNK = 16
HEAD_DIM = 64
N_EXPERT_GROUPS = 4
EXPERTS_PER_GROUP = 8
N_EXPERTS = N_EXPERT_GROUPS * EXPERTS_PER_GROUP
ROUTER_LANE0 = N_EXPERT_GROUPS
ROUTE_FIELDS = 8
MOE_ROWS = 512
MOE_PARTS = 2
MOE_BLOCKS_PER_STEP = 4
ATT_SKIP = 110.0

S5_GPB = LANES // SSM_GROUP
S5_CHUNKS_PER_STEP = 8

IN_TM = 512
IN_BATCHES = 8
ATT_TILE = 256
ATT_KEYS_AHEAD = 256
ATT_TILES_PER_STEP = 2
POST_TM = 512
POST_ROW_GROUPS = 2
COMBINE_TM = 1024
SC_CORES = 2
SC_SUBCORES = 16
SC_WINDOW = 64


def _cparams(*sem):
    return pltpu.CompilerParams(dimension_semantics=sem, vmem_limit_bytes=VMEM_LIMIT_BYTES)


def _lane_iota(shape):
    return lax.broadcasted_iota(jnp.int32, shape, len(shape) - 1)


def _head_rms(t, gain):
    outs = []
    for c in range(t.shape[1] // LANES):
        blk = t[:, c * LANES:(c + 1) * LANES]
        sq = blk * blk
        lo = _lane_iota(blk.shape) < HEAD_DIM
        s_lo = jnp.sum(jnp.where(lo, sq, 0.0), axis=-1, keepdims=True)
        s_hi = jnp.sum(jnp.where(lo, 0.0, sq), axis=-1, keepdims=True)
        inv = jnp.where(lo, lax.rsqrt(s_lo * (1.0 / HEAD_DIM) + EPS),
                        lax.rsqrt(s_hi * (1.0 / HEAD_DIM) + EPS))
        outs.append(blk * inv * gain[:, c * LANES:(c + 1) * LANES])
    return jnp.concatenate(outs, axis=-1)


def _in_proj_kernel(x_ref, g_ref, w_ref, gq_ref, gk_ref, u_ref, q_ref, k_ref, v_ref, *, d_ssm, d_att, scale):
    nb, tt, d = x_ref.shape
    x = x_ref[...].reshape(nb * tt, d)
    inv = lax.rsqrt(jnp.mean(x * x, axis=-1, keepdims=True) + EPS)
    h = (x * inv * g_ref[...]).astype(BF16)
    proj = jnp.dot(h, w_ref[...], preferred_element_type=F32)
    u_ref[...] = pltpu.einshape("btc->tbc", proj[:, :d_ssm].astype(BF16).reshape(nb, tt, d_ssm))
    q = _head_rms(proj[:, d_ssm:d_ssm + d_att], gq_ref[...])
    k = _head_rms(proj[:, d_ssm + d_att:d_ssm + 2 * d_att], gk_ref[...])
    q_ref[...] = (q * scale).astype(BF16).reshape(nb, tt, d_att)
    k_ref[...] = k.astype(BF16).reshape(nb, tt, d_att)
    v_ref[...] = proj[:, d_ssm + 2 * d_att:].astype(BF16).reshape(nb, tt, d_att)


def _in_proj(x, g_mix, w_in_bf, gq_t, gk_t, d_ssm, d_att):
    bsz, t_len, d = x.shape
    nb = IN_BATCHES
    tt = IN_TM // nb
    tok = lambda w: pl.BlockSpec((nb, tt, w), lambda b, t: (b, t, 0))
    full = lambda a: pl.BlockSpec(a.shape, lambda b, t: (0,) * a.ndim)
    out_sd = jax.ShapeDtypeStruct((bsz, t_len, d_att), BF16)
    return pl.pallas_call(
        functools.partial(_in_proj_kernel, d_ssm=d_ssm, d_att=d_att, scale=1.0 / math.sqrt(HEAD_DIM)),
        grid=(bsz // nb, t_len // tt),
        in_specs=[tok(d), full(g_mix), full(w_in_bf), full(gq_t), full(gk_t)],
        out_specs=[pl.BlockSpec((tt, nb, d_ssm), lambda b, t: (t, b, 0)), tok(d_att), tok(d_att), tok(d_att)],
        out_shape=[jax.ShapeDtypeStruct((t_len, bsz, d_ssm), BF16), out_sd, out_sd, out_sd],
        compiler_params=_cparams("parallel", "parallel"),
        name="in_proj",
    )(x, g_mix, w_in_bf, gq_t, gk_t)


def _s5_tables(lam_re, lam_im, log_dt, b_re, b_im, c_re, c_im, d_skip):
    hp = lax.Precision.HIGHEST
    L = SSM_CHUNK
    g_n, p_n = lam_re.shape
    dt = jnp.exp(log_dt)[:, None]
    lr, li = lam_re, lam_im
    ls = jnp.arange(L + 1, dtype=F32)[:, None, None]
    mag = jnp.exp(lr * dt * ls)
    pr, pi = mag * jnp.cos(li * dt * ls), mag * jnp.sin(li * dt * ls)
    abar_r, abar_i = pr[1], pi[1]
    den = lr * lr + li * li
    nr, ni = abar_r - 1.0, abar_i
    coef_r = (nr * lr + ni * li) / den
    coef_i = (ni * lr - nr * li) / den
    bbr = coef_r[..., None] * b_re - coef_i[..., None] * b_im
    bbi = coef_r[..., None] * b_im + coef_i[..., None] * b_re
    wr = pr[..., None] * bbr - pi[..., None] * bbi
    wi = pr[..., None] * bbi + pi[..., None] * bbr
    kl = (jnp.einsum('gop,lgpi->lgoi', c_re, wr[:L], precision=hp)
          - jnp.einsum('gop,lgpi->lgoi', c_im, wi[:L], precision=hp))
    kl = kl.at[0].add(jax.vmap(jnp.diag)(d_skip))
    n_lb = g_n // S5_GPB
    sg = SSM_GROUP
    kc = kl.transpose(1, 3, 0, 2).reshape(n_lb, S5_GPB * sg, L * sg)
    b_rows = lambda w: (w[:L][::-1].reshape(L, n_lb, S5_GPB, p_n, sg).transpose(1, 0, 2, 4, 3)
                        .reshape(n_lb, L * S5_GPB * sg, p_n))
    p1r, p1i = pr[1:], pi[1:]
    cst_r = (c_re[None] * p1r[:, :, None, :] - c_im[None] * p1i[:, :, None, :])
    cst_i = -(c_re[None] * p1i[:, :, None, :] + c_im[None] * p1r[:, :, None, :])
    c_rows = lambda c: c.transpose(1, 3, 0, 2).reshape(n_lb, S5_GPB * p_n, L * sg)
    a_l = jnp.stack([pr[L].reshape(n_lb, S5_GPB * p_n), pi[L].reshape(n_lb, S5_GPB * p_n)], axis=1)
    return (kc.astype(BF16), b_rows(wr).astype(BF16), b_rows(wi).astype(BF16),
            c_rows(cst_r).astype(BF16), c_rows(cst_i).astype(BF16), a_l)


def _s5_expand_tables(kc_ref, bcr_ref, bci_ref, ccr_ref, cci_ref, w0_ref, br_ref, bi_ref, cr_ref, ci_ref):
    sg, gpb = SSM_GROUP, S5_GPB
    n_lo = w0_ref.shape[1]
    p_n = bcr_ref.shape[2]

    def iotas(shape):
        return lax.broadcasted_iota(jnp.int32, shape, 0), lax.broadcasted_iota(jnp.int32, shape, 1)

    r, c = iotas((kc_ref.shape[2], n_lo))
    spread_o = jnp.where(jnp.logical_and(r // sg == c // LANES, r % sg == c % sg), 1.0, 0.0).astype(BF16)
    r, c = iotas((p_n, gpb * p_n))
    spread_p = jnp.where(r == c % p_n, 1.0, 0.0).astype(BF16)

    def expand(compact, spread, row_group, col_group):
        full = jnp.dot(compact, spread, preferred_element_type=F32)
        r, c = iotas(full.shape)
        return jnp.where(row_group(r) == col_group(c), full, 0.0).astype(BF16)

    lane_group = lambda c: (c % LANES) // sg
    top = expand(kc_ref[0], spread_o, lambda r: r // sg, lane_group)
    w0_ref[:LANES, :] = top
    w0_ref[LANES:, :LANES] = jnp.zeros((LANES, LANES), BF16)
    w0_ref[LANES:, LANES:] = top[:, :n_lo - LANES]
    state_group = lambda c: c // p_n
    br_ref[...] = expand(bcr_ref[0], spread_p, lambda r: (r // sg) % gpb, state_group)
    bi_ref[...] = expand(bci_ref[0], spread_p, lambda r: (r // sg) % gpb, state_group)
    cr_ref[...] = expand(ccr_ref[0], spread_o, lambda r: r // p_n, lane_group)
    ci_ref[...] = expand(cci_ref[0], spread_o, lambda r: r // p_n, lane_group)


def _s5_kernel(u_ref, kc_ref, bcr_ref, bci_ref, ccr_ref, cci_ref, a_ref, y_ref,
               hr_ref, hi_ref, acc_ref, w0_ref, br_ref, bi_ref, cr_ref, ci_ref):
    n_chunks, L, bsz, _ = u_ref.shape
    rows = n_chunks * bsz

    @pl.when(pl.program_id(1) == 0)
    def _():
        hr_ref[...] = jnp.zeros_like(hr_ref)
        hi_ref[...] = jnp.zeros_like(hi_ref)
        _s5_expand_tables(kc_ref, bcr_ref, bci_ref, ccr_ref, cci_ref, w0_ref, br_ref, bi_ref, cr_ref, ci_ref)

    us = [u_ref[:, s].reshape(rows, LANES) for s in range(L)]
    lhs = jnp.concatenate(us, axis=1)
    sin_r = jnp.dot(lhs, br_ref[...], preferred_element_type=F32)
    sin_i = jnp.dot(lhs, bi_ref[...], preferred_element_type=F32)
    ar = a_ref[0, 0:1, :]
    ai = a_ref[0, 1:2, :]
    hr, hi = hr_ref[...], hi_ref[...]
    prev_r, prev_i = [], []
    for c in range(n_chunks):
        prev_r.append(hr)
        prev_i.append(hi)
        sl = slice(c * bsz, (c + 1) * bsz)
        hr, hi = ar * hr - ai * hi + sin_r[sl], ar * hi + ai * hr + sin_i[sl]
    hr_ref[...] = hr
    hi_ref[...] = hi
    pr = jnp.concatenate(prev_r, axis=0).astype(BF16)
    pi = jnp.concatenate(prev_i, axis=0).astype(BF16)
    acc_ref[...] = (jnp.dot(pr, cr_ref[...], preferred_element_type=F32)
                    + jnp.dot(pi, ci_ref[...], preferred_element_type=F32))
    for p in range(L // 2):
        off = 2 * p * LANES
        pair = jnp.concatenate([us[2 * p], us[2 * p + 1]], axis=1)
        acc_ref[:, off:] += jnp.dot(pair, w0_ref[:, :L * LANES - off], preferred_element_type=F32)
    for t in range(L):
        y_ref[:, t] = acc_ref[:, t * LANES:(t + 1) * LANES].reshape(n_chunks, bsz, LANES)


def _s5(u4, tables):
    n_chunks, L, bsz, d_ssm = u4.shape
    a_l = tables[-1]
    cb = S5_CHUNKS_PER_STEP
    data = pl.BlockSpec((cb, L, bsz, LANES), lambda lb, c: (c, 0, 0, lb))
    per_lb = lambda a: pl.BlockSpec((1,) + a.shape[1:], lambda lb, c: (lb,) + (0,) * (a.ndim - 1))
    n_state = a_l.shape[2]
    state = pltpu.VMEM((bsz, n_state), F32)
    n_lo = L * LANES
    return pl.pallas_call(
        _s5_kernel,
        grid=(d_ssm // LANES, n_chunks // cb),
        in_specs=[data] + [per_lb(t) for t in tables],
        out_specs=data,
        out_shape=jax.ShapeDtypeStruct(u4.shape, F32),
        scratch_shapes=[state, state, pltpu.VMEM((cb * bsz, n_lo), F32),
                        pltpu.VMEM((2 * LANES, n_lo), BF16),
                        pltpu.VMEM((n_lo, n_state), BF16), pltpu.VMEM((n_lo, n_state), BF16),
                        pltpu.VMEM((n_state, n_lo), BF16), pltpu.VMEM((n_state, n_lo), BF16)],
        compiler_params=_cparams("arbitrary", "arbitrary"),
        name="s5",
    )(u4, *tables)


def _softplus(z):
    return jnp.maximum(z, 0.0) + jnp.log(1.0 + jnp.exp(-jnp.abs(z)))


def _att_tiles(q2s, kvss, tri, r_ins, causal_first):
    tq = q2s[0].shape[0] // 2
    n_t = len(q2s)
    contract_last = (((1,), (1,)), ((), ()))
    zs = [[lax.dot_general(q2, k, contract_last, preferred_element_type=F32) for k, _ in kvs]
          for q2, kvs in zip(q2s, kvss)]
    sps = [[_softplus(z) for z in row] for row in zs]
    sp_ms = [list(row) for row in sps]
    if causal_first:
        rows = lax.broadcasted_iota(jnp.int32, zs[0][0].shape, 0)
        cols = lax.broadcasted_iota(jnp.int32, zs[0][0].shape, 1)
        keep = cols < jnp.where(rows >= tq, rows - tq, rows)
        for t in range(n_t):
            sp_ms[t][0] = jnp.where(keep, sps[t][0], 0.0)
    stacked = jnp.concatenate([blk for row in sp_ms for blk in row], axis=0).astype(BF16)
    newer_all = jnp.dot(stacked, tri, preferred_element_type=F32)
    pvs, totals = [None] * n_t, list(r_ins)
    base = 0
    offsets = []
    for row in sp_ms:
        offsets.append(base)
        base += len(row) * 2 * tq
    for p in range(max(len(kvs) for kvs in kvss)):
        for t in range(n_t):
            if p >= len(kvss[t]):
                continue
            newer = newer_all[offsets[t] + p * 2 * tq:offsets[t] + (p + 1) * 2 * tq]
            att = jnp.exp(zs[t][p] - sps[t][p] - newer - totals[t])
            if causal_first and p == 0:
                att = jnp.where(keep, att, 0.0)
            part = jnp.dot(att.astype(BF16), kvss[t][p][1], preferred_element_type=F32)
            pvs[t] = part if pvs[t] is None else pvs[t] + part
            totals[t] = totals[t] + (newer[:, 0:1] + sp_ms[t][p][:, 0:1])
    return [(pv, total - r_in) for pv, total, r_in in zip(pvs, totals, r_ins)]


def _attn_kernel(q_ref, k_ref, v_ref, o_ref, *, t_len, tile):
    nq = t_len // tile
    r_i = lax.broadcasted_iota(jnp.int32, (tile, tile), 0)
    c_i = lax.broadcasted_iota(jnp.int32, (tile, tile), 1)
    tri = jnp.where(r_i > c_i, 1.0, 0.0).astype(BF16)
    head0 = _lane_iota((tile, LANES)) < HEAD_DIM
    zero_r = jnp.zeros((2 * tile, 1), F32)

    def near(q0s, n_prev):
        q2s, kvss = [], []
        for q0 in q0s:
            q = q_ref[pl.ds(q0, tile), :]
            zq = jnp.zeros_like(q)
            q2s.append(jnp.concatenate([jnp.where(head0, q, zq), jnp.where(head0, zq, q)], axis=0))
            kvss.append([(k_ref[pl.ds(q0 - p * tile, tile), :], v_ref[pl.ds(q0 - p * tile, tile), :])
                         for p in range(n_prev + 1)])
        outs = _att_tiles(q2s, kvss, tri, [zero_r] * len(q0s), causal_first=True)
        return [(q2, acc, r) for q2, (acc, r) in zip(q2s, outs)]

    def far_and_store(q0, q2, acc, r, j_older):
        if j_older is not None:
            def cond(c):
                j, _, r_c = c
                return jnp.logical_and(j >= 0, jnp.min(r_c) < ATT_SKIP)

            def body(c):
                j, a_c, r_c = c
                k0 = pl.multiple_of(j * tile, tile)
                [(pv, dr)] = _att_tiles([q2], [[(k_ref[pl.ds(k0, tile), :], v_ref[pl.ds(k0, tile), :])]], tri,
                                        [r_c], causal_first=False)
                return j - 1, a_c + pv, r_c + dr

            _, acc, _ = lax.while_loop(cond, body, (j_older, acc, r))
        o_ref[pl.ds(q0, tile), :] = jnp.where(head0, acc[:tile], acc[tile:])

    n_prev = ATT_KEYS_AHEAD // tile
    group = ATT_TILES_PER_STEP
    first = n_prev + (nq - n_prev) % group
    for i in range(first):
        [part] = near([i * tile], min(i, n_prev))
        far_and_store(i * tile, *part, None if i <= n_prev else i - n_prev - 1)

    def later(g, _):
        tiles = [first + g * group + t for t in range(group)]
        q0s = [pl.multiple_of(i * tile, tile) for i in tiles]
        for i, q0, part in zip(tiles, q0s, near(q0s, n_prev)):
            far_and_store(q0, *part, i - n_prev - 1)
        return 0

    lax.fori_loop(0, (nq - first) // group, later, 0)


def _attention(q, k, v, bsz, t_len):
    n, d_att = q.shape
    spec = pl.BlockSpec((t_len, LANES), lambda b, p: (b, p))
    return pl.pallas_call(
        functools.partial(_attn_kernel, t_len=t_len, tile=ATT_TILE),
        grid=(bsz, d_att // LANES),
        in_specs=[spec, spec, spec],
        out_specs=spec,
        out_shape=jax.ShapeDtypeStruct((n, d_att), F32),
        compiler_params=_cparams("parallel", "parallel"),
        name="attn",
    )(q, k, v)


def _rms(t, gain):
    return t * lax.rsqrt(jnp.mean(t * t, axis=-1, keepdims=True) + EPS) * gain


def _gelu_tanh(y):
    return 0.5 * y * (1.0 + jnp.tanh(math.sqrt(2.0 / math.pi) * (y + 0.044715 * (y * y * y))))


def _pack_bf16_pairs(a, b):
    ua = pltpu.bitcast(a.astype(BF16).astype(F32), jnp.uint32)
    ub = pltpu.bitcast(b.astype(BF16).astype(F32), jnp.uint32)
    return ua | (ub >> 16)


def _unpack_bf16_pairs(w):
    return pltpu.bitcast(w & jnp.uint32(0xFFFF0000), F32), pltpu.bitcast(w << 16, F32)


def _post_kernel(x_ref, ys_ref, ya_ref, wglu_ref, bglu_ref, gs_ref, ga_ref, wo_ref, gf_ref,
                 wr_ref, br_ref, x1_ref, h2_ref, route_ref, route_t_ref, cnt_ref, run_ref, *, d_ssm):
    i = pl.program_id(0)

    @pl.when(i == 0)
    def _():
        run_ref[...] = jnp.zeros_like(run_ref)

    tm = x_ref.shape[0]
    groups = [pl.ds(g * (tm // POST_ROW_GROUPS), tm // POST_ROW_GROUPS) for g in range(POST_ROW_GROUPS)]
    ys = [_gelu_tanh(ys_ref[rows, :]) for rows in groups]
    gates = [jnp.dot(y.astype(BF16), wglu_ref[...], preferred_element_type=F32) + bglu_ref[...] for y in ys]
    m_as = [_rms(ya_ref[rows, :], ga_ref[...]).astype(BF16) for rows in groups]
    m_ss = [_rms(y * jax.nn.sigmoid(gate), gs_ref[...]).astype(BF16) for y, gate in zip(ys, gates)]
    mixes = [jnp.dot(m_s, wo_ref[:d_ssm, :], preferred_element_type=F32)
             + jnp.dot(m_a, wo_ref[d_ssm:, :], preferred_element_type=F32) for m_s, m_a in zip(m_ss, m_as)]
    x1s = [x_ref[rows, :] + mix for rows, mix in zip(groups, mixes)]
    h2s = [_rms(x1, gf_ref[...]) for x1 in x1s]
    logits = [jnp.dot(h2.astype(BF16), wr_ref[...], preferred_element_type=F32) for h2 in h2s]
    half = x_ref.shape[1] // 2
    for rows, x1, h2 in zip(groups, x1s, h2s):
        x1_ref[rows, :] = x1
        h2_ref[rows, :] = _pack_bf16_pairs(h2[:, :half], h2[:, half:])
    lg = jnp.concatenate(logits, axis=0) + br_ref[...]
    lane = _lane_iota(lg.shape).astype(F32)
    neg = -jnp.inf
    first = lambda hit: jnp.min(jnp.where(hit, lane, float(LANES)), axis=-1, keepdims=True)
    glog = jnp.where(lane < N_EXPERT_GROUPS, lg, neg)
    gmax = jnp.max(glog, axis=-1, keepdims=True)
    p_grp = 1.0 / jnp.sum(jnp.exp(glog - gmax), axis=-1, keepdims=True)
    grp = first(glog == gmax)
    e0 = ROUTER_LANE0 + grp * EXPERTS_PER_GROUP
    elog = jnp.where(jnp.logical_and(lane >= e0, lane < e0 + EXPERTS_PER_GROUP), lg, neg)
    m1 = jnp.max(elog, axis=-1, keepdims=True)
    i1 = first(elog == m1)
    elog2 = jnp.where(lane == i1, neg, elog)
    m2 = jnp.max(elog2, axis=-1, keepdims=True)
    i2 = first(elog2 == m2)
    e21 = jnp.exp(m2 - m1)
    g1 = p_grp * (1.0 / (1.0 + e21))
    g2 = p_grp * (e21 / (1.0 + e21))

    sel1 = lane == i1
    sel2 = lane == i2
    onehot = jnp.where(jnp.logical_or(sel1, sel2), 1.0, 0.0)
    r_i = lax.broadcasted_iota(jnp.int32, (tm, tm), 0)
    c_i = lax.broadcasted_iota(jnp.int32, (tm, tm), 1)
    lower = jnp.where(c_i < r_i, 1.0, 0.0).astype(BF16)
    before = jnp.dot(lower, onehot.astype(BF16), preferred_element_type=F32) + run_ref[0:1, :]
    rank1 = jnp.sum(jnp.where(sel1, before, 0.0), axis=-1, keepdims=True)
    rank2 = jnp.sum(jnp.where(sel2, before, 0.0), axis=-1, keepdims=True)
    run_ref[0:1, :] = run_ref[0:1, :] + jnp.sum(onehot, axis=0, keepdims=True)
    cnt_ref[...] = run_ref[...]

    fields = (i1 - ROUTER_LANE0, i2 - ROUTER_LANE0, g1, g2, rank1, rank2)
    route = jnp.zeros(lg.shape, F32)
    for pos, val in enumerate(fields):
        route = jnp.where(lane == pos, val, route)
    route_ref[...] = route
    route_t_ref[...] = route.T[:ROUTE_FIELDS, :]


def _post(x2, y_ssm_t, y_att, w_glu_bf, b_glu, g_ssm, g_att, w_out_bf, g_ffn, w_r_bf, b_r, part):
    n, d = x2.shape
    d_ssm = w_glu_bf.shape[0]
    tm = POST_TM
    nt = y_ssm_t.shape[0] // tm
    steps = n // (tm * MOE_PARTS)
    i0 = part * steps
    row_in = lambda w: pl.BlockSpec((tm, w), lambda i: (i0 + i, 0))
    row = lambda w: pl.BlockSpec((tm, w), lambda i: (i, 0))
    ssm_spec = pl.BlockSpec((tm, d_ssm), lambda i: ((i0 + i) % nt, (i0 + i) // nt))
    full = lambda a: pl.BlockSpec(a.shape, lambda i: (0,) * a.ndim)
    cnt_spec = pl.BlockSpec((8, LANES), lambda i: (0, 0))
    n = n // MOE_PARTS
    return pl.pallas_call(
        functools.partial(_post_kernel, d_ssm=d_ssm),
        grid=(steps,),
        in_specs=[row_in(d), ssm_spec, row_in(y_att.shape[1]), full(w_glu_bf), full(b_glu), full(g_ssm),
                  full(g_att), full(w_out_bf), full(g_ffn), full(w_r_bf), full(b_r)],
        out_specs=[row(d), row(d // 2), row(LANES), pl.BlockSpec((ROUTE_FIELDS, tm), lambda i: (0, i)), cnt_spec],
        out_shape=[jax.ShapeDtypeStruct((n, d), F32), jax.ShapeDtypeStruct((n, d // 2), jnp.uint32),
                   jax.ShapeDtypeStruct((n, LANES), F32), jax.ShapeDtypeStruct((ROUTE_FIELDS, n), F32),
                   jax.ShapeDtypeStruct((8, LANES), F32)],
        scratch_shapes=[pltpu.VMEM((8, LANES), F32)],
        compiler_params=_cparams("arbitrary"),
        name="post",
    )(x2, y_ssm_t, y_att, w_glu_bf, b_glu, g_ssm, g_att, w_out_bf, g_ffn, w_r_bf, b_r)


def _dispatch_sc(d0, d1, h2p, n_rows):
    n, w = h2p.shape
    workers = SC_CORES * SC_SUBCORES
    n_win = n // (workers * SC_WINDOW)
    mesh = plsc.VectorSubcoreMesh(core_axis_name="c", subcore_axis_name="s")

    def body(h_hbm, d0_hbm, d1_hbm, o_hbm, rows_v, i0_v, i1_v):
        wid = lax.axis_index("c") * SC_SUBCORES + lax.axis_index("s")

        @pl.loop(0, n_win)
        def _(j):
            blk = wid * n_win + j
            pltpu.sync_copy(h_hbm.at[pl.ds(blk * SC_WINDOW, SC_WINDOW)], rows_v)
            pltpu.sync_copy(d0_hbm.at[blk], i0_v)
            pltpu.sync_copy(d1_hbm.at[blk], i1_v)
            pltpu.sync_copy(rows_v, o_hbm.at[i0_v])
            pltpu.sync_copy(rows_v, o_hbm.at[i1_v])

    return pl.kernel(
        body,
        out_type=jax.ShapeDtypeStruct((n_rows, w), h2p.dtype),
        mesh=mesh,
        scratch_types=[pltpu.VMEM((SC_WINDOW, w), h2p.dtype), pltpu.VMEM((SC_WINDOW,), jnp.int32),
                       pltpu.VMEM((SC_WINDOW,), jnp.int32)],
        name="dispatch_sc",
    )(h2p, d0, d1)


def _experts_kernel(blk_e_ref, next_e_ref, used_ref, x_ref, wg_hbm, wu_hbm, wd_hbm, o_ref,
                    wg_f32, wu_f32, wd_f32, wg_bf, wu_bf, wd_bf, slot_ref, sems):
    staged = ((wg_hbm, wg_f32, wg_bf), (wu_hbm, wu_f32, wu_bf), (wd_hbm, wd_f32, wd_bf))

    def fetch(expert, slot):
        return [pltpu.make_async_copy(hbm.at[expert], f32.at[slot], sems.at[slot, k])
                for k, (hbm, f32, _) in enumerate(staged)]

    @pl.when(pl.program_id(0) == 0)
    def _():
        slot_ref[0] = 0
        for cp in fetch(blk_e_ref[0], 0):
            cp.start()

    for s in range(MOE_BLOCKS_PER_STEP):
        i = pl.program_id(0) * MOE_BLOCKS_PER_STEP + s
        rows = pl.ds(s * MOE_ROWS, MOE_ROWS)

        @pl.when(jnp.logical_or(i == 0, blk_e_ref[i] != blk_e_ref[jnp.maximum(i - 1, 0)]))
        def _():
            slot = slot_ref[0]
            for cp in fetch(blk_e_ref[i], slot):
                cp.wait()
            for _, f32, bf in staged:
                bf[...] = f32[slot].astype(BF16)
            nxt = next_e_ref[i]

            @pl.when(nxt >= 0)
            def _():
                for cp in fetch(nxt, 1 - slot):
                    cp.start()

            slot_ref[0] = 1 - slot

        @pl.when(i < used_ref[0])
        def _():
            xa, xb = (t.astype(BF16) for t in _unpack_bf16_pairs(x_ref[rows, :]))
            half = xa.shape[1]
            gate = (jnp.dot(xa, wg_bf[:half, :], preferred_element_type=F32)
                    + jnp.dot(xb, wg_bf[half:, :], preferred_element_type=F32))
            up = (jnp.dot(xa, wu_bf[:half, :], preferred_element_type=F32)
                  + jnp.dot(xb, wu_bf[half:, :], preferred_element_type=F32))
            hid = (jax.nn.silu(gate) * up).astype(BF16)
            out = jnp.dot(hid, wd_bf[...], preferred_element_type=F32)
            o_ref[rows, :] = _pack_bf16_pairs(out[:, :half], out[:, half:])

        @pl.when(i >= used_ref[0])
        def _():
            o_ref[rows, :] = jnp.zeros((MOE_ROWS, o_ref.shape[1]), o_ref.dtype)


def _experts(blk_e, next_e, n_used, buf, w_gate, w_up, w_down):
    n_rows, w = buf.shape
    d = w_down.shape[2]
    hbm = pl.BlockSpec(memory_space=pl.ANY)
    weights = (w_gate, w_up, w_down)
    step_rows = MOE_ROWS * MOE_BLOCKS_PER_STEP
    grid_spec = pltpu.PrefetchScalarGridSpec(
        num_scalar_prefetch=3,
        grid=(n_rows // step_rows,),
        in_specs=[pl.BlockSpec((step_rows, w), lambda i, be, ne, nu: (i, 0)), hbm, hbm, hbm],
        out_specs=pl.BlockSpec((step_rows, d // 2), lambda i, be, ne, nu: (i, 0)),
        scratch_shapes=([pltpu.VMEM((2,) + a.shape[1:], F32) for a in weights]
                        + [pltpu.VMEM(a.shape[1:], BF16) for a in weights]
                        + [pltpu.SMEM((1,), jnp.int32), pltpu.SemaphoreType.DMA((2, len(weights)))]),
    )
    return pl.pallas_call(
        _experts_kernel,
        grid_spec=grid_spec,
        out_shape=jax.ShapeDtypeStruct((n_rows, d // 2), jnp.uint32),
        compiler_params=_cparams("arbitrary"),
        name="experts",
    )(blk_e, next_e, n_used, buf, w_gate, w_up, w_down)


def _gather_sc(d0, d1, eo):
    n = d0.size
    w = eo.shape[1]
    workers = SC_CORES * SC_SUBCORES
    n_win = n // (workers * SC_WINDOW)
    mesh = plsc.VectorSubcoreMesh(core_axis_name="c", subcore_axis_name="s")

    def body(eo_hbm, d0_hbm, d1_hbm, o0_hbm, o1_hbm, rows_v, i_v):
        wid = lax.axis_index("c") * SC_SUBCORES + lax.axis_index("s")

        @pl.loop(0, n_win)
        def _(j):
            blk = wid * n_win + j
            for d_hbm, o_hbm in ((d0_hbm, o0_hbm), (d1_hbm, o1_hbm)):
                pltpu.sync_copy(d_hbm.at[blk], i_v)
                pltpu.sync_copy(eo_hbm.at[i_v], rows_v)
                pltpu.sync_copy(rows_v, o_hbm.at[pl.ds(blk * SC_WINDOW, SC_WINDOW)])

    out_sd = jax.ShapeDtypeStruct((n, w), eo.dtype)
    return pl.kernel(
        body,
        out_type=(out_sd, out_sd),
        mesh=mesh,
        scratch_types=[pltpu.VMEM((SC_WINDOW, w), eo.dtype), pltpu.VMEM((SC_WINDOW,), jnp.int32)],
        name="gather_sc",
    )(eo, d0, d1)


def _combine_kernel(x1_ref, route_ref, r0_ref, r1_ref, *rest):
    o_ref = rest[-1]
    route = route_ref[...]
    row0 = jnp.concatenate(_unpack_bf16_pairs(r0_ref[...]), axis=1)
    row1 = jnp.concatenate(_unpack_bf16_pairs(r1_ref[...]), axis=1)
    o_ref[...] = x1_ref[...] + (route[:, 2:3] * row0 + route[:, 3:4] * row1)


def _combine(x1, route, rows0, rows1, part, out_prev):
    n_slice, d = x1.shape
    tm = COMBINE_TM
    steps = n_slice // tm
    row = lambda w: pl.BlockSpec((tm, w), lambda i: (i, 0))
    in_specs = [row(d), row(LANES), row(d // 2), row(d // 2)]
    args = [x1, route, rows0, rows1]
    aliases = {}
    if out_prev is not None:
        in_specs.append(pl.BlockSpec(memory_space=pl.ANY))
        args.append(out_prev)
        aliases = {len(args) - 1: 0}
    return pl.pallas_call(
        _combine_kernel,
        grid=(steps,),
        in_specs=in_specs,
        out_specs=pl.BlockSpec((tm, d), lambda i: (part * steps + i, 0)),
        out_shape=jax.ShapeDtypeStruct((n_slice * MOE_PARTS, d), F32),
        input_output_aliases=aliases,
        compiler_params=_cparams("parallel"),
        name="combine",
    )(*args)


def _layer(x, g_mix, w_in, lam_re, lam_im, log_dt, b_re, b_im, c_re, c_im, d_skip, w_glu, b_glu, g_q, g_k,
           g_ssm_out, g_attn_out, w_out, g_ffn, w_rg, b_rg, w_re, b_re_router, w_gate, w_up, w_down):
    bsz, t_len, d = x.shape
    n = bsz * t_len
    d_ssm = w_glu.shape[0]
    d_att = g_attn_out.shape[0]
    n_heads = d_att // HEAD_DIM
    n_chunks = t_len // SSM_CHUNK
    x2 = x.reshape(n, d)

    u_t, q, k, v = _in_proj(x, g_mix[None], w_in.astype(BF16), jnp.tile(g_q, n_heads)[None],
                            jnp.tile(g_k, n_heads)[None], d_ssm, d_att)
    tables = _s5_tables(lam_re, lam_im, log_dt, b_re, b_im, c_re, c_im, d_skip)
    y_ssm_t = _s5(u_t.reshape(n_chunks, SSM_CHUNK, bsz, d_ssm), tables).reshape(t_len, bsz * d_ssm)

    y_att = _attention(q.reshape(n, d_att), k.reshape(n, d_att), v.reshape(n, d_att), bsz, t_len)

    w_r = jnp.concatenate([w_rg, w_re.reshape(d, N_EXPERTS)], axis=1)
    w_r = jnp.pad(w_r, ((0, 0), (0, LANES - w_r.shape[1]))).astype(BF16)
    b_r = jnp.pad(jnp.concatenate([b_rg, b_re_router.reshape(N_EXPERTS)]), (0, LANES - ROUTER_LANE0 - N_EXPERTS))[None]
    w_glu_bf, w_out_bf = w_glu.astype(BF16), w_out.astype(BF16)
    n_slice = n // MOE_PARTS
    n_rows = n_slice * 2 + N_EXPERTS * MOE_ROWS
    n_blk = n_rows // MOE_ROWS
    out = None
    for part in range(MOE_PARTS):
        x1, h2p, route, route_t, cnt = _post(x2, y_ssm_t, y_att, w_glu_bf, b_glu[None], g_ssm_out[None],
                                             g_attn_out[None], w_out_bf, g_ffn[None], w_r, b_r, part)
        counts = cnt[0, ROUTER_LANE0:ROUTER_LANE0 + N_EXPERTS].astype(jnp.int32)
        pcounts = ((counts + MOE_ROWS - 1) // MOE_ROWS) * MOE_ROWS
        pends = jnp.cumsum(pcounts)
        pstarts = pends - pcounts
        e_ids = jnp.arange(N_EXPERTS, dtype=jnp.int32)
        dests = []
        for k in range(2):
            expert = route_t[k].astype(jnp.int32)
            start = jnp.sum(jnp.where(expert[:, None] == e_ids, pstarts, 0), axis=-1)
            dest = start + route_t[4 + k].astype(jnp.int32)
            dests.append(dest.reshape(n_slice // SC_WINDOW, SC_WINDOW))
        blk_row0 = jnp.arange(n_blk, dtype=jnp.int32)[:, None] * MOE_ROWS
        blk_e = jnp.minimum(jnp.sum((pends[None, :] <= blk_row0).astype(jnp.int32), axis=1), N_EXPERTS - 1)
        n_used = (pends[-1:] // MOE_ROWS).astype(jnp.int32)
        present = jnp.any(blk_e[:, None] == e_ids, axis=0)
        later = jnp.logical_and(present[None, :], e_ids[None, :] > blk_e[:, None])
        next_e = jnp.min(jnp.where(later, e_ids[None, :], N_EXPERTS), axis=1)
        next_e = jnp.where(next_e == N_EXPERTS, -1, next_e).astype(jnp.int32)

        buf = _dispatch_sc(*dests, h2p, n_rows)
        eo = _experts(blk_e, next_e, n_used, buf, w_gate, w_up, w_down)
        out = _combine(x1, route, *_gather_sc(*dests, eo), part, out)
    return out.reshape(bsz, t_len, d)


def kernel(x, g_mix, w_in, ssm_lambda_re, ssm_lambda_im, ssm_log_dt, ssm_b_re, ssm_b_im, ssm_c_re, ssm_c_im, ssm_d, ssm_w_glu, ssm_b_glu, g_q, g_k, g_ssm_out, g_attn_out, w_out, g_ffn, w_router_group, b_router_group, w_router_expert, b_router_expert, w_gate, w_up, w_down):
    for l in range(g_mix.shape[0]):
        x = _layer(x, g_mix[l], w_in[l], ssm_lambda_re[l], ssm_lambda_im[l], ssm_log_dt[l], ssm_b_re[l],
                   ssm_b_im[l], ssm_c_re[l], ssm_c_im[l], ssm_d[l], ssm_w_glu[l], ssm_b_glu[l], g_q[l], g_k[l],
                   g_ssm_out[l], g_attn_out[l], w_out[l], g_ffn[l], w_router_group[l], b_router_group[l],
                   w_router_expert[l], b_router_expert[l], w_gate[l], w_up[l], w_down[l])
    return x
```

```python
import functools
import math

import jax
import jax.numpy as jnp
from jax import lax
from jax.experimental import pallas as pl
from jax.experimental.pallas import tpu as pltpu
from jax.experimental.pallas import tpu_sc as plsc

F32 = jnp.float32
BF16 = jnp.bfloat16
EPS = 1e-6

LANES = 128
VMEM_LIMIT_BYTES = 56 * 1024 * 1024

SSM_GROUP = 16
SSM_STATE = 64
SSM_CHUNK = 16
HEAD_DIM = 64
N_EXPERT_GROUPS = 4
EXPERTS_PER_GROUP = 8
N_EXPERTS = N_EXPERT_GROUPS * EXPERTS_PER_GROUP
ROUTER_LANE0 = N_EXPERT_GROUPS
ROUTE_FIELDS = 8
MOE_ROWS = 512
MOE_PARTS = 2
MOE_BLOCKS_PER_STEP = 4
ATT_SKIP = 110.0

S5_GPB = LANES // SSM_GROUP
S5_CHUNKS_PER_STEP = 8

IN_TM = 1024
IN_BATCHES = 8
ATT_TILE = 256
ATT_KEYS_AHEAD = 256
ATT_TILES_PER_STEP = 2
POST_TM = 1024
POST_ROW_GROUPS = 2
COMBINE_TM = 1024
SC_CORES = 2
SC_SUBCORES = 16
SC_WINDOW = 64


def _cparams(*sem):
    return pltpu.CompilerParams(dimension_semantics=sem, vmem_limit_bytes=VMEM_LIMIT_BYTES)


def _lane_iota(shape):
    return lax.broadcasted_iota(jnp.int32, shape, len(shape) - 1)


def _head_rms(t, gain):
    outs = []
    for c in range(t.shape[1] // LANES):
        blk = t[:, c * LANES:(c + 1) * LANES]
        sq = blk * blk
        lo = _lane_iota(blk.shape) < HEAD_DIM
        s_lo = jnp.sum(jnp.where(lo, sq, 0.0), axis=-1, keepdims=True)
        s_hi = jnp.sum(jnp.where(lo, 0.0, sq), axis=-1, keepdims=True)
        inv = jnp.where(lo, lax.rsqrt(s_lo * (1.0 / HEAD_DIM) + EPS),
                        lax.rsqrt(s_hi * (1.0 / HEAD_DIM) + EPS))
        outs.append(blk * inv * gain[:, c * LANES:(c + 1) * LANES])
    return jnp.concatenate(outs, axis=-1)


def _in_proj_kernel(x_ref, g_ref, w_ref, gq_ref, gk_ref, u_ref, q_ref, k_ref, v_ref, *, d_ssm, d_att, scale):
    nb, tt, d = x_ref.shape
    x = x_ref[...].reshape(nb * tt, d)
    inv = lax.rsqrt(jnp.mean(x * x, axis=-1, keepdims=True) + EPS)
    h = (x * inv * g_ref[...]).astype(BF16)
    proj = jnp.dot(h, w_ref[...], preferred_element_type=F32)
    u_ref[...] = pltpu.einshape("btc->tbc", proj[:, :d_ssm].astype(BF16).reshape(nb, tt, d_ssm))
    q = _head_rms(proj[:, d_ssm:d_ssm + d_att], gq_ref[...])
    k = _head_rms(proj[:, d_ssm + d_att:d_ssm + 2 * d_att], gk_ref[...])
    q_ref[...] = (q * scale).astype(BF16).reshape(nb, tt, d_att)
    k_ref[...] = k.astype(BF16).reshape(nb, tt, d_att)
    v_ref[...] = proj[:, d_ssm + 2 * d_att:].astype(BF16).reshape(nb, tt, d_att)


def _in_proj(x, g_mix, w_in_bf, gq_t, gk_t, d_ssm, d_att):
    bsz, t_len, d = x.shape
    nb = IN_BATCHES
    tt = IN_TM // nb
    tok = lambda w: pl.BlockSpec((nb, tt, w), lambda b, t: (b, t, 0))
    full = lambda a: pl.BlockSpec(a.shape, lambda b, t: (0,) * a.ndim)
    out_sd = jax.ShapeDtypeStruct((bsz, t_len, d_att), BF16)
    return pl.pallas_call(
        functools.partial(_in_proj_kernel, d_ssm=d_ssm, d_att=d_att, scale=1.0 / math.sqrt(HEAD_DIM)),
        grid=(bsz // nb, t_len // tt),
        in_specs=[tok(d), full(g_mix), full(w_in_bf), full(gq_t), full(gk_t)],
        out_specs=[pl.BlockSpec((tt, nb, d_ssm), lambda b, t: (t, b, 0)), tok(d_att), tok(d_att), tok(d_att)],
        out_shape=[jax.ShapeDtypeStruct((t_len, bsz, d_ssm), BF16), out_sd, out_sd, out_sd],
        compiler_params=_cparams("parallel", "parallel"),
        name="in_proj",
    )(x, g_mix, w_in_bf, gq_t, gk_t)


def _s5_tables(lam_re, lam_im, log_dt, b_re, b_im, c_re, c_im, d_skip):
    hp = lax.Precision.HIGHEST
    L = SSM_CHUNK
    g_n, p_n = lam_re.shape
    dt = jnp.exp(log_dt)[:, None]
    lr, li = lam_re, lam_im
    ls = jnp.arange(L + 1, dtype=F32)[:, None, None]
    mag = jnp.exp(lr * dt * ls)
    pr, pi = mag * jnp.cos(li * dt * ls), mag * jnp.sin(li * dt * ls)
    abar_r, abar_i = pr[1], pi[1]
    den = lr * lr + li * li
    nr, ni = abar_r - 1.0, abar_i
    coef_r = (nr * lr + ni * li) / den
    coef_i = (ni * lr - nr * li) / den
    bbr = coef_r[..., None] * b_re - coef_i[..., None] * b_im
    bbi = coef_r[..., None] * b_im + coef_i[..., None] * b_re
    wr = pr[..., None] * bbr - pi[..., None] * bbi
    wi = pr[..., None] * bbi + pi[..., None] * bbr
    kl = (jnp.einsum('gop,lgpi->lgoi', c_re, wr[:L], precision=hp)
          - jnp.einsum('gop,lgpi->lgoi', c_im, wi[:L], precision=hp))
    kl = kl.at[0].add(jax.vmap(jnp.diag)(d_skip))
    n_lb = g_n // S5_GPB
    sg = SSM_GROUP
    kc = kl.transpose(1, 3, 0, 2).reshape(n_lb, S5_GPB * sg, L * sg)
    b_rows = lambda w: (w[:L][::-1].reshape(L, n_lb, S5_GPB, p_n, sg).transpose(1, 0, 2, 4, 3)
                        .reshape(n_lb, L * S5_GPB * sg, p_n))
    p1r, p1i = pr[1:], pi[1:]
    cst_r = (c_re[None] * p1r[:, :, None, :] - c_im[None] * p1i[:, :, None, :])
    cst_i = -(c_re[None] * p1i[:, :, None, :] + c_im[None] * p1r[:, :, None, :])
    c_rows = lambda c: c.transpose(1, 3, 0, 2).reshape(n_lb, S5_GPB * p_n, L * sg)
    a_l = jnp.stack([pr[L].reshape(n_lb, S5_GPB * p_n), pi[L].reshape(n_lb, S5_GPB * p_n)], axis=1)
    return (kc.astype(BF16), b_rows(wr).astype(BF16), b_rows(wi).astype(BF16),
            c_rows(cst_r).astype(BF16), c_rows(cst_i).astype(BF16), a_l)


def _s5_expand_tables(kc_ref, bcr_ref, bci_ref, ccr_ref, cci_ref, w0_ref, br_ref, bi_ref, cr_ref, ci_ref):
    sg, gpb = SSM_GROUP, S5_GPB
    n_lo = w0_ref.shape[1]
    p_n = bcr_ref.shape[2]

    def iotas(shape):
        return lax.broadcasted_iota(jnp.int32, shape, 0), lax.broadcasted_iota(jnp.int32, shape, 1)

    r, c = iotas((kc_ref.shape[2], n_lo))
    spread_o = jnp.where(jnp.logical_and(r // sg == c // LANES, r % sg == c % sg), 1.0, 0.0).astype(BF16)
    r, c = iotas((p_n, gpb * p_n))
    spread_p = jnp.where(r == c % p_n, 1.0, 0.0).astype(BF16)

    def expand(compact, spread, row_group, col_group):
        full = jnp.dot(compact, spread, preferred_element_type=F32)
        r, c = iotas(full.shape)
        return jnp.where(row_group(r) == col_group(c), full, 0.0).astype(BF16)

    lane_group = lambda c: (c % LANES) // sg
    top = expand(kc_ref[0], spread_o, lambda r: r // sg, lane_group)
    w0_ref[:LANES, :] = top
    w0_ref[LANES:, :LANES] = jnp.zeros((LANES, LANES), BF16)
    w0_ref[LANES:, LANES:] = top[:, :n_lo - LANES]
    state_group = lambda c: c // p_n
    br_ref[...] = expand(bcr_ref[0], spread_p, lambda r: (r // sg) % gpb, state_group)
    bi_ref[...] = expand(bci_ref[0], spread_p, lambda r: (r // sg) % gpb, state_group)
    cr_ref[...] = expand(ccr_ref[0], spread_o, lambda r: r // p_n, lane_group)
    ci_ref[...] = expand(cci_ref[0], spread_o, lambda r: r // p_n, lane_group)


def _s5_kernel(u_ref, kc_ref, bcr_ref, bci_ref, ccr_ref, cci_ref, a_ref, y_ref,
               hr_ref, hi_ref, acc_ref, w0_ref, br_ref, bi_ref, cr_ref, ci_ref):
    n_chunks, L, bsz, _ = u_ref.shape
    rows = n_chunks * bsz

    @pl.when(pl.program_id(1) == 0)
    def _():
        hr_ref[...] = jnp.zeros_like(hr_ref)
        hi_ref[...] = jnp.zeros_like(hi_ref)
        _s5_expand_tables(kc_ref, bcr_ref, bci_ref, ccr_ref, cci_ref, w0_ref, br_ref, bi_ref, cr_ref, ci_ref)

    us = [u_ref[:, s].reshape(rows, LANES) for s in range(L)]
    lhs = jnp.concatenate(us, axis=1)
    sin_r = jnp.dot(lhs, br_ref[...], preferred_element_type=F32)
    sin_i = jnp.dot(lhs, bi_ref[...], preferred_element_type=F32)
    ar = a_ref[0, 0:1, :]
    ai = a_ref[0, 1:2, :]
    hr, hi = hr_ref[...], hi_ref[...]
    prev_r, prev_i = [], []
    for c in range(n_chunks):
        prev_r.append(hr)
        prev_i.append(hi)
        sl = slice(c * bsz, (c + 1) * bsz)
        hr, hi = ar * hr - ai * hi + sin_r[sl], ar * hi + ai * hr + sin_i[sl]
    hr_ref[...] = hr
    hi_ref[...] = hi
    pr = jnp.concatenate(prev_r, axis=0).astype(BF16)
    pi = jnp.concatenate(prev_i, axis=0).astype(BF16)
    acc_ref[...] = (jnp.dot(pr, cr_ref[...], preferred_element_type=F32)
                    + jnp.dot(pi, ci_ref[...], preferred_element_type=F32))
    for p in range(L // 2):
        off = 2 * p * LANES
        pair = jnp.concatenate([us[2 * p], us[2 * p + 1]], axis=1)
        acc_ref[:, off:] += jnp.dot(pair, w0_ref[:, :L * LANES - off], preferred_element_type=F32)
    for t in range(L):
        y_ref[:, t] = acc_ref[:, t * LANES:(t + 1) * LANES].reshape(n_chunks, bsz, LANES)


def _s5(u4, tables):
    n_chunks, L, bsz, d_ssm = u4.shape
    a_l = tables[-1]
    cb = S5_CHUNKS_PER_STEP
    data = pl.BlockSpec((cb, L, bsz, LANES), lambda lb, c: (c, 0, 0, lb))
    per_lb = lambda a: pl.BlockSpec((1,) + a.shape[1:], lambda lb, c: (lb,) + (0,) * (a.ndim - 1))
    n_state = a_l.shape[2]
    state = pltpu.VMEM((bsz, n_state), F32)
    n_lo = L * LANES
    return pl.pallas_call(
        _s5_kernel,
        grid=(d_ssm // LANES, n_chunks // cb),
        in_specs=[data] + [per_lb(t) for t in tables],
        out_specs=data,
        out_shape=jax.ShapeDtypeStruct(u4.shape, F32),
        scratch_shapes=[state, state, pltpu.VMEM((cb * bsz, n_lo), F32),
                        pltpu.VMEM((2 * LANES, n_lo), BF16),
                        pltpu.VMEM((n_lo, n_state), BF16), pltpu.VMEM((n_lo, n_state), BF16),
                        pltpu.VMEM((n_state, n_lo), BF16), pltpu.VMEM((n_state, n_lo), BF16)],
        compiler_params=_cparams("arbitrary", "arbitrary"),
        name="s5",
    )(u4, *tables)


def _softplus(z):
    return jnp.maximum(z, 0.0) + jnp.log(1.0 + jnp.exp(-jnp.abs(z)))


def _att_tiles(q2s, kvss, tri, r_ins, causal_first):
    tq = q2s[0].shape[0] // 2
    n_t = len(q2s)
    contract_last = (((1,), (1,)), ((), ()))
    zs = [[lax.dot_general(q2, k, contract_last, preferred_element_type=F32) for k, _ in kvs]
          for q2, kvs in zip(q2s, kvss)]
    sps = [[_softplus(z) for z in row] for row in zs]
    sp_ms = [list(row) for row in sps]
    if causal_first:
        rows = lax.broadcasted_iota(jnp.int32, zs[0][0].shape, 0)
        cols = lax.broadcasted_iota(jnp.int32, zs[0][0].shape, 1)
        keep = cols < jnp.where(rows >= tq, rows - tq, rows)
        for t in range(n_t):
            sp_ms[t][0] = jnp.where(keep, sps[t][0], 0.0)
    stacked = jnp.concatenate([blk for row in sp_ms for blk in row], axis=0).astype(BF16)
    newer_all = jnp.dot(stacked, tri, preferred_element_type=F32)
    pvs, totals = [None] * n_t, list(r_ins)
    base = 0
    offsets = []
    for row in sp_ms:
        offsets.append(base)
        base += len(row) * 2 * tq
    for p in range(max(len(kvs) for kvs in kvss)):
        for t in range(n_t):
            if p >= len(kvss[t]):
                continue
            newer = newer_all[offsets[t] + p * 2 * tq:offsets[t] + (p + 1) * 2 * tq]
            att = jnp.exp(zs[t][p] - sps[t][p] - newer - totals[t])
            if causal_first and p == 0:
                att = jnp.where(keep, att, 0.0)
            part = jnp.dot(att.astype(BF16), kvss[t][p][1], preferred_element_type=F32)
            pvs[t] = part if pvs[t] is None else pvs[t] + part
            totals[t] = totals[t] + (newer[:, 0:1] + sp_ms[t][p][:, 0:1])
    return [(pv, total - r_in) for pv, total, r_in in zip(pvs, totals, r_ins)]


def _attn_kernel(q_ref, k_ref, v_ref, o_ref, *, t_len, tile):
    nq = t_len // tile
    r_i = lax.broadcasted_iota(jnp.int32, (tile, tile), 0)
    c_i = lax.broadcasted_iota(jnp.int32, (tile, tile), 1)
    tri = jnp.where(r_i > c_i, 1.0, 0.0).astype(BF16)
    head0 = _lane_iota((tile, LANES)) < HEAD_DIM
    zero_r = jnp.zeros((2 * tile, 1), F32)

    def near(q0s, n_prev):
        q2s, kvss = [], []
        for q0 in q0s:
            q = q_ref[pl.ds(q0, tile), :]
            zq = jnp.zeros_like(q)
            q2s.append(jnp.concatenate([jnp.where(head0, q, zq), jnp.where(head0, zq, q)], axis=0))
            kvss.append([(k_ref[pl.ds(q0 - p * tile, tile), :], v_ref[pl.ds(q0 - p * tile, tile), :])
                         for p in range(n_prev + 1)])
        outs = _att_tiles(q2s, kvss, tri, [zero_r] * len(q0s), causal_first=True)
        return [(q2, acc, r) for q2, (acc, r) in zip(q2s, outs)]

    def far_and_store(q0, q2, acc, r, j_older):
        if j_older is not None:
            def cond(c):
                j, _, r_c = c
                return jnp.logical_and(j >= 0, jnp.min(r_c) < ATT_SKIP)

            def body(c):
                j, a_c, r_c = c
                k0 = pl.multiple_of(j * tile, tile)
                [(pv, dr)] = _att_tiles([q2], [[(k_ref[pl.ds(k0, tile), :], v_ref[pl.ds(k0, tile), :])]], tri,
                                        [r_c], causal_first=False)
                return j - 1, a_c + pv, r_c + dr

            _, acc, _ = lax.while_loop(cond, body, (j_older, acc, r))
        o_ref[pl.ds(q0, tile), :] = jnp.where(head0, acc[:tile], acc[tile:])

    n_prev = ATT_KEYS_AHEAD // tile
    group = ATT_TILES_PER_STEP
    first = n_prev + (nq - n_prev) % group
    for i in range(first):
        [part] = near([i * tile], min(i, n_prev))
        far_and_store(i * tile, *part, None if i <= n_prev else i - n_prev - 1)

    def later(g, _):
        tiles = [first + g * group + t for t in range(group)]
        q0s = [pl.multiple_of(i * tile, tile) for i in tiles]
        for i, q0, part in zip(tiles, q0s, near(q0s, n_prev)):
            far_and_store(q0, *part, i - n_prev - 1)
        return 0

    lax.fori_loop(0, (nq - first) // group, later, 0)


def _attention(q, k, v, bsz, t_len):
    n, d_att = q.shape
    spec = pl.BlockSpec((t_len, LANES), lambda b, p: (b, p))
    return pl.pallas_call(
        functools.partial(_attn_kernel, t_len=t_len, tile=ATT_TILE),
        grid=(bsz, d_att // LANES),
        in_specs=[spec, spec, spec],
        out_specs=spec,
        out_shape=jax.ShapeDtypeStruct((n, d_att), F32),
        compiler_params=_cparams("parallel", "parallel"),
        name="attn",
    )(q, k, v)


def _rms(t, gain):
    return t * lax.rsqrt(jnp.mean(t * t, axis=-1, keepdims=True) + EPS) * gain


def _gelu_tanh(y):
    return 0.5 * y * (1.0 + jnp.tanh(math.sqrt(2.0 / math.pi) * (y + 0.044715 * (y * y * y))))


def _pack_bf16_pairs(a, b):
    ua = pltpu.bitcast(a.astype(BF16).astype(F32), jnp.uint32)
    ub = pltpu.bitcast(b.astype(BF16).astype(F32), jnp.uint32)
    return ua | (ub >> 16)


def _unpack_bf16_pairs(w):
    return pltpu.bitcast(w & jnp.uint32(0xFFFF0000), F32), pltpu.bitcast(w << 16, F32)


def _post_kernel(x_ref, ys_ref, ya_ref, wglu_ref, bglu_ref, gs_ref, ga_ref, wo_ref, gf_ref,
                 wr_ref, br_ref, x1_ref, h2_ref, route_ref, route_t_ref, cnt_ref, run_ref, *, d_ssm):
    i = pl.program_id(0)

    @pl.when(i == 0)
    def _():
        run_ref[...] = jnp.zeros_like(run_ref)

    tm = x_ref.shape[0]
    groups = [pl.ds(g * (tm // POST_ROW_GROUPS), tm // POST_ROW_GROUPS) for g in range(POST_ROW_GROUPS)]
    ys = [_gelu_tanh(ys_ref[rows, :]) for rows in groups]
    gates = [jnp.dot(y.astype(BF16), wglu_ref[...], preferred_element_type=F32) + bglu_ref[...] for y in ys]
    m_as = [_rms(ya_ref[rows, :], ga_ref[...]).astype(BF16) for rows in groups]
    m_ss = [_rms(y * jax.nn.sigmoid(gate), gs_ref[...]).astype(BF16) for y, gate in zip(ys, gates)]
    mixes = [jnp.dot(m_s, wo_ref[:d_ssm, :], preferred_element_type=F32)
             + jnp.dot(m_a, wo_ref[d_ssm:, :], preferred_element_type=F32) for m_s, m_a in zip(m_ss, m_as)]
    x1s = [x_ref[rows, :] + mix for rows, mix in zip(groups, mixes)]
    h2s = [_rms(x1, gf_ref[...]) for x1 in x1s]
    logits = [jnp.dot(h2.astype(BF16), wr_ref[...], preferred_element_type=F32) for h2 in h2s]
    half = x_ref.shape[1] // 2
    for rows, x1, h2 in zip(groups, x1s, h2s):
        x1_ref[rows, :] = x1
        h2_ref[rows, :] = _pack_bf16_pairs(h2[:, :half], h2[:, half:])
    lg = jnp.concatenate(logits, axis=0) + br_ref[...]
    lane = _lane_iota(lg.shape).astype(F32)
    neg = -jnp.inf
    first = lambda hit: jnp.min(jnp.where(hit, lane, float(LANES)), axis=-1, keepdims=True)
    glog = jnp.where(lane < N_EXPERT_GROUPS, lg, neg)
    gmax = jnp.max(glog, axis=-1, keepdims=True)
    p_grp = 1.0 / jnp.sum(jnp.exp(glog - gmax), axis=-1, keepdims=True)
    grp = first(glog == gmax)
    e0 = ROUTER_LANE0 + grp * EXPERTS_PER_GROUP
    elog = jnp.where(jnp.logical_and(lane >= e0, lane < e0 + EXPERTS_PER_GROUP), lg, neg)
    m1 = jnp.max(elog, axis=-1, keepdims=True)
    i1 = first(elog == m1)
    elog2 = jnp.where(lane == i1, neg, elog)
    m2 = jnp.max(elog2, axis=-1, keepdims=True)
    i2 = first(elog2 == m2)
    e21 = jnp.exp(m2 - m1)
    g1 = p_grp * (1.0 / (1.0 + e21))
    g2 = p_grp * (e21 / (1.0 + e21))

    sel1 = lane == i1
    sel2 = lane == i2
    onehot = jnp.where(jnp.logical_or(sel1, sel2), 1.0, 0.0)
    r_i = lax.broadcasted_iota(jnp.int32, (tm, tm), 0)
    c_i = lax.broadcasted_iota(jnp.int32, (tm, tm), 1)
    lower = jnp.where(c_i < r_i, 1.0, 0.0).astype(BF16)
    before = jnp.dot(lower, onehot.astype(BF16), preferred_element_type=F32) + run_ref[0:1, :]
    rank1 = jnp.sum(jnp.where(sel1, before, 0.0), axis=-1, keepdims=True)
    rank2 = jnp.sum(jnp.where(sel2, before, 0.0), axis=-1, keepdims=True)
    run_ref[0:1, :] = run_ref[0:1, :] + jnp.sum(onehot, axis=0, keepdims=True)
    cnt_ref[...] = run_ref[...]

    fields = (i1 - ROUTER_LANE0, i2 - ROUTER_LANE0, g1, g2, rank1, rank2)
    route = jnp.zeros(lg.shape, F32)
    for pos, val in enumerate(fields):
        route = jnp.where(lane == pos, val, route)
    route_ref[...] = route
    route_t_ref[...] = route.T[:ROUTE_FIELDS, :]


def _post(x2, y_ssm_t, y_att, w_glu_bf, b_glu, g_ssm, g_att, w_out_bf, g_ffn, w_r_bf, b_r, part):
    n, d = x2.shape
    d_ssm = w_glu_bf.shape[0]
    tm = POST_TM
    nt = y_ssm_t.shape[0] // tm
    steps = n // (tm * MOE_PARTS)
    i0 = part * steps
    row_in = lambda w: pl.BlockSpec((tm, w), lambda i: (i0 + i, 0))
    row = lambda w: pl.BlockSpec((tm, w), lambda i: (i, 0))
    ssm_spec = pl.BlockSpec((tm, d_ssm), lambda i: ((i0 + i) % nt, (i0 + i) // nt))
    full = lambda a: pl.BlockSpec(a.shape, lambda i: (0,) * a.ndim)
    cnt_spec = pl.BlockSpec((8, LANES), lambda i: (0, 0))
    n = n // MOE_PARTS
    return pl.pallas_call(
        functools.partial(_post_kernel, d_ssm=d_ssm),
        grid=(steps,),
        in_specs=[row_in(d), ssm_spec, row_in(y_att.shape[1]), full(w_glu_bf), full(b_glu), full(g_ssm),
                  full(g_att), full(w_out_bf), full(g_ffn), full(w_r_bf), full(b_r)],
        out_specs=[row(d), row(d // 2), row(LANES), pl.BlockSpec((ROUTE_FIELDS, tm), lambda i: (0, i)), cnt_spec],
        out_shape=[jax.ShapeDtypeStruct((n, d), F32), jax.ShapeDtypeStruct((n, d // 2), jnp.uint32),
                   jax.ShapeDtypeStruct((n, LANES), F32), jax.ShapeDtypeStruct((ROUTE_FIELDS, n), F32),
                   jax.ShapeDtypeStruct((8, LANES), F32)],
        scratch_shapes=[pltpu.VMEM((8, LANES), F32)],
        compiler_params=_cparams("arbitrary"),
        name="post",
    )(x2, y_ssm_t, y_att, w_glu_bf, b_glu, g_ssm, g_att, w_out_bf, g_ffn, w_r_bf, b_r)


def _dispatch_sc(d0, d1, h2p, n_rows):
    n, w = h2p.shape
    workers = SC_CORES * SC_SUBCORES
    n_win = n // (workers * SC_WINDOW)
    mesh = plsc.VectorSubcoreMesh(core_axis_name="c", subcore_axis_name="s")

    def body(h_hbm, d0_hbm, d1_hbm, o_hbm, rows_v, i0_v, i1_v):
        wid = lax.axis_index("c") * SC_SUBCORES + lax.axis_index("s")

        @pl.loop(0, n_win)
        def _(j):
            blk = wid * n_win + j
            pltpu.sync_copy(h_hbm.at[pl.ds(blk * SC_WINDOW, SC_WINDOW)], rows_v)
            pltpu.sync_copy(d0_hbm.at[blk], i0_v)
            pltpu.sync_copy(d1_hbm.at[blk], i1_v)
            pltpu.sync_copy(rows_v, o_hbm.at[i0_v])
            pltpu.sync_copy(rows_v, o_hbm.at[i1_v])

    return pl.kernel(
        body,
        out_type=jax.ShapeDtypeStruct((n_rows, w), h2p.dtype),
        mesh=mesh,
        scratch_types=[pltpu.VMEM((SC_WINDOW, w), h2p.dtype), pltpu.VMEM((SC_WINDOW,), jnp.int32),
                       pltpu.VMEM((SC_WINDOW,), jnp.int32)],
        name="dispatch_sc",
    )(h2p, d0, d1)


def _experts_kernel(blk_e_ref, next_e_ref, used_ref, x_ref, wg_hbm, wu_hbm, wd_hbm, o_ref,
                    wg_f32, wu_f32, wd_f32, wg_bf, wu_bf, wd_bf, slot_ref, sems):
    staged = ((wg_hbm, wg_f32, wg_bf), (wu_hbm, wu_f32, wu_bf), (wd_hbm, wd_f32, wd_bf))

    def fetch(expert, slot):
        return [pltpu.make_async_copy(hbm.at[expert], f32.at[slot], sems.at[slot, k])
                for k, (hbm, f32, _) in enumerate(staged)]

    @pl.when(pl.program_id(0) == 0)
    def _():
        slot_ref[0] = 0
        for cp in fetch(blk_e_ref[0], 0):
            cp.start()

    for s in range(MOE_BLOCKS_PER_STEP):
        i = pl.program_id(0) * MOE_BLOCKS_PER_STEP + s
        rows = pl.ds(s * MOE_ROWS, MOE_ROWS)

        @pl.when(jnp.logical_or(i == 0, blk_e_ref[i] != blk_e_ref[jnp.maximum(i - 1, 0)]))
        def _():
            slot = slot_ref[0]
            for cp in fetch(blk_e_ref[i], slot):
                cp.wait()
            for _, f32, bf in staged:
                bf[...] = f32[slot].astype(BF16)
            nxt = next_e_ref[i]

            @pl.when(nxt >= 0)
            def _():
                for cp in fetch(nxt, 1 - slot):
                    cp.start()

            slot_ref[0] = 1 - slot

        @pl.when(i < used_ref[0])
        def _():
            xa, xb = (t.astype(BF16) for t in _unpack_bf16_pairs(x_ref[rows, :]))
            half = xa.shape[1]
            gate = (jnp.dot(xa, wg_bf[:half, :], preferred_element_type=F32)
                    + jnp.dot(xb, wg_bf[half:, :], preferred_element_type=F32))
            up = (jnp.dot(xa, wu_bf[:half, :], preferred_element_type=F32)
                  + jnp.dot(xb, wu_bf[half:, :], preferred_element_type=F32))
            hid = (jax.nn.silu(gate) * up).astype(BF16)
            out = jnp.dot(hid, wd_bf[...], preferred_element_type=F32)
            o_ref[rows, :] = _pack_bf16_pairs(out[:, :half], out[:, half:])

        @pl.when(i >= used_ref[0])
        def _():
            o_ref[rows, :] = jnp.zeros((MOE_ROWS, o_ref.shape[1]), o_ref.dtype)


def _experts(blk_e, next_e, n_used, buf, w_gate, w_up, w_down):
    n_rows, w = buf.shape
    d = w_down.shape[2]
    hbm = pl.BlockSpec(memory_space=pl.ANY)
    weights = (w_gate, w_up, w_down)
    step_rows = MOE_ROWS * MOE_BLOCKS_PER_STEP
    grid_spec = pltpu.PrefetchScalarGridSpec(
        num_scalar_prefetch=3,
        grid=(n_rows // step_rows,),
        in_specs=[pl.BlockSpec((step_rows, w), lambda i, be, ne, nu: (i, 0)), hbm, hbm, hbm],
        out_specs=pl.BlockSpec((step_rows, d // 2), lambda i, be, ne, nu: (i, 0)),
        scratch_shapes=([pltpu.VMEM((2,) + a.shape[1:], F32) for a in weights]
                        + [pltpu.VMEM(a.shape[1:], BF16) for a in weights]
                        + [pltpu.SMEM((1,), jnp.int32), pltpu.SemaphoreType.DMA((2, len(weights)))]),
    )
    return pl.pallas_call(
        _experts_kernel,
        grid_spec=grid_spec,
        out_shape=jax.ShapeDtypeStruct((n_rows, d // 2), jnp.uint32),
        compiler_params=_cparams("arbitrary"),
        name="experts",
    )(blk_e, next_e, n_used, buf, w_gate, w_up, w_down)


def _gather_sc(d0, d1, eo):
    n = d0.size
    w = eo.shape[1]
    workers = SC_CORES * SC_SUBCORES
    n_win = n // (workers * SC_WINDOW)
    mesh = plsc.VectorSubcoreMesh(core_axis_name="c", subcore_axis_name="s")

    def body(eo_hbm, d0_hbm, d1_hbm, o0_hbm, o1_hbm, rows_v, i_v):
        wid = lax.axis_index("c") * SC_SUBCORES + lax.axis_index("s")

        @pl.loop(0, n_win)
        def _(j):
            blk = wid * n_win + j
            for d_hbm, o_hbm in ((d0_hbm, o0_hbm), (d1_hbm, o1_hbm)):
                pltpu.sync_copy(d_hbm.at[blk], i_v)
                pltpu.sync_copy(eo_hbm.at[i_v], rows_v)
                pltpu.sync_copy(rows_v, o_hbm.at[pl.ds(blk * SC_WINDOW, SC_WINDOW)])

    out_sd = jax.ShapeDtypeStruct((n, w), eo.dtype)
    return pl.kernel(
        body,
        out_type=(out_sd, out_sd),
        mesh=mesh,
        scratch_types=[pltpu.VMEM((SC_WINDOW, w), eo.dtype), pltpu.VMEM((SC_WINDOW,), jnp.int32)],
        name="gather_sc",
    )(eo, d0, d1)


def _combine_kernel(x1_ref, route_ref, r0_ref, r1_ref, *rest):
    o_ref = rest[-1]
    route = route_ref[...]
    row0 = jnp.concatenate(_unpack_bf16_pairs(r0_ref[...]), axis=1)
    row1 = jnp.concatenate(_unpack_bf16_pairs(r1_ref[...]), axis=1)
    o_ref[...] = x1_ref[...] + (route[:, 2:3] * row0 + route[:, 3:4] * row1)


def _combine(x1, route, rows0, rows1, part, out_prev):
    n_slice, d = x1.shape
    tm = COMBINE_TM
    steps = n_slice // tm
    row = lambda w: pl.BlockSpec((tm, w), lambda i: (i, 0))
    in_specs = [row(d), row(LANES), row(d // 2), row(d // 2)]
    args = [x1, route, rows0, rows1]
    aliases = {}
    if out_prev is not None:
        in_specs.append(pl.BlockSpec(memory_space=pl.ANY))
        args.append(out_prev)
        aliases = {len(args) - 1: 0}
    return pl.pallas_call(
        _combine_kernel,
        grid=(steps,),
        in_specs=in_specs,
        out_specs=pl.BlockSpec((tm, d), lambda i: (part * steps + i, 0)),
        out_shape=jax.ShapeDtypeStruct((n_slice * MOE_PARTS, d), F32),
        input_output_aliases=aliases,
        compiler_params=_cparams("parallel"),
        name="combine",
    )(*args)


def _layer(x, g_mix, w_in, lam_re, lam_im, log_dt, b_re, b_im, c_re, c_im, d_skip, w_glu, b_glu, g_q, g_k,
           g_ssm_out, g_attn_out, w_out, g_ffn, w_rg, b_rg, w_re, b_re_router, w_gate, w_up, w_down):
    bsz, t_len, d = x.shape
    n = bsz * t_len
    d_ssm = w_glu.shape[0]
    d_att = g_attn_out.shape[0]
    n_heads = d_att // HEAD_DIM
    n_chunks = t_len // SSM_CHUNK
    x2 = x.reshape(n, d)

    u_t, q, k, v = _in_proj(x, g_mix[None], w_in.astype(BF16), jnp.tile(g_q, n_heads)[None],
                            jnp.tile(g_k, n_heads)[None], d_ssm, d_att)
    tables = _s5_tables(lam_re, lam_im, log_dt, b_re, b_im, c_re, c_im, d_skip)
    y_ssm_t = _s5(u_t.reshape(n_chunks, SSM_CHUNK, bsz, d_ssm), tables).reshape(t_len, bsz * d_ssm)

    y_att = _attention(q.reshape(n, d_att), k.reshape(n, d_att), v.reshape(n, d_att), bsz, t_len)

    w_r = jnp.concatenate([w_rg, w_re.reshape(d, N_EXPERTS)], axis=1)
    w_r = jnp.pad(w_r, ((0, 0), (0, LANES - w_r.shape[1]))).astype(BF16)
    b_r = jnp.pad(jnp.concatenate([b_rg, b_re_router.reshape(N_EXPERTS)]), (0, LANES - ROUTER_LANE0 - N_EXPERTS))[None]
    w_glu_bf, w_out_bf = w_glu.astype(BF16), w_out.astype(BF16)
    n_slice = n // MOE_PARTS
    n_rows = n_slice * 2 + N_EXPERTS * MOE_ROWS
    n_blk = n_rows // MOE_ROWS
    out = None
    for part in range(MOE_PARTS):
        x1, h2p, route, route_t, cnt = _post(x2, y_ssm_t, y_att, w_glu_bf, b_glu[None], g_ssm_out[None],
                                             g_attn_out[None], w_out_bf, g_ffn[None], w_r, b_r, part)
        counts = cnt[0, ROUTER_LANE0:ROUTER_LANE0 + N_EXPERTS].astype(jnp.int32)
        pcounts = ((counts + MOE_ROWS - 1) // MOE_ROWS) * MOE_ROWS
        pends = jnp.cumsum(pcounts)
        pstarts = pends - pcounts
        e_ids = jnp.arange(N_EXPERTS, dtype=jnp.int32)
        dests = []
        for k in range(2):
            expert = route_t[k].astype(jnp.int32)
            start = jnp.sum(jnp.where(expert[:, None] == e_ids, pstarts, 0), axis=-1)
            dest = start + route_t[4 + k].astype(jnp.int32)
            dests.append(dest.reshape(n_slice // SC_WINDOW, SC_WINDOW))
        blk_row0 = jnp.arange(n_blk, dtype=jnp.int32)[:, None] * MOE_ROWS
        blk_e = jnp.minimum(jnp.sum((pends[None, :] <= blk_row0).astype(jnp.int32), axis=1), N_EXPERTS - 1)
        n_used = (pends[-1:] // MOE_ROWS).astype(jnp.int32)
        present = jnp.any(blk_e[:, None] == e_ids, axis=0)
        later = jnp.logical_and(present[None, :], e_ids[None, :] > blk_e[:, None])
        next_e = jnp.min(jnp.where(later, e_ids[None, :], N_EXPERTS), axis=1)
        next_e = jnp.where(next_e == N_EXPERTS, -1, next_e).astype(jnp.int32)

        buf = _dispatch_sc(*dests, h2p, n_rows)
        eo = _experts(blk_e, next_e, n_used, buf, w_gate, w_up, w_down)
        out = _combine(x1, route, *_gather_sc(*dests, eo), part, out)
    return out.reshape(bsz, t_len, d)


def kernel(x, g_mix, w_in, ssm_lambda_re, ssm_lambda_im, ssm_log_dt, ssm_b_re, ssm_b_im, ssm_c_re, ssm_c_im, ssm_d, ssm_w_glu, ssm_b_glu, g_q, g_k, g_ssm_out, g_attn_out, w_out, g_ffn, w_router_group, b_router_group, w_router_expert, b_router_expert, w_gate, w_up, w_down):
    for l in range(g_mix.shape[0]):
        x = _layer(x, g_mix[l], w_in[l], ssm_lambda_re[l], ssm_lambda_im[l], ssm_log_dt[l], ssm_b_re[l],
                   ssm_b_im[l], ssm_c_re[l], ssm_c_im[l], ssm_d[l], ssm_w_glu[l], ssm_b_glu[l], g_q[l], g_k[l],
                   g_ssm_out[l], g_attn_out[l], w_out[l], g_ffn[l], w_router_group[l], b_router_group[l],
                   w_router_expert[l], b_router_expert[l], w_gate[l], w_up[l], w_down[l])
    return x
```

```python
import functools
import math

import jax
import jax.numpy as jnp
from jax import lax
from jax.experimental import pallas as pl
from jax.experimental.pallas import tpu as pltpu
from jax.experimental.pallas import tpu_sc as plsc

F32 = jnp.float32
BF16 = jnp.bfloat16
EPS = 1e-6

LANES = 128
VMEM_LIMIT_BYTES = 56 * 1024 * 1024

SSM_GROUP = 16
SSM_STATE = 64
SSM_CHUNK = 16
HEAD_DIM = 64
N_EXPERT_GROUPS = 4
EXPERTS_PER_GROUP = 8
N_EXPERTS = N_EXPERT_GROUPS * EXPERTS_PER_GROUP
ROUTER_LANE0 = N_EXPERT_GROUPS
ROUTE_FIELDS = 8
MOE_ROWS = 512
MOE_PARTS = 2
MOE_BLOCKS_PER_STEP = 4
ATT_SKIP = 110.0

S5_GPB = LANES // SSM_GROUP
S5_CHUNKS_PER_STEP = 16

IN_TM = 1024
IN_BATCHES = 8
ATT_TILE = 256
ATT_KEYS_AHEAD = 256
ATT_TILES_PER_STEP = 2
POST_TM = 1024
POST_ROW_GROUPS = 2
COMBINE_TM = 1024
SC_CORES = 2
SC_SUBCORES = 16
SC_WINDOW = 64


def _cparams(*sem):
    return pltpu.CompilerParams(dimension_semantics=sem, vmem_limit_bytes=VMEM_LIMIT_BYTES)


def _lane_iota(shape):
    return lax.broadcasted_iota(jnp.int32, shape, len(shape) - 1)


def _head_rms(t, gain):
    outs = []
    for c in range(t.shape[1] // LANES):
        blk = t[:, c * LANES:(c + 1) * LANES]
        sq = blk * blk
        lo = _lane_iota(blk.shape) < HEAD_DIM
        s_lo = jnp.sum(jnp.where(lo, sq, 0.0), axis=-1, keepdims=True)
        s_hi = jnp.sum(jnp.where(lo, 0.0, sq), axis=-1, keepdims=True)
        inv = jnp.where(lo, lax.rsqrt(s_lo * (1.0 / HEAD_DIM) + EPS),
                        lax.rsqrt(s_hi * (1.0 / HEAD_DIM) + EPS))
        outs.append(blk * inv * gain[:, c * LANES:(c + 1) * LANES])
    return jnp.concatenate(outs, axis=-1)


def _in_proj_kernel(x_ref, g_ref, w_ref, gq_ref, gk_ref, u_ref, q_ref, k_ref, v_ref, *, d_ssm, d_att, scale):
    nb, tt, d = x_ref.shape
    x = x_ref[...].reshape(nb * tt, d)
    inv = lax.rsqrt(jnp.mean(x * x, axis=-1, keepdims=True) + EPS)
    h = (x * inv * g_ref[...]).astype(BF16)
    proj = jnp.dot(h, w_ref[...], preferred_element_type=F32)
    u_ref[...] = pltpu.einshape("btc->tbc", proj[:, :d_ssm].astype(BF16).reshape(nb, tt, d_ssm))
    q = _head_rms(proj[:, d_ssm:d_ssm + d_att], gq_ref[...])
    k = _head_rms(proj[:, d_ssm + d_att:d_ssm + 2 * d_att], gk_ref[...])
    q_ref[...] = (q * scale).astype(BF16).reshape(nb, tt, d_att)
    k_ref[...] = k.astype(BF16).reshape(nb, tt, d_att)
    v_ref[...] = proj[:, d_ssm + 2 * d_att:].astype(BF16).reshape(nb, tt, d_att)


def _in_proj(x, g_mix, w_in_bf, gq_t, gk_t, d_ssm, d_att):
    bsz, t_len, d = x.shape
    nb = IN_BATCHES
    tt = IN_TM // nb
    tok = lambda w: pl.BlockSpec((nb, tt, w), lambda b, t: (b, t, 0))
    full = lambda a: pl.BlockSpec(a.shape, lambda b, t: (0,) * a.ndim)
    out_sd = jax.ShapeDtypeStruct((bsz, t_len, d_att), BF16)
    return pl.pallas_call(
        functools.partial(_in_proj_kernel, d_ssm=d_ssm, d_att=d_att, scale=1.0 / math.sqrt(HEAD_DIM)),
        grid=(bsz // nb, t_len // tt),
        in_specs=[tok(d), full(g_mix), full(w_in_bf), full(gq_t), full(gk_t)],
        out_specs=[pl.BlockSpec((tt, nb, d_ssm), lambda b, t: (t, b, 0)), tok(d_att), tok(d_att), tok(d_att)],
        out_shape=[jax.ShapeDtypeStruct((t_len, bsz, d_ssm), BF16), out_sd, out_sd, out_sd],
        compiler_params=_cparams("parallel", "parallel"),
        name="in_proj",
    )(x, g_mix, w_in_bf, gq_t, gk_t)


def _s5_tables(lam_re, lam_im, log_dt, b_re, b_im, c_re, c_im, d_skip):
    hp = lax.Precision.HIGHEST
    L = SSM_CHUNK
    g_n, p_n = lam_re.shape
    dt = jnp.exp(log_dt)[:, None]
    lr, li = lam_re, lam_im
    ls = jnp.arange(L + 1, dtype=F32)[:, None, None]
    mag = jnp.exp(lr * dt * ls)
    pr, pi = mag * jnp.cos(li * dt * ls), mag * jnp.sin(li * dt * ls)
    abar_r, abar_i = pr[1], pi[1]
    den = lr * lr + li * li
    nr, ni = abar_r - 1.0, abar_i
    coef_r = (nr * lr + ni * li) / den
    coef_i = (ni * lr - nr * li) / den
    bbr = coef_r[..., None] * b_re - coef_i[..., None] * b_im
    bbi = coef_r[..., None] * b_im + coef_i[..., None] * b_re
    wr = pr[..., None] * bbr - pi[..., None] * bbi
    wi = pr[..., None] * bbi + pi[..., None] * bbr
    kl = (jnp.einsum('gop,lgpi->lgoi', c_re, wr[:L], precision=hp)
          - jnp.einsum('gop,lgpi->lgoi', c_im, wi[:L], precision=hp))
    kl = kl.at[0].add(jax.vmap(jnp.diag)(d_skip))
    n_lb = g_n // S5_GPB
    sg = SSM_GROUP
    kc = kl.transpose(1, 3, 0, 2).reshape(n_lb, S5_GPB * sg, L * sg)
    b_rows = lambda w: (w[:L][::-1].reshape(L, n_lb, S5_GPB, p_n, sg).transpose(1, 0, 2, 4, 3)
                        .reshape(n_lb, L * S5_GPB * sg, p_n))
    p1r, p1i = pr[1:], pi[1:]
    cst_r = (c_re[None] * p1r[:, :, None, :] - c_im[None] * p1i[:, :, None, :])
    cst_i = -(c_re[None] * p1i[:, :, None, :] + c_im[None] * p1r[:, :, None, :])
    c_rows = lambda c: c.transpose(1, 3, 0, 2).reshape(n_lb, S5_GPB * p_n, L * sg)
    a_l = jnp.stack([pr[L].reshape(n_lb, S5_GPB * p_n), pi[L].reshape(n_lb, S5_GPB * p_n)], axis=1)
    return (kc.astype(BF16), b_rows(wr).astype(BF16), b_rows(wi).astype(BF16),
            c_rows(cst_r).astype(BF16), c_rows(cst_i).astype(BF16), a_l)


def _s5_expand_tables(kc_ref, bcr_ref, bci_ref, ccr_ref, cci_ref, w0_ref, br_ref, bi_ref, cr_ref, ci_ref):
    sg, gpb = SSM_GROUP, S5_GPB
    n_lo = w0_ref.shape[1]
    p_n = bcr_ref.shape[2]

    def iotas(shape):
        return lax.broadcasted_iota(jnp.int32, shape, 0), lax.broadcasted_iota(jnp.int32, shape, 1)

    r, c = iotas((kc_ref.shape[2], n_lo))
    spread_o = jnp.where(jnp.logical_and(r // sg == c // LANES, r % sg == c % sg), 1.0, 0.0).astype(BF16)
    r, c = iotas((p_n, gpb * p_n))
    spread_p = jnp.where(r == c % p_n, 1.0, 0.0).astype(BF16)

    def expand(compact, spread, row_group, col_group):
        full = jnp.dot(compact, spread, preferred_element_type=F32)
        r, c = iotas(full.shape)
        return jnp.where(row_group(r) == col_group(c), full, 0.0).astype(BF16)

    lane_group = lambda c: (c % LANES) // sg
    top = expand(kc_ref[0], spread_o, lambda r: r // sg, lane_group)
    w0_ref[:LANES, :] = top
    w0_ref[LANES:, :LANES] = jnp.zeros((LANES, LANES), BF16)
    w0_ref[LANES:, LANES:] = top[:, :n_lo - LANES]
    state_group = lambda c: c // p_n
    br_ref[...] = expand(bcr_ref[0], spread_p, lambda r: (r // sg) % gpb, state_group)
    bi_ref[...] = expand(bci_ref[0], spread_p, lambda r: (r // sg) % gpb, state_group)
    cr_ref[...] = expand(ccr_ref[0], spread_o, lambda r: r // p_n, lane_group)
    ci_ref[...] = expand(cci_ref[0], spread_o, lambda r: r // p_n, lane_group)


def _s5_kernel(u_ref, kc_ref, bcr_ref, bci_ref, ccr_ref, cci_ref, a_ref, y_ref,
               hr_ref, hi_ref, acc_ref, w0_ref, br_ref, bi_ref, cr_ref, ci_ref):
    n_chunks, L, bsz, _ = u_ref.shape
    rows = n_chunks * bsz

    @pl.when(pl.program_id(1) == 0)
    def _():
        hr_ref[...] = jnp.zeros_like(hr_ref)
        hi_ref[...] = jnp.zeros_like(hi_ref)
        _s5_expand_tables(kc_ref, bcr_ref, bci_ref, ccr_ref, cci_ref, w0_ref, br_ref, bi_ref, cr_ref, ci_ref)

    us = [u_ref[:, s].reshape(rows, LANES) for s in range(L)]
    lhs = jnp.concatenate(us, axis=1)
    sin_r = jnp.dot(lhs, br_ref[...], preferred_element_type=F32)
    sin_i = jnp.dot(lhs, bi_ref[...], preferred_element_type=F32)
    ar = a_ref[0, 0:1, :]
    ai = a_ref[0, 1:2, :]
    hr, hi = hr_ref[...], hi_ref[...]
    prev_r, prev_i = [], []
    for c in range(n_chunks):
        prev_r.append(hr)
        prev_i.append(hi)
        sl = slice(c * bsz, (c + 1) * bsz)
        hr, hi = ar * hr - ai * hi + sin_r[sl], ar * hi + ai * hr + sin_i[sl]
    hr_ref[...] = hr
    hi_ref[...] = hi
    pr = jnp.concatenate(prev_r, axis=0).astype(BF16)
    pi = jnp.concatenate(prev_i, axis=0).astype(BF16)
    acc_ref[...] = (jnp.dot(pr, cr_ref[...], preferred_element_type=F32)
                    + jnp.dot(pi, ci_ref[...], preferred_element_type=F32))
    for p in range(L // 2):
        off = 2 * p * LANES
        pair = jnp.concatenate([us[2 * p], us[2 * p + 1]], axis=1)
        acc_ref[:, off:] += jnp.dot(pair, w0_ref[:, :L * LANES - off], preferred_element_type=F32)
    for t in range(L):
        y_ref[:, t] = acc_ref[:, t * LANES:(t + 1) * LANES].astype(BF16).reshape(n_chunks, bsz, LANES)


def _s5(u4, tables):
    n_chunks, L, bsz, d_ssm = u4.shape
    a_l = tables[-1]
    cb = S5_CHUNKS_PER_STEP
    data = pl.BlockSpec((cb, L, bsz, LANES), lambda lb, c: (c, 0, 0, lb))
    per_lb = lambda a: pl.BlockSpec((1,) + a.shape[1:], lambda lb, c: (lb,) + (0,) * (a.ndim - 1))
    n_state = a_l.shape[2]
    state = pltpu.VMEM((bsz, n_state), F32)
    n_lo = L * LANES
    return pl.pallas_call(
        _s5_kernel,
        grid=(d_ssm // LANES, n_chunks // cb),
        in_specs=[data] + [per_lb(t) for t in tables],
        out_specs=data,
        out_shape=jax.ShapeDtypeStruct(u4.shape, BF16),
        scratch_shapes=[state, state, pltpu.VMEM((cb * bsz, n_lo), F32),
                        pltpu.VMEM((2 * LANES, n_lo), BF16),
                        pltpu.VMEM((n_lo, n_state), BF16), pltpu.VMEM((n_lo, n_state), BF16),
                        pltpu.VMEM((n_state, n_lo), BF16), pltpu.VMEM((n_state, n_lo), BF16)],
        compiler_params=_cparams("arbitrary", "arbitrary"),
        name="s5",
    )(u4, *tables)


def _softplus(z):
    return jnp.maximum(z, 0.0) + jnp.log(1.0 + jnp.exp(-jnp.abs(z)))


def _att_tiles(q2s, kvss, tri, r_ins, causal_first):
    tq = q2s[0].shape[0] // 2
    n_t = len(q2s)
    contract_last = (((1,), (1,)), ((), ()))
    zs = [[lax.dot_general(q2, k, contract_last, preferred_element_type=F32) for k, _ in kvs]
          for q2, kvs in zip(q2s, kvss)]
    sps = [[_softplus(z) for z in row] for row in zs]
    sp_ms = [list(row) for row in sps]
    if causal_first:
        rows = lax.broadcasted_iota(jnp.int32, zs[0][0].shape, 0)
        cols = lax.broadcasted_iota(jnp.int32, zs[0][0].shape, 1)
        keep = cols < jnp.where(rows >= tq, rows - tq, rows)
        for t in range(n_t):
            sp_ms[t][0] = jnp.where(keep, sps[t][0], 0.0)
    stacked = jnp.concatenate([blk for row in sp_ms for blk in row], axis=0).astype(BF16)
    newer_all = jnp.dot(stacked, tri, preferred_element_type=F32)
    pvs, totals = [None] * n_t, list(r_ins)
    base = 0
    offsets = []
    for row in sp_ms:
        offsets.append(base)
        base += len(row) * 2 * tq
    for p in range(max(len(kvs) for kvs in kvss)):
        for t in range(n_t):
            if p >= len(kvss[t]):
                continue
            newer = newer_all[offsets[t] + p * 2 * tq:offsets[t] + (p + 1) * 2 * tq]
            att = jnp.exp(zs[t][p] - sps[t][p] - newer - totals[t])
            if causal_first and p == 0:
                att = jnp.where(keep, att, 0.0)
            part = jnp.dot(att.astype(BF16), kvss[t][p][1], preferred_element_type=F32)
            pvs[t] = part if pvs[t] is None else pvs[t] + part
            totals[t] = totals[t] + (newer[:, 0:1] + sp_ms[t][p][:, 0:1])
    return [(pv, total - r_in) for pv, total, r_in in zip(pvs, totals, r_ins)]


def _attn_kernel(q_ref, k_ref, v_ref, o_ref, *, t_len, tile):
    nq = t_len // tile
    r_i = lax.broadcasted_iota(jnp.int32, (tile, tile), 0)
    c_i = lax.broadcasted_iota(jnp.int32, (tile, tile), 1)
    tri = jnp.where(r_i > c_i, 1.0, 0.0).astype(BF16)
    head0 = _lane_iota((tile, LANES)) < HEAD_DIM
    zero_r = jnp.zeros((2 * tile, 1), F32)

    def near(q0s, n_prev):
        q2s, kvss = [], []
        for q0 in q0s:
            q = q_ref[pl.ds(q0, tile), :]
            zq = jnp.zeros_like(q)
            q2s.append(jnp.concatenate([jnp.where(head0, q, zq), jnp.where(head0, zq, q)], axis=0))
            kvss.append([(k_ref[pl.ds(q0 - p * tile, tile), :], v_ref[pl.ds(q0 - p * tile, tile), :])
                         for p in range(n_prev + 1)])
        outs = _att_tiles(q2s, kvss, tri, [zero_r] * len(q0s), causal_first=True)
        return [(q2, acc, r) for q2, (acc, r) in zip(q2s, outs)]

    def far_and_store(q0, q2, acc, r, j_older):
        if j_older is not None:
            def cond(c):
                j, _, r_c = c
                return jnp.logical_and(j >= 0, jnp.min(r_c) < ATT_SKIP)

            def body(c):
                j, a_c, r_c = c
                k0 = pl.multiple_of(j * tile, tile)
                [(pv, dr)] = _att_tiles([q2], [[(k_ref[pl.ds(k0, tile), :], v_ref[pl.ds(k0, tile), :])]], tri,
                                        [r_c], causal_first=False)
                return j - 1, a_c + pv, r_c + dr

            _, acc, _ = lax.while_loop(cond, body, (j_older, acc, r))
        o_ref[pl.ds(q0, tile), :] = jnp.where(head0, acc[:tile], acc[tile:]).astype(BF16)

    n_prev = ATT_KEYS_AHEAD // tile
    group = ATT_TILES_PER_STEP
    first = n_prev + (nq - n_prev) % group
    for i in range(first):
        [part] = near([i * tile], min(i, n_prev))
        far_and_store(i * tile, *part, None if i <= n_prev else i - n_prev - 1)

    def later(g, _):
        tiles = [first + g * group + t for t in range(group)]
        q0s = [pl.multiple_of(i * tile, tile) for i in tiles]
        for i, q0, part in zip(tiles, q0s, near(q0s, n_prev)):
            far_and_store(q0, *part, i - n_prev - 1)
        return 0

    lax.fori_loop(0, (nq - first) // group, later, 0)


def _attention(q, k, v, bsz, t_len):
    n, d_att = q.shape
    spec = pl.BlockSpec((t_len, LANES), lambda b, p: (b, p))
    return pl.pallas_call(
        functools.partial(_attn_kernel, t_len=t_len, tile=ATT_TILE),
        grid=(bsz, d_att // LANES),
        in_specs=[spec, spec, spec],
        out_specs=spec,
        out_shape=jax.ShapeDtypeStruct((n, d_att), BF16),
        compiler_params=_cparams("parallel", "parallel"),
        name="attn",
    )(q, k, v)


def _rms(t, gain):
    return t * lax.rsqrt(jnp.mean(t * t, axis=-1, keepdims=True) + EPS) * gain


def _gelu_tanh(y):
    return 0.5 * y * (1.0 + jnp.tanh(math.sqrt(2.0 / math.pi) * (y + 0.044715 * (y * y * y))))


def _pack_bf16_pairs(a, b):
    ua = pltpu.bitcast(a.astype(BF16).astype(F32), jnp.uint32)
    ub = pltpu.bitcast(b.astype(BF16).astype(F32), jnp.uint32)
    return ua | (ub >> 16)


def _unpack_bf16_pairs(w):
    return pltpu.bitcast(w & jnp.uint32(0xFFFF0000), F32), pltpu.bitcast(w << 16, F32)


def _post_kernel(x_ref, ys_ref, ya_ref, wglu_ref, bglu_ref, gs_ref, ga_ref, wo_ref, gf_ref,
                 wr_ref, br_ref, x1_ref, h2_ref, route_ref, route_t_ref, cnt_ref, run_ref, *, d_ssm):
    i = pl.program_id(0)

    @pl.when(i == 0)
    def _():
        run_ref[...] = jnp.zeros_like(run_ref)

    tm = x_ref.shape[0]
    groups = [pl.ds(g * (tm // POST_ROW_GROUPS), tm // POST_ROW_GROUPS) for g in range(POST_ROW_GROUPS)]
    ys = [_gelu_tanh(ys_ref[rows, :].astype(F32)) for rows in groups]
    gates = [jnp.dot(y.astype(BF16), wglu_ref[...], preferred_element_type=F32) + bglu_ref[...] for y in ys]
    m_as = [_rms(ya_ref[rows, :].astype(F32), ga_ref[...]).astype(BF16) for rows in groups]
    m_ss = [_rms(y * jax.nn.sigmoid(gate), gs_ref[...]).astype(BF16) for y, gate in zip(ys, gates)]
    mixes = [jnp.dot(m_s, wo_ref[:d_ssm, :], preferred_element_type=F32)
             + jnp.dot(m_a, wo_ref[d_ssm:, :], preferred_element_type=F32) for m_s, m_a in zip(m_ss, m_as)]
    x1s = [x_ref[rows, :] + mix for rows, mix in zip(groups, mixes)]
    h2s = [_rms(x1, gf_ref[...]) for x1 in x1s]
    logits = [jnp.dot(h2.astype(BF16), wr_ref[...], preferred_element_type=F32) for h2 in h2s]
    half = x_ref.shape[1] // 2
    for rows, x1, h2 in zip(groups, x1s, h2s):
        x1_ref[rows, :] = x1
        h2_ref[rows, :] = _pack_bf16_pairs(h2[:, :half], h2[:, half:])
    lg = jnp.concatenate(logits, axis=0) + br_ref[...]
    lane = _lane_iota(lg.shape).astype(F32)
    neg = -jnp.inf
    first = lambda hit: jnp.min(jnp.where(hit, lane, float(LANES)), axis=-1, keepdims=True)
    glog = jnp.where(lane < N_EXPERT_GROUPS, lg, neg)
    gmax = jnp.max(glog, axis=-1, keepdims=True)
    p_grp = 1.0 / jnp.sum(jnp.exp(glog - gmax), axis=-1, keepdims=True)
    grp = first(glog == gmax)
    e0 = ROUTER_LANE0 + grp * EXPERTS_PER_GROUP
    elog = jnp.where(jnp.logical_and(lane >= e0, lane < e0 + EXPERTS_PER_GROUP), lg, neg)
    m1 = jnp.max(elog, axis=-1, keepdims=True)
    i1 = first(elog == m1)
    elog2 = jnp.where(lane == i1, neg, elog)
    m2 = jnp.max(elog2, axis=-1, keepdims=True)
    i2 = first(elog2 == m2)
    e21 = jnp.exp(m2 - m1)
    g1 = p_grp * (1.0 / (1.0 + e21))
    g2 = p_grp * (e21 / (1.0 + e21))

    sel1 = lane == i1
    sel2 = lane == i2
    onehot = jnp.where(jnp.logical_or(sel1, sel2), 1.0, 0.0)
    r_i = lax.broadcasted_iota(jnp.int32, (tm, tm), 0)
    c_i = lax.broadcasted_iota(jnp.int32, (tm, tm), 1)
    lower = jnp.where(c_i < r_i, 1.0, 0.0).astype(BF16)
    before = jnp.dot(lower, onehot.astype(BF16), preferred_element_type=F32) + run_ref[0:1, :]
    rank1 = jnp.sum(jnp.where(sel1, before, 0.0), axis=-1, keepdims=True)
    rank2 = jnp.sum(jnp.where(sel2, before, 0.0), axis=-1, keepdims=True)
    run_ref[0:1, :] = run_ref[0:1, :] + jnp.sum(onehot, axis=0, keepdims=True)
    cnt_ref[...] = run_ref[...]

    fields = (i1 - ROUTER_LANE0, i2 - ROUTER_LANE0, g1, g2, rank1, rank2)
    route = jnp.zeros(lg.shape, F32)
    for pos, val in enumerate(fields):
        route = jnp.where(lane == pos, val, route)
    route_ref[...] = route
    route_t_ref[...] = route.T[:ROUTE_FIELDS, :]


def _post(x2, y_ssm_t, y_att, w_glu_bf, b_glu, g_ssm, g_att, w_out_bf, g_ffn, w_r_bf, b_r, part):
    n, d = x2.shape
    d_ssm = w_glu_bf.shape[0]
    tm = POST_TM
    nt = y_ssm_t.shape[0] // tm
    steps = n // (tm * MOE_PARTS)
    i0 = part * steps
    row_in = lambda w: pl.BlockSpec((tm, w), lambda i: (i0 + i, 0))
    row = lambda w: pl.BlockSpec((tm, w), lambda i: (i, 0))
    ssm_spec = pl.BlockSpec((tm, d_ssm), lambda i: ((i0 + i) % nt, (i0 + i) // nt))
    full = lambda a: pl.BlockSpec(a.shape, lambda i: (0,) * a.ndim)
    cnt_spec = pl.BlockSpec((8, LANES), lambda i: (0, 0))
    n = n // MOE_PARTS
    return pl.pallas_call(
        functools.partial(_post_kernel, d_ssm=d_ssm),
        grid=(steps,),
        in_specs=[row_in(d), ssm_spec, row_in(y_att.shape[1]), full(w_glu_bf), full(b_glu), full(g_ssm),
                  full(g_att), full(w_out_bf), full(g_ffn), full(w_r_bf), full(b_r)],
        out_specs=[row(d), row(d // 2), row(LANES), pl.BlockSpec((ROUTE_FIELDS, tm), lambda i: (0, i)), cnt_spec],
        out_shape=[jax.ShapeDtypeStruct((n, d), F32), jax.ShapeDtypeStruct((n, d // 2), jnp.uint32),
                   jax.ShapeDtypeStruct((n, LANES), F32), jax.ShapeDtypeStruct((ROUTE_FIELDS, n), F32),
                   jax.ShapeDtypeStruct((8, LANES), F32)],
        scratch_shapes=[pltpu.VMEM((8, LANES), F32)],
        compiler_params=_cparams("arbitrary"),
        name="post",
    )(x2, y_ssm_t, y_att, w_glu_bf, b_glu, g_ssm, g_att, w_out_bf, g_ffn, w_r_bf, b_r)


def _dispatch_sc(d0, d1, h2p, n_rows):
    n, w = h2p.shape
    workers = SC_CORES * SC_SUBCORES
    n_win = n // (workers * SC_WINDOW)
    mesh = plsc.VectorSubcoreMesh(core_axis_name="c", subcore_axis_name="s")

    def body(h_hbm, d0_hbm, d1_hbm, o_hbm, rows_v, i0_v, i1_v):
        wid = lax.axis_index("c") * SC_SUBCORES + lax.axis_index("s")

        @pl.loop(0, n_win)
        def _(j):
            blk = wid * n_win + j
            pltpu.sync_copy(h_hbm.at[pl.ds(blk * SC_WINDOW, SC_WINDOW)], rows_v)
            pltpu.sync_copy(d0_hbm.at[blk], i0_v)
            pltpu.sync_copy(d1_hbm.at[blk], i1_v)
            pltpu.sync_copy(rows_v, o_hbm.at[i0_v])
            pltpu.sync_copy(rows_v, o_hbm.at[i1_v])

    return pl.kernel(
        body,
        out_type=jax.ShapeDtypeStruct((n_rows, w), h2p.dtype),
        mesh=mesh,
        scratch_types=[pltpu.VMEM((SC_WINDOW, w), h2p.dtype), pltpu.VMEM((SC_WINDOW,), jnp.int32),
                       pltpu.VMEM((SC_WINDOW,), jnp.int32)],
        name="dispatch_sc",
    )(h2p, d0, d1)


def _experts_kernel(blk_e_ref, next_e_ref, used_ref, x_ref, wg_hbm, wu_hbm, wd_hbm, o_ref,
                    wg_f32, wu_f32, wd_f32, wg_bf, wu_bf, wd_bf, slot_ref, sems):
    staged = ((wg_hbm, wg_f32, wg_bf), (wu_hbm, wu_f32, wu_bf), (wd_hbm, wd_f32, wd_bf))

    def fetch(expert, slot):
        return [pltpu.make_async_copy(hbm.at[expert], f32.at[slot], sems.at[slot, k])
                for k, (hbm, f32, _) in enumerate(staged)]

    @pl.when(pl.program_id(0) == 0)
    def _():
        slot_ref[0] = 0
        for cp in fetch(blk_e_ref[0], 0):
            cp.start()

    for s in range(MOE_BLOCKS_PER_STEP):
        i = pl.program_id(0) * MOE_BLOCKS_PER_STEP + s
        rows = pl.ds(s * MOE_ROWS, MOE_ROWS)

        @pl.when(jnp.logical_or(i == 0, blk_e_ref[i] != blk_e_ref[jnp.maximum(i - 1, 0)]))
        def _():
            slot = slot_ref[0]
            for cp in fetch(blk_e_ref[i], slot):
                cp.wait()
            for _, f32, bf in staged:
                bf[...] = f32[slot].astype(BF16)
            nxt = next_e_ref[i]

            @pl.when(nxt >= 0)
            def _():
                for cp in fetch(nxt, 1 - slot):
                    cp.start()

            slot_ref[0] = 1 - slot

        @pl.when(i < used_ref[0])
        def _():
            xa, xb = (t.astype(BF16) for t in _unpack_bf16_pairs(x_ref[rows, :]))
            half = xa.shape[1]
            gate = (jnp.dot(xa, wg_bf[:half, :], preferred_element_type=F32)
                    + jnp.dot(xb, wg_bf[half:, :], preferred_element_type=F32))
            up = (jnp.dot(xa, wu_bf[:half, :], preferred_element_type=F32)
                  + jnp.dot(xb, wu_bf[half:, :], preferred_element_type=F32))
            hid = (jax.nn.silu(gate) * up).astype(BF16)
            out = jnp.dot(hid, wd_bf[...], preferred_element_type=F32)
            o_ref[rows, :] = _pack_bf16_pairs(out[:, :half], out[:, half:])

        @pl.when(i >= used_ref[0])
        def _():
            o_ref[rows, :] = jnp.zeros((MOE_ROWS, o_ref.shape[1]), o_ref.dtype)


def _experts(blk_e, next_e, n_used, buf, w_gate, w_up, w_down):
    n_rows, w = buf.shape
    d = w_down.shape[2]
    hbm = pl.BlockSpec(memory_space=pl.ANY)
    weights = (w_gate, w_up, w_down)
    step_rows = MOE_ROWS * MOE_BLOCKS_PER_STEP
    grid_spec = pltpu.PrefetchScalarGridSpec(
        num_scalar_prefetch=3,
        grid=(n_rows // step_rows,),
        in_specs=[pl.BlockSpec((step_rows, w), lambda i, be, ne, nu: (i, 0)), hbm, hbm, hbm],
        out_specs=pl.BlockSpec((step_rows, d // 2), lambda i, be, ne, nu: (i, 0)),
        scratch_shapes=([pltpu.VMEM((2,) + a.shape[1:], F32) for a in weights]
                        + [pltpu.VMEM(a.shape[1:], BF16) for a in weights]
                        + [pltpu.SMEM((1,), jnp.int32), pltpu.SemaphoreType.DMA((2, len(weights)))]),
    )
    return pl.pallas_call(
        _experts_kernel,
        grid_spec=grid_spec,
        out_shape=jax.ShapeDtypeStruct((n_rows, d // 2), jnp.uint32),
        compiler_params=_cparams("arbitrary"),
        name="experts",
    )(blk_e, next_e, n_used, buf, w_gate, w_up, w_down)


def _gather_sc(d0, d1, eo):
    n = d0.size
    w = eo.shape[1]
    workers = SC_CORES * SC_SUBCORES
    n_win = n // (workers * SC_WINDOW)
    mesh = plsc.VectorSubcoreMesh(core_axis_name="c", subcore_axis_name="s")

    def body(eo_hbm, d0_hbm, d1_hbm, o0_hbm, o1_hbm, rows_v, i_v):
        wid = lax.axis_index("c") * SC_SUBCORES + lax.axis_index("s")

        @pl.loop(0, n_win)
        def _(j):
            blk = wid * n_win + j
            for d_hbm, o_hbm in ((d0_hbm, o0_hbm), (d1_hbm, o1_hbm)):
                pltpu.sync_copy(d_hbm.at[blk], i_v)
                pltpu.sync_copy(eo_hbm.at[i_v], rows_v)
                pltpu.sync_copy(rows_v, o_hbm.at[pl.ds(blk * SC_WINDOW, SC_WINDOW)])

    out_sd = jax.ShapeDtypeStruct((n, w), eo.dtype)
    return pl.kernel(
        body,
        out_type=(out_sd, out_sd),
        mesh=mesh,
        scratch_types=[pltpu.VMEM((SC_WINDOW, w), eo.dtype), pltpu.VMEM((SC_WINDOW,), jnp.int32)],
        name="gather_sc",
    )(eo, d0, d1)


def _combine_kernel(x1_ref, route_ref, r0_ref, r1_ref, *rest):
    o_ref = rest[-1]
    route = route_ref[...]
    row0 = jnp.concatenate(_unpack_bf16_pairs(r0_ref[...]), axis=1)
    row1 = jnp.concatenate(_unpack_bf16_pairs(r1_ref[...]), axis=1)
    o_ref[...] = x1_ref[...] + (route[:, 2:3] * row0 + route[:, 3:4] * row1)


def _combine(x1, route, rows0, rows1, part, out_prev):
    n_slice, d = x1.shape
    tm = COMBINE_TM
    steps = n_slice // tm
    row = lambda w: pl.BlockSpec((tm, w), lambda i: (i, 0))
    in_specs = [row(d), row(LANES), row(d // 2), row(d // 2)]
    args = [x1, route, rows0, rows1]
    aliases = {}
    if out_prev is not None:
        in_specs.append(pl.BlockSpec(memory_space=pl.ANY))
        args.append(out_prev)
        aliases = {len(args) - 1: 0}
    return pl.pallas_call(
        _combine_kernel,
        grid=(steps,),
        in_specs=in_specs,
        out_specs=pl.BlockSpec((tm, d), lambda i: (part * steps + i, 0)),
        out_shape=jax.ShapeDtypeStruct((n_slice * MOE_PARTS, d), F32),
        input_output_aliases=aliases,
        compiler_params=_cparams("parallel"),
        name="combine",
    )(*args)


def _layer(x, g_mix, w_in, lam_re, lam_im, log_dt, b_re, b_im, c_re, c_im, d_skip, w_glu, b_glu, g_q, g_k,
           g_ssm_out, g_attn_out, w_out, g_ffn, w_rg, b_rg, w_re, b_re_router, w_gate, w_up, w_down):
    bsz, t_len, d = x.shape
    n = bsz * t_len
    d_ssm = w_glu.shape[0]
    d_att = g_attn_out.shape[0]
    n_heads = d_att // HEAD_DIM
    n_chunks = t_len // SSM_CHUNK
    x2 = x.reshape(n, d)

    u_t, q, k, v = _in_proj(x, g_mix[None], w_in.astype(BF16), jnp.tile(g_q, n_heads)[None],
                            jnp.tile(g_k, n_heads)[None], d_ssm, d_att)
    tables = _s5_tables(lam_re, lam_im, log_dt, b_re, b_im, c_re, c_im, d_skip)
    y_ssm_t = _s5(u_t.reshape(n_chunks, SSM_CHUNK, bsz, d_ssm), tables).reshape(t_len, bsz * d_ssm)

    y_att = _attention(q.reshape(n, d_att), k.reshape(n, d_att), v.reshape(n, d_att), bsz, t_len)

    w_r = jnp.concatenate([w_rg, w_re.reshape(d, N_EXPERTS)], axis=1)
    w_r = jnp.pad(w_r, ((0, 0), (0, LANES - w_r.shape[1]))).astype(BF16)
    b_r = jnp.pad(jnp.concatenate([b_rg, b_re_router.reshape(N_EXPERTS)]), (0, LANES - ROUTER_LANE0 - N_EXPERTS))[None]
    w_glu_bf, w_out_bf = w_glu.astype(BF16), w_out.astype(BF16)
    n_slice = n // MOE_PARTS
    n_rows = n_slice * 2 + N_EXPERTS * MOE_ROWS
    n_blk = n_rows // MOE_ROWS
    out = None
    for part in range(MOE_PARTS):
        x1, h2p, route, route_t, cnt = _post(x2, y_ssm_t, y_att, w_glu_bf, b_glu[None], g_ssm_out[None],
                                             g_attn_out[None], w_out_bf, g_ffn[None], w_r, b_r, part)
        counts = cnt[0, ROUTER_LANE0:ROUTER_LANE0 + N_EXPERTS].astype(jnp.int32)
        pcounts = ((counts + MOE_ROWS - 1) // MOE_ROWS) * MOE_ROWS
        pends = jnp.cumsum(pcounts)
        pstarts = pends - pcounts
        e_ids = jnp.arange(N_EXPERTS, dtype=jnp.int32)
        dests = []
        for k in range(2):
            expert = route_t[k].astype(jnp.int32)
            start = jnp.sum(jnp.where(expert[:, None] == e_ids, pstarts, 0), axis=-1)
            dest = start + route_t[4 + k].astype(jnp.int32)
            dests.append(dest.reshape(n_slice // SC_WINDOW, SC_WINDOW))
        blk_row0 = jnp.arange(n_blk, dtype=jnp.int32)[:, None] * MOE_ROWS
        blk_e = jnp.minimum(jnp.sum((pends[None, :] <= blk_row0).astype(jnp.int32), axis=1), N_EXPERTS - 1)
        n_used = (pends[-1:] // MOE_ROWS).astype(jnp.int32)
        present = jnp.any(blk_e[:, None] == e_ids, axis=0)
        later = jnp.logical_and(present[None, :], e_ids[None, :] > blk_e[:, None])
        next_e = jnp.min(jnp.where(later, e_ids[None, :], N_EXPERTS), axis=1)
        next_e = jnp.where(next_e == N_EXPERTS, -1, next_e).astype(jnp.int32)

        buf = _dispatch_sc(*dests, h2p, n_rows)
        eo = _experts(blk_e, next_e, n_used, buf, w_gate, w_up, w_down)
        out = _combine(x1, route, *_gather_sc(*dests, eo), part, out)
    return out.reshape(bsz, t_len, d)


def kernel(x, g_mix, w_in, ssm_lambda_re, ssm_lambda_im, ssm_log_dt, ssm_b_re, ssm_b_im, ssm_c_re, ssm_c_im, ssm_d, ssm_w_glu, ssm_b_glu, g_q, g_k, g_ssm_out, g_attn_out, w_out, g_ffn, w_router_group, b_router_group, w_router_expert, b_router_expert, w_gate, w_up, w_down):
    for l in range(g_mix.shape[0]):
        x = _layer(x, g_mix[l], w_in[l], ssm_lambda_re[l], ssm_lambda_im[l], ssm_log_dt[l], ssm_b_re[l],
                   ssm_b_im[l], ssm_c_re[l], ssm_c_im[l], ssm_d[l], ssm_w_glu[l], ssm_b_glu[l], g_q[l], g_k[l],
                   g_ssm_out[l], g_attn_out[l], w_out[l], g_ffn[l], w_router_group[l], b_router_group[l],
                   w_router_expert[l], b_router_expert[l], w_gate[l], w_up[l], w_down[l])
    return x
```

```python
import functools
import math

import jax
import jax.numpy as jnp
from jax import lax
from jax.experimental import pallas as pl
from jax.experimental.pallas import tpu as pltpu
from jax.experimental.pallas import tpu_sc as plsc

F32 = jnp.float32
BF16 = jnp.bfloat16
EPS = 1e-6

LANES = 128
VMEM_LIMIT_BYTES = 56 * 1024 * 1024

SSM_GROUP = 16
SSM_CHUNK = 16
HEAD_DIM = 64
N_EXPERT_GROUPS = 4
EXPERTS_PER_GROUP = 8
N_EXPERTS = N_EXPERT_GROUPS * EXPERTS_PER_GROUP
ROUTER_LANE0 = N_EXPERT_GROUPS
ROUTE_FIELDS = 8
MOE_ROWS = 512
MOE_PARTS = 2
MOE_BLOCKS_PER_STEP = 4
ATT_SKIP = 110.0

S5_GPB = LANES // SSM_GROUP
S5_CHUNKS_PER_STEP = 16

IN_TM = 1024
IN_BATCHES = 8
ATT_TILE = 256
ATT_KEYS_AHEAD = 256
ATT_TILES_PER_STEP = 2
POST_TM = 1024
POST_ROW_GROUPS = 2
COMBINE_TM = 1024
SC_CORES = 2
SC_SUBCORES = 16
SC_WINDOW = 64


def _cparams(*sem):
    return pltpu.CompilerParams(dimension_semantics=sem, vmem_limit_bytes=VMEM_LIMIT_BYTES)


def _lane_iota(shape):
    return lax.broadcasted_iota(jnp.int32, shape, len(shape) - 1)


def _head_rms(t, gain):
    outs = []
    for c in range(t.shape[1] // LANES):
        blk = t[:, c * LANES:(c + 1) * LANES]
        sq = blk * blk
        lo = _lane_iota(blk.shape) < HEAD_DIM
        s_lo = jnp.sum(jnp.where(lo, sq, 0.0), axis=-1, keepdims=True)
        s_hi = jnp.sum(jnp.where(lo, 0.0, sq), axis=-1, keepdims=True)
        inv = jnp.where(lo, lax.rsqrt(s_lo * (1.0 / HEAD_DIM) + EPS),
                        lax.rsqrt(s_hi * (1.0 / HEAD_DIM) + EPS))
        outs.append(blk * inv * gain[:, c * LANES:(c + 1) * LANES])
    return jnp.concatenate(outs, axis=-1)


def _in_proj_kernel(x_ref, g_ref, w_ref, gq_ref, gk_ref, u_ref, q_ref, k_ref, v_ref, *, d_ssm, d_att, scale):
    nb, tt, d = x_ref.shape
    x = x_ref[...].reshape(nb * tt, d)
    inv = lax.rsqrt(jnp.mean(x * x, axis=-1, keepdims=True) + EPS)
    h = (x * inv * g_ref[...]).astype(BF16)
    proj = jnp.dot(h, w_ref[...], preferred_element_type=F32)
    u_ref[...] = pltpu.einshape("btc->tbc", proj[:, :d_ssm].astype(BF16).reshape(nb, tt, d_ssm))
    q = _head_rms(proj[:, d_ssm:d_ssm + d_att], gq_ref[...])
    k = _head_rms(proj[:, d_ssm + d_att:d_ssm + 2 * d_att], gk_ref[...])
    q_ref[...] = (q * scale).astype(BF16).reshape(nb, tt, d_att)
    k_ref[...] = k.astype(BF16).reshape(nb, tt, d_att)
    v_ref[...] = proj[:, d_ssm + 2 * d_att:].astype(BF16).reshape(nb, tt, d_att)


def _in_proj(x, g_mix, w_in_bf, gq_t, gk_t, d_ssm, d_att):
    bsz, t_len, d = x.shape
    nb = IN_BATCHES
    tt = IN_TM // nb
    tok = lambda w: pl.BlockSpec((nb, tt, w), lambda b, t: (b, t, 0))
    full = lambda a: pl.BlockSpec(a.shape, lambda b, t: (0,) * a.ndim)
    out_sd = jax.ShapeDtypeStruct((bsz, t_len, d_att), BF16)
    return pl.pallas_call(
        functools.partial(_in_proj_kernel, d_ssm=d_ssm, d_att=d_att, scale=1.0 / math.sqrt(HEAD_DIM)),
        grid=(bsz // nb, t_len // tt),
        in_specs=[tok(d), full(g_mix), full(w_in_bf), full(gq_t), full(gk_t)],
        out_specs=[pl.BlockSpec((tt, nb, d_ssm), lambda b, t: (t, b, 0)), tok(d_att), tok(d_att), tok(d_att)],
        out_shape=[jax.ShapeDtypeStruct((t_len, bsz, d_ssm), BF16), out_sd, out_sd, out_sd],
        compiler_params=_cparams("parallel", "parallel"),
        name="in_proj",
    )(x, g_mix, w_in_bf, gq_t, gk_t)


def _s5_tables(lam_re, lam_im, log_dt, b_re, b_im, c_re, c_im, d_skip):
    hp = lax.Precision.HIGHEST
    L = SSM_CHUNK
    g_n, p_n = lam_re.shape
    dt = jnp.exp(log_dt)[:, None]
    lr, li = lam_re, lam_im
    ls = jnp.arange(L + 1, dtype=F32)[:, None, None]
    mag = jnp.exp(lr * dt * ls)
    pr, pi = mag * jnp.cos(li * dt * ls), mag * jnp.sin(li * dt * ls)
    abar_r, abar_i = pr[1], pi[1]
    den = lr * lr + li * li
    nr, ni = abar_r - 1.0, abar_i
    coef_r = (nr * lr + ni * li) / den
    coef_i = (ni * lr - nr * li) / den
    bbr = coef_r[..., None] * b_re - coef_i[..., None] * b_im
    bbi = coef_r[..., None] * b_im + coef_i[..., None] * b_re
    wr = pr[..., None] * bbr - pi[..., None] * bbi
    wi = pr[..., None] * bbi + pi[..., None] * bbr
    kl = (jnp.einsum('gop,lgpi->lgoi', c_re, wr[:L], precision=hp)
          - jnp.einsum('gop,lgpi->lgoi', c_im, wi[:L], precision=hp))
    kl = kl.at[0].add(jax.vmap(jnp.diag)(d_skip))
    n_lb = g_n // S5_GPB
    sg = SSM_GROUP
    kc = kl.transpose(1, 3, 0, 2).reshape(n_lb, S5_GPB * sg, L * sg)
    b_rows = lambda w: (w[:L][::-1].reshape(L, n_lb, S5_GPB, p_n, sg).transpose(1, 0, 2, 4, 3)
                        .reshape(n_lb, L * S5_GPB * sg, p_n))
    p1r, p1i = pr[1:], pi[1:]
    cst_r = (c_re[None] * p1r[:, :, None, :] - c_im[None] * p1i[:, :, None, :])
    cst_i = -(c_re[None] * p1i[:, :, None, :] + c_im[None] * p1r[:, :, None, :])
    c_rows = lambda c: c.transpose(1, 3, 0, 2).reshape(n_lb, S5_GPB * p_n, L * sg)
    a_l = jnp.stack([pr[L].reshape(n_lb, S5_GPB * p_n), pi[L].reshape(n_lb, S5_GPB * p_n)], axis=1)
    return (kc.astype(BF16), b_rows(wr).astype(BF16), b_rows(wi).astype(BF16),
            c_rows(cst_r).astype(BF16), c_rows(cst_i).astype(BF16), a_l)


def _s5_expand_tables(kc_ref, bcr_ref, bci_ref, ccr_ref, cci_ref, w0_ref, br_ref, bi_ref, cr_ref, ci_ref):
    sg, gpb = SSM_GROUP, S5_GPB
    n_lo = w0_ref.shape[1]
    p_n = bcr_ref.shape[2]

    def iotas(shape):
        return lax.broadcasted_iota(jnp.int32, shape, 0), lax.broadcasted_iota(jnp.int32, shape, 1)

    r, c = iotas((kc_ref.shape[2], n_lo))
    spread_o = jnp.where(jnp.logical_and(r // sg == c // LANES, r % sg == c % sg), 1.0, 0.0).astype(BF16)
    r, c = iotas((p_n, gpb * p_n))
    spread_p = jnp.where(r == c % p_n, 1.0, 0.0).astype(BF16)

    def expand(compact, spread, row_group, col_group):
        full = jnp.dot(compact, spread, preferred_element_type=F32)
        r, c = iotas(full.shape)
        return jnp.where(row_group(r) == col_group(c), full, 0.0).astype(BF16)

    lane_group = lambda c: (c % LANES) // sg
    top = expand(kc_ref[0], spread_o, lambda r: r // sg, lane_group)
    w0_ref[:LANES, :] = top
    w0_ref[LANES:, :LANES] = jnp.zeros((LANES, LANES), BF16)
    w0_ref[LANES:, LANES:] = top[:, :n_lo - LANES]
    state_group = lambda c: c // p_n
    br_ref[...] = expand(bcr_ref[0], spread_p, lambda r: (r // sg) % gpb, state_group)
    bi_ref[...] = expand(bci_ref[0], spread_p, lambda r: (r // sg) % gpb, state_group)
    cr_ref[...] = expand(ccr_ref[0], spread_o, lambda r: r // p_n, lane_group)
    ci_ref[...] = expand(cci_ref[0], spread_o, lambda r: r // p_n, lane_group)


def _s5_kernel(u_ref, kc_ref, bcr_ref, bci_ref, ccr_ref, cci_ref, a_ref, y_ref,
               hr_ref, hi_ref, acc_ref, w0_ref, br_ref, bi_ref, cr_ref, ci_ref):
    n_chunks, L, bsz, _ = u_ref.shape
    rows = n_chunks * bsz

    @pl.when(pl.program_id(1) == 0)
    def _():
        hr_ref[...] = jnp.zeros_like(hr_ref)
        hi_ref[...] = jnp.zeros_like(hi_ref)
        _s5_expand_tables(kc_ref, bcr_ref, bci_ref, ccr_ref, cci_ref, w0_ref, br_ref, bi_ref, cr_ref, ci_ref)

    us = [u_ref[:, s].reshape(rows, LANES) for s in range(L)]
    lhs = jnp.concatenate(us, axis=1)
    sin_r = jnp.dot(lhs, br_ref[...], preferred_element_type=F32)
    sin_i = jnp.dot(lhs, bi_ref[...], preferred_element_type=F32)
    ar = a_ref[0, 0:1, :]
    ai = a_ref[0, 1:2, :]
    hr, hi = hr_ref[...], hi_ref[...]
    prev_r, prev_i = [], []
    for c in range(n_chunks):
        prev_r.append(hr)
        prev_i.append(hi)
        sl = slice(c * bsz, (c + 1) * bsz)
        hr, hi = ar * hr - ai * hi + sin_r[sl], ar * hi + ai * hr + sin_i[sl]
    hr_ref[...] = hr
    hi_ref[...] = hi
    pr = jnp.concatenate(prev_r, axis=0).astype(BF16)
    pi = jnp.concatenate(prev_i, axis=0).astype(BF16)
    acc_ref[...] = (jnp.dot(pr, cr_ref[...], preferred_element_type=F32)
                    + jnp.dot(pi, ci_ref[...], preferred_element_type=F32))
    for p in range(L // 2):
        off = 2 * p * LANES
        pair = jnp.concatenate([us[2 * p], us[2 * p + 1]], axis=1)
        acc_ref[:, off:] += jnp.dot(pair, w0_ref[:, :L * LANES - off], preferred_element_type=F32)
    for t in range(L):
        y_ref[:, t] = acc_ref[:, t * LANES:(t + 1) * LANES].reshape(n_chunks, bsz, LANES)


def _s5(u4, tables):
    n_chunks, L, bsz, d_ssm = u4.shape
    a_l = tables[-1]
    cb = S5_CHUNKS_PER_STEP
    data = pl.BlockSpec((cb, L, bsz, LANES), lambda lb, c: (c, 0, 0, lb))
    per_lb = lambda a: pl.BlockSpec((1,) + a.shape[1:], lambda lb, c: (lb,) + (0,) * (a.ndim - 1))
    n_state = a_l.shape[2]
    state = pltpu.VMEM((bsz, n_state), F32)
    n_lo = L * LANES
    return pl.pallas_call(
        _s5_kernel,
        grid=(d_ssm // LANES, n_chunks // cb),
        in_specs=[data] + [per_lb(t) for t in tables],
        out_specs=data,
        out_shape=jax.ShapeDtypeStruct(u4.shape, F32),
        scratch_shapes=[state, state, pltpu.VMEM((cb * bsz, n_lo), F32),
                        pltpu.VMEM((2 * LANES, n_lo), BF16),
                        pltpu.VMEM((n_lo, n_state), BF16), pltpu.VMEM((n_lo, n_state), BF16),
                        pltpu.VMEM((n_state, n_lo), BF16), pltpu.VMEM((n_state, n_lo), BF16)],
        compiler_params=_cparams("arbitrary", "arbitrary"),
        name="s5",
    )(u4, *tables)


def _softplus(z):
    return jnp.maximum(z, 0.0) + jnp.log(1.0 + jnp.exp(-jnp.abs(z)))


def _att_tiles(q2s, kvss, tri, r_ins, causal_first):
    tq = q2s[0].shape[0] // 2
    n_t = len(q2s)
    contract_last = (((1,), (1,)), ((), ()))
    zs = [[lax.dot_general(q2, k, contract_last, preferred_element_type=F32) for k, _ in kvs]
          for q2, kvs in zip(q2s, kvss)]
    sps = [[_softplus(z) for z in row] for row in zs]
    sp_ms = [list(row) for row in sps]
    if causal_first:
        rows = lax.broadcasted_iota(jnp.int32, zs[0][0].shape, 0)
        cols = lax.broadcasted_iota(jnp.int32, zs[0][0].shape, 1)
        keep = cols < jnp.where(rows >= tq, rows - tq, rows)
        for t in range(n_t):
            sp_ms[t][0] = jnp.where(keep, sps[t][0], 0.0)
    stacked = jnp.concatenate([blk for row in sp_ms for blk in row], axis=0).astype(BF16)
    newer_all = jnp.dot(stacked, tri, preferred_element_type=F32)
    pvs, totals = [None] * n_t, list(r_ins)
    base = 0
    offsets = []
    for row in sp_ms:
        offsets.append(base)
        base += len(row) * 2 * tq
    for p in range(max(len(kvs) for kvs in kvss)):
        for t in range(n_t):
            if p >= len(kvss[t]):
                continue
            newer = newer_all[offsets[t] + p * 2 * tq:offsets[t] + (p + 1) * 2 * tq]
            att = jnp.exp(zs[t][p] - sps[t][p] - newer - totals[t])
            if causal_first and p == 0:
                att = jnp.where(keep, att, 0.0)
            part = jnp.dot(att.astype(BF16), kvss[t][p][1], preferred_element_type=F32)
            pvs[t] = part if pvs[t] is None else pvs[t] + part
            totals[t] = totals[t] + (newer[:, 0:1] + sp_ms[t][p][:, 0:1])
    return [(pv, total - r_in) for pv, total, r_in in zip(pvs, totals, r_ins)]


def _attn_kernel(q_ref, k_ref, v_ref, o_ref, *, t_len, tile):
    nq = t_len // tile
    r_i = lax.broadcasted_iota(jnp.int32, (tile, tile), 0)
    c_i = lax.broadcasted_iota(jnp.int32, (tile, tile), 1)
    tri = jnp.where(r_i > c_i, 1.0, 0.0).astype(BF16)
    head0 = _lane_iota((tile, LANES)) < HEAD_DIM
    zero_r = jnp.zeros((2 * tile, 1), F32)

    def near(q0s, n_prev):
        q2s, kvss = [], []
        for q0 in q0s:
            q = q_ref[pl.ds(q0, tile), :]
            zq = jnp.zeros_like(q)
            q2s.append(jnp.concatenate([jnp.where(head0, q, zq), jnp.where(head0, zq, q)], axis=0))
            kvss.append([(k_ref[pl.ds(q0 - p * tile, tile), :], v_ref[pl.ds(q0 - p * tile, tile), :])
                         for p in range(n_prev + 1)])
        outs = _att_tiles(q2s, kvss, tri, [zero_r] * len(q0s), causal_first=True)
        return [(q2, acc, r) for q2, (acc, r) in zip(q2s, outs)]

    def far_and_store(q0, q2, acc, r, j_older):
        if j_older is not None:
            def cond(c):
                j, _, r_c = c
                return jnp.logical_and(j >= 0, jnp.min(r_c) < ATT_SKIP)

            def body(c):
                j, a_c, r_c = c
                k0 = pl.multiple_of(j * tile, tile)
                [(pv, dr)] = _att_tiles([q2], [[(k_ref[pl.ds(k0, tile), :], v_ref[pl.ds(k0, tile), :])]], tri,
                                        [r_c], causal_first=False)
                return j - 1, a_c + pv, r_c + dr

            _, acc, _ = lax.while_loop(cond, body, (j_older, acc, r))
        o_ref[pl.ds(q0, tile), :] = jnp.where(head0, acc[:tile], acc[tile:])

    n_prev = ATT_KEYS_AHEAD // tile
    group = ATT_TILES_PER_STEP
    first = n_prev + (nq - n_prev) % group
    for i in range(first):
        [part] = near([i * tile], min(i, n_prev))
        far_and_store(i * tile, *part, None if i <= n_prev else i - n_prev - 1)

    def later(g, _):
        tiles = [first + g * group + t for t in range(group)]
        q0s = [pl.multiple_of(i * tile, tile) for i in tiles]
        for i, q0, part in zip(tiles, q0s, near(q0s, n_prev)):
            far_and_store(q0, *part, i - n_prev - 1)
        return 0

    lax.fori_loop(0, (nq - first) // group, later, 0)


def _attention(q, k, v, bsz, t_len):
    n, d_att = q.shape
    spec = pl.BlockSpec((t_len, LANES), lambda b, p: (b, p))
    return pl.pallas_call(
        functools.partial(_attn_kernel, t_len=t_len, tile=ATT_TILE),
        grid=(bsz, d_att // LANES),
        in_specs=[spec, spec, spec],
        out_specs=spec,
        out_shape=jax.ShapeDtypeStruct((n, d_att), F32),
        compiler_params=_cparams("parallel", "parallel"),
        name="attn",
    )(q, k, v)


def _rms(t, gain):
    return t * lax.rsqrt(jnp.mean(t * t, axis=-1, keepdims=True) + EPS) * gain


def _gelu_tanh(y):
    return 0.5 * y * (1.0 + jnp.tanh(math.sqrt(2.0 / math.pi) * (y + 0.044715 * (y * y * y))))


def _pack_bf16_pairs(a, b):
    ua = pltpu.bitcast(a.astype(BF16).astype(F32), jnp.uint32)
    ub = pltpu.bitcast(b.astype(BF16).astype(F32), jnp.uint32)
    return ua | (ub >> 16)


def _unpack_bf16_pairs(w):
    return pltpu.bitcast(w & jnp.uint32(0xFFFF0000), F32), pltpu.bitcast(w << 16, F32)


def _post_kernel(x_ref, ys_ref, ya_ref, wglu_ref, bglu_ref, gs_ref, ga_ref, wo_ref, gf_ref,
                 wr_ref, br_ref, x1_ref, h2_ref, route_ref, route_t_ref, cnt_ref, run_ref, *, d_ssm):
    i = pl.program_id(0)

    @pl.when(i == 0)
    def _():
        run_ref[...] = jnp.zeros_like(run_ref)

    tm = x_ref.shape[0]
    groups = [pl.ds(g * (tm // POST_ROW_GROUPS), tm // POST_ROW_GROUPS) for g in range(POST_ROW_GROUPS)]
    ys = [_gelu_tanh(ys_ref[rows, :]) for rows in groups]
    gates = [jnp.dot(y.astype(BF16), wglu_ref[...], preferred_element_type=F32) + bglu_ref[...] for y in ys]
    m_as = [_rms(ya_ref[rows, :], ga_ref[...]).astype(BF16) for rows in groups]
    m_ss = [_rms(y * jax.nn.sigmoid(gate), gs_ref[...]).astype(BF16) for y, gate in zip(ys, gates)]
    mixes = [jnp.dot(m_s, wo_ref[:d_ssm, :], preferred_element_type=F32)
             + jnp.dot(m_a, wo_ref[d_ssm:, :], preferred_element_type=F32) for m_s, m_a in zip(m_ss, m_as)]
    x1s = [x_ref[rows, :] + mix for rows, mix in zip(groups, mixes)]
    h2s = [_rms(x1, gf_ref[...]) for x1 in x1s]
    logits = [jnp.dot(h2.astype(BF16), wr_ref[...], preferred_element_type=F32) for h2 in h2s]
    half = x_ref.shape[1] // 2
    for rows, x1, h2 in zip(groups, x1s, h2s):
        x1_ref[rows, :] = x1
        h2_ref[rows, :] = _pack_bf16_pairs(h2[:, :half], h2[:, half:])
    lg = jnp.concatenate(logits, axis=0) + br_ref[...]
    lane = _lane_iota(lg.shape).astype(F32)
    neg = -jnp.inf
    first = lambda hit: jnp.min(jnp.where(hit, lane, float(LANES)), axis=-1, keepdims=True)
    glog = jnp.where(lane < N_EXPERT_GROUPS, lg, neg)
    gmax = jnp.max(glog, axis=-1, keepdims=True)
    p_grp = 1.0 / jnp.sum(jnp.exp(glog - gmax), axis=-1, keepdims=True)
    grp = first(glog == gmax)
    e0 = ROUTER_LANE0 + grp * EXPERTS_PER_GROUP
    elog = jnp.where(jnp.logical_and(lane >= e0, lane < e0 + EXPERTS_PER_GROUP), lg, neg)
    m1 = jnp.max(elog, axis=-1, keepdims=True)
    i1 = first(elog == m1)
    elog2 = jnp.where(lane == i1, neg, elog)
    m2 = jnp.max(elog2, axis=-1, keepdims=True)
    i2 = first(elog2 == m2)
    e21 = jnp.exp(m2 - m1)
    g1 = p_grp * (1.0 / (1.0 + e21))
    g2 = p_grp * (e21 / (1.0 + e21))

    sel1 = lane == i1
    sel2 = lane == i2
    onehot = jnp.where(jnp.logical_or(sel1, sel2), 1.0, 0.0)
    r_i = lax.broadcasted_iota(jnp.int32, (tm, tm), 0)
    c_i = lax.broadcasted_iota(jnp.int32, (tm, tm), 1)
    lower = jnp.where(c_i < r_i, 1.0, 0.0).astype(BF16)
    before = jnp.dot(lower, onehot.astype(BF16), preferred_element_type=F32) + run_ref[0:1, :]
    rank1 = jnp.sum(jnp.where(sel1, before, 0.0), axis=-1, keepdims=True)
    rank2 = jnp.sum(jnp.where(sel2, before, 0.0), axis=-1, keepdims=True)
    run_ref[0:1, :] = run_ref[0:1, :] + jnp.sum(onehot, axis=0, keepdims=True)
    cnt_ref[...] = run_ref[...]

    fields = (i1 - ROUTER_LANE0, i2 - ROUTER_LANE0, g1, g2, rank1, rank2)
    route = jnp.zeros(lg.shape, F32)
    for pos, val in enumerate(fields):
        route = jnp.where(lane == pos, val, route)
    route_ref[...] = route
    route_t_ref[...] = route.T[:ROUTE_FIELDS, :]


def _post(x2, y_ssm_t, y_att, w_glu_bf, b_glu, g_ssm, g_att, w_out_bf, g_ffn, w_r_bf, b_r, part):
    n, d = x2.shape
    d_ssm = w_glu_bf.shape[0]
    tm = POST_TM
    nt = y_ssm_t.shape[0] // tm
    steps = n // (tm * MOE_PARTS)
    i0 = part * steps
    row_in = lambda w: pl.BlockSpec((tm, w), lambda i: (i0 + i, 0))
    row = lambda w: pl.BlockSpec((tm, w), lambda i: (i, 0))
    ssm_spec = pl.BlockSpec((tm, d_ssm), lambda i: ((i0 + i) % nt, (i0 + i) // nt))
    full = lambda a: pl.BlockSpec(a.shape, lambda i: (0,) * a.ndim)
    cnt_spec = pl.BlockSpec((8, LANES), lambda i: (0, 0))
    n = n // MOE_PARTS
    return pl.pallas_call(
        functools.partial(_post_kernel, d_ssm=d_ssm),
        grid=(steps,),
        in_specs=[row_in(d), ssm_spec, row_in(y_att.shape[1]), full(w_glu_bf), full(b_glu), full(g_ssm),
                  full(g_att), full(w_out_bf), full(g_ffn), full(w_r_bf), full(b_r)],
        out_specs=[row(d), row(d // 2), row(LANES), pl.BlockSpec((ROUTE_FIELDS, tm), lambda i: (0, i)), cnt_spec],
        out_shape=[jax.ShapeDtypeStruct((n, d), F32), jax.ShapeDtypeStruct((n, d // 2), jnp.uint32),
                   jax.ShapeDtypeStruct((n, LANES), F32), jax.ShapeDtypeStruct((ROUTE_FIELDS, n), F32),
                   jax.ShapeDtypeStruct((8, LANES), F32)],
        scratch_shapes=[pltpu.VMEM((8, LANES), F32)],
        compiler_params=_cparams("arbitrary"),
        name="post",
    )(x2, y_ssm_t, y_att, w_glu_bf, b_glu, g_ssm, g_att, w_out_bf, g_ffn, w_r_bf, b_r)


def _dispatch_sc(d0, d1, h2p, n_rows):
    n, w = h2p.shape
    workers = SC_CORES * SC_SUBCORES
    n_win = n // (workers * SC_WINDOW)
    mesh = plsc.VectorSubcoreMesh(core_axis_name="c", subcore_axis_name="s")

    def body(h_hbm, d0_hbm, d1_hbm, o_hbm, rows_v, i0_v, i1_v):
        wid = lax.axis_index("c") * SC_SUBCORES + lax.axis_index("s")

        @pl.loop(0, n_win)
        def _(j):
            blk = wid * n_win + j
            pltpu.sync_copy(h_hbm.at[pl.ds(blk * SC_WINDOW, SC_WINDOW)], rows_v)
            pltpu.sync_copy(d0_hbm.at[blk], i0_v)
            pltpu.sync_copy(d1_hbm.at[blk], i1_v)
            pltpu.sync_copy(rows_v, o_hbm.at[i0_v])
            pltpu.sync_copy(rows_v, o_hbm.at[i1_v])

    return pl.kernel(
        body,
        out_type=jax.ShapeDtypeStruct((n_rows, w), h2p.dtype),
        mesh=mesh,
        scratch_types=[pltpu.VMEM((SC_WINDOW, w), h2p.dtype), pltpu.VMEM((SC_WINDOW,), jnp.int32),
                       pltpu.VMEM((SC_WINDOW,), jnp.int32)],
        name="dispatch_sc",
    )(h2p, d0, d1)


def _experts_kernel(blk_e_ref, next_e_ref, used_ref, x_ref, wg_hbm, wu_hbm, wd_hbm, o_ref,
                    wg_f32, wu_f32, wd_f32, wg_bf, wu_bf, wd_bf, slot_ref, sems):
    staged = ((wg_hbm, wg_f32, wg_bf), (wu_hbm, wu_f32, wu_bf), (wd_hbm, wd_f32, wd_bf))

    def fetch(expert, slot):
        return [pltpu.make_async_copy(hbm.at[expert], f32.at[slot], sems.at[slot, k])
                for k, (hbm, f32, _) in enumerate(staged)]

    @pl.when(pl.program_id(0) == 0)
    def _():
        slot_ref[0] = 0
        for cp in fetch(blk_e_ref[0], 0):
            cp.start()

    for s in range(MOE_BLOCKS_PER_STEP):
        i = pl.program_id(0) * MOE_BLOCKS_PER_STEP + s
        rows = pl.ds(s * MOE_ROWS, MOE_ROWS)

        @pl.when(jnp.logical_or(i == 0, blk_e_ref[i] != blk_e_ref[jnp.maximum(i - 1, 0)]))
        def _():
            slot = slot_ref[0]
            for cp in fetch(blk_e_ref[i], slot):
                cp.wait()
            for _, f32, bf in staged:
                bf[...] = f32[slot].astype(BF16)
            nxt = next_e_ref[i]

            @pl.when(nxt >= 0)
            def _():
                for cp in fetch(nxt, 1 - slot):
                    cp.start()

            slot_ref[0] = 1 - slot

        @pl.when(i < used_ref[0])
        def _():
            xa, xb = (t.astype(BF16) for t in _unpack_bf16_pairs(x_ref[rows, :]))
            half = xa.shape[1]
            gate = (jnp.dot(xa, wg_bf[:half, :], preferred_element_type=F32)
                    + jnp.dot(xb, wg_bf[half:, :], preferred_element_type=F32))
            up = (jnp.dot(xa, wu_bf[:half, :], preferred_element_type=F32)
                  + jnp.dot(xb, wu_bf[half:, :], preferred_element_type=F32))
            hid = (jax.nn.silu(gate) * up).astype(BF16)
            out = jnp.dot(hid, wd_bf[...], preferred_element_type=F32)
            o_ref[rows, :] = _pack_bf16_pairs(out[:, :half], out[:, half:])

        @pl.when(i >= used_ref[0])
        def _():
            o_ref[rows, :] = jnp.zeros((MOE_ROWS, o_ref.shape[1]), o_ref.dtype)


def _experts(blk_e, next_e, n_used, buf, w_gate, w_up, w_down):
    n_rows, w = buf.shape
    d = w_down.shape[2]
    hbm = pl.BlockSpec(memory_space=pl.ANY)
    weights = (w_gate, w_up, w_down)
    step_rows = MOE_ROWS * MOE_BLOCKS_PER_STEP
    grid_spec = pltpu.PrefetchScalarGridSpec(
        num_scalar_prefetch=3,
        grid=(n_rows // step_rows,),
        in_specs=[pl.BlockSpec((step_rows, w), lambda i, be, ne, nu: (i, 0)), hbm, hbm, hbm],
        out_specs=pl.BlockSpec((step_rows, d // 2), lambda i, be, ne, nu: (i, 0)),
        scratch_shapes=([pltpu.VMEM((2,) + a.shape[1:], F32) for a in weights]
                        + [pltpu.VMEM(a.shape[1:], BF16) for a in weights]
                        + [pltpu.SMEM((1,), jnp.int32), pltpu.SemaphoreType.DMA((2, len(weights)))]),
    )
    return pl.pallas_call(
        _experts_kernel,
        grid_spec=grid_spec,
        out_shape=jax.ShapeDtypeStruct((n_rows, d // 2), jnp.uint32),
        compiler_params=_cparams("arbitrary"),
        name="experts",
    )(blk_e, next_e, n_used, buf, w_gate, w_up, w_down)


def _gather_sc(d0, d1, eo):
    n = d0.size
    w = eo.shape[1]
    workers = SC_CORES * SC_SUBCORES
    n_win = n // (workers * SC_WINDOW)
    mesh = plsc.VectorSubcoreMesh(core_axis_name="c", subcore_axis_name="s")

    def body(eo_hbm, d0_hbm, d1_hbm, o0_hbm, o1_hbm, rows_v, i_v):
        wid = lax.axis_index("c") * SC_SUBCORES + lax.axis_index("s")

        @pl.loop(0, n_win)
        def _(j):
            blk = wid * n_win + j
            for d_hbm, o_hbm in ((d0_hbm, o0_hbm), (d1_hbm, o1_hbm)):
                pltpu.sync_copy(d_hbm.at[blk], i_v)
                pltpu.sync_copy(eo_hbm.at[i_v], rows_v)
                pltpu.sync_copy(rows_v, o_hbm.at[pl.ds(blk * SC_WINDOW, SC_WINDOW)])

    out_sd = jax.ShapeDtypeStruct((n, w), eo.dtype)
    return pl.kernel(
        body,
        out_type=(out_sd, out_sd),
        mesh=mesh,
        scratch_types=[pltpu.VMEM((SC_WINDOW, w), eo.dtype), pltpu.VMEM((SC_WINDOW,), jnp.int32)],
        name="gather_sc",
    )(eo, d0, d1)


def _combine_kernel(x1_ref, route_ref, r0_ref, r1_ref, *rest):
    o_ref = rest[-1]
    route = route_ref[...]
    row0 = jnp.concatenate(_unpack_bf16_pairs(r0_ref[...]), axis=1)
    row1 = jnp.concatenate(_unpack_bf16_pairs(r1_ref[...]), axis=1)
    o_ref[...] = x1_ref[...] + (route[:, 2:3] * row0 + route[:, 3:4] * row1)


def _combine(x1, route, rows0, rows1, part, out_prev):
    n_slice, d = x1.shape
    tm = COMBINE_TM
    steps = n_slice // tm
    row = lambda w: pl.BlockSpec((tm, w), lambda i: (i, 0))
    in_specs = [row(d), row(LANES), row(d // 2), row(d // 2)]
    args = [x1, route, rows0, rows1]
    aliases = {}
    if out_prev is not None:
        in_specs.append(pl.BlockSpec(memory_space=pl.ANY))
        args.append(out_prev)
        aliases = {len(args) - 1: 0}
    return pl.pallas_call(
        _combine_kernel,
        grid=(steps,),
        in_specs=in_specs,
        out_specs=pl.BlockSpec((tm, d), lambda i: (part * steps + i, 0)),
        out_shape=jax.ShapeDtypeStruct((n_slice * MOE_PARTS, d), F32),
        input_output_aliases=aliases,
        compiler_params=_cparams("parallel"),
        name="combine",
    )(*args)


def _layer(x, g_mix, w_in, lam_re, lam_im, log_dt, b_re, b_im, c_re, c_im, d_skip, w_glu, b_glu, g_q, g_k,
           g_ssm_out, g_attn_out, w_out, g_ffn, w_rg, b_rg, w_re, b_re_router, w_gate, w_up, w_down):
    bsz, t_len, d = x.shape
    n = bsz * t_len
    d_ssm = w_glu.shape[0]
    d_att = g_attn_out.shape[0]
    n_heads = d_att // HEAD_DIM
    n_chunks = t_len // SSM_CHUNK
    x2 = x.reshape(n, d)

    u_t, q, k, v = _in_proj(x, g_mix[None], w_in.astype(BF16), jnp.tile(g_q, n_heads)[None],
                            jnp.tile(g_k, n_heads)[None], d_ssm, d_att)
    tables = _s5_tables(lam_re, lam_im, log_dt, b_re, b_im, c_re, c_im, d_skip)
    y_ssm_t = _s5(u_t.reshape(n_chunks, SSM_CHUNK, bsz, d_ssm), tables).reshape(t_len, bsz * d_ssm)

    y_att = _attention(q.reshape(n, d_att), k.reshape(n, d_att), v.reshape(n, d_att), bsz, t_len)

    w_r = jnp.concatenate([w_rg, w_re.reshape(d, N_EXPERTS)], axis=1)
    w_r = jnp.pad(w_r, ((0, 0), (0, LANES - w_r.shape[1]))).astype(BF16)
    b_r = jnp.pad(jnp.concatenate([b_rg, b_re_router.reshape(N_EXPERTS)]), (0, LANES - ROUTER_LANE0 - N_EXPERTS))[None]
    w_glu_bf, w_out_bf = w_glu.astype(BF16), w_out.astype(BF16)
    n_slice = n // MOE_PARTS
    n_rows = n_slice * 2 + N_EXPERTS * MOE_ROWS
    n_blk = n_rows // MOE_ROWS
    out = None
    for part in range(MOE_PARTS):
        x1, h2p, route, route_t, cnt = _post(x2, y_ssm_t, y_att, w_glu_bf, b_glu[None], g_ssm_out[None],
                                             g_attn_out[None], w_out_bf, g_ffn[None], w_r, b_r, part)
        counts = cnt[0, ROUTER_LANE0:ROUTER_LANE0 + N_EXPERTS].astype(jnp.int32)
        pcounts = ((counts + MOE_ROWS - 1) // MOE_ROWS) * MOE_ROWS
        pends = jnp.cumsum(pcounts)
        pstarts = pends - pcounts
        e_ids = jnp.arange(N_EXPERTS, dtype=jnp.int32)
        dests = []
        for k in range(2):
            expert = route_t[k].astype(jnp.int32)
            start = jnp.sum(jnp.where(expert[:, None] == e_ids, pstarts, 0), axis=-1)
            dest = start + route_t[4 + k].astype(jnp.int32)
            dests.append(dest.reshape(n_slice // SC_WINDOW, SC_WINDOW))
        blk_row0 = jnp.arange(n_blk, dtype=jnp.int32)[:, None] * MOE_ROWS
        blk_e = jnp.minimum(jnp.sum((pends[None, :] <= blk_row0).astype(jnp.int32), axis=1), N_EXPERTS - 1)
        n_used = (pends[-1:] // MOE_ROWS).astype(jnp.int32)
        present = jnp.any(blk_e[:, None] == e_ids, axis=0)
        later = jnp.logical_and(present[None, :], e_ids[None, :] > blk_e[:, None])
        next_e = jnp.min(jnp.where(later, e_ids[None, :], N_EXPERTS), axis=1)
        next_e = jnp.where(next_e == N_EXPERTS, -1, next_e).astype(jnp.int32)

        buf = _dispatch_sc(*dests, h2p, n_rows)
        eo = _experts(blk_e, next_e, n_used, buf, w_gate, w_up, w_down)
        out = _combine(x1, route, *_gather_sc(*dests, eo), part, out)
    return out.reshape(bsz, t_len, d)


def kernel(x, g_mix, w_in, ssm_lambda_re, ssm_lambda_im, ssm_log_dt, ssm_b_re, ssm_b_im, ssm_c_re, ssm_c_im, ssm_d, ssm_w_glu, ssm_b_glu, g_q, g_k, g_ssm_out, g_attn_out, w_out, g_ffn, w_router_group, b_router_group, w_router_expert, b_router_expert, w_gate, w_up, w_down):
    for l in range(g_mix.shape[0]):
        x = _layer(x, g_mix[l], w_in[l], ssm_lambda_re[l], ssm_lambda_im[l], ssm_log_dt[l], ssm_b_re[l],
                   ssm_b_im[l], ssm_c_re[l], ssm_c_im[l], ssm_d[l], ssm_w_glu[l], ssm_b_glu[l], g_q[l], g_k[l],
                   g_ssm_out[l], g_attn_out[l], w_out[l], g_ffn[l], w_router_group[l], b_router_group[l],
                   w_router_expert[l], b_router_expert[l], w_gate[l], w_up[l], w_down[l])
    return x
```

```python
import functools
import math

import jax
import jax.numpy as jnp
from jax import lax
from jax.experimental import pallas as pl
from jax.experimental.pallas import tpu as pltpu
from jax.experimental.pallas import tpu_sc as plsc

F32 = jnp.float32
BF16 = jnp.bfloat16
EPS = 1e-6

LANES = 128
VMEM_LIMIT_BYTES = 56 * 1024 * 1024

SSM_GROUP = 16
SSM_CHUNK = 16
HEAD_DIM = 64
N_EXPERT_GROUPS = 4
EXPERTS_PER_GROUP = 8
N_EXPERTS = N_EXPERT_GROUPS * EXPERTS_PER_GROUP
ROUTER_LANE0 = N_EXPERT_GROUPS
ROUTE_FIELDS = 8
MOE_ROWS = 512
MOE_PARTS = 2
MOE_BLOCKS_PER_STEP = 4
ATT_SKIP = 110.0

S5_GPB = LANES // SSM_GROUP
S5_CHUNKS_PER_STEP = 16

IN_TM = 1024
IN_BATCHES = 8
ATT_TILE = 256
ATT_KEYS_AHEAD = 256
ATT_TILES_PER_STEP = 6
POST_TM = 1024
POST_ROW_GROUPS = 2
COMBINE_TM = 1024
SC_CORES = 2
SC_SUBCORES = 16
SC_WINDOW = 64


def _cparams(*sem):
    return pltpu.CompilerParams(dimension_semantics=sem, vmem_limit_bytes=VMEM_LIMIT_BYTES)


def _lane_iota(shape):
    return lax.broadcasted_iota(jnp.int32, shape, len(shape) - 1)


def _head_rms(t, gain):
    outs = []
    for c in range(t.shape[1] // LANES):
        blk = t[:, c * LANES:(c + 1) * LANES]
        sq = blk * blk
        lo = _lane_iota(blk.shape) < HEAD_DIM
        s_lo = jnp.sum(jnp.where(lo, sq, 0.0), axis=-1, keepdims=True)
        s_hi = jnp.sum(jnp.where(lo, 0.0, sq), axis=-1, keepdims=True)
        inv = jnp.where(lo, lax.rsqrt(s_lo * (1.0 / HEAD_DIM) + EPS),
                        lax.rsqrt(s_hi * (1.0 / HEAD_DIM) + EPS))
        outs.append(blk * inv * gain[:, c * LANES:(c + 1) * LANES])
    return jnp.concatenate(outs, axis=-1)


def _in_proj_kernel(x_ref, g_ref, w_ref, gq_ref, gk_ref, u_ref, q_ref, k_ref, v_ref, *, d_ssm, d_att, scale):
    nb, tt, d = x_ref.shape
    x = x_ref[...].reshape(nb * tt, d)
    inv = lax.rsqrt(jnp.mean(x * x, axis=-1, keepdims=True) + EPS)
    h = (x * inv * g_ref[...]).astype(BF16)
    proj = jnp.dot(h, w_ref[...], preferred_element_type=F32)
    u_ref[...] = pltpu.einshape("btc->tbc", proj[:, :d_ssm].astype(BF16).reshape(nb, tt, d_ssm))
    q = _head_rms(proj[:, d_ssm:d_ssm + d_att], gq_ref[...])
    k = _head_rms(proj[:, d_ssm + d_att:d_ssm + 2 * d_att], gk_ref[...])
    q_ref[...] = (q * scale).astype(BF16).reshape(nb, tt, d_att)
    k_ref[...] = k.astype(BF16).reshape(nb, tt, d_att)
    v_ref[...] = proj[:, d_ssm + 2 * d_att:].astype(BF16).reshape(nb, tt, d_att)


def _in_proj(x, g_mix, w_in_bf, gq_t, gk_t, d_ssm, d_att):
    bsz, t_len, d = x.shape
    nb = IN_BATCHES
    tt = IN_TM // nb
    tok = lambda w: pl.BlockSpec((nb, tt, w), lambda b, t: (b, t, 0))
    full = lambda a: pl.BlockSpec(a.shape, lambda b, t: (0,) * a.ndim)
    out_sd = jax.ShapeDtypeStruct((bsz, t_len, d_att), BF16)
    return pl.pallas_call(
        functools.partial(_in_proj_kernel, d_ssm=d_ssm, d_att=d_att, scale=1.0 / math.sqrt(HEAD_DIM)),
        grid=(bsz // nb, t_len // tt),
        in_specs=[tok(d), full(g_mix), full(w_in_bf), full(gq_t), full(gk_t)],
        out_specs=[pl.BlockSpec((tt, nb, d_ssm), lambda b, t: (t, b, 0)), tok(d_att), tok(d_att), tok(d_att)],
        out_shape=[jax.ShapeDtypeStruct((t_len, bsz, d_ssm), BF16), out_sd, out_sd, out_sd],
        compiler_params=_cparams("parallel", "parallel"),
        name="in_proj",
    )(x, g_mix, w_in_bf, gq_t, gk_t)


def _s5_tables(lam_re, lam_im, log_dt, b_re, b_im, c_re, c_im, d_skip):
    hp = lax.Precision.HIGHEST
    L = SSM_CHUNK
    g_n, p_n = lam_re.shape
    dt = jnp.exp(log_dt)[:, None]
    lr, li = lam_re, lam_im
    ls = jnp.arange(L + 1, dtype=F32)[:, None, None]
    mag = jnp.exp(lr * dt * ls)
    pr, pi = mag * jnp.cos(li * dt * ls), mag * jnp.sin(li * dt * ls)
    abar_r, abar_i = pr[1], pi[1]
    den = lr * lr + li * li
    nr, ni = abar_r - 1.0, abar_i
    coef_r = (nr * lr + ni * li) / den
    coef_i = (ni * lr - nr * li) / den
    bbr = coef_r[..., None] * b_re - coef_i[..., None] * b_im
    bbi = coef_r[..., None] * b_im + coef_i[..., None] * b_re
    wr = pr[..., None] * bbr - pi[..., None] * bbi
    wi = pr[..., None] * bbi + pi[..., None] * bbr
    kl = (jnp.einsum('gop,lgpi->lgoi', c_re, wr[:L], precision=hp)
          - jnp.einsum('gop,lgpi->lgoi', c_im, wi[:L], precision=hp))
    kl = kl.at[0].add(jax.vmap(jnp.diag)(d_skip))
    n_lb = g_n // S5_GPB
    sg = SSM_GROUP
    kc = kl.transpose(1, 3, 0, 2).reshape(n_lb, S5_GPB * sg, L * sg)
    b_rows = lambda w: (w[:L][::-1].reshape(L, n_lb, S5_GPB, p_n, sg).transpose(1, 0, 2, 4, 3)
                        .reshape(n_lb, L * S5_GPB * sg, p_n))
    p1r, p1i = pr[1:], pi[1:]
    cst_r = (c_re[None] * p1r[:, :, None, :] - c_im[None] * p1i[:, :, None, :])
    cst_i = -(c_re[None] * p1i[:, :, None, :] + c_im[None] * p1r[:, :, None, :])
    c_rows = lambda c: c.transpose(1, 3, 0, 2).reshape(n_lb, S5_GPB * p_n, L * sg)
    a_l = jnp.stack([pr[L].reshape(n_lb, S5_GPB * p_n), pi[L].reshape(n_lb, S5_GPB * p_n)], axis=1)
    return (kc.astype(BF16), b_rows(wr).astype(BF16), b_rows(wi).astype(BF16),
            c_rows(cst_r).astype(BF16), c_rows(cst_i).astype(BF16), a_l)


def _s5_expand_tables(kc_ref, bcr_ref, bci_ref, ccr_ref, cci_ref, w0_ref, br_ref, bi_ref, cr_ref, ci_ref):
    sg, gpb = SSM_GROUP, S5_GPB
    n_lo = w0_ref.shape[1]
    p_n = bcr_ref.shape[2]

    def iotas(shape):
        return lax.broadcasted_iota(jnp.int32, shape, 0), lax.broadcasted_iota(jnp.int32, shape, 1)

    r, c = iotas((kc_ref.shape[2], n_lo))
    spread_o = jnp.where(jnp.logical_and(r // sg == c // LANES, r % sg == c % sg), 1.0, 0.0).astype(BF16)
    r, c = iotas((p_n, gpb * p_n))
    spread_p = jnp.where(r == c % p_n, 1.0, 0.0).astype(BF16)

    def expand(compact, spread, row_group, col_group):
        full = jnp.dot(compact, spread, preferred_element_type=F32)
        r, c = iotas(full.shape)
        return jnp.where(row_group(r) == col_group(c), full, 0.0).astype(BF16)

    lane_group = lambda c: (c % LANES) // sg
    top = expand(kc_ref[0], spread_o, lambda r: r // sg, lane_group)
    w0_ref[:LANES, :] = top
    w0_ref[LANES:, :LANES] = jnp.zeros((LANES, LANES), BF16)
    w0_ref[LANES:, LANES:] = top[:, :n_lo - LANES]
    state_group = lambda c: c // p_n
    br_ref[...] = expand(bcr_ref[0], spread_p, lambda r: (r // sg) % gpb, state_group)
    bi_ref[...] = expand(bci_ref[0], spread_p, lambda r: (r // sg) % gpb, state_group)
    cr_ref[...] = expand(ccr_ref[0], spread_o, lambda r: r // p_n, lane_group)
    ci_ref[...] = expand(cci_ref[0], spread_o, lambda r: r // p_n, lane_group)


def _s5_kernel(u_ref, kc_ref, bcr_ref, bci_ref, ccr_ref, cci_ref, a_ref, y_ref,
               hr_ref, hi_ref, acc_ref, w0_ref, br_ref, bi_ref, cr_ref, ci_ref):
    n_chunks, L, bsz, _ = u_ref.shape
    rows = n_chunks * bsz

    @pl.when(pl.program_id(1) == 0)
    def _():
        hr_ref[...] = jnp.zeros_like(hr_ref)
        hi_ref[...] = jnp.zeros_like(hi_ref)
        _s5_expand_tables(kc_ref, bcr_ref, bci_ref, ccr_ref, cci_ref, w0_ref, br_ref, bi_ref, cr_ref, ci_ref)

    us = [u_ref[:, s].reshape(rows, LANES) for s in range(L)]
    lhs = jnp.concatenate(us, axis=1)
    sin_r = jnp.dot(lhs, br_ref[...], preferred_element_type=F32)
    sin_i = jnp.dot(lhs, bi_ref[...], preferred_element_type=F32)
    ar = a_ref[0, 0:1, :]
    ai = a_ref[0, 1:2, :]
    hr, hi = hr_ref[...], hi_ref[...]
    prev_r, prev_i = [], []
    for c in range(n_chunks):
        prev_r.append(hr)
        prev_i.append(hi)
        sl = slice(c * bsz, (c + 1) * bsz)
        hr, hi = ar * hr - ai * hi + sin_r[sl], ar * hi + ai * hr + sin_i[sl]
    hr_ref[...] = hr
    hi_ref[...] = hi
    pr = jnp.concatenate(prev_r, axis=0).astype(BF16)
    pi = jnp.concatenate(prev_i, axis=0).astype(BF16)
    acc_ref[...] = (jnp.dot(pr, cr_ref[...], preferred_element_type=F32)
                    + jnp.dot(pi, ci_ref[...], preferred_element_type=F32))
    for p in range(L // 2):
        off = 2 * p * LANES
        pair = jnp.concatenate([us[2 * p], us[2 * p + 1]], axis=1)
        acc_ref[:, off:] += jnp.dot(pair, w0_ref[:, :L * LANES - off], preferred_element_type=F32)
    for t in range(L):
        y_ref[:, t] = acc_ref[:, t * LANES:(t + 1) * LANES].reshape(n_chunks, bsz, LANES)


def _s5(u4, tables):
    n_chunks, L, bsz, d_ssm = u4.shape
    a_l = tables[-1]
    cb = S5_CHUNKS_PER_STEP
    data = pl.BlockSpec((cb, L, bsz, LANES), lambda lb, c: (c, 0, 0, lb))
    per_lb = lambda a: pl.BlockSpec((1,) + a.shape[1:], lambda lb, c: (lb,) + (0,) * (a.ndim - 1))
    n_state = a_l.shape[2]
    state = pltpu.VMEM((bsz, n_state), F32)
    n_lo = L * LANES
    return pl.pallas_call(
        _s5_kernel,
        grid=(d_ssm // LANES, n_chunks // cb),
        in_specs=[data] + [per_lb(t) for t in tables],
        out_specs=data,
        out_shape=jax.ShapeDtypeStruct(u4.shape, F32),
        scratch_shapes=[state, state, pltpu.VMEM((cb * bsz, n_lo), F32),
                        pltpu.VMEM((2 * LANES, n_lo), BF16),
                        pltpu.VMEM((n_lo, n_state), BF16), pltpu.VMEM((n_lo, n_state), BF16),
                        pltpu.VMEM((n_state, n_lo), BF16), pltpu.VMEM((n_state, n_lo), BF16)],
        compiler_params=_cparams("arbitrary", "arbitrary"),
        name="s5",
    )(u4, *tables)


def _softplus(z):
    return jnp.maximum(z, 0.0) + jnp.log(1.0 + jnp.exp(-jnp.abs(z)))


def _att_tiles(q2s, kvss, tri, r_ins, causal_first):
    tq = q2s[0].shape[0] // 2
    n_t = len(q2s)
    contract_last = (((1,), (1,)), ((), ()))
    zs = [[lax.dot_general(q2, k, contract_last, preferred_element_type=F32) for k, _ in kvs]
          for q2, kvs in zip(q2s, kvss)]
    sps = [[_softplus(z) for z in row] for row in zs]
    sp_ms = [list(row) for row in sps]
    if causal_first:
        rows = lax.broadcasted_iota(jnp.int32, zs[0][0].shape, 0)
        cols = lax.broadcasted_iota(jnp.int32, zs[0][0].shape, 1)
        keep = cols < jnp.where(rows >= tq, rows - tq, rows)
        for t in range(n_t):
            sp_ms[t][0] = jnp.where(keep, sps[t][0], 0.0)
    stacked = jnp.concatenate([blk for row in sp_ms for blk in row], axis=0).astype(BF16)
    newer_all = jnp.dot(stacked, tri, preferred_element_type=F32)
    pvs, totals = [None] * n_t, list(r_ins)
    base = 0
    offsets = []
    for row in sp_ms:
        offsets.append(base)
        base += len(row) * 2 * tq
    for p in range(max(len(kvs) for kvs in kvss)):
        for t in range(n_t):
            if p >= len(kvss[t]):
                continue
            newer = newer_all[offsets[t] + p * 2 * tq:offsets[t] + (p + 1) * 2 * tq]
            att = jnp.exp(zs[t][p] - sps[t][p] - newer - totals[t])
            if causal_first and p == 0:
                att = jnp.where(keep, att, 0.0)
            part = jnp.dot(att.astype(BF16), kvss[t][p][1], preferred_element_type=F32)
            pvs[t] = part if pvs[t] is None else pvs[t] + part
            totals[t] = totals[t] + (newer[:, 0:1] + sp_ms[t][p][:, 0:1])
    return [(pv, total - r_in) for pv, total, r_in in zip(pvs, totals, r_ins)]


def _attn_kernel(q_ref, k_ref, v_ref, o_ref, *, t_len, tile):
    nq = t_len // tile
    r_i = lax.broadcasted_iota(jnp.int32, (tile, tile), 0)
    c_i = lax.broadcasted_iota(jnp.int32, (tile, tile), 1)
    tri = jnp.where(r_i > c_i, 1.0, 0.0).astype(BF16)
    head0 = _lane_iota((tile, LANES)) < HEAD_DIM
    zero_r = jnp.zeros((2 * tile, 1), F32)

    def near(q0s, n_prev):
        q2s, kvss = [], []
        for q0 in q0s:
            q = q_ref[pl.ds(q0, tile), :]
            zq = jnp.zeros_like(q)
            q2s.append(jnp.concatenate([jnp.where(head0, q, zq), jnp.where(head0, zq, q)], axis=0))
            kvss.append([(k_ref[pl.ds(q0 - p * tile, tile), :], v_ref[pl.ds(q0 - p * tile, tile), :])
                         for p in range(n_prev + 1)])
        outs = _att_tiles(q2s, kvss, tri, [zero_r] * len(q0s), causal_first=True)
        return [(q2, acc, r) for q2, (acc, r) in zip(q2s, outs)]

    def far_and_store(q0, q2, acc, r, j_older):
        if j_older is not None:
            def cond(c):
                j, _, r_c = c
                return jnp.logical_and(j >= 0, jnp.min(r_c) < ATT_SKIP)

            def body(c):
                j, a_c, r_c = c
                k0 = pl.multiple_of(j * tile, tile)
                [(pv, dr)] = _att_tiles([q2], [[(k_ref[pl.ds(k0, tile), :], v_ref[pl.ds(k0, tile), :])]], tri,
                                        [r_c], causal_first=False)
                return j - 1, a_c + pv, r_c + dr

            _, acc, _ = lax.while_loop(cond, body, (j_older, acc, r))
        o_ref[pl.ds(q0, tile), :] = jnp.where(head0, acc[:tile], acc[tile:])

    n_prev = ATT_KEYS_AHEAD // tile
    group = ATT_TILES_PER_STEP
    first = n_prev + (nq - n_prev) % group
    for i in range(first):
        [part] = near([i * tile], min(i, n_prev))
        far_and_store(i * tile, *part, None if i <= n_prev else i - n_prev - 1)

    def later(g, _):
        tiles = [first + g * group + t for t in range(group)]
        q0s = [pl.multiple_of(i * tile, tile) for i in tiles]
        for i, q0, part in zip(tiles, q0s, near(q0s, n_prev)):
            far_and_store(q0, *part, i - n_prev - 1)
        return 0

    lax.fori_loop(0, (nq - first) // group, later, 0)


def _attention(q, k, v, bsz, t_len):
    n, d_att = q.shape
    spec = pl.BlockSpec((t_len, LANES), lambda b, p: (b, p))
    return pl.pallas_call(
        functools.partial(_attn_kernel, t_len=t_len, tile=ATT_TILE),
        grid=(bsz, d_att // LANES),
        in_specs=[spec, spec, spec],
        out_specs=spec,
        out_shape=jax.ShapeDtypeStruct((n, d_att), F32),
        compiler_params=_cparams("parallel", "parallel"),
        name="attn",
    )(q, k, v)


def _rms(t, gain):
    return t * lax.rsqrt(jnp.mean(t * t, axis=-1, keepdims=True) + EPS) * gain


def _gelu_tanh(y):
    return 0.5 * y * (1.0 + jnp.tanh(math.sqrt(2.0 / math.pi) * (y + 0.044715 * (y * y * y))))


def _pack_bf16_pairs(a, b):
    ua = pltpu.bitcast(a.astype(BF16).astype(F32), jnp.uint32)
    ub = pltpu.bitcast(b.astype(BF16).astype(F32), jnp.uint32)
    return ua | (ub >> 16)


def _unpack_bf16_pairs(w):
    return pltpu.bitcast(w & jnp.uint32(0xFFFF0000), F32), pltpu.bitcast(w << 16, F32)


def _post_kernel(x_ref, ys_ref, ya_ref, wglu_ref, bglu_ref, gs_ref, ga_ref, wo_ref, gf_ref,
                 wr_ref, br_ref, x1_ref, h2_ref, route_ref, route_t_ref, cnt_ref, run_ref, *, d_ssm):
    i = pl.program_id(0)

    @pl.when(i == 0)
    def _():
        run_ref[...] = jnp.zeros_like(run_ref)

    tm = x_ref.shape[0]
    groups = [pl.ds(g * (tm // POST_ROW_GROUPS), tm // POST_ROW_GROUPS) for g in range(POST_ROW_GROUPS)]
    ys = [_gelu_tanh(ys_ref[rows, :]) for rows in groups]
    gates = [jnp.dot(y.astype(BF16), wglu_ref[...], preferred_element_type=F32) + bglu_ref[...] for y in ys]
    m_as = [_rms(ya_ref[rows, :], ga_ref[...]).astype(BF16) for rows in groups]
    m_ss = [_rms(y * jax.nn.sigmoid(gate), gs_ref[...]).astype(BF16) for y, gate in zip(ys, gates)]
    mixes = [jnp.dot(m_s, wo_ref[:d_ssm, :], preferred_element_type=F32)
             + jnp.dot(m_a, wo_ref[d_ssm:, :], preferred_element_type=F32) for m_s, m_a in zip(m_ss, m_as)]
    x1s = [x_ref[rows, :] + mix for rows, mix in zip(groups, mixes)]
    h2s = [_rms(x1, gf_ref[...]) for x1 in x1s]
    logits = [jnp.dot(h2.astype(BF16), wr_ref[...], preferred_element_type=F32) for h2 in h2s]
    half = x_ref.shape[1] // 2
    for rows, x1, h2 in zip(groups, x1s, h2s):
        x1_ref[rows, :] = x1
        h2_ref[rows, :] = _pack_bf16_pairs(h2[:, :half], h2[:, half:])
    lg = jnp.concatenate(logits, axis=0) + br_ref[...]
    lane = _lane_iota(lg.shape).astype(F32)
    neg = -jnp.inf
    first = lambda hit: jnp.min(jnp.where(hit, lane, float(LANES)), axis=-1, keepdims=True)
    glog = jnp.where(lane < N_EXPERT_GROUPS, lg, neg)
    gmax = jnp.max(glog, axis=-1, keepdims=True)
    p_grp = 1.0 / jnp.sum(jnp.exp(glog - gmax), axis=-1, keepdims=True)
    grp = first(glog == gmax)
    e0 = ROUTER_LANE0 + grp * EXPERTS_PER_GROUP
    elog = jnp.where(jnp.logical_and(lane >= e0, lane < e0 + EXPERTS_PER_GROUP), lg, neg)
    m1 = jnp.max(elog, axis=-1, keepdims=True)
    i1 = first(elog == m1)
    elog2 = jnp.where(lane == i1, neg, elog)
    m2 = jnp.max(elog2, axis=-1, keepdims=True)
    i2 = first(elog2 == m2)
    e21 = jnp.exp(m2 - m1)
    g1 = p_grp * (1.0 / (1.0 + e21))
    g2 = p_grp * (e21 / (1.0 + e21))

    sel1 = lane == i1
    sel2 = lane == i2
    onehot = jnp.where(jnp.logical_or(sel1, sel2), 1.0, 0.0)
    r_i = lax.broadcasted_iota(jnp.int32, (tm, tm), 0)
    c_i = lax.broadcasted_iota(jnp.int32, (tm, tm), 1)
    lower = jnp.where(c_i < r_i, 1.0, 0.0).astype(BF16)
    before = jnp.dot(lower, onehot.astype(BF16), preferred_element_type=F32) + run_ref[0:1, :]
    rank1 = jnp.sum(jnp.where(sel1, before, 0.0), axis=-1, keepdims=True)
    rank2 = jnp.sum(jnp.where(sel2, before, 0.0), axis=-1, keepdims=True)
    run_ref[0:1, :] = run_ref[0:1, :] + jnp.sum(onehot, axis=0, keepdims=True)
    cnt_ref[...] = run_ref[...]

    fields = (i1 - ROUTER_LANE0, i2 - ROUTER_LANE0, g1, g2, rank1, rank2)
    route = jnp.zeros(lg.shape, F32)
    for pos, val in enumerate(fields):
        route = jnp.where(lane == pos, val, route)
    route_ref[...] = route
    route_t_ref[...] = route.T[:ROUTE_FIELDS, :]


def _post(x2, y_ssm_t, y_att, w_glu_bf, b_glu, g_ssm, g_att, w_out_bf, g_ffn, w_r_bf, b_r, part):
    n, d = x2.shape
    d_ssm = w_glu_bf.shape[0]
    tm = POST_TM
    nt = y_ssm_t.shape[0] // tm
    steps = n // (tm * MOE_PARTS)
    i0 = part * steps
    row_in = lambda w: pl.BlockSpec((tm, w), lambda i: (i0 + i, 0))
    row = lambda w: pl.BlockSpec((tm, w), lambda i: (i, 0))
    ssm_spec = pl.BlockSpec((tm, d_ssm), lambda i: ((i0 + i) % nt, (i0 + i) // nt))
    full = lambda a: pl.BlockSpec(a.shape, lambda i: (0,) * a.ndim)
    cnt_spec = pl.BlockSpec((8, LANES), lambda i: (0, 0))
    n = n // MOE_PARTS
    return pl.pallas_call(
        functools.partial(_post_kernel, d_ssm=d_ssm),
        grid=(steps,),
        in_specs=[row_in(d), ssm_spec, row_in(y_att.shape[1]), full(w_glu_bf), full(b_glu), full(g_ssm),
                  full(g_att), full(w_out_bf), full(g_ffn), full(w_r_bf), full(b_r)],
        out_specs=[row(d), row(d // 2), row(LANES), pl.BlockSpec((ROUTE_FIELDS, tm), lambda i: (0, i)), cnt_spec],
        out_shape=[jax.ShapeDtypeStruct((n, d), F32), jax.ShapeDtypeStruct((n, d // 2), jnp.uint32),
                   jax.ShapeDtypeStruct((n, LANES), F32), jax.ShapeDtypeStruct((ROUTE_FIELDS, n), F32),
                   jax.ShapeDtypeStruct((8, LANES), F32)],
        scratch_shapes=[pltpu.VMEM((8, LANES), F32)],
        compiler_params=_cparams("arbitrary"),
        name="post",
    )(x2, y_ssm_t, y_att, w_glu_bf, b_glu, g_ssm, g_att, w_out_bf, g_ffn, w_r_bf, b_r)


def _dispatch_sc(d0, d1, h2p, n_rows):
    n, w = h2p.shape
    workers = SC_CORES * SC_SUBCORES
    n_win = n // (workers * SC_WINDOW)
    mesh = plsc.VectorSubcoreMesh(core_axis_name="c", subcore_axis_name="s")

    def body(h_hbm, d0_hbm, d1_hbm, o_hbm, rows_v, i0_v, i1_v):
        wid = lax.axis_index("c") * SC_SUBCORES + lax.axis_index("s")

        @pl.loop(0, n_win)
        def _(j):
            blk = wid * n_win + j
            pltpu.sync_copy(h_hbm.at[pl.ds(blk * SC_WINDOW, SC_WINDOW)], rows_v)
            pltpu.sync_copy(d0_hbm.at[blk], i0_v)
            pltpu.sync_copy(d1_hbm.at[blk], i1_v)
            pltpu.sync_copy(rows_v, o_hbm.at[i0_v])
            pltpu.sync_copy(rows_v, o_hbm.at[i1_v])

    return pl.kernel(
        body,
        out_type=jax.ShapeDtypeStruct((n_rows, w), h2p.dtype),
        mesh=mesh,
        scratch_types=[pltpu.VMEM((SC_WINDOW, w), h2p.dtype), pltpu.VMEM((SC_WINDOW,), jnp.int32),
                       pltpu.VMEM((SC_WINDOW,), jnp.int32)],
        name="dispatch_sc",
    )(h2p, d0, d1)


def _experts_kernel(blk_e_ref, next_e_ref, used_ref, x_ref, wg_hbm, wu_hbm, wd_hbm, o_ref,
                    wg_f32, wu_f32, wd_f32, wg_bf, wu_bf, wd_bf, slot_ref, sems):
    staged = ((wg_hbm, wg_f32, wg_bf), (wu_hbm, wu_f32, wu_bf), (wd_hbm, wd_f32, wd_bf))

    def fetch(expert, slot):
        return [pltpu.make_async_copy(hbm.at[expert], f32.at[slot], sems.at[slot, k])
                for k, (hbm, f32, _) in enumerate(staged)]

    @pl.when(pl.program_id(0) == 0)
    def _():
        slot_ref[0] = 0
        for cp in fetch(blk_e_ref[0], 0):
            cp.start()

    for s in range(MOE_BLOCKS_PER_STEP):
        i = pl.program_id(0) * MOE_BLOCKS_PER_STEP + s
        rows = pl.ds(s * MOE_ROWS, MOE_ROWS)

        @pl.when(jnp.logical_or(i == 0, blk_e_ref[i] != blk_e_ref[jnp.maximum(i - 1, 0)]))
        def _():
            slot = slot_ref[0]
            for cp in fetch(blk_e_ref[i], slot):
                cp.wait()
            for _, f32, bf in staged:
                bf[...] = f32[slot].astype(BF16)
            nxt = next_e_ref[i]

            @pl.when(nxt >= 0)
            def _():
                for cp in fetch(nxt, 1 - slot):
                    cp.start()

            slot_ref[0] = 1 - slot

        @pl.when(i < used_ref[0])
        def _():
            xa, xb = (t.astype(BF16) for t in _unpack_bf16_pairs(x_ref[rows, :]))
            half = xa.shape[1]
            gate = (jnp.dot(xa, wg_bf[:half, :], preferred_element_type=F32)
                    + jnp.dot(xb, wg_bf[half:, :], preferred_element_type=F32))
            up = (jnp.dot(xa, wu_bf[:half, :], preferred_element_type=F32)
                  + jnp.dot(xb, wu_bf[half:, :], preferred_element_type=F32))
            hid = (jax.nn.silu(gate) * up).astype(BF16)
            out = jnp.dot(hid, wd_bf[...], preferred_element_type=F32)
            o_ref[rows, :] = _pack_bf16_pairs(out[:, :half], out[:, half:])

        @pl.when(i >= used_ref[0])
        def _():
            o_ref[rows, :] = jnp.zeros((MOE_ROWS, o_ref.shape[1]), o_ref.dtype)


def _experts(blk_e, next_e, n_used, buf, w_gate, w_up, w_down):
    n_rows, w = buf.shape
    d = w_down.shape[2]
    hbm = pl.BlockSpec(memory_space=pl.ANY)
    weights = (w_gate, w_up, w_down)
    step_rows = MOE_ROWS * MOE_BLOCKS_PER_STEP
    grid_spec = pltpu.PrefetchScalarGridSpec(
        num_scalar_prefetch=3,
        grid=(n_rows // step_rows,),
        in_specs=[pl.BlockSpec((step_rows, w), lambda i, be, ne, nu: (i, 0)), hbm, hbm, hbm],
        out_specs=pl.BlockSpec((step_rows, d // 2), lambda i, be, ne, nu: (i, 0)),
        scratch_shapes=([pltpu.VMEM((2,) + a.shape[1:], F32) for a in weights]
                        + [pltpu.VMEM(a.shape[1:], BF16) for a in weights]
                        + [pltpu.SMEM((1,), jnp.int32), pltpu.SemaphoreType.DMA((2, len(weights)))]),
    )
    return pl.pallas_call(
        _experts_kernel,
        grid_spec=grid_spec,
        out_shape=jax.ShapeDtypeStruct((n_rows, d // 2), jnp.uint32),
        compiler_params=_cparams("arbitrary"),
        name="experts",
    )(blk_e, next_e, n_used, buf, w_gate, w_up, w_down)


def _gather_sc(d0, d1, eo):
    n = d0.size
    w = eo.shape[1]
    workers = SC_CORES * SC_SUBCORES
    n_win = n // (workers * SC_WINDOW)
    mesh = plsc.VectorSubcoreMesh(core_axis_name="c", subcore_axis_name="s")

    def body(eo_hbm, d0_hbm, d1_hbm, o0_hbm, o1_hbm, rows_v, i_v):
        wid = lax.axis_index("c") * SC_SUBCORES + lax.axis_index("s")

        @pl.loop(0, n_win)
        def _(j):
            blk = wid * n_win + j
            for d_hbm, o_hbm in ((d0_hbm, o0_hbm), (d1_hbm, o1_hbm)):
                pltpu.sync_copy(d_hbm.at[blk], i_v)
                pltpu.sync_copy(eo_hbm.at[i_v], rows_v)
                pltpu.sync_copy(rows_v, o_hbm.at[pl.ds(blk * SC_WINDOW, SC_WINDOW)])

    out_sd = jax.ShapeDtypeStruct((n, w), eo.dtype)
    return pl.kernel(
        body,
        out_type=(out_sd, out_sd),
        mesh=mesh,
        scratch_types=[pltpu.VMEM((SC_WINDOW, w), eo.dtype), pltpu.VMEM((SC_WINDOW,), jnp.int32)],
        name="gather_sc",
    )(eo, d0, d1)


def _combine_kernel(x1_ref, route_ref, r0_ref, r1_ref, *rest):
    o_ref = rest[-1]
    route = route_ref[...]
    row0 = jnp.concatenate(_unpack_bf16_pairs(r0_ref[...]), axis=1)
    row1 = jnp.concatenate(_unpack_bf16_pairs(r1_ref[...]), axis=1)
    o_ref[...] = x1_ref[...] + (route[:, 2:3] * row0 + route[:, 3:4] * row1)


def _combine(x1, route, rows0, rows1, part, out_prev):
    n_slice, d = x1.shape
    tm = COMBINE_TM
    steps = n_slice // tm
    row = lambda w: pl.BlockSpec((tm, w), lambda i: (i, 0))
    in_specs = [row(d), row(LANES), row(d // 2), row(d // 2)]
    args = [x1, route, rows0, rows1]
    aliases = {}
    if out_prev is not None:
        in_specs.append(pl.BlockSpec(memory_space=pl.ANY))
        args.append(out_prev)
        aliases = {len(args) - 1: 0}
    return pl.pallas_call(
        _combine_kernel,
        grid=(steps,),
        in_specs=in_specs,
        out_specs=pl.BlockSpec((tm, d), lambda i: (part * steps + i, 0)),
        out_shape=jax.ShapeDtypeStruct((n_slice * MOE_PARTS, d), F32),
        input_output_aliases=aliases,
        compiler_params=_cparams("parallel"),
        name="combine",
    )(*args)


def _layer(x, g_mix, w_in, lam_re, lam_im, log_dt, b_re, b_im, c_re, c_im, d_skip, w_glu, b_glu, g_q, g_k,
           g_ssm_out, g_attn_out, w_out, g_ffn, w_rg, b_rg, w_re, b_re_router, w_gate, w_up, w_down):
    bsz, t_len, d = x.shape
    n = bsz * t_len
    d_ssm = w_glu.shape[0]
    d_att = g_attn_out.shape[0]
    n_heads = d_att // HEAD_DIM
    n_chunks = t_len // SSM_CHUNK
    x2 = x.reshape(n, d)

    u_t, q, k, v = _in_proj(x, g_mix[None], w_in.astype(BF16), jnp.tile(g_q, n_heads)[None],
                            jnp.tile(g_k, n_heads)[None], d_ssm, d_att)
    tables = _s5_tables(lam_re, lam_im, log_dt, b_re, b_im, c_re, c_im, d_skip)
    y_ssm_t = _s5(u_t.reshape(n_chunks, SSM_CHUNK, bsz, d_ssm), tables).reshape(t_len, bsz * d_ssm)

    y_att = _attention(q.reshape(n, d_att), k.reshape(n, d_att), v.reshape(n, d_att), bsz, t_len)

    w_r = jnp.concatenate([w_rg, w_re.reshape(d, N_EXPERTS)], axis=1)
    w_r = jnp.pad(w_r, ((0, 0), (0, LANES - w_r.shape[1]))).astype(BF16)
    b_r = jnp.pad(jnp.concatenate([b_rg, b_re_router.reshape(N_EXPERTS)]), (0, LANES - ROUTER_LANE0 - N_EXPERTS))[None]
    w_glu_bf, w_out_bf = w_glu.astype(BF16), w_out.astype(BF16)
    n_slice = n // MOE_PARTS
    n_rows = n_slice * 2 + N_EXPERTS * MOE_ROWS
    n_blk = n_rows // MOE_ROWS
    out = None
    for part in range(MOE_PARTS):
        x1, h2p, route, route_t, cnt = _post(x2, y_ssm_t, y_att, w_glu_bf, b_glu[None], g_ssm_out[None],
                                             g_attn_out[None], w_out_bf, g_ffn[None], w_r, b_r, part)
        counts = cnt[0, ROUTER_LANE0:ROUTER_LANE0 + N_EXPERTS].astype(jnp.int32)
        pcounts = ((counts + MOE_ROWS - 1) // MOE_ROWS) * MOE_ROWS
        pends = jnp.cumsum(pcounts)
        pstarts = pends - pcounts
        e_ids = jnp.arange(N_EXPERTS, dtype=jnp.int32)
        dests = []
        for k in range(2):
            expert = route_t[k].astype(jnp.int32)
            start = jnp.sum(jnp.where(expert[:, None] == e_ids, pstarts, 0), axis=-1)
            dest = start + route_t[4 + k].astype(jnp.int32)
            dests.append(dest.reshape(n_slice // SC_WINDOW, SC_WINDOW))
        blk_row0 = jnp.arange(n_blk, dtype=jnp.int32)[:, None] * MOE_ROWS
        blk_e = jnp.minimum(jnp.sum((pends[None, :] <= blk_row0).astype(jnp.int32), axis=1), N_EXPERTS - 1)
        n_used = (pends[-1:] // MOE_ROWS).astype(jnp.int32)
        present = jnp.any(blk_e[:, None] == e_ids, axis=0)
        later = jnp.logical_and(present[None, :], e_ids[None, :] > blk_e[:, None])
        next_e = jnp.min(jnp.where(later, e_ids[None, :], N_EXPERTS), axis=1)
        next_e = jnp.where(next_e == N_EXPERTS, -1, next_e).astype(jnp.int32)

        buf = _dispatch_sc(*dests, h2p, n_rows)
        eo = _experts(blk_e, next_e, n_used, buf, w_gate, w_up, w_down)
        out = _combine(x1, route, *_gather_sc(*dests, eo), part, out)
    return out.reshape(bsz, t_len, d)


def kernel(x, g_mix, w_in, ssm_lambda_re, ssm_lambda_im, ssm_log_dt, ssm_b_re, ssm_b_im, ssm_c_re, ssm_c_im, ssm_d, ssm_w_glu, ssm_b_glu, g_q, g_k, g_ssm_out, g_attn_out, w_out, g_ffn, w_router_group, b_router_group, w_router_expert, b_router_expert, w_gate, w_up, w_down):
    for l in range(g_mix.shape[0]):
        x = _layer(x, g_mix[l], w_in[l], ssm_lambda_re[l], ssm_lambda_im[l], ssm_log_dt[l], ssm_b_re[l],
                   ssm_b_im[l], ssm_c_re[l], ssm_c_im[l], ssm_d[l], ssm_w_glu[l], ssm_b_glu[l], g_q[l], g_k[l],
                   g_ssm_out[l], g_attn_out[l], w_out[l], g_ffn[l], w_router_group[l], b_router_group[l],
                   w_router_expert[l], b_router_expert[l], w_gate[l], w_up[l], w_down[l])
    return x
```

```python
import functools
import math

import jax
import jax.numpy as jnp
from jax import lax
from jax.experimental import pallas as pl
from jax.experimental.pallas import tpu as pltpu
from jax.experimental.pallas import tpu_sc as plsc

F32 = jnp.float32
BF16 = jnp.bfloat16
EPS = 1e-6

LANES = 128
VMEM_LIMIT_BYTES = 56 * 1024 * 1024

SSM_GROUP = 16
SSM_CHUNK = 16
HEAD_DIM = 64
N_EXPERT_GROUPS = 4
EXPERTS_PER_GROUP = 8
N_EXPERTS = N_EXPERT_GROUPS * EXPERTS_PER_GROUP
ROUTER_LANE0 = N_EXPERT_GROUPS
ROUTE_FIELDS = 8
MOE_ROWS = 512
MOE_PARTS = 2
MOE_BLOCKS_PER_STEP = 8
ATT_SKIP = 110.0

S5_GPB = LANES // SSM_GROUP
S5_CHUNKS_PER_STEP = 32

IN_TM = 1024
IN_BATCHES = 8
ATT_TILE = 256
ATT_KEYS_AHEAD = 256
ATT_TILES_PER_STEP = 6
POST_TM = 1024
POST_ROW_GROUPS = 2
COMBINE_TM = 1024
SC_CORES = 2
SC_SUBCORES = 16
SC_WINDOW = 64


def _cparams(*sem):
    return pltpu.CompilerParams(dimension_semantics=sem, vmem_limit_bytes=VMEM_LIMIT_BYTES)


def _lane_iota(shape):
    return lax.broadcasted_iota(jnp.int32, shape, len(shape) - 1)


def _head_rms(t, gain):
    outs = []
    for c in range(t.shape[1] // LANES):
        blk = t[:, c * LANES:(c + 1) * LANES]
        sq = blk * blk
        lo = _lane_iota(blk.shape) < HEAD_DIM
        s_lo = jnp.sum(jnp.where(lo, sq, 0.0), axis=-1, keepdims=True)
        s_hi = jnp.sum(jnp.where(lo, 0.0, sq), axis=-1, keepdims=True)
        inv = jnp.where(lo, lax.rsqrt(s_lo * (1.0 / HEAD_DIM) + EPS),
                        lax.rsqrt(s_hi * (1.0 / HEAD_DIM) + EPS))
        outs.append(blk * inv * gain[:, c * LANES:(c + 1) * LANES])
    return jnp.concatenate(outs, axis=-1)


def _in_proj_kernel(x_ref, g_ref, w_ref, gq_ref, gk_ref, u_ref, q_ref, k_ref, v_ref, *, d_ssm, d_att, scale):
    nb, tt, d = x_ref.shape
    x = x_ref[...].reshape(nb * tt, d)
    inv = lax.rsqrt(jnp.mean(x * x, axis=-1, keepdims=True) + EPS)
    h = (x * inv * g_ref[...]).astype(BF16)
    proj = jnp.dot(h, w_ref[...], preferred_element_type=F32)
    u_ref[...] = pltpu.einshape("btc->tbc", proj[:, :d_ssm].astype(BF16).reshape(nb, tt, d_ssm))
    q = _head_rms(proj[:, d_ssm:d_ssm + d_att], gq_ref[...])
    k = _head_rms(proj[:, d_ssm + d_att:d_ssm + 2 * d_att], gk_ref[...])
    q_ref[...] = (q * scale).astype(BF16).reshape(nb, tt, d_att)
    k_ref[...] = k.astype(BF16).reshape(nb, tt, d_att)
    v_ref[...] = proj[:, d_ssm + 2 * d_att:].astype(BF16).reshape(nb, tt, d_att)


def _in_proj(x, g_mix, w_in_bf, gq_t, gk_t, d_ssm, d_att):
    bsz, t_len, d = x.shape
    nb = IN_BATCHES
    tt = IN_TM // nb
    tok = lambda w: pl.BlockSpec((nb, tt, w), lambda b, t: (b, t, 0))
    full = lambda a: pl.BlockSpec(a.shape, lambda b, t: (0,) * a.ndim)
    out_sd = jax.ShapeDtypeStruct((bsz, t_len, d_att), BF16)
    return pl.pallas_call(
        functools.partial(_in_proj_kernel, d_ssm=d_ssm, d_att=d_att, scale=1.0 / math.sqrt(HEAD_DIM)),
        grid=(bsz // nb, t_len // tt),
        in_specs=[tok(d), full(g_mix), full(w_in_bf), full(gq_t), full(gk_t)],
        out_specs=[pl.BlockSpec((tt, nb, d_ssm), lambda b, t: (t, b, 0)), tok(d_att), tok(d_att), tok(d_att)],
        out_shape=[jax.ShapeDtypeStruct((t_len, bsz, d_ssm), BF16), out_sd, out_sd, out_sd],
        compiler_params=_cparams("parallel", "parallel"),
        name="in_proj",
    )(x, g_mix, w_in_bf, gq_t, gk_t)


def _s5_tables(lam_re, lam_im, log_dt, b_re, b_im, c_re, c_im, d_skip):
    hp = lax.Precision.HIGHEST
    L = SSM_CHUNK
    g_n, p_n = lam_re.shape
    dt = jnp.exp(log_dt)[:, None]
    lr, li = lam_re, lam_im
    ls = jnp.arange(L + 1, dtype=F32)[:, None, None]
    mag = jnp.exp(lr * dt * ls)
    pr, pi = mag * jnp.cos(li * dt * ls), mag * jnp.sin(li * dt * ls)
    abar_r, abar_i = pr[1], pi[1]
    den = lr * lr + li * li
    nr, ni = abar_r - 1.0, abar_i
    coef_r = (nr * lr + ni * li) / den
    coef_i = (ni * lr - nr * li) / den
    bbr = coef_r[..., None] * b_re - coef_i[..., None] * b_im
    bbi = coef_r[..., None] * b_im + coef_i[..., None] * b_re
    wr = pr[..., None] * bbr - pi[..., None] * bbi
    wi = pr[..., None] * bbi + pi[..., None] * bbr
    kl = (jnp.einsum('gop,lgpi->lgoi', c_re, wr[:L], precision=hp)
          - jnp.einsum('gop,lgpi->lgoi', c_im, wi[:L], precision=hp))
    kl = kl.at[0].add(jax.vmap(jnp.diag)(d_skip))
    n_lb = g_n // S5_GPB
    sg = SSM_GROUP
    kc = kl.transpose(1, 3, 0, 2).reshape(n_lb, S5_GPB * sg, L * sg)
    b_rows = lambda w: (w[:L][::-1].reshape(L, n_lb, S5_GPB, p_n, sg).transpose(1, 0, 2, 4, 3)
                        .reshape(n_lb, L * S5_GPB * sg, p_n))
    p1r, p1i = pr[1:], pi[1:]
    cst_r = (c_re[None] * p1r[:, :, None, :] - c_im[None] * p1i[:, :, None, :])
    cst_i = -(c_re[None] * p1i[:, :, None, :] + c_im[None] * p1r[:, :, None, :])
    c_rows = lambda c: c.transpose(1, 3, 0, 2).reshape(n_lb, S5_GPB * p_n, L * sg)
    a_l = jnp.stack([pr[L].reshape(n_lb, S5_GPB * p_n), pi[L].reshape(n_lb, S5_GPB * p_n)], axis=1)
    return (kc.astype(BF16), b_rows(wr).astype(BF16), b_rows(wi).astype(BF16),
            c_rows(cst_r).astype(BF16), c_rows(cst_i).astype(BF16), a_l)


def _s5_expand_tables(kc_ref, bcr_ref, bci_ref, ccr_ref, cci_ref, w0_ref, br_ref, bi_ref, cr_ref, ci_ref):
    sg, gpb = SSM_GROUP, S5_GPB
    n_lo = w0_ref.shape[1]
    p_n = bcr_ref.shape[2]

    def iotas(shape):
        return lax.broadcasted_iota(jnp.int32, shape, 0), lax.broadcasted_iota(jnp.int32, shape, 1)

    r, c = iotas((kc_ref.shape[2], n_lo))
    spread_o = jnp.where(jnp.logical_and(r // sg == c // LANES, r % sg == c % sg), 1.0, 0.0).astype(BF16)
    r, c = iotas((p_n, gpb * p_n))
    spread_p = jnp.where(r == c % p_n, 1.0, 0.0).astype(BF16)

    def expand(compact, spread, row_group, col_group):
        full = jnp.dot(compact, spread, preferred_element_type=F32)
        r, c = iotas(full.shape)
        return jnp.where(row_group(r) == col_group(c), full, 0.0).astype(BF16)

    lane_group = lambda c: (c % LANES) // sg
    top = expand(kc_ref[0], spread_o, lambda r: r // sg, lane_group)
    w0_ref[:LANES, :] = top
    w0_ref[LANES:, :LANES] = jnp.zeros((LANES, LANES), BF16)
    w0_ref[LANES:, LANES:] = top[:, :n_lo - LANES]
    state_group = lambda c: c // p_n
    br_ref[...] = expand(bcr_ref[0], spread_p, lambda r: (r // sg) % gpb, state_group)
    bi_ref[...] = expand(bci_ref[0], spread_p, lambda r: (r // sg) % gpb, state_group)
    cr_ref[...] = expand(ccr_ref[0], spread_o, lambda r: r // p_n, lane_group)
    ci_ref[...] = expand(cci_ref[0], spread_o, lambda r: r // p_n, lane_group)


def _s5_kernel(u_ref, kc_ref, bcr_ref, bci_ref, ccr_ref, cci_ref, a_ref, y_ref,
               hr_ref, hi_ref, acc_ref, w0_ref, br_ref, bi_ref, cr_ref, ci_ref):
    n_chunks, L, bsz, _ = u_ref.shape
    rows = n_chunks * bsz

    @pl.when(pl.program_id(1) == 0)
    def _():
        hr_ref[...] = jnp.zeros_like(hr_ref)
        hi_ref[...] = jnp.zeros_like(hi_ref)
        _s5_expand_tables(kc_ref, bcr_ref, bci_ref, ccr_ref, cci_ref, w0_ref, br_ref, bi_ref, cr_ref, ci_ref)

    us = [u_ref[:, s].reshape(rows, LANES) for s in range(L)]
    lhs = jnp.concatenate(us, axis=1)
    sin_r = jnp.dot(lhs, br_ref[...], preferred_element_type=F32)
    sin_i = jnp.dot(lhs, bi_ref[...], preferred_element_type=F32)
    ar = a_ref[0, 0:1, :]
    ai = a_ref[0, 1:2, :]
    hr, hi = hr_ref[...], hi_ref[...]
    prev_r, prev_i = [], []
    for c in range(n_chunks):
        prev_r.append(hr)
        prev_i.append(hi)
        sl = slice(c * bsz, (c + 1) * bsz)
        hr, hi = ar * hr - ai * hi + sin_r[sl], ar * hi + ai * hr + sin_i[sl]
    hr_ref[...] = hr
    hi_ref[...] = hi
    pr = jnp.concatenate(prev_r, axis=0).astype(BF16)
    pi = jnp.concatenate(prev_i, axis=0).astype(BF16)
    acc_ref[...] = (jnp.dot(pr, cr_ref[...], preferred_element_type=F32)
                    + jnp.dot(pi, ci_ref[...], preferred_element_type=F32))
    for p in range(L // 2):
        off = 2 * p * LANES
        pair = jnp.concatenate([us[2 * p], us[2 * p + 1]], axis=1)
        acc_ref[:, off:] += jnp.dot(pair, w0_ref[:, :L * LANES - off], preferred_element_type=F32)
    for t in range(L):
        y_ref[:, t] = acc_ref[:, t * LANES:(t + 1) * LANES].reshape(n_chunks, bsz, LANES)


def _s5(u4, tables):
    n_chunks, L, bsz, d_ssm = u4.shape
    a_l = tables[-1]
    cb = S5_CHUNKS_PER_STEP
    data = pl.BlockSpec((cb, L, bsz, LANES), lambda lb, c: (c, 0, 0, lb))
    per_lb = lambda a: pl.BlockSpec((1,) + a.shape[1:], lambda lb, c: (lb,) + (0,) * (a.ndim - 1))
    n_state = a_l.shape[2]
    state = pltpu.VMEM((bsz, n_state), F32)
    n_lo = L * LANES
    return pl.pallas_call(
        _s5_kernel,
        grid=(d_ssm // LANES, n_chunks // cb),
        in_specs=[data] + [per_lb(t) for t in tables],
        out_specs=data,
        out_shape=jax.ShapeDtypeStruct(u4.shape, F32),
        scratch_shapes=[state, state, pltpu.VMEM((cb * bsz, n_lo), F32),
                        pltpu.VMEM((2 * LANES, n_lo), BF16),
                        pltpu.VMEM((n_lo, n_state), BF16), pltpu.VMEM((n_lo, n_state), BF16),
                        pltpu.VMEM((n_state, n_lo), BF16), pltpu.VMEM((n_state, n_lo), BF16)],
        compiler_params=_cparams("arbitrary", "arbitrary"),
        name="s5",
    )(u4, *tables)


def _softplus(z):
    return jnp.maximum(z, 0.0) + jnp.log(1.0 + jnp.exp(-jnp.abs(z)))


def _att_tiles(q2s, kvss, tri, r_ins, causal_first):
    tq = q2s[0].shape[0] // 2
    n_t = len(q2s)
    contract_last = (((1,), (1,)), ((), ()))
    zs = [[lax.dot_general(q2, k, contract_last, preferred_element_type=F32) for k, _ in kvs]
          for q2, kvs in zip(q2s, kvss)]
    sps = [[_softplus(z) for z in row] for row in zs]
    sp_ms = [list(row) for row in sps]
    if causal_first:
        rows = lax.broadcasted_iota(jnp.int32, zs[0][0].shape, 0)
        cols = lax.broadcasted_iota(jnp.int32, zs[0][0].shape, 1)
        keep = cols < jnp.where(rows >= tq, rows - tq, rows)
        for t in range(n_t):
            sp_ms[t][0] = jnp.where(keep, sps[t][0], 0.0)
    stacked = jnp.concatenate([blk for row in sp_ms for blk in row], axis=0).astype(BF16)
    newer_all = jnp.dot(stacked, tri, preferred_element_type=F32)
    pvs, totals = [None] * n_t, list(r_ins)
    base = 0
    offsets = []
    for row in sp_ms:
        offsets.append(base)
        base += len(row) * 2 * tq
    for p in range(max(len(kvs) for kvs in kvss)):
        for t in range(n_t):
            if p >= len(kvss[t]):
                continue
            newer = newer_all[offsets[t] + p * 2 * tq:offsets[t] + (p + 1) * 2 * tq]
            att = jnp.exp(zs[t][p] - sps[t][p] - newer - totals[t])
            if causal_first and p == 0:
                att = jnp.where(keep, att, 0.0)
            part = jnp.dot(att.astype(BF16), kvss[t][p][1], preferred_element_type=F32)
            pvs[t] = part if pvs[t] is None else pvs[t] + part
            totals[t] = totals[t] + (newer[:, 0:1] + sp_ms[t][p][:, 0:1])
    return [(pv, total - r_in) for pv, total, r_in in zip(pvs, totals, r_ins)]


def _attn_kernel(q_ref, k_ref, v_ref, o_ref, *, t_len, tile):
    nq = t_len // tile
    r_i = lax.broadcasted_iota(jnp.int32, (tile, tile), 0)
    c_i = lax.broadcasted_iota(jnp.int32, (tile, tile), 1)
    tri = jnp.where(r_i > c_i, 1.0, 0.0).astype(BF16)
    head0 = _lane_iota((tile, LANES)) < HEAD_DIM
    zero_r = jnp.zeros((2 * tile, 1), F32)

    def near(q0s, n_prev):
        q2s, kvss = [], []
        for q0 in q0s:
            q = q_ref[pl.ds(q0, tile), :]
            zq = jnp.zeros_like(q)
            q2s.append(jnp.concatenate([jnp.where(head0, q, zq), jnp.where(head0, zq, q)], axis=0))
            kvss.append([(k_ref[pl.ds(q0 - p * tile, tile), :], v_ref[pl.ds(q0 - p * tile, tile), :])
                         for p in range(n_prev + 1)])
        outs = _att_tiles(q2s, kvss, tri, [zero_r] * len(q0s), causal_first=True)
        return [(q2, acc, r) for q2, (acc, r) in zip(q2s, outs)]

    def far_and_store(q0, q2, acc, r, j_older):
        if j_older is not None:
            def cond(c):
                j, _, r_c = c
                return jnp.logical_and(j >= 0, jnp.min(r_c) < ATT_SKIP)

            def body(c):
                j, a_c, r_c = c
                k0 = pl.multiple_of(j * tile, tile)
                [(pv, dr)] = _att_tiles([q2], [[(k_ref[pl.ds(k0, tile), :], v_ref[pl.ds(k0, tile), :])]], tri,
                                        [r_c], causal_first=False)
                return j - 1, a_c + pv, r_c + dr

            _, acc, _ = lax.while_loop(cond, body, (j_older, acc, r))
        o_ref[pl.ds(q0, tile), :] = jnp.where(head0, acc[:tile], acc[tile:])

    n_prev = ATT_KEYS_AHEAD // tile
    group = ATT_TILES_PER_STEP
    first = n_prev + (nq - n_prev) % group
    for i in range(first):
        [part] = near([i * tile], min(i, n_prev))
        far_and_store(i * tile, *part, None if i <= n_prev else i - n_prev - 1)

    def later(g, _):
        tiles = [first + g * group + t for t in range(group)]
        q0s = [pl.multiple_of(i * tile, tile) for i in tiles]
        for i, q0, part in zip(tiles, q0s, near(q0s, n_prev)):
            far_and_store(q0, *part, i - n_prev - 1)
        return 0

    lax.fori_loop(0, (nq - first) // group, later, 0)


def _attention(q, k, v, bsz, t_len):
    n, d_att = q.shape
    spec = pl.BlockSpec((t_len, LANES), lambda b, p: (b, p))
    return pl.pallas_call(
        functools.partial(_attn_kernel, t_len=t_len, tile=ATT_TILE),
        grid=(bsz, d_att // LANES),
        in_specs=[spec, spec, spec],
        out_specs=spec,
        out_shape=jax.ShapeDtypeStruct((n, d_att), F32),
        compiler_params=_cparams("parallel", "parallel"),
        name="attn",
    )(q, k, v)


def _rms(t, gain):
    return t * lax.rsqrt(jnp.mean(t * t, axis=-1, keepdims=True) + EPS) * gain


def _gelu_tanh(y):
    return 0.5 * y * (1.0 + jnp.tanh(math.sqrt(2.0 / math.pi) * (y + 0.044715 * (y * y * y))))


def _pack_bf16_pairs(a, b):
    ua = pltpu.bitcast(a.astype(BF16).astype(F32), jnp.uint32)
    ub = pltpu.bitcast(b.astype(BF16).astype(F32), jnp.uint32)
    return ua | (ub >> 16)


def _unpack_bf16_pairs(w):
    return pltpu.bitcast(w & jnp.uint32(0xFFFF0000), F32), pltpu.bitcast(w << 16, F32)


def _post_kernel(x_ref, ys_ref, ya_ref, wglu_ref, bglu_ref, gs_ref, ga_ref, wo_ref, gf_ref,
                 wr_ref, br_ref, x1_ref, h2_ref, route_ref, route_t_ref, cnt_ref, run_ref, *, d_ssm):
    i = pl.program_id(0)

    @pl.when(i == 0)
    def _():
        run_ref[...] = jnp.zeros_like(run_ref)

    tm = x_ref.shape[0]
    groups = [pl.ds(g * (tm // POST_ROW_GROUPS), tm // POST_ROW_GROUPS) for g in range(POST_ROW_GROUPS)]
    ys = [_gelu_tanh(ys_ref[rows, :]) for rows in groups]
    gates = [jnp.dot(y.astype(BF16), wglu_ref[...], preferred_element_type=F32) + bglu_ref[...] for y in ys]
    m_as = [_rms(ya_ref[rows, :], ga_ref[...]).astype(BF16) for rows in groups]
    m_ss = [_rms(y * jax.nn.sigmoid(gate), gs_ref[...]).astype(BF16) for y, gate in zip(ys, gates)]
    mixes = [jnp.dot(m_s, wo_ref[:d_ssm, :], preferred_element_type=F32)
             + jnp.dot(m_a, wo_ref[d_ssm:, :], preferred_element_type=F32) for m_s, m_a in zip(m_ss, m_as)]
    x1s = [x_ref[rows, :] + mix for rows, mix in zip(groups, mixes)]
    h2s = [_rms(x1, gf_ref[...]) for x1 in x1s]
    logits = [jnp.dot(h2.astype(BF16), wr_ref[...], preferred_element_type=F32) for h2 in h2s]
    half = x_ref.shape[1] // 2
    for rows, x1, h2 in zip(groups, x1s, h2s):
        x1_ref[rows, :] = x1
        h2_ref[rows, :] = _pack_bf16_pairs(h2[:, :half], h2[:, half:])
    lg = jnp.concatenate(logits, axis=0) + br_ref[...]
    lane = _lane_iota(lg.shape).astype(F32)
    neg = -jnp.inf
    first = lambda hit: jnp.min(jnp.where(hit, lane, float(LANES)), axis=-1, keepdims=True)
    glog = jnp.where(lane < N_EXPERT_GROUPS, lg, neg)
    gmax = jnp.max(glog, axis=-1, keepdims=True)
    p_grp = 1.0 / jnp.sum(jnp.exp(glog - gmax), axis=-1, keepdims=True)
    grp = first(glog == gmax)
    e0 = ROUTER_LANE0 + grp * EXPERTS_PER_GROUP
    elog = jnp.where(jnp.logical_and(lane >= e0, lane < e0 + EXPERTS_PER_GROUP), lg, neg)
    m1 = jnp.max(elog, axis=-1, keepdims=True)
    i1 = first(elog == m1)
    elog2 = jnp.where(lane == i1, neg, elog)
    m2 = jnp.max(elog2, axis=-1, keepdims=True)
    i2 = first(elog2 == m2)
    e21 = jnp.exp(m2 - m1)
    g1 = p_grp * (1.0 / (1.0 + e21))
    g2 = p_grp * (e21 / (1.0 + e21))

    sel1 = lane == i1
    sel2 = lane == i2
    onehot = jnp.where(jnp.logical_or(sel1, sel2), 1.0, 0.0)
    r_i = lax.broadcasted_iota(jnp.int32, (tm, tm), 0)
    c_i = lax.broadcasted_iota(jnp.int32, (tm, tm), 1)
    lower = jnp.where(c_i < r_i, 1.0, 0.0).astype(BF16)
    before = jnp.dot(lower, onehot.astype(BF16), preferred_element_type=F32) + run_ref[0:1, :]
    rank1 = jnp.sum(jnp.where(sel1, before, 0.0), axis=-1, keepdims=True)
    rank2 = jnp.sum(jnp.where(sel2, before, 0.0), axis=-1, keepdims=True)
    run_ref[0:1, :] = run_ref[0:1, :] + jnp.sum(onehot, axis=0, keepdims=True)
    cnt_ref[...] = run_ref[...]

    fields = (i1 - ROUTER_LANE0, i2 - ROUTER_LANE0, g1, g2, rank1, rank2)
    route = jnp.zeros(lg.shape, F32)
    for pos, val in enumerate(fields):
        route = jnp.where(lane == pos, val, route)
    route_ref[...] = route
    route_t_ref[...] = route.T[:ROUTE_FIELDS, :]


def _post(x2, y_ssm_t, y_att, w_glu_bf, b_glu, g_ssm, g_att, w_out_bf, g_ffn, w_r_bf, b_r, part):
    n, d = x2.shape
    d_ssm = w_glu_bf.shape[0]
    tm = POST_TM
    nt = y_ssm_t.shape[0] // tm
    steps = n // (tm * MOE_PARTS)
    i0 = part * steps
    row_in = lambda w: pl.BlockSpec((tm, w), lambda i: (i0 + i, 0))
    row = lambda w: pl.BlockSpec((tm, w), lambda i: (i, 0))
    ssm_spec = pl.BlockSpec((tm, d_ssm), lambda i: ((i0 + i) % nt, (i0 + i) // nt))
    full = lambda a: pl.BlockSpec(a.shape, lambda i: (0,) * a.ndim)
    cnt_spec = pl.BlockSpec((8, LANES), lambda i: (0, 0))
    n = n // MOE_PARTS
    return pl.pallas_call(
        functools.partial(_post_kernel, d_ssm=d_ssm),
        grid=(steps,),
        in_specs=[row_in(d), ssm_spec, row_in(y_att.shape[1]), full(w_glu_bf), full(b_glu), full(g_ssm),
                  full(g_att), full(w_out_bf), full(g_ffn), full(w_r_bf), full(b_r)],
        out_specs=[row(d), row(d // 2), row(LANES), pl.BlockSpec((ROUTE_FIELDS, tm), lambda i: (0, i)), cnt_spec],
        out_shape=[jax.ShapeDtypeStruct((n, d), F32), jax.ShapeDtypeStruct((n, d // 2), jnp.uint32),
                   jax.ShapeDtypeStruct((n, LANES), F32), jax.ShapeDtypeStruct((ROUTE_FIELDS, n), F32),
                   jax.ShapeDtypeStruct((8, LANES), F32)],
        scratch_shapes=[pltpu.VMEM((8, LANES), F32)],
        compiler_params=_cparams("arbitrary"),
        name="post",
    )(x2, y_ssm_t, y_att, w_glu_bf, b_glu, g_ssm, g_att, w_out_bf, g_ffn, w_r_bf, b_r)


def _dispatch_sc(d0, d1, h2p, n_rows):
    n, w = h2p.shape
    workers = SC_CORES * SC_SUBCORES
    n_win = n // (workers * SC_WINDOW)
    mesh = plsc.VectorSubcoreMesh(core_axis_name="c", subcore_axis_name="s")

    def body(h_hbm, d0_hbm, d1_hbm, o_hbm, rows_v, i0_v, i1_v):
        wid = lax.axis_index("c") * SC_SUBCORES + lax.axis_index("s")

        @pl.loop(0, n_win)
        def _(j):
            blk = wid * n_win + j
            pltpu.sync_copy(h_hbm.at[pl.ds(blk * SC_WINDOW, SC_WINDOW)], rows_v)
            pltpu.sync_copy(d0_hbm.at[blk], i0_v)
            pltpu.sync_copy(d1_hbm.at[blk], i1_v)
            pltpu.sync_copy(rows_v, o_hbm.at[i0_v])
            pltpu.sync_copy(rows_v, o_hbm.at[i1_v])

    return pl.kernel(
        body,
        out_type=jax.ShapeDtypeStruct((n_rows, w), h2p.dtype),
        mesh=mesh,
        scratch_types=[pltpu.VMEM((SC_WINDOW, w), h2p.dtype), pltpu.VMEM((SC_WINDOW,), jnp.int32),
                       pltpu.VMEM((SC_WINDOW,), jnp.int32)],
        name="dispatch_sc",
    )(h2p, d0, d1)


def _experts_kernel(blk_e_ref, next_e_ref, used_ref, x_ref, wg_hbm, wu_hbm, wd_hbm, o_ref,
                    wg_f32, wu_f32, wd_f32, wg_bf, wu_bf, wd_bf, slot_ref, sems):
    staged = ((wg_hbm, wg_f32, wg_bf), (wu_hbm, wu_f32, wu_bf), (wd_hbm, wd_f32, wd_bf))

    def fetch(expert, slot):
        return [pltpu.make_async_copy(hbm.at[expert], f32.at[slot], sems.at[slot, k])
                for k, (hbm, f32, _) in enumerate(staged)]

    @pl.when(pl.program_id(0) == 0)
    def _():
        slot_ref[0] = 0
        for cp in fetch(blk_e_ref[0], 0):
            cp.start()

    for s in range(MOE_BLOCKS_PER_STEP):
        i = pl.program_id(0) * MOE_BLOCKS_PER_STEP + s
        rows = pl.ds(s * MOE_ROWS, MOE_ROWS)

        @pl.when(jnp.logical_or(i == 0, blk_e_ref[i] != blk_e_ref[jnp.maximum(i - 1, 0)]))
        def _():
            slot = slot_ref[0]
            for cp in fetch(blk_e_ref[i], slot):
                cp.wait()
            for _, f32, bf in staged:
                bf[...] = f32[slot].astype(BF16)
            nxt = next_e_ref[i]

            @pl.when(nxt >= 0)
            def _():
                for cp in fetch(nxt, 1 - slot):
                    cp.start()

            slot_ref[0] = 1 - slot

        @pl.when(i < used_ref[0])
        def _():
            xa, xb = (t.astype(BF16) for t in _unpack_bf16_pairs(x_ref[rows, :]))
            half = xa.shape[1]
            gate = (jnp.dot(xa, wg_bf[:half, :], preferred_element_type=F32)
                    + jnp.dot(xb, wg_bf[half:, :], preferred_element_type=F32))
            up = (jnp.dot(xa, wu_bf[:half, :], preferred_element_type=F32)
                  + jnp.dot(xb, wu_bf[half:, :], preferred_element_type=F32))
            hid = (jax.nn.silu(gate) * up).astype(BF16)
            out = jnp.dot(hid, wd_bf[...], preferred_element_type=F32)
            o_ref[rows, :] = _pack_bf16_pairs(out[:, :half], out[:, half:])

        @pl.when(i >= used_ref[0])
        def _():
            o_ref[rows, :] = jnp.zeros((MOE_ROWS, o_ref.shape[1]), o_ref.dtype)


def _experts(blk_e, next_e, n_used, buf, w_gate, w_up, w_down):
    n_rows, w = buf.shape
    d = w_down.shape[2]
    hbm = pl.BlockSpec(memory_space=pl.ANY)
    weights = (w_gate, w_up, w_down)
    step_rows = MOE_ROWS * MOE_BLOCKS_PER_STEP
    grid_spec = pltpu.PrefetchScalarGridSpec(
        num_scalar_prefetch=3,
        grid=(n_rows // step_rows,),
        in_specs=[pl.BlockSpec((step_rows, w), lambda i, be, ne, nu: (i, 0)), hbm, hbm, hbm],
        out_specs=pl.BlockSpec((step_rows, d // 2), lambda i, be, ne, nu: (i, 0)),
        scratch_shapes=([pltpu.VMEM((2,) + a.shape[1:], F32) for a in weights]
                        + [pltpu.VMEM(a.shape[1:], BF16) for a in weights]
                        + [pltpu.SMEM((1,), jnp.int32), pltpu.SemaphoreType.DMA((2, len(weights)))]),
    )
    return pl.pallas_call(
        _experts_kernel,
        grid_spec=grid_spec,
        out_shape=jax.ShapeDtypeStruct((n_rows, d // 2), jnp.uint32),
        compiler_params=_cparams("arbitrary"),
        name="experts",
    )(blk_e, next_e, n_used, buf, w_gate, w_up, w_down)


def _gather_sc(d0, d1, eo):
    n = d0.size
    w = eo.shape[1]
    workers = SC_CORES * SC_SUBCORES
    n_win = n // (workers * SC_WINDOW)
    mesh = plsc.VectorSubcoreMesh(core_axis_name="c", subcore_axis_name="s")

    def body(eo_hbm, d0_hbm, d1_hbm, o0_hbm, o1_hbm, rows_v, i_v):
        wid = lax.axis_index("c") * SC_SUBCORES + lax.axis_index("s")

        @pl.loop(0, n_win)
        def _(j):
            blk = wid * n_win + j
            for d_hbm, o_hbm in ((d0_hbm, o0_hbm), (d1_hbm, o1_hbm)):
                pltpu.sync_copy(d_hbm.at[blk], i_v)
                pltpu.sync_copy(eo_hbm.at[i_v], rows_v)
                pltpu.sync_copy(rows_v, o_hbm.at[pl.ds(blk * SC_WINDOW, SC_WINDOW)])

    out_sd = jax.ShapeDtypeStruct((n, w), eo.dtype)
    return pl.kernel(
        body,
        out_type=(out_sd, out_sd),
        mesh=mesh,
        scratch_types=[pltpu.VMEM((SC_WINDOW, w), eo.dtype), pltpu.VMEM((SC_WINDOW,), jnp.int32)],
        name="gather_sc",
    )(eo, d0, d1)


def _combine_kernel(x1_ref, route_ref, r0_ref, r1_ref, *rest):
    o_ref = rest[-1]
    route = route_ref[...]
    row0 = jnp.concatenate(_unpack_bf16_pairs(r0_ref[...]), axis=1)
    row1 = jnp.concatenate(_unpack_bf16_pairs(r1_ref[...]), axis=1)
    o_ref[...] = x1_ref[...] + (route[:, 2:3] * row0 + route[:, 3:4] * row1)


def _combine(x1, route, rows0, rows1, part, out_prev):
    n_slice, d = x1.shape
    tm = COMBINE_TM
    steps = n_slice // tm
    row = lambda w: pl.BlockSpec((tm, w), lambda i: (i, 0))
    in_specs = [row(d), row(LANES), row(d // 2), row(d // 2)]
    args = [x1, route, rows0, rows1]
    aliases = {}
    if out_prev is not None:
        in_specs.append(pl.BlockSpec(memory_space=pl.ANY))
        args.append(out_prev)
        aliases = {len(args) - 1: 0}
    return pl.pallas_call(
        _combine_kernel,
        grid=(steps,),
        in_specs=in_specs,
        out_specs=pl.BlockSpec((tm, d), lambda i: (part * steps + i, 0)),
        out_shape=jax.ShapeDtypeStruct((n_slice * MOE_PARTS, d), F32),
        input_output_aliases=aliases,
        compiler_params=_cparams("parallel"),
        name="combine",
    )(*args)


def _layer(x, g_mix, w_in, lam_re, lam_im, log_dt, b_re, b_im, c_re, c_im, d_skip, w_glu, b_glu, g_q, g_k,
           g_ssm_out, g_attn_out, w_out, g_ffn, w_rg, b_rg, w_re, b_re_router, w_gate, w_up, w_down):
    bsz, t_len, d = x.shape
    n = bsz * t_len
    d_ssm = w_glu.shape[0]
    d_att = g_attn_out.shape[0]
    n_heads = d_att // HEAD_DIM
    n_chunks = t_len // SSM_CHUNK
    x2 = x.reshape(n, d)

    u_t, q, k, v = _in_proj(x, g_mix[None], w_in.astype(BF16), jnp.tile(g_q, n_heads)[None],
                            jnp.tile(g_k, n_heads)[None], d_ssm, d_att)
    tables = _s5_tables(lam_re, lam_im, log_dt, b_re, b_im, c_re, c_im, d_skip)
    y_ssm_t = _s5(u_t.reshape(n_chunks, SSM_CHUNK, bsz, d_ssm), tables).reshape(t_len, bsz * d_ssm)

    y_att = _attention(q.reshape(n, d_att), k.reshape(n, d_att), v.reshape(n, d_att), bsz, t_len)

    w_r = jnp.concatenate([w_rg, w_re.reshape(d, N_EXPERTS)], axis=1)
    w_r = jnp.pad(w_r, ((0, 0), (0, LANES - w_r.shape[1]))).astype(BF16)
    b_r = jnp.pad(jnp.concatenate([b_rg, b_re_router.reshape(N_EXPERTS)]), (0, LANES - ROUTER_LANE0 - N_EXPERTS))[None]
    w_glu_bf, w_out_bf = w_glu.astype(BF16), w_out.astype(BF16)
    n_slice = n // MOE_PARTS
    n_rows = n_slice * 2 + N_EXPERTS * MOE_ROWS
    n_blk = n_rows // MOE_ROWS
    out = None
    for part in range(MOE_PARTS):
        x1, h2p, route, route_t, cnt = _post(x2, y_ssm_t, y_att, w_glu_bf, b_glu[None], g_ssm_out[None],
                                             g_attn_out[None], w_out_bf, g_ffn[None], w_r, b_r, part)
        counts = cnt[0, ROUTER_LANE0:ROUTER_LANE0 + N_EXPERTS].astype(jnp.int32)
        pcounts = ((counts + MOE_ROWS - 1) // MOE_ROWS) * MOE_ROWS
        pends = jnp.cumsum(pcounts)
        pstarts = pends - pcounts
        e_ids = jnp.arange(N_EXPERTS, dtype=jnp.int32)
        dests = []
        for k in range(2):
            expert = route_t[k].astype(jnp.int32)
            start = jnp.sum(jnp.where(expert[:, None] == e_ids, pstarts, 0), axis=-1)
            dest = start + route_t[4 + k].astype(jnp.int32)
            dests.append(dest.reshape(n_slice // SC_WINDOW, SC_WINDOW))
        blk_row0 = jnp.arange(n_blk, dtype=jnp.int32)[:, None] * MOE_ROWS
        blk_e = jnp.minimum(jnp.sum((pends[None, :] <= blk_row0).astype(jnp.int32), axis=1), N_EXPERTS - 1)
        n_used = (pends[-1:] // MOE_ROWS).astype(jnp.int32)
        present = jnp.any(blk_e[:, None] == e_ids, axis=0)
        later = jnp.logical_and(present[None, :], e_ids[None, :] > blk_e[:, None])
        next_e = jnp.min(jnp.where(later, e_ids[None, :], N_EXPERTS), axis=1)
        next_e = jnp.where(next_e == N_EXPERTS, -1, next_e).astype(jnp.int32)

        buf = _dispatch_sc(*dests, h2p, n_rows)
        eo = _experts(blk_e, next_e, n_used, buf, w_gate, w_up, w_down)
        out = _combine(x1, route, *_gather_sc(*dests, eo), part, out)
    return out.reshape(bsz, t_len, d)


def kernel(x, g_mix, w_in, ssm_lambda_re, ssm_lambda_im, ssm_log_dt, ssm_b_re, ssm_b_im, ssm_c_re, ssm_c_im, ssm_d, ssm_w_glu, ssm_b_glu, g_q, g_k, g_ssm_out, g_attn_out, w_out, g_ffn, w_router_group, b_router_group, w_router_expert, b_router_expert, w_gate, w_up, w_down):
    for l in range(g_mix.shape[0]):
        x = _layer(x, g_mix[l], w_in[l], ssm_lambda_re[l], ssm_lambda_im[l], ssm_log_dt[l], ssm_b_re[l],
                   ssm_b_im[l], ssm_c_re[l], ssm_c_im[l], ssm_d[l], ssm_w_glu[l], ssm_b_glu[l], g_q[l], g_k[l],
                   g_ssm_out[l], g_attn_out[l], w_out[l], g_ffn[l], w_router_group[l], b_router_group[l],
                   w_router_expert[l], b_router_expert[l], w_gate[l], w_up[l], w_down[l])
    return x
```
